```python
import numpy as np
import jax, jax.numpy as jnp
from jax import lax

D_MODEL = 2048
BATCH = 8
SEQ = 4096
DEPTH = 4

ATTN_HEADS = 16
ATTN_KV_HEADS = 4
ATTN_GROUP = ATTN_HEADS // ATTN_KV_HEADS
ATTN_HEAD_DIM = 128
WINDOW = 128
ATTN_BLOCK = 128
ROPE_THETA = 500000.0
ROT_DIM = ATTN_HEAD_DIM // 4
DN_HEADS = 16
DN_KEY_DIM = 128
DN_VALUE_DIM = 128
DN_CONV = 4
DN_CHUNK = 64
D_FF = 11 * D_MODEL // 4
FFN_CONV = 3
EPS = 1e-6

ATTN_Q_W = ATTN_HEADS * ATTN_HEAD_DIM
ATTN_KV_W = ATTN_KV_HEADS * ATTN_HEAD_DIM
DN_K_W = DN_HEADS * DN_KEY_DIM
DN_V_W = DN_HEADS * DN_VALUE_DIM
DN_QKV_W = 2 * DN_K_W + DN_V_W
IN_SPLITS = (ATTN_Q_W, ATTN_KV_W, ATTN_KV_W, DN_QKV_W, DN_HEADS, DN_HEADS, DN_V_W, D_MODEL, D_MODEL)
IN_WIDTH = sum(IN_SPLITS)

kernel_name = "hybrid_swa_sink_gdn_convglu_sandwich"


def rms_norm(x, w):
    xf = x.astype(jnp.float32)
    y = xf * lax.rsqrt(jnp.mean(xf * xf, axis=-1, keepdims=True) + EPS)
    return (y * w.astype(jnp.float32)).astype(x.dtype)


def l2_norm(x):
    return x * lax.rsqrt(jnp.sum(x * x, axis=-1, keepdims=True) + EPS)


def causal_depthwise_conv(x, w):
    k_width = w.shape[0]
    s = x.shape[1]
    xp = jnp.pad(x, ((0, 0), (k_width - 1, 0), (0, 0)))
    y = xp[:, 0:s] * w[0]
    for j in range(1, k_width):
        y = y + xp[:, j:j + s] * w[j]
    return y


def rope_tables(positions):
    inv_freq = ROPE_THETA ** (-jnp.arange(0, ROT_DIM, 2, dtype=jnp.float32) / ROT_DIM)
    ang = positions.astype(jnp.float32)[..., None] * inv_freq
    return jnp.cos(ang)[:, :, None, :], jnp.sin(ang)[:, :, None, :]


def partial_rope(x, cos, sin):
    half = ROT_DIM // 2
    x1 = x[..., :half].astype(jnp.float32)
    x2 = x[..., half:ROT_DIM].astype(jnp.float32)
    r1 = (x1 * cos - x2 * sin).astype(x.dtype)
    r2 = (x2 * cos + x1 * sin).astype(x.dtype)
    return jnp.concatenate([r1, r2, x[..., ROT_DIM:]], axis=-1)


def sliding_window_attention(q, k, v, sinks):
    b, s = q.shape[0], q.shape[1]
    nb = s // ATTN_BLOCK
    qb = q.reshape(b, nb, ATTN_BLOCK, ATTN_KV_HEADS, ATTN_GROUP, ATTN_HEAD_DIM).transpose(1, 0, 2, 3, 4, 5)

    def band(t):
        tb = t.reshape(b, nb, ATTN_BLOCK, ATTN_KV_HEADS, ATTN_HEAD_DIM)
        prev = jnp.concatenate([jnp.zeros_like(tb[:, :1]), tb[:, :-1]], axis=1)
        return jnp.concatenate([prev, tb], axis=2).transpose(1, 0, 2, 3, 4)

    kb, vb = band(k), band(v)
    r = jnp.arange(ATTN_BLOCK)[:, None]
    c = jnp.arange(2 * ATTN_BLOCK)[None, :]
    dist = r + ATTN_BLOCK - c
    in_band = (dist >= 0) & (dist < WINDOW)
    sink = sinks.astype(jnp.float32).reshape(ATTN_KV_HEADS, ATTN_GROUP)[None, :, :, None, None]
    scale = ATTN_HEAD_DIM ** -0.5

    def one_block(args):
        qi, ki, vi, bi = args
        logits = jnp.einsum('bqkgd,bckd->bkgqc', qi, ki).astype(jnp.float32) * scale
        valid = in_band & ((bi * ATTN_BLOCK - ATTN_BLOCK + c) >= 0)
        logits = jnp.where(valid, logits, -jnp.inf)
        m = jnp.maximum(jnp.max(logits, axis=-1, keepdims=True), sink)
        p = jnp.exp(logits - m)
        denom = jnp.sum(p, axis=-1, keepdims=True) + jnp.exp(sink - m)
        return jnp.einsum('bkgqc,bckd->bqkgd', (p / denom).astype(vi.dtype), vi)

    out = lax.map(one_block, (qb, kb, vb, jnp.arange(nb)))
    return out.transpose(1, 0, 2, 3, 4, 5).reshape(b, s, ATTN_Q_W)


def gated_delta_rule(q, k, v, beta, g):
    b, s, h, _ = q.shape
    nc = s // DN_CHUNK
    q = l2_norm(q) * (DN_KEY_DIM ** -0.5)
    k = l2_norm(k)

    def chunks(t):
        return t.reshape(b, nc, DN_CHUNK, h, t.shape[-1]).transpose(0, 1, 3, 2, 4)

    q, k, v = chunks(q), chunks(k), chunks(v)
    beta = beta.reshape(b, nc, DN_CHUNK, h).transpose(0, 1, 3, 2)
    g = jnp.cumsum(g.reshape(b, nc, DN_CHUNK, h).transpose(0, 1, 3, 2), axis=-1)
    tri = jnp.tril(jnp.ones((DN_CHUNK, DN_CHUNK), dtype=bool))
    strict = jnp.tril(jnp.ones((DN_CHUNK, DN_CHUNK), dtype=bool), -1)
    decay = jnp.exp(jnp.where(tri, g[..., :, None] - g[..., None, :], -jnp.inf))
    kk = jnp.einsum('bnhid,bnhjd->bnhij', k, k)
    a_mat = jnp.where(strict, beta[..., :, None] * kk * decay, 0.0)
    eye = jnp.eye(DN_CHUNK, dtype=jnp.float32)
    t_inv = lax.linalg.triangular_solve(eye + a_mat, jnp.broadcast_to(eye, a_mat.shape),
                                        left_side=True, lower=True, unit_diagonal=True)
    u = jnp.einsum('bnhij,bnhjd->bnhid', t_inv, v * beta[..., None])
    w = jnp.einsum('bnhij,bnhjd->bnhid', t_inv, k * (beta * jnp.exp(g))[..., None])
    qk = jnp.where(tri, jnp.einsum('bnhid,bnhjd->bnhij', q, k) * decay, 0.0)

    def step(state, inp):
        qc, kc, uc, wc, gc, qkc = inp
        v_new = uc - jnp.einsum('bhcd,bhde->bhce', wc, state)
        o = (jnp.einsum('bhcd,bhde->bhce', qc * jnp.exp(gc)[..., None], state)
             + jnp.einsum('bhij,bhje->bhie', qkc, v_new))
        g_last = gc[..., -1]
        state = (state * jnp.exp(g_last)[..., None, None]
                 + jnp.einsum('bhcd,bhce->bhde', kc * jnp.exp(g_last[..., None] - gc)[..., None], v_new))
        return state, o

    xs = tuple(jnp.moveaxis(t, 1, 0) for t in (q, k, u, w, g, qk))
    state0 = jnp.zeros((b, h, DN_KEY_DIM, DN_VALUE_DIM), jnp.float32)
    _, o = lax.scan(step, state0, xs)
    return o.transpose(1, 0, 3, 2, 4).reshape(b, s, h, DN_VALUE_DIM)


def hybrid_mixer(xn, cos, sin, w_in, sinks, dn_conv_w, dn_a_log, dn_dt_bias, dn_norm_w,
                 w_branch_attn, w_branch_dn, w_out):
    b, s, _ = xn.shape
    proj = xn @ w_in
    idx = np.cumsum(IN_SPLITS)[:-1].tolist()
    q_a, k_a, v_a, qkv_d, b_d, a_d, z_d, g_a, g_d = jnp.split(proj, idx, axis=-1)

    q_a = partial_rope(q_a.reshape(b, s, ATTN_HEADS, ATTN_HEAD_DIM), cos, sin)
    k_a = partial_rope(k_a.reshape(b, s, ATTN_KV_HEADS, ATTN_HEAD_DIM), cos, sin)
    v_a = v_a.reshape(b, s, ATTN_KV_HEADS, ATTN_HEAD_DIM)
    o_a = sliding_window_attention(q_a, k_a, v_a, sinks)

    qkv_d = jax.nn.silu(causal_depthwise_conv(qkv_d, dn_conv_w)).astype(jnp.float32)
    q_d, k_d, v_d = jnp.split(qkv_d, [DN_K_W, 2 * DN_K_W], axis=-1)
    beta = jax.nn.sigmoid(b_d.astype(jnp.float32))
    g = -jnp.exp(dn_a_log.astype(jnp.float32)) * jax.nn.softplus(b_d.dtype.type(1) * a_d.astype(jnp.float32) + dn_dt_bias.astype(jnp.float32))
    o_d = gated_delta_rule(q_d.reshape(b, s, DN_HEADS, DN_KEY_DIM), k_d.reshape(b, s, DN_HEADS, DN_KEY_DIM),
                           v_d.reshape(b, s, DN_HEADS, DN_VALUE_DIM), beta, g)
    z = z_d.reshape(b, s, DN_HEADS, DN_VALUE_DIM).astype(jnp.float32)
    o_d = (o_d * lax.rsqrt(jnp.mean(o_d * o_d, axis=-1, keepdims=True) + EPS)
           * dn_norm_w.astype(jnp.float32) * jax.nn.silu(z))
    o_d = o_d.reshape(b, s, DN_V_W).astype(xn.dtype)

    y = jax.nn.sigmoid(g_a) * (o_a @ w_branch_attn) + jax.nn.sigmoid(g_d) * (o_d @ w_branch_dn)
    return y @ w_out


def conv_glu_ffn(xn, w_up, conv_w, conv_b, w_down):
    u = causal_depthwise_conv(xn @ w_up, conv_w) + conv_b
    gate, val = jnp.split(u, 2, axis=-1)
    return (jax.nn.silu(gate) * val) @ w_down


def _fwd_setup_inputs(seed: int = 0) -> dict:
    key = jax.random.key(seed)
    ks = jax.random.split(key, 20)
    f32 = jnp.float32
    L, D = DEPTH, D_MODEL

    def normal(k, shape, scale):
        return jax.random.normal(k, shape, f32) * scale

    def gain(k, shape):
        return 1.0 + 0.02 * jax.random.normal(k, shape, f32)

    x = normal(ks[0], (BATCH, SEQ, D), 1.0)
    positions = (jax.random.randint(ks[1], (BATCH, 1), 0, 4096, dtype=jnp.int32)
                 + jnp.arange(SEQ, dtype=jnp.int32)[None, :])
    dt = jnp.exp(jax.random.uniform(ks[7], (L, DN_HEADS), f32, np.log(1e-3), np.log(1e-1)))
    return {
        "x": x,
        "positions": positions,
        "norm_mix_pre": gain(ks[2], (L, D)),
        "w_in": normal(ks[3], (L, D, IN_WIDTH), D ** -0.5),
        "attn_sinks": normal(ks[4], (L, ATTN_HEADS), 1.0),
        "dn_conv_w": normal(ks[5], (L, DN_CONV, DN_QKV_W), DN_CONV ** -0.5),
        "dn_a_log": jnp.log(jax.random.uniform(ks[6], (L, DN_HEADS), f32, 1.0, 16.0)),
        "dn_dt_bias": dt + jnp.log(-jnp.expm1(-dt)),
        "dn_norm_w": gain(ks[8], (L, DN_VALUE_DIM)),
        "w_branch_attn": normal(ks[9], (L, ATTN_Q_W, D), ATTN_Q_W ** -0.5),
        "w_branch_dn": normal(ks[10], (L, DN_V_W, D), DN_V_W ** -0.5),
        "w_out": normal(ks[11], (L, D, D), D ** -0.5),
        "norm_mix_post": gain(ks[12], (L, D)),
        "norm_ffn_pre": gain(ks[13], (L, D)),
        "w_ffn_up": normal(ks[14], (L, D, 2 * D_FF), D ** -0.5),
        "ffn_conv_w": normal(ks[15], (L, FFN_CONV, 2 * D_FF), FFN_CONV ** -0.5),
        "ffn_conv_b": normal(ks[16], (L, 2 * D_FF), 0.01),
        "w_ffn_down": normal(ks[17], (L, D_FF, D), D_FF ** -0.5),
        "norm_ffn_post": gain(ks[18], (L, D)),
    }


def _fwd_reference(x, positions, norm_mix_pre, w_in, attn_sinks, dn_conv_w, dn_a_log, dn_dt_bias, dn_norm_w,
              w_branch_attn, w_branch_dn, w_out, norm_mix_post, norm_ffn_pre, w_ffn_up, ffn_conv_w,
              ffn_conv_b, w_ffn_down, norm_ffn_post):
    cos, sin = rope_tables(positions)
    h = x
    for l in range(DEPTH):
        xn = rms_norm(h, norm_mix_pre[l])
        mix = hybrid_mixer(xn, cos, sin, w_in[l], attn_sinks[l], dn_conv_w[l], dn_a_log[l], dn_dt_bias[l],
                           dn_norm_w[l], w_branch_attn[l], w_branch_dn[l], w_out[l])
        h = h + rms_norm(mix, norm_mix_post[l])
        xf = rms_norm(h, norm_ffn_pre[l])
        f = conv_glu_ffn(xf, w_ffn_up[l], ffn_conv_w[l], ffn_conv_b[l], w_ffn_down[l])
        h = h + rms_norm(f, norm_ffn_post[l])
    return h


import jax as _jax
import jax.numpy as _jnp

TWIN_FORMAT = 'train_step'
FWD_PARAMS = ['x', 'positions', 'norm_mix_pre', 'w_in', 'attn_sinks', 'dn_conv_w', 'dn_a_log', 'dn_dt_bias', 'dn_norm_w', 'w_branch_attn', 'w_branch_dn', 'w_out', 'norm_mix_post', 'norm_ffn_pre', 'w_ffn_up', 'ffn_conv_w', 'ffn_conv_b', 'w_ffn_down', 'norm_ffn_post']
TWIN_WEIGHTS = ['norm_mix_pre', 'w_in', 'attn_sinks', 'dn_conv_w', 'dn_a_log', 'dn_dt_bias', 'dn_norm_w', 'w_branch_attn', 'w_branch_dn', 'w_out', 'norm_mix_post', 'norm_ffn_pre', 'w_ffn_up', 'ffn_conv_w', 'ffn_conv_b', 'w_ffn_down', 'norm_ffn_post']
TWIN_DIFF_INPUT = 'x'
TWIN_INPUTS = ['x', 'positions', 'norm_mix_pre', 'w_in', 'attn_sinks', 'dn_conv_w', 'dn_a_log', 'dn_dt_bias', 'dn_norm_w', 'w_branch_attn', 'w_branch_dn', 'w_out', 'norm_mix_post', 'norm_ffn_pre', 'w_ffn_up', 'ffn_conv_w', 'ffn_conv_b', 'w_ffn_down', 'norm_ffn_post', 'loss_target', 'm_norm_mix_pre', 'm_w_in', 'm_attn_sinks', 'm_dn_conv_w', 'm_dn_a_log', 'm_dn_dt_bias', 'm_dn_norm_w', 'm_w_branch_attn', 'm_w_branch_dn', 'm_w_out', 'm_norm_mix_post', 'm_norm_ffn_pre', 'm_w_ffn_up', 'm_ffn_conv_w', 'm_ffn_conv_b', 'm_w_ffn_down', 'm_norm_ffn_post', 'v_norm_mix_pre', 'v_w_in', 'v_attn_sinks', 'v_dn_conv_w', 'v_dn_a_log', 'v_dn_dt_bias', 'v_dn_norm_w', 'v_w_branch_attn', 'v_w_branch_dn', 'v_w_out', 'v_norm_mix_post', 'v_norm_ffn_pre', 'v_w_ffn_up', 'v_ffn_conv_w', 'v_ffn_conv_b', 'v_w_ffn_down', 'v_norm_ffn_post']
TWIN_OUTPUTS = ['loss', 'grad_x', 'grad_norm_mix_pre', 'grad_w_in', 'grad_attn_sinks', 'grad_dn_conv_w', 'grad_dn_a_log', 'grad_dn_dt_bias', 'grad_dn_norm_w', 'grad_w_branch_attn', 'grad_w_branch_dn', 'grad_w_out', 'grad_norm_mix_post', 'grad_norm_ffn_pre', 'grad_w_ffn_up', 'grad_ffn_conv_w', 'grad_ffn_conv_b', 'grad_w_ffn_down', 'grad_norm_ffn_post', 'delta_norm_mix_pre', 'delta_w_in', 'delta_attn_sinks', 'delta_dn_conv_w', 'delta_dn_a_log', 'delta_dn_dt_bias', 'delta_dn_norm_w', 'delta_w_branch_attn', 'delta_w_branch_dn', 'delta_w_out', 'delta_norm_mix_post', 'delta_norm_ffn_pre', 'delta_w_ffn_up', 'delta_ffn_conv_w', 'delta_ffn_conv_b', 'delta_w_ffn_down', 'delta_norm_ffn_post', 'new_m_norm_mix_pre', 'new_m_w_in', 'new_m_attn_sinks', 'new_m_dn_conv_w', 'new_m_dn_a_log', 'new_m_dn_dt_bias', 'new_m_dn_norm_w', 'new_m_w_branch_attn', 'new_m_w_branch_dn', 'new_m_w_out', 'new_m_norm_mix_post', 'new_m_norm_ffn_pre', 'new_m_w_ffn_up', 'new_m_ffn_conv_w', 'new_m_ffn_conv_b', 'new_m_w_ffn_down', 'new_m_norm_ffn_post', 'new_v_norm_mix_pre', 'new_v_w_in', 'new_v_attn_sinks', 'new_v_dn_conv_w', 'new_v_dn_a_log', 'new_v_dn_dt_bias', 'new_v_dn_norm_w', 'new_v_w_branch_attn', 'new_v_w_branch_dn', 'new_v_w_out', 'new_v_norm_mix_post', 'new_v_norm_ffn_pre', 'new_v_w_ffn_up', 'new_v_ffn_conv_w', 'new_v_ffn_conv_b', 'new_v_w_ffn_down', 'new_v_norm_ffn_post']
TWIN_LEAF_KINDS = {'loss': 'loss', 'grad_x': 'grad_x', 'grad_norm_mix_pre': 'grad_w', 'grad_w_in': 'grad_w', 'grad_attn_sinks': 'grad_w', 'grad_dn_conv_w': 'grad_w', 'grad_dn_a_log': 'grad_w', 'grad_dn_dt_bias': 'grad_w', 'grad_dn_norm_w': 'grad_w', 'grad_w_branch_attn': 'grad_w', 'grad_w_branch_dn': 'grad_w', 'grad_w_out': 'grad_w', 'grad_norm_mix_post': 'grad_w', 'grad_norm_ffn_pre': 'grad_w', 'grad_w_ffn_up': 'grad_w', 'grad_ffn_conv_w': 'grad_w', 'grad_ffn_conv_b': 'grad_w', 'grad_w_ffn_down': 'grad_w', 'grad_norm_ffn_post': 'grad_w', 'delta_norm_mix_pre': 'delta_w', 'delta_w_in': 'delta_w', 'delta_attn_sinks': 'delta_w', 'delta_dn_conv_w': 'delta_w', 'delta_dn_a_log': 'delta_w', 'delta_dn_dt_bias': 'delta_w', 'delta_dn_norm_w': 'delta_w', 'delta_w_branch_attn': 'delta_w', 'delta_w_branch_dn': 'delta_w', 'delta_w_out': 'delta_w', 'delta_norm_mix_post': 'delta_w', 'delta_norm_ffn_pre': 'delta_w', 'delta_w_ffn_up': 'delta_w', 'delta_ffn_conv_w': 'delta_w', 'delta_ffn_conv_b': 'delta_w', 'delta_w_ffn_down': 'delta_w', 'delta_norm_ffn_post': 'delta_w', 'new_m_norm_mix_pre': 'new_m', 'new_m_w_in': 'new_m', 'new_m_attn_sinks': 'new_m', 'new_m_dn_conv_w': 'new_m', 'new_m_dn_a_log': 'new_m', 'new_m_dn_dt_bias': 'new_m', 'new_m_dn_norm_w': 'new_m', 'new_m_w_branch_attn': 'new_m', 'new_m_w_branch_dn': 'new_m', 'new_m_w_out': 'new_m', 'new_m_norm_mix_post': 'new_m', 'new_m_norm_ffn_pre': 'new_m', 'new_m_w_ffn_up': 'new_m', 'new_m_ffn_conv_w': 'new_m', 'new_m_ffn_conv_b': 'new_m', 'new_m_w_ffn_down': 'new_m', 'new_m_norm_ffn_post': 'new_m', 'new_v_norm_mix_pre': 'new_v', 'new_v_w_in': 'new_v', 'new_v_attn_sinks': 'new_v', 'new_v_dn_conv_w': 'new_v', 'new_v_dn_a_log': 'new_v', 'new_v_dn_dt_bias': 'new_v', 'new_v_dn_norm_w': 'new_v', 'new_v_w_branch_attn': 'new_v', 'new_v_w_branch_dn': 'new_v', 'new_v_w_out': 'new_v', 'new_v_norm_mix_post': 'new_v', 'new_v_norm_ffn_pre': 'new_v', 'new_v_w_ffn_up': 'new_v', 'new_v_ffn_conv_w': 'new_v', 'new_v_ffn_conv_b': 'new_v', 'new_v_w_ffn_down': 'new_v', 'new_v_norm_ffn_post': 'new_v'}


def _forward(args):
    return _fwd_reference(*[args[k] for k in FWD_PARAMS])


def _output_shape():
    out = _jax.eval_shape(lambda: _forward(_fwd_setup_inputs(0)))
    return out.shape, out.dtype

N_MICROBATCH = 1
ADAM_LR = 0.001
ADAM_B1 = 0.9
ADAM_B2 = 0.999
ADAM_EPS = 1e-08
ADAM_WD = 0.01
ADAM_STEP = 10
PER_EXAMPLE_BATCH_AXIS = {'x': 0, 'positions': 0, 'loss_target': 0}
SHARED_INPUTS = []
_WEIGHT_DTYPES = {'norm_mix_pre': _jnp.float32, 'w_in': _jnp.float32, 'attn_sinks': _jnp.float32, 'dn_conv_w': _jnp.float32, 'dn_a_log': _jnp.float32, 'dn_dt_bias': _jnp.float32, 'dn_norm_w': _jnp.float32, 'w_branch_attn': _jnp.float32, 'w_branch_dn': _jnp.float32, 'w_out': _jnp.float32, 'norm_mix_post': _jnp.float32, 'norm_ffn_pre': _jnp.float32, 'w_ffn_up': _jnp.float32, 'ffn_conv_w': _jnp.float32, 'ffn_conv_b': _jnp.float32, 'w_ffn_down': _jnp.float32, 'norm_ffn_post': _jnp.float32}
MOMENT_SCALE = {'norm_mix_pre': 1.759159e+00, 'w_in': 6.152291e-01, 'attn_sinks': 4.715437e-01, 'dn_conv_w': 8.241544e-01, 'dn_a_log': 4.615523e+00, 'dn_dt_bias': 4.553003e+00, 'dn_norm_w': 6.575707e+00, 'w_branch_attn': 1.072469e+00, 'w_branch_dn': 1.750398e+00, 'w_out': 2.013441e+00, 'norm_mix_post': 1.587379e+01, 'norm_ffn_pre': 9.261652e-01, 'w_ffn_up': 4.054207e-01, 'ffn_conv_w': 4.280633e-01, 'ffn_conv_b': 1.778426e+00, 'w_ffn_down': 7.450357e-01, 'norm_ffn_post': 1.575944e+01}


def _to_microbatches(a, axis):
    t = _jnp.moveaxis(a, axis, 0)
    t = t.reshape((N_MICROBATCH, t.shape[0] // N_MICROBATCH) + t.shape[1:])
    return _jnp.moveaxis(t, 1, axis + 1)


def setup_inputs(seed: int = 0) -> dict:
    inp = _fwd_setup_inputs(seed)
    key = _jax.random.fold_in(_jax.random.key(seed), 7919)
    shape, _ = _output_shape()
    out = dict(inp)
    out["loss_target"] = _jax.random.normal(_jax.random.fold_in(key, 0), shape, _jnp.float32)
    for i, name in enumerate(TWIN_WEIGHTS):
        w = inp[name].astype(_jnp.float32)
        if MOMENT_SCALE is None:
            s = _jnp.sqrt(_jnp.mean(_jnp.square(w)) + 1e-30)
        else:
            s = MOMENT_SCALE[name]
        km, kv = _jax.random.split(_jax.random.fold_in(key, i + 1))
        out[name] = w
        out["m_" + name] = s * _jax.random.normal(km, w.shape, _jnp.float32)
        out["v_" + name] = (s * s) * _jax.random.uniform(kv, w.shape, _jnp.float32, 0.5, 1.5)
    if N_MICROBATCH > 1:
        for name, axis in PER_EXAMPLE_BATCH_AXIS.items():
            out[name] = _to_microbatches(out[name], axis)
    return {'x': out['x'], 'positions': out['positions'], 'norm_mix_pre': out['norm_mix_pre'], 'w_in': out['w_in'], 'attn_sinks': out['attn_sinks'], 'dn_conv_w': out['dn_conv_w'], 'dn_a_log': out['dn_a_log'], 'dn_dt_bias': out['dn_dt_bias'], 'dn_norm_w': out['dn_norm_w'], 'w_branch_attn': out['w_branch_attn'], 'w_branch_dn': out['w_branch_dn'], 'w_out': out['w_out'], 'norm_mix_post': out['norm_mix_post'], 'norm_ffn_pre': out['norm_ffn_pre'], 'w_ffn_up': out['w_ffn_up'], 'ffn_conv_w': out['ffn_conv_w'], 'ffn_conv_b': out['ffn_conv_b'], 'w_ffn_down': out['w_ffn_down'], 'norm_ffn_post': out['norm_ffn_post'], 'loss_target': out['loss_target'], 'm_norm_mix_pre': out['m_norm_mix_pre'], 'm_w_in': out['m_w_in'], 'm_attn_sinks': out['m_attn_sinks'], 'm_dn_conv_w': out['m_dn_conv_w'], 'm_dn_a_log': out['m_dn_a_log'], 'm_dn_dt_bias': out['m_dn_dt_bias'], 'm_dn_norm_w': out['m_dn_norm_w'], 'm_w_branch_attn': out['m_w_branch_attn'], 'm_w_branch_dn': out['m_w_branch_dn'], 'm_w_out': out['m_w_out'], 'm_norm_mix_post': out['m_norm_mix_post'], 'm_norm_ffn_pre': out['m_norm_ffn_pre'], 'm_w_ffn_up': out['m_w_ffn_up'], 'm_ffn_conv_w': out['m_ffn_conv_w'], 'm_ffn_conv_b': out['m_ffn_conv_b'], 'm_w_ffn_down': out['m_w_ffn_down'], 'm_norm_ffn_post': out['m_norm_ffn_post'], 'v_norm_mix_pre': out['v_norm_mix_pre'], 'v_w_in': out['v_w_in'], 'v_attn_sinks': out['v_attn_sinks'], 'v_dn_conv_w': out['v_dn_conv_w'], 'v_dn_a_log': out['v_dn_a_log'], 'v_dn_dt_bias': out['v_dn_dt_bias'], 'v_dn_norm_w': out['v_dn_norm_w'], 'v_w_branch_attn': out['v_w_branch_attn'], 'v_w_branch_dn': out['v_w_branch_dn'], 'v_w_out': out['v_w_out'], 'v_norm_mix_post': out['v_norm_mix_post'], 'v_norm_ffn_pre': out['v_norm_ffn_pre'], 'v_w_ffn_up': out['v_w_ffn_up'], 'v_ffn_conv_w': out['v_ffn_conv_w'], 'v_ffn_conv_b': out['v_ffn_conv_b'], 'v_w_ffn_down': out['v_w_ffn_down'], 'v_norm_ffn_post': out['v_norm_ffn_post']}


def _loss(weights, diff, rest, loss_target):
    with _jax.named_scope("forward"):
        args = {**rest, TWIN_DIFF_INPUT: diff, **{k: w.astype(_WEIGHT_DTYPES[k]) for k, w in weights.items()}}
        y = _forward(args)
    with _jax.named_scope("loss_head"):
        err = _jnp.square(y.astype(_jnp.float32) - loss_target)
        return 0.5 * _jnp.sum(_jnp.mean(err, axis=-1)) if err.ndim else 0.5 * err


def _adamw(w, g, m, v):
    m = ADAM_B1 * m + (1.0 - ADAM_B1) * g
    v = ADAM_B2 * v + (1.0 - ADAM_B2) * _jnp.square(g)
    m_hat = m / (1.0 - ADAM_B1 ** ADAM_STEP)
    v_hat = v / (1.0 - ADAM_B2 ** ADAM_STEP)
    delta = -ADAM_LR * (m_hat / (_jnp.sqrt(v_hat) + ADAM_EPS) + ADAM_WD * w)
    return delta, m, v


def reference(x, positions, norm_mix_pre, w_in, attn_sinks, dn_conv_w, dn_a_log, dn_dt_bias, dn_norm_w, w_branch_attn, w_branch_dn, w_out, norm_mix_post, norm_ffn_pre, w_ffn_up, ffn_conv_w, ffn_conv_b, w_ffn_down, norm_ffn_post, loss_target, m_norm_mix_pre, m_w_in, m_attn_sinks, m_dn_conv_w, m_dn_a_log, m_dn_dt_bias, m_dn_norm_w, m_w_branch_attn, m_w_branch_dn, m_w_out, m_norm_mix_post, m_norm_ffn_pre, m_w_ffn_up, m_ffn_conv_w, m_ffn_conv_b, m_w_ffn_down, m_norm_ffn_post, v_norm_mix_pre, v_w_in, v_attn_sinks, v_dn_conv_w, v_dn_a_log, v_dn_dt_bias, v_dn_norm_w, v_w_branch_attn, v_w_branch_dn, v_w_out, v_norm_mix_post, v_norm_ffn_pre, v_w_ffn_up, v_ffn_conv_w, v_ffn_conv_b, v_w_ffn_down, v_norm_ffn_post):
    given = dict(x=x, positions=positions, norm_mix_pre=norm_mix_pre, w_in=w_in, attn_sinks=attn_sinks, dn_conv_w=dn_conv_w, dn_a_log=dn_a_log, dn_dt_bias=dn_dt_bias, dn_norm_w=dn_norm_w, w_branch_attn=w_branch_attn, w_branch_dn=w_branch_dn, w_out=w_out, norm_mix_post=norm_mix_post, norm_ffn_pre=norm_ffn_pre, w_ffn_up=w_ffn_up, ffn_conv_w=ffn_conv_w, ffn_conv_b=ffn_conv_b, w_ffn_down=w_ffn_down, norm_ffn_post=norm_ffn_post, loss_target=loss_target, m_norm_mix_pre=m_norm_mix_pre, m_w_in=m_w_in, m_attn_sinks=m_attn_sinks, m_dn_conv_w=m_dn_conv_w, m_dn_a_log=m_dn_a_log, m_dn_dt_bias=m_dn_dt_bias, m_dn_norm_w=m_dn_norm_w, m_w_branch_attn=m_w_branch_attn, m_w_branch_dn=m_w_branch_dn, m_w_out=m_w_out, m_norm_mix_post=m_norm_mix_post, m_norm_ffn_pre=m_norm_ffn_pre, m_w_ffn_up=m_w_ffn_up, m_ffn_conv_w=m_ffn_conv_w, m_ffn_conv_b=m_ffn_conv_b, m_w_ffn_down=m_w_ffn_down, m_norm_ffn_post=m_norm_ffn_post, v_norm_mix_pre=v_norm_mix_pre, v_w_in=v_w_in, v_attn_sinks=v_attn_sinks, v_dn_conv_w=v_dn_conv_w, v_dn_a_log=v_dn_a_log, v_dn_dt_bias=v_dn_dt_bias, v_dn_norm_w=v_dn_norm_w, v_w_branch_attn=v_w_branch_attn, v_w_branch_dn=v_w_branch_dn, v_w_out=v_w_out, v_norm_mix_post=v_norm_mix_post, v_norm_ffn_pre=v_norm_ffn_pre, v_w_ffn_up=v_w_ffn_up, v_ffn_conv_w=v_ffn_conv_w, v_ffn_conv_b=v_ffn_conv_b, v_w_ffn_down=v_w_ffn_down, v_norm_ffn_post=v_norm_ffn_post)
    weights = {n: given[n] for n in TWIN_WEIGHTS}
    shared = {n: given[n] for n in SHARED_INPUTS}
    per_example = {n: given[n] for n in ['x', 'positions']}
    grad_fn = _jax.value_and_grad(_loss, argnums=(0, 1))

    def one_microbatch(ex, loss_target):
        ex = dict(ex)
        diff = ex.pop(TWIN_DIFF_INPUT)
        return grad_fn(weights, diff, {**shared, **ex}, loss_target)

    if N_MICROBATCH == 1:
        loss, (grad_w, grad_x) = one_microbatch(per_example, given["loss_target"])
    else:
        def body(carry, xs):
            loss_sum, grad_sum = carry
            l_k, (gw_k, gx_k) = one_microbatch(xs[0], xs[1])
            with _jax.named_scope("update"):
                return (loss_sum + l_k, _jax.tree.map(_jnp.add, grad_sum, gw_k)), gx_k

        init = (_jnp.zeros((), _jnp.float32), _jax.tree.map(_jnp.zeros_like, weights))
        (loss, grad_w), grad_x = _jax.lax.scan(body, init, (per_example, given["loss_target"]))
    with _jax.named_scope("update"):
        delta_w, new_m, new_v = {}, {}, {}
        for n in TWIN_WEIGHTS:
            delta_w[n], new_m[n], new_v[n] = _adamw(weights[n], grad_w[n], given["m_" + n], given["v_" + n])
    return (loss, grad_x, *[grad_w[n] for n in TWIN_WEIGHTS], *[delta_w[n] for n in TWIN_WEIGHTS],
            *[new_m[n] for n in TWIN_WEIGHTS], *[new_v[n] for n in TWIN_WEIGHTS])
```

```python
import functools
import math

import jax
import jax.numpy as jnp
from jax import lax
from jax.experimental import pallas as pl
from jax.experimental.pallas import tpu as pltpu

F32, BF16 = jnp.float32, jnp.bfloat16
EPS = 1e-6
HEAD = 128
ATTN_BLOCK = 128
ROT_HALF = 16
ROPE_THETA = 500000.0
CHUNK = 64
DN_CONV, FFN_CONV = 4, 3
HALO = 8
LANES = 128
PACK_W = 512
PACK_TR = 512
SMALL_W = 1024
ADAM_LR, ADAM_B1, ADAM_B2, ADAM_EPS, ADAM_WD, ADAM_STEP = 0.001, 0.9, 0.999, 1e-08, 0.01, 10
VMEM_LIMIT = 56 * 1024 * 1024
MESH = pl.DeviceIdType.MESH
ANY = pl.BlockSpec(memory_space=pl.ANY)


def _cp(*sem):
    return pltpu.CompilerParams(dimension_semantics=sem, vmem_limit_bytes=VMEM_LIMIT)


def _tile(n, pref, unit=LANES):
    if n <= pref:
        return n
    t = (pref // unit) * unit
    while t >= unit:
        if n % t == 0:
            return t
        t -= unit
    raise ValueError((n, pref, unit))


def _sigmoid(x):
    return 1.0 / (1.0 + jnp.exp(-x))


def _softplus(x):
    return jnp.maximum(x, 0.0) + jnp.log(1.0 + jnp.exp(-jnp.abs(x)))


def _dsilu(y):
    s = _sigmoid(y)
    return s * (1.0 + y * (1.0 - s))


def _mm(a, b, *, ta=False, tb=False, add=None, out_dtype=F32, name):
    if ta:
        kdim, m = a.shape
    else:
        m, kdim = a.shape
    if tb:
        n, k2 = b.shape
    else:
        k2, n = b.shape
    assert kdim == k2, (a.shape, b.shape, ta, tb)
    tm, tn, tk = _tile(m, 512), _tile(n, 1024), _tile(kdim, 512)
    nk = kdim // tk
    dims = (((0 if ta else 1,), (1 if tb else 0,)), ((), ()))

    def body(*refs):
        if add is None:
            a_ref, b_ref, o_ref, acc_ref = refs
        else:
            a_ref, b_ref, add_ref, o_ref, acc_ref = refs
        k = pl.program_id(2)

        @pl.when(k == 0)
        def _():
            acc_ref[...] = jnp.zeros_like(acc_ref)

        acc_ref[...] += lax.dot_general(a_ref[...].astype(BF16), b_ref[...].astype(BF16), dims,
                                        preferred_element_type=F32)

        @pl.when(k == nk - 1)
        def _():
            r = acc_ref[...]
            if add is not None:
                r = r + add_ref[...]
            o_ref[...] = r.astype(o_ref.dtype)

    a_spec = pl.BlockSpec((tk, tm), lambda i, j, k: (k, i)) if ta else pl.BlockSpec((tm, tk), lambda i, j, k: (i, k))
    b_spec = pl.BlockSpec((tn, tk), lambda i, j, k: (j, k)) if tb else pl.BlockSpec((tk, tn), lambda i, j, k: (k, j))
    o_spec = pl.BlockSpec((tm, tn), lambda i, j, k: (i, j))
    ins, specs = [a, b], [a_spec, b_spec]
    if add is not None:
        ins.append(add)
        specs.append(o_spec)
    return pl.pallas_call(
        body, grid=(m // tm, n // tn, nk), in_specs=specs, out_specs=o_spec,
        out_shape=jax.ShapeDtypeStruct((m, n), out_dtype),
        scratch_shapes=[pltpu.VMEM((tm, tn), F32)],
        compiler_params=_cp("parallel", "parallel", "arbitrary"), name=name,
    )(*ins)


def _rms_fwd(x, w, res=None, *, out_dtype, name):
    s, d = x.shape
    tr = _tile(s, 256, 8)

    def body(*refs):
        if res is None:
            x_ref, w_ref, o_ref = refs
        else:
            x_ref, w_ref, r_ref, o_ref = refs
        xv = x_ref[...]
        y = xv * lax.rsqrt(jnp.mean(xv * xv, axis=-1, keepdims=True) + EPS) * w_ref[...]
        if res is not None:
            y = y + r_ref[...]
        o_ref[...] = y.astype(o_ref.dtype)

    row = pl.BlockSpec((tr, d), lambda i: (i, 0))
    par = pl.BlockSpec((1, d), lambda i: (0, 0))
    ins, specs = [x, w.reshape(1, d)], [row, par]
    if res is not None:
        ins.append(res)
        specs.append(row)
    return pl.pallas_call(body, grid=(s // tr,), in_specs=specs, out_specs=row,
                          out_shape=jax.ShapeDtypeStruct((s, d), out_dtype),
                          compiler_params=_cp("parallel"), name=name)(*ins)


def _rms_bwd(x, w, dy, add=None, *, name):
    s, d = x.shape
    tr = _tile(s, 256, 8)

    def body(*refs):
        if add is None:
            x_ref, w_ref, dy_ref, dx_ref, dw_ref = refs
        else:
            x_ref, w_ref, dy_ref, add_ref, dx_ref, dw_ref = refs
        xv = x_ref[...]
        dyv = dy_ref[...].astype(F32)
        r = lax.rsqrt(jnp.mean(xv * xv, axis=-1, keepdims=True) + EPS)
        xh = xv * r
        dyw = dyv * w_ref[...]
        dx = r * (dyw - xh * jnp.mean(dyw * xh, axis=-1, keepdims=True))
        if add is not None:
            dx = dx + add_ref[...]
        dx_ref[...] = dx

        @pl.when(pl.program_id(0) == 0)
        def _():
            dw_ref[...] = jnp.zeros_like(dw_ref)

        dw_ref[...] += jnp.sum(dyv * xh, axis=0, keepdims=True)

    row = pl.BlockSpec((tr, d), lambda i: (i, 0))
    par = pl.BlockSpec((1, d), lambda i: (0, 0))
    ins, specs = [x, w.reshape(1, d), dy], [row, par, row]
    if add is not None:
        ins.append(add)
        specs.append(row)
    dx, dw = pl.pallas_call(
        body, grid=(s // tr,), in_specs=specs, out_specs=(row, par),
        out_shape=(jax.ShapeDtypeStruct((s, d), F32), jax.ShapeDtypeStruct((1, d), F32)),
        compiler_params=_cp("arbitrary"), name=name)(*ins)
    return dx, dw.reshape(d)


def _loss_head(h, tgt):
    s, d = h.shape
    tr = _tile(s, 256, 8)

    def body(h_ref, t_ref, dh_ref, l_ref):
        e = h_ref[...] - t_ref[...]
        dh_ref[...] = e * (1.0 / d)

        @pl.when(pl.program_id(0) == 0)
        def _():
            l_ref[...] = jnp.zeros_like(l_ref)

        l_ref[...] += 0.5 * jnp.sum(jnp.mean(e * e, axis=-1, keepdims=True), axis=0, keepdims=True)

    row = pl.BlockSpec((tr, d), lambda i: (i, 0))
    dh, l = pl.pallas_call(
        body, grid=(s // tr,), in_specs=[row, row], out_specs=(row, pl.BlockSpec((1, 1), lambda i: (0, 0))),
        out_shape=(jax.ShapeDtypeStruct((s, d), F32), jax.ShapeDtypeStruct((1, 1), F32)),
        compiler_params=_cp("arbitrary"), name="loss_head")(h, tgt)
    return l[0, 0], dh


def _conv_rows(xe, w, width):
    y = None
    for j in range(width):
        s = width - 1 - j
        term = (pltpu.roll(xe, s, 0) if s else xe)[HALO:] * w[j:j + 1, :]
        y = term if y is None else y + term
    return y


def _conv_rows_t(dy_ext, w, width, tr):
    n = dy_ext.shape[0]
    dx = None
    for j in range(width):
        s = width - 1 - j
        term = (pltpu.roll(dy_ext, n - s, 0) if s else dy_ext)[:tr] * w[j:j + 1, :]
        dx = term if dx is None else dx + term
    return dx


def _conv_dw(dy, xe, width):
    rows = []
    for j in range(width):
        s = width - 1 - j
        rows.append(jnp.sum(dy * (pltpu.roll(xe, s, 0) if s else xe)[HALO:], axis=0, keepdims=True))
    return jnp.concatenate(rows, axis=0)


def _halo_specs(tr, tc, coff, nrow_blocks):
    per = tr // HALO
    last = nrow_blocks * per - 1
    cur = pl.BlockSpec((tr, tc), lambda j, r: (r, coff + j))
    prev = pl.BlockSpec((HALO, tc), lambda j, r: (jnp.maximum(r * per - 1, 0), coff + j))
    nxt = pl.BlockSpec((HALO, tc), lambda j, r: (jnp.minimum((r + 1) * per, last), coff + j))
    return cur, prev, nxt


def _dnconv_fwd(proj, w, cfg):
    s, width = cfg.S, 3 * cfg.HD * HEAD
    tc = _tile(math.gcd(cfg.QD, width), 512)
    tr = _tile(s, 512, 8)
    nr = s // tr
    cur, prev, _ = _halo_specs(tr, tc, cfg.QD // tc, nr)

    def body(x_ref, xp_ref, w_ref, o_ref):
        hist = jnp.where(pl.program_id(1) > 0, xp_ref[...], 0.0)
        y = _conv_rows(jnp.concatenate([hist, x_ref[...]], axis=0), w_ref[...], DN_CONV)
        o_ref[...] = y * _sigmoid(y)

    return pl.pallas_call(
        body, grid=(width // tc, nr),
        in_specs=[cur, prev, pl.BlockSpec((DN_CONV, tc), lambda j, r: (0, j))],
        out_specs=pl.BlockSpec((tr, tc), lambda j, r: (r, j)),
        out_shape=jax.ShapeDtypeStruct((s, width), F32),
        compiler_params=_cp("parallel", "parallel"), name="dnconv_fwd")(proj, proj, w)


def _dnconv_bwd(proj, du3, w, cfg):
    s, hw = cfg.S, cfg.HD * HEAD
    width = 3 * hw
    tc = _tile(math.gcd(cfg.QD, hw), 512)
    tr = _tile(s, 512, 8)
    nr = s // tr
    per, last = tr // HALO, s // HALO - 1
    cur, prev, nxt = _halo_specs(tr, tc, cfg.QD // tc, nr)
    pc = hw // tc
    du_cur = pl.BlockSpec((1, tr, tc), lambda j, r: (j // pc, r, j % pc))
    du_nxt = pl.BlockSpec((1, HALO, tc), lambda j, r: (j // pc, jnp.minimum((r + 1) * per, last), j % pc))

    def body(x_ref, xp_ref, xn_ref, du_ref, dun_ref, w_ref, dx_ref, dw_ref):
        r = pl.program_id(1)
        wv = w_ref[...]
        hist = jnp.where(r > 0, xp_ref[...], 0.0)
        xe = jnp.concatenate([hist, x_ref[...]], axis=0)
        y_ext = _conv_rows(jnp.concatenate([xe, xn_ref[...]], axis=0), wv, DN_CONV)
        du_ext = jnp.concatenate([du_ref[0], jnp.where(r < nr - 1, dun_ref[0], 0.0)], axis=0)
        dy_ext = du_ext * _dsilu(y_ext)
        dx_ref[...] = _conv_rows_t(dy_ext, wv, DN_CONV, tr).astype(dx_ref.dtype)

        @pl.when(r == 0)
        def _():
            dw_ref[...] = jnp.zeros_like(dw_ref)

        dw_ref[...] += _conv_dw(dy_ext[:tr], xe, DN_CONV)

    wspec = pl.BlockSpec((DN_CONV, tc), lambda j, r: (0, j))
    return pl.pallas_call(
        body, grid=(width // tc, nr),
        in_specs=[cur, prev, nxt, du_cur, du_nxt, wspec],
        out_specs=(pl.BlockSpec((tr, tc), lambda j, r: (r, j)), wspec),
        out_shape=(jax.ShapeDtypeStruct((s, width), BF16), jax.ShapeDtypeStruct((DN_CONV, width), F32)),
        compiler_params=_cp("parallel", "arbitrary"), name="dnconv_bwd")(proj, proj, proj, du3, du3, w)


def _glu_fwd(up, w, b, cfg):
    s, ff = cfg.S, cfg.FF
    tc = _tile(ff, 512)
    tr = _tile(s, 512, 8)
    nr, nc = s // tr, ff // tc
    g_cur, g_prev, _ = _halo_specs(tr, tc, 0, nr)
    v_cur, v_prev, _ = _halo_specs(tr, tc, nc, nr)

    def wspec(rows, off):
        return pl.BlockSpec((rows, tc), lambda j, r: (0, off + j))

    def body(g_ref, gp_ref, v_ref, vp_ref, wg_ref, wv_ref, bg_ref, bv_ref, o_ref):
        first = pl.program_id(1) > 0
        ug = _conv_rows(jnp.concatenate([jnp.where(first, gp_ref[...], 0.0), g_ref[...]], axis=0), wg_ref[...], FFN_CONV) + bg_ref[...]
        uv = _conv_rows(jnp.concatenate([jnp.where(first, vp_ref[...], 0.0), v_ref[...]], axis=0), wv_ref[...], FFN_CONV) + bv_ref[...]
        o_ref[...] = (ug * _sigmoid(ug) * uv).astype(o_ref.dtype)

    b2 = b.reshape(1, 2 * ff)
    return pl.pallas_call(
        body, grid=(nc, nr),
        in_specs=[g_cur, g_prev, v_cur, v_prev, wspec(FFN_CONV, 0), wspec(FFN_CONV, nc), wspec(1, 0), wspec(1, nc)],
        out_specs=pl.BlockSpec((tr, tc), lambda j, r: (r, j)),
        out_shape=jax.ShapeDtypeStruct((s, ff), BF16),
        compiler_params=_cp("parallel", "parallel"), name="glu_fwd")(up, up, up, up, w, w, b2, b2)


def _glu_bwd(up, dact, w, b, cfg):
    s, ff = cfg.S, cfg.FF
    tc = _tile(ff, 512)
    tr = _tile(s, 512, 8)
    nr, nc = s // tr, ff // tc
    g_cur, g_prev, g_nxt = _halo_specs(tr, tc, 0, nr)
    v_cur, v_prev, v_nxt = _halo_specs(tr, tc, nc, nr)
    d_cur, _, d_nxt = _halo_specs(tr, tc, 0, nr)

    def wspec(rows, off):
        return pl.BlockSpec((rows, tc), lambda j, r: (0, off + j))

    def body(g_ref, gp_ref, gn_ref, v_ref, vp_ref, vn_ref, d_ref, dn_ref, wg_ref, wv_ref, bg_ref, bv_ref,
             dup_ref, dw_ref, db_ref):
        r = pl.program_id(1)
        wg, wv = wg_ref[...], wv_ref[...]
        ge = jnp.concatenate([jnp.where(r > 0, gp_ref[...], 0.0), g_ref[...]], axis=0)
        ve = jnp.concatenate([jnp.where(r > 0, vp_ref[...], 0.0), v_ref[...]], axis=0)
        ug = _conv_rows(jnp.concatenate([ge, gn_ref[...]], axis=0), wg, FFN_CONV) + bg_ref[...]
        uv = _conv_rows(jnp.concatenate([ve, vn_ref[...]], axis=0), wv, FFN_CONV) + bv_ref[...]
        da = jnp.concatenate([d_ref[...].astype(F32), jnp.where(r < nr - 1, dn_ref[...].astype(F32), 0.0)], axis=0)
        dug = da * uv * _dsilu(ug)
        duv = da * ug * _sigmoid(ug)
        dup_ref[0] = _conv_rows_t(dug, wg, FFN_CONV, tr).astype(dup_ref.dtype)
        dup_ref[1] = _conv_rows_t(duv, wv, FFN_CONV, tr).astype(dup_ref.dtype)

        @pl.when(r == 0)
        def _():
            dw_ref[...] = jnp.zeros_like(dw_ref)
            db_ref[...] = jnp.zeros_like(db_ref)

        dw_ref[0] += _conv_dw(dug[:tr], ge, FFN_CONV)
        dw_ref[1] += _conv_dw(duv[:tr], ve, FFN_CONV)
        db_ref[0] += jnp.sum(dug[:tr], axis=0, keepdims=True)
        db_ref[1] += jnp.sum(duv[:tr], axis=0, keepdims=True)

    b2 = b.reshape(1, 2 * ff)
    dup, dw, db = pl.pallas_call(
        body, grid=(nc, nr),
        in_specs=[g_cur, g_prev, g_nxt, v_cur, v_prev, v_nxt, d_cur, d_nxt,
                  wspec(FFN_CONV, 0), wspec(FFN_CONV, nc), wspec(1, 0), wspec(1, nc)],
        out_specs=(pl.BlockSpec((2, tr, tc), lambda j, r: (0, r, j)),
                   pl.BlockSpec((2, FFN_CONV, tc), lambda j, r: (0, 0, j)),
                   pl.BlockSpec((2, 1, tc), lambda j, r: (0, 0, j))),
        out_shape=(jax.ShapeDtypeStruct((2, s, ff), BF16), jax.ShapeDtypeStruct((2, FFN_CONV, ff), F32),
                   jax.ShapeDtypeStruct((2, 1, ff), F32)),
        compiler_params=_cp("parallel", "arbitrary"), name="glu_bwd",
    )(up, up, up, up, up, up, dact, dact, w, w, b2, b2)
    dup = jnp.concatenate([dup[0], dup[1]], axis=1)
    dw = jnp.transpose(dw, (1, 0, 2)).reshape(FFN_CONV, 2 * ff)
    return dup, dw, db.reshape(2 * ff)


def _merge_specs(cfg):
    tc = _tile(math.gcd(math.gcd(cfg.GA, cfg.GD), cfg.D), 512)
    tr = _tile(cfg.S, 512, 8)
    ga = pl.BlockSpec((tr, tc), lambda r, j: (r, cfg.GA // tc + j))
    gd = pl.BlockSpec((tr, tc), lambda r, j: (r, cfg.GD // tc + j))
    pd = pl.BlockSpec((tr, tc), lambda r, j: (r, j))
    return (cfg.S // tr, cfg.D // tc), ga, gd, pd


def _merge_fwd(proj, pa, pdn, cfg):
    grid, ga, gd, pd = _merge_specs(cfg)

    def body(ga_ref, gd_ref, pa_ref, pd_ref, y_ref):
        y_ref[...] = (_sigmoid(ga_ref[...]) * pa_ref[...] + _sigmoid(gd_ref[...]) * pd_ref[...]).astype(y_ref.dtype)

    return pl.pallas_call(body, grid=grid, in_specs=[ga, gd, pd, pd], out_specs=pd,
                          out_shape=jax.ShapeDtypeStruct((cfg.S, cfg.D), BF16),
                          compiler_params=_cp("parallel", "parallel"), name="merge_fwd")(proj, proj, pa, pdn)


def _merge_bwd(proj, pa, pdn, dy, cfg):
    grid, ga, gd, pd = _merge_specs(cfg)

    def body(ga_ref, gd_ref, pa_ref, pd_ref, dy_ref, dpa_ref, dpd_ref, dga_ref, dgd_ref):
        dyv = dy_ref[...]
        sa, sd = _sigmoid(ga_ref[...]), _sigmoid(gd_ref[...])
        dpa_ref[...] = (dyv * sa).astype(BF16)
        dpd_ref[...] = (dyv * sd).astype(BF16)
        dga_ref[...] = (dyv * pa_ref[...] * sa * (1.0 - sa)).astype(BF16)
        dgd_ref[...] = (dyv * pd_ref[...] * sd * (1.0 - sd)).astype(BF16)

    out = jax.ShapeDtypeStruct((cfg.S, cfg.D), BF16)
    return pl.pallas_call(body, grid=grid, in_specs=[ga, gd, pd, pd, pd], out_specs=(pd,) * 4,
                          out_shape=(out,) * 4, compiler_params=_cp("parallel", "parallel"),
                          name="merge_bwd")(proj, proj, pa, pdn, dy)


def _rope(x, cos, sin):
    lane = lax.broadcasted_iota(jnp.int32, x.shape, 1)
    swapped = jnp.where(lane < ROT_HALF, pltpu.roll(x, LANES - ROT_HALF, 1), pltpu.roll(x, ROT_HALF, 1))
    return x * cos + swapped * sin


def _rope_t(dx, cos, sin):
    t = dx * sin
    lane = lax.broadcasted_iota(jnp.int32, dx.shape, 1)
    swapped = jnp.where(lane < ROT_HALF, pltpu.roll(t, LANES - ROT_HALF, 1), pltpu.roll(t, ROT_HALF, 1))
    return dx * cos + swapped


def _band_mask(i):
    r = lax.broadcasted_iota(jnp.int32, (ATTN_BLOCK, 2 * ATTN_BLOCK), 0)
    c = lax.broadcasted_iota(jnp.int32, (ATTN_BLOCK, 2 * ATTN_BLOCK), 1)
    dist = r + ATTN_BLOCK - c
    return (dist >= 0) & (dist < ATTN_BLOCK) & ((c >= ATTN_BLOCK) | (i > 0))


def _dot(a, b, ca, cb):
    return lax.dot_general(a.astype(BF16), b.astype(BF16), (((ca,), (cb,)), ((), ())), preferred_element_type=F32)


def _attn_probs(qg, k_cat, sink, valid):
    logits = jnp.where(valid, _dot(qg, k_cat, 1, 1) * (HEAD ** -0.5), -1e30)
    m = jnp.maximum(jnp.max(logits, axis=-1, keepdims=True), sink)
    p = jnp.exp(logits - m)
    es = jnp.exp(sink - m)
    inv = 1.0 / (jnp.sum(p, axis=-1, keepdims=True) + es)
    return p * inv, es * inv


def _attn_specs(cfg, nb, clamp):
    gw = cfg.G * HEAD
    qi = (lambda kv, i: (jnp.minimum(i, nb - 1), kv)) if clamp else (lambda kv, i: (i, kv))
    ci = (lambda i: jnp.minimum(i, nb - 1)) if clamp else (lambda i: i)
    pi = lambda i: jnp.maximum(ci(i) - 1, 0)
    ko, vo = cfg.KA // HEAD, cfg.VA // HEAD
    blk = (ATTN_BLOCK, HEAD)
    specs = [
        pl.BlockSpec((ATTN_BLOCK, gw), qi),
        pl.BlockSpec(blk, lambda kv, i: (ci(i), ko + kv)), pl.BlockSpec(blk, lambda kv, i: (pi(i), ko + kv)),
        pl.BlockSpec(blk, lambda kv, i: (ci(i), vo + kv)), pl.BlockSpec(blk, lambda kv, i: (pi(i), vo + kv)),
        pl.BlockSpec(blk, lambda kv, i: (ci(i), 0)), pl.BlockSpec(blk, lambda kv, i: (pi(i), 0)),
        pl.BlockSpec(blk, lambda kv, i: (ci(i), 0)), pl.BlockSpec(blk, lambda kv, i: (pi(i), 0)),
        pl.BlockSpec((1, 8, LANES), lambda kv, i: (kv, 0, 0)),
    ]
    return specs, pl.BlockSpec((ATTN_BLOCK, gw), qi)


def _sink_rows(sinks, cfg):
    sk = jnp.broadcast_to(sinks.reshape(cfg.KV, cfg.G, 1), (cfg.KV, cfg.G, LANES))
    return jnp.pad(sk, ((0, 0), (0, 8 - cfg.G), (0, 0)))


def _attn_fwd(proj, cosf, sinf, sinks, cfg):
    nb = cfg.S // ATTN_BLOCK
    specs, qspec = _attn_specs(cfg, nb, False)

    def body(q_ref, kc_ref, kp_ref, vc_ref, vp_ref, cc_ref, cp_ref, sc_ref, sp_ref, sk_ref, o_ref):
        i = pl.program_id(1)
        cc, sc = cc_ref[...], sc_ref[...]
        k_cat = jnp.concatenate([_rope(kp_ref[...], cp_ref[...], sp_ref[...]), _rope(kc_ref[...], cc, sc)], axis=0).astype(BF16)
        v_cat = jnp.concatenate([vp_ref[...], vc_ref[...]], axis=0).astype(BF16)
        valid = _band_mask(i)
        for g in range(cfg.G):
            qg = _rope(q_ref[:, g * HEAD:(g + 1) * HEAD], cc, sc)
            p, _ = _attn_probs(qg, k_cat, sk_ref[0, g:g + 1, 0:1], valid)
            o_ref[:, g * HEAD:(g + 1) * HEAD] = _dot(p, v_cat, 1, 0).astype(o_ref.dtype)

    return pl.pallas_call(
        body, grid=(cfg.KV, nb), in_specs=specs, out_specs=qspec,
        out_shape=jax.ShapeDtypeStruct((cfg.S, cfg.HA * HEAD), BF16),
        compiler_params=_cp("parallel", "parallel"), name="attn_fwd",
    )(proj, proj, proj, proj, proj, cosf, cosf, sinf, sinf, _sink_rows(sinks, cfg))


def _attn_bwd(proj, do, cosf, sinf, sinks, cfg):
    nb = cfg.S // ATTN_BLOCK
    specs, qspec = _attn_specs(cfg, nb, True)
    kv_out = pl.BlockSpec((ATTN_BLOCK, HEAD), lambda kv, i: (jnp.maximum(i - 1, 0), kv))

    def body(q_ref, kc_ref, kp_ref, vc_ref, vp_ref, cc_ref, cp_ref, sc_ref, sp_ref, sk_ref, do_ref,
             dq_ref, dk_ref, dv_ref, dsk_ref, ck_ref, cv_ref):
        i = pl.program_id(1)

        @pl.when(i == 0)
        def _():
            ck_ref[...] = jnp.zeros_like(ck_ref)
            cv_ref[...] = jnp.zeros_like(cv_ref)
            dsk_ref[...] = jnp.zeros_like(dsk_ref)

        @pl.when(i < nb)
        def _():
            cc, sc, cp, sp = cc_ref[...], sc_ref[...], cp_ref[...], sp_ref[...]
            k_cat = jnp.concatenate([_rope(kp_ref[...], cp, sp), _rope(kc_ref[...], cc, sc)], axis=0).astype(BF16)
            v_cat = jnp.concatenate([vp_ref[...], vc_ref[...]], axis=0).astype(BF16)
            valid = _band_mask(i)
            dk_cat = jnp.zeros((2 * ATTN_BLOCK, HEAD), F32)
            dv_cat = jnp.zeros((2 * ATTN_BLOCK, HEAD), F32)
            for g in range(cfg.G):
                sl = slice(g * HEAD, (g + 1) * HEAD)
                qg = _rope(q_ref[:, sl], cc, sc)
                dog = do_ref[:, sl]
                p, ps = _attn_probs(qg, k_cat, sk_ref[0, g:g + 1, 0:1], valid)
                dp = _dot(dog, v_cat, 1, 1)
                delta = jnp.sum(p * dp, axis=-1, keepdims=True)
                ds = p * (dp - delta) * (HEAD ** -0.5)
                dq_ref[:, sl] = _rope_t(_dot(ds, k_cat, 1, 0), cc, sc).astype(dq_ref.dtype)
                dk_cat = dk_cat + _dot(ds, qg, 0, 0)
                dv_cat = dv_cat + _dot(p, dog, 0, 0)
                dsk_ref[0, g:g + 1, :] += jnp.broadcast_to(-jnp.sum(ps * delta, axis=0, keepdims=True), (1, LANES))
            dk_ref[...] = (ck_ref[...] + _rope_t(dk_cat[:ATTN_BLOCK], cp, sp)).astype(dk_ref.dtype)
            dv_ref[...] = (cv_ref[...] + dv_cat[:ATTN_BLOCK]).astype(dv_ref.dtype)
            ck_ref[...] = _rope_t(dk_cat[ATTN_BLOCK:], cc, sc)
            cv_ref[...] = dv_cat[ATTN_BLOCK:]

        @pl.when(i == nb)
        def _():
            dk_ref[...] = ck_ref[...].astype(dk_ref.dtype)
            dv_ref[...] = cv_ref[...].astype(dv_ref.dtype)

    kvw = cfg.KV * HEAD
    dq, dk, dv, dsk = pl.pallas_call(
        body, grid=(cfg.KV, nb + 1), in_specs=specs + [qspec],
        out_specs=(qspec, kv_out, kv_out, pl.BlockSpec((1, 8, LANES), lambda kv, i: (kv, 0, 0))),
        out_shape=(jax.ShapeDtypeStruct((cfg.S, cfg.HA * HEAD), BF16), jax.ShapeDtypeStruct((cfg.S, kvw), BF16),
                   jax.ShapeDtypeStruct((cfg.S, kvw), BF16), jax.ShapeDtypeStruct((cfg.KV, 8, LANES), F32)),
        scratch_shapes=[pltpu.VMEM((ATTN_BLOCK, HEAD), F32), pltpu.VMEM((ATTN_BLOCK, HEAD), F32)],
        compiler_params=_cp("parallel", "arbitrary"), name="attn_bwd",
    )(proj, proj, proj, proj, proj, cosf, cosf, sinf, sinf, _sink_rows(sinks, cfg), do)
    return jnp.concatenate([dq, dk, dv], axis=1), dsk[:, :cfg.G, 0].reshape(cfg.HA)


def _dg(a, b, ca, cb, exact=False):
    if exact:
        return lax.dot_general(a, b, (((ca,), (cb,)), ((), ())), precision=lax.Precision.HIGHEST,
                               preferred_element_type=F32)
    return _dot(a, b, ca, cb)


@jax.custom_vjp
def _nn(a, b):
    return _dg(a, b, 1, 0)


@jax.custom_vjp
def _nt(a, b):
    return _dg(a, b, 1, 1)


@jax.custom_vjp
def _tn(a, b):
    return _dg(a, b, 0, 0)


_nn.defvjp(lambda a, b: (_nn(a, b), (a, b)), lambda r, g: (_nt(g, r[1]), _tn(r[0], g)))
_nt.defvjp(lambda a, b: (_nt(a, b), (a, b)), lambda r, g: (_nn(g, r[1]), _tn(g, r[0])))
_tn.defvjp(lambda a, b: (_tn(a, b), (a, b)), lambda r, g: (_nt(r[1], g), _nn(r[0], g)))


def _unit_lower_inverse(a):
    ii = lax.broadcasted_iota(jnp.int32, a.shape, 0)
    jj = lax.broadcasted_iota(jnp.int32, a.shape, 1)
    t = jnp.where(ii == jj, 1.0, 0.0) - a
    p = a
    for _ in range(int(math.log2(CHUNK)) - 1):
        p = _dg(p, p, 1, 0, exact=True)
        t = t + _dg(t, p, 1, 0, exact=True)
    return t


@jax.custom_vjp
def _tri_inv(a):
    return _unit_lower_inverse(a)


def _tri_inv_fwd(a):
    t = _unit_lower_inverse(a)
    return t, t


def _tri_inv_bwd(t, g):
    return (-_dg(t, _dg(g, t, 1, 1, exact=True), 0, 0, exact=True),)


_tri_inv.defvjp(_tri_inv_fwd, _tri_inv_bwd)


def _gdn_chunk(q, k, v, z, b_col, a_col, a_row, a_log, dt, nw, s0):
    ii = lax.broadcasted_iota(jnp.int32, (CHUNK, CHUNK), 0)
    jj = lax.broadcasted_iota(jnp.int32, (CHUNK, CHUNK), 1)
    tri, strict = ii >= jj, ii > jj
    beta = _sigmoid(b_col)
    coef = -jnp.exp(a_log)
    g_col = coef * _softplus(a_col + dt)
    g_row = coef * _softplus(a_row + dt)
    gc_col = jnp.sum(jnp.where(tri, g_row, 0.0), axis=1, keepdims=True)
    gc_row = jnp.sum(jnp.where(ii <= jj, g_col, 0.0), axis=0, keepdims=True)
    decay = jnp.where(tri, jnp.exp(jnp.where(tri, gc_col - gc_row, 0.0)), 0.0)
    qn = q * lax.rsqrt(jnp.sum(q * q, axis=-1, keepdims=True) + EPS) * (HEAD ** -0.5)
    kn = k * lax.rsqrt(jnp.sum(k * k, axis=-1, keepdims=True) + EPS)
    t_inv = _tri_inv(jnp.where(strict, beta * _nt(kn, kn) * decay, 0.0))
    eg = jnp.exp(gc_col)
    u = _nn(t_inv, v * beta)
    w = _nn(t_inv, kn * (beta * eg))
    qk = jnp.where(tri, _nt(qn, kn) * decay, 0.0)
    v_new = u - _nn(w, s0)
    o = _nn(qn * eg, s0) + _nn(qk, v_new)
    g_last = gc_col[CHUNK - 1:CHUNK, :]
    s1 = s0 * jnp.exp(g_last) + _tn(kn * jnp.exp(g_last - gc_col), v_new)
    od = o * lax.rsqrt(jnp.mean(o * o, axis=-1, keepdims=True) + EPS) * nw * (z * _sigmoid(z))
    return od, s1


def _gdn_in_specs(cfg, nc, rev):
    hd = cfg.HD
    ci = (lambda n: nc - 1 - n) if rev else (lambda n: n)
    zo = cfg.Z // HEAD
    blk = (CHUNK, HEAD)
    return [
        pl.BlockSpec(blk, lambda h, n: (ci(n), h)), pl.BlockSpec(blk, lambda h, n: (ci(n), hd + h)),
        pl.BlockSpec(blk, lambda h, n: (ci(n), 2 * hd + h)), pl.BlockSpec(blk, lambda h, n: (ci(n), zo + h)),
        pl.BlockSpec((1, CHUNK, 1), lambda h, n: (h, ci(n), 0)), pl.BlockSpec((1, CHUNK, 1), lambda h, n: (h, ci(n), 0)),
        pl.BlockSpec((1, 1, 1, CHUNK), lambda h, n: (h, ci(n), 0, 0)),
        pl.BlockSpec((1, 1, 1), lambda h, n: (h, 0, 0)), pl.BlockSpec((1, 1, 1), lambda h, n: (h, 0, 0)),
        pl.BlockSpec((1, HEAD), lambda h, n: (0, 0)),
    ]


def _gdn_layouts(pba, cfg):
    hd, s = cfg.HD, cfg.S
    bt, at = pba[:, :hd].T, pba[:, hd:2 * hd].T
    return bt.reshape(hd, s, 1), at.reshape(hd, s, 1), at.reshape(hd, s // CHUNK, 1, CHUNK)


def _gdn_args(qkv, proj, pba, a_log, dt_bias, nw, cfg):
    b_col, a_col, a_row = _gdn_layouts(pba, cfg)
    return (qkv, qkv, qkv, proj, b_col, a_col, a_row, a_log.reshape(cfg.HD, 1, 1), dt_bias.reshape(cfg.HD, 1, 1),
            nw.reshape(1, HEAD))


def _gdn_load(refs):
    q, k, v, z, b, a, ar, al, dt, nw = refs
    return (q[...], k[...], v[...], z[...], b[0], a[0], ar[0, 0], al[0], dt[0], nw[...])


def _gdn_fwd(qkv, proj, pba, a_log, dt_bias, nw, cfg):
    nc = cfg.S // CHUNK
    st_spec = pl.BlockSpec((1, 1, HEAD, HEAD), lambda h, n: (h, n, 0, 0))

    def body(*refs):
        o_ref, st_ref, s_ref = refs[10:]

        @pl.when(pl.program_id(1) == 0)
        def _():
            s_ref[...] = jnp.zeros_like(s_ref)

        s0 = s_ref[...]
        st_ref[0, 0] = s0
        od, s1 = _gdn_chunk(*_gdn_load(refs[:10]), s0)
        o_ref[...] = od.astype(o_ref.dtype)
        s_ref[...] = s1

    return pl.pallas_call(
        body, grid=(cfg.HD, nc), in_specs=_gdn_in_specs(cfg, nc, False),
        out_specs=(pl.BlockSpec((CHUNK, HEAD), lambda h, n: (n, h)), st_spec),
        out_shape=(jax.ShapeDtypeStruct((cfg.S, cfg.HD * HEAD), BF16),
                   jax.ShapeDtypeStruct((cfg.HD, nc, HEAD, HEAD), F32)),
        scratch_shapes=[pltpu.VMEM((HEAD, HEAD), F32)],
        compiler_params=_cp("parallel", "arbitrary"), name="gdn_fwd",
    )(*_gdn_args(qkv, proj, pba, a_log, dt_bias, nw, cfg))


def _gdn_bwd(qkv, proj, pba, a_log, dt_bias, nw, states, dod, cfg):
    nc, hd, s = cfg.S // CHUNK, cfg.HD, cfg.S
    rv = lambda n: nc - 1 - n
    col = pl.BlockSpec((1, CHUNK, 1), lambda h, n: (h, rv(n), 0))
    head1 = pl.BlockSpec((1, 1, 1), lambda h, n: (h, 0, 0))

    def body(*refs):
        st_ref, dod_ref = refs[10:12]
        dqkv_ref, dz_ref, db_ref, da_ref, dar_ref, dal_ref, ddt_ref, dnw_ref, ds_ref = refs[12:]
        h, n = pl.program_id(0), pl.program_id(1)

        @pl.when(n == 0)
        def _():
            ds_ref[...] = jnp.zeros_like(ds_ref)
            dal_ref[...] = jnp.zeros_like(dal_ref)
            ddt_ref[...] = jnp.zeros_like(ddt_ref)

        @pl.when((n == 0) & (h == 0))
        def _():
            dnw_ref[...] = jnp.zeros_like(dnw_ref)

        _, vjp = jax.vjp(_gdn_chunk, *_gdn_load(refs[:10]), st_ref[0, 0])
        dq, dk, dv, dz, db, da, dar, dal, ddt, dnw, ds0 = vjp((dod_ref[...].astype(F32), ds_ref[...]))
        dqkv_ref[0], dqkv_ref[1], dqkv_ref[2] = dq, dk, dv
        dz_ref[...] = dz.astype(dz_ref.dtype)
        db_ref[0], da_ref[0], dar_ref[0, 0] = db, da, dar
        dal_ref[0] += dal
        ddt_ref[0] += ddt
        dnw_ref[...] += dnw
        ds_ref[...] = ds0

    blk = pl.BlockSpec((CHUNK, HEAD), lambda h, n: (rv(n), h))
    outs = pl.pallas_call(
        body, grid=(hd, nc),
        in_specs=_gdn_in_specs(cfg, nc, True) + [pl.BlockSpec((1, 1, HEAD, HEAD), lambda h, n: (h, rv(n), 0, 0)), blk],
        out_specs=(pl.BlockSpec((3, CHUNK, HEAD), lambda h, n: (0, rv(n), h)), blk, col, col,
                   pl.BlockSpec((1, 1, 1, CHUNK), lambda h, n: (h, rv(n), 0, 0)), head1, head1,
                   pl.BlockSpec((1, HEAD), lambda h, n: (0, 0))),
        out_shape=(jax.ShapeDtypeStruct((3, s, hd * HEAD), F32), jax.ShapeDtypeStruct((s, hd * HEAD), BF16),
                   jax.ShapeDtypeStruct((hd, s, 1), F32), jax.ShapeDtypeStruct((hd, s, 1), F32),
                   jax.ShapeDtypeStruct((hd, nc, 1, CHUNK), F32), jax.ShapeDtypeStruct((hd, 1, 1), F32),
                   jax.ShapeDtypeStruct((hd, 1, 1), F32), jax.ShapeDtypeStruct((1, HEAD), F32)),
        scratch_shapes=[pltpu.VMEM((HEAD, HEAD), F32)],
        compiler_params=_cp("arbitrary", "arbitrary"), name="gdn_bwd",
    )(*_gdn_args(qkv, proj, pba, a_log, dt_bias, nw, cfg), states, dod)
    dqkv, dz, db, da, dar, dal, ddt, dnw = outs
    d_b = db[:, :, 0].T
    d_a = da[:, :, 0].T + dar.reshape(hd, s).T
    dba = jnp.concatenate([d_b, d_a, jnp.zeros((s, LANES - 2 * hd), F32)], axis=1).astype(BF16)
    return dqkv, dz, dba, dal.reshape(hd), ddt.reshape(hd), dnw.reshape(HEAD)


def _adamw(w, g, m, v, *, name):
    r, c = w.shape
    tr = _tile(r, max(8, (1 << 18) // c // 8 * 8), 8) if r % 8 == 0 else r
    c1, c2 = 1.0 - ADAM_B1 ** ADAM_STEP, 1.0 - ADAM_B2 ** ADAM_STEP

    def body(w_ref, g_ref, m_ref, v_ref, d_ref, nm_ref, nv_ref):
        gv = g_ref[...]
        nm = ADAM_B1 * m_ref[...] + (1.0 - ADAM_B1) * gv
        nv = ADAM_B2 * v_ref[...] + (1.0 - ADAM_B2) * (gv * gv)
        d_ref[...] = -ADAM_LR * ((nm / c1) / (jnp.sqrt(nv / c2) + ADAM_EPS) + ADAM_WD * w_ref[...])
        nm_ref[...] = nm
        nv_ref[...] = nv

    spec = pl.BlockSpec((tr, c), lambda i: (i, 0))
    out = jax.ShapeDtypeStruct((r, c), F32)
    return pl.pallas_call(body, grid=(r // tr,), in_specs=[spec] * 4, out_specs=(spec,) * 3, out_shape=(out,) * 3,
                          compiler_params=_cp("parallel"), name=name)(w, g, m, v)


def _sum_leading(a, *, name):
    n, r, c = a.shape
    tr = _tile(r, PACK_TR, 8)

    def body(a_ref, o_ref):
        acc = a_ref[0].astype(F32)
        for i in range(1, n):
            acc = acc + a_ref[i].astype(F32)
        o_ref[...] = acc

    return pl.pallas_call(body, grid=(r // tr,), in_specs=[pl.BlockSpec((n, tr, c), lambda i: (0, i, 0))],
                          out_specs=pl.BlockSpec((tr, c), lambda i: (i, 0)),
                          out_shape=jax.ShapeDtypeStruct((r, c), F32), compiler_params=_cp("parallel"), name=name)(a)


def _pair_sum(gp, b1, cidx):
    n, r, c = gp.shape
    half = r // 2
    tr = _tile(half, PACK_TR, 16)
    nh = half // tr

    def body(c_ref, a_ref, b_ref, o_ref):
        o_ref[...] = (a_ref[...].astype(F32) + b_ref[...].astype(F32)).astype(o_ref.dtype)

    return pl.pallas_call(
        body,
        grid_spec=pltpu.PrefetchScalarGridSpec(
            num_scalar_prefetch=1, grid=(n, nh),
            in_specs=[pl.BlockSpec((1, tr, c), lambda j, i, c_ref: (j, c_ref[0] * nh + i, 0)),
                      pl.BlockSpec((1, tr, c), lambda j, i, c_ref: (j, i, 0))],
            out_specs=pl.BlockSpec((1, tr, c), lambda j, i, c_ref: (j, i, 0))),
        out_shape=jax.ShapeDtypeStruct((n, half, c), BF16),
        compiler_params=_cp("parallel", "parallel"), name="rs_pair_sum")(cidx, gp, b1)


def _place():
    x, y, c = lax.axis_index("x"), lax.axis_index("y"), lax.axis_index("c")
    chips = [(1 - x, y), (x, 1 - y), (1 - x, 1 - y)]
    return x, y, c, chips


def _remote(src, dst, send_sems, recv_sems, k, to):
    return pltpu.make_async_remote_copy(src_ref=src, dst_ref=dst, send_sem=send_sems.at[k], recv_sem=recv_sems.at[k],
                                        device_id=to, device_id_type=MESH)


def _comm_call(body, out_shape, n_sems, name, n_local=1):
    return pl.pallas_call(
        body, out_shape=out_shape, in_specs=[ANY], out_specs=ANY,
        scratch_shapes=[pltpu.SemaphoreType.DMA((n_sems,)), pltpu.SemaphoreType.DMA((n_sems,)),
                        pltpu.SemaphoreType.DMA((n_local,))],
        compiler_params=pltpu.CompilerParams(has_side_effects=True), name=name)


def _gather_chips(p, *, name):
    r, w = p.shape
    half = r // 2

    def body(p_ref, g_ref, send_sems, recv_sems, local_sem):
        x, y, c, chips = _place()
        me = 2 * x + y
        mine = pl.ds(pl.multiple_of(c * half, 16), half)
        theirs = pl.ds(pl.multiple_of((1 - c) * half, 16), half)
        own = pltpu.make_async_copy(p_ref, g_ref.at[me], local_sem.at[0])
        own.start()
        first = [_remote(p_ref.at[mine], g_ref.at[me, mine], send_sems, recv_sems, j, (cx, cy, c))
                 for j, (cx, cy) in enumerate(chips)]
        for cp in first:
            cp.start()
        passed = []
        for j, (cx, cy) in enumerate(chips):
            rows = g_ref.at[2 * cx + cy, mine]
            _remote(rows, rows, send_sems, recv_sems, j, (cx, cy, c)).wait_recv()
            cp = _remote(rows, rows, send_sems, recv_sems, 3 + j, (x, y, 1 - c))
            cp.start()
            passed.append(cp)
        for j, (cx, cy) in enumerate(chips):
            rows = g_ref.at[2 * cx + cy, theirs]
            _remote(rows, rows, send_sems, recv_sems, 3 + j, (x, y, 1 - c)).wait_recv()
        for cp in first + passed:
            cp.wait_send()
        own.wait()

    return _comm_call(body, jax.ShapeDtypeStruct((4, r, w), p.dtype), 6, name)(p)


def _swap_halves(gp):
    n, r, w = gp.shape
    half = r // 2

    def body(g_ref, b_ref, send_sems, recv_sems, local_sem):
        x, y, c, _ = _place()
        theirs = pl.ds(pl.multiple_of((1 - c) * half, 16), half)
        cps = [_remote(g_ref.at[j, theirs], b_ref.at[j], send_sems, recv_sems, j, (x, y, 1 - c)) for j in range(n)]
        for cp in cps:
            cp.start()
        for cp in cps:
            cp.wait()

    return _comm_call(body, jax.ShapeDtypeStruct((n, half, w), gp.dtype), n, "rs_swap_halves")(gp)


def _scatter_chips(sp):
    n, h, w = sp.shape

    def body(s_ref, b_ref, send_sems, recv_sems, local_sem):
        x, y, c, chips = _place()
        me = 2 * x + y
        own = pltpu.make_async_copy(s_ref.at[me], b_ref.at[me], local_sem.at[0])
        own.start()
        cps = [_remote(s_ref.at[2 * cx + cy], b_ref.at[me], send_sems, recv_sems, j, (cx, cy, c))
               for j, (cx, cy) in enumerate(chips)]
        for cp in cps:
            cp.start()
        for j, (cx, cy) in enumerate(chips):
            slot = b_ref.at[2 * cx + cy]
            _remote(slot, slot, send_sems, recv_sems, j, (cx, cy, c)).wait_recv()
        for cp in cps:
            cp.wait_send()
        own.wait()

    return _comm_call(body, jax.ShapeDtypeStruct((n, h, w), sp.dtype), 3, "rs_scatter_chips")(sp)


def _share_halves(t):
    h, w = t.shape

    def body(t_ref, f_ref, send_sems, recv_sems, local_sem):
        x, y, c, _ = _place()
        mine = pl.ds(pl.multiple_of(c * h, 16), h)
        theirs = pl.ds(pl.multiple_of((1 - c) * h, 16), h)
        own = pltpu.make_async_copy(t_ref, f_ref.at[mine], local_sem.at[0])
        own.start()
        cp = _remote(t_ref, f_ref.at[mine], send_sems, recv_sems, 0, (x, y, 1 - c))
        cp.start()
        _remote(t_ref, f_ref.at[theirs], send_sems, recv_sems, 0, (x, y, 1 - c)).wait_recv()
        cp.wait_send()
        own.wait()

    return _comm_call(body, jax.ShapeDtypeStruct((2 * h, w), t.dtype), 1, "rs_share_halves")(t)


def _gather_all(p):
    r, w = p.shape
    rel = [(dx, dy, dc) for dx in (0, 1) for dy in (0, 1) for dc in (0, 1)][1:]

    def body(p_ref, g_ref, send_sems, recv_sems):
        x, y, c = lax.axis_index("x"), lax.axis_index("y"), lax.axis_index("c")
        me = 4 * x + 2 * y + c
        g_ref[me] = p_ref[...]
        peers = [(jnp.bitwise_xor(x, dx), jnp.bitwise_xor(y, dy), jnp.bitwise_xor(c, dc)) for dx, dy, dc in rel]
        cps = [_remote(p_ref, g_ref.at[me], send_sems, recv_sems, k, peer) for k, peer in enumerate(peers)]
        for cp in cps:
            cp.start()
        for k, (px, py, pc) in enumerate(peers):
            slot = g_ref.at[4 * px + 2 * py + pc]
            _remote(slot, slot, send_sems, recv_sems, k, (px, py, pc)).wait_recv()
        for cp in cps:
            cp.wait_send()

    return pl.pallas_call(
        body, out_shape=jax.ShapeDtypeStruct((8, r, w), p.dtype),
        in_specs=[pl.BlockSpec(memory_space=pltpu.VMEM)], out_specs=pl.BlockSpec(memory_space=pltpu.VMEM),
        scratch_shapes=[pltpu.SemaphoreType.DMA((7,)), pltpu.SemaphoreType.DMA((7,))],
        compiler_params=pltpu.CompilerParams(has_side_effects=True, vmem_limit_bytes=VMEM_LIMIT),
        name="gather_all")(p)


def _pack_rows(n):
    unit = 2 * PACK_TR
    return -(-n // (PACK_W * unit)) * unit


def _pack(parts):
    flat = jnp.concatenate([p.reshape(-1) for p in parts])
    rows = _pack_rows(flat.shape[0])
    return jnp.pad(flat, (0, rows * PACK_W - flat.shape[0])).reshape(rows, PACK_W)


def _unpack(buf, shapes):
    flat = buf.reshape(-1)
    out, off = [], 0
    for sh in shapes:
        n = math.prod(sh)
        out.append(flat[off:off + n].reshape(sh))
        off += n
    return out


def _f32_as_bf16_pairs(a):
    return lax.bitcast_convert_type(a, BF16)


def _bf16_pairs_as_f32(a):
    return lax.bitcast_convert_type(a, F32)


class _Cfg:
    def __init__(self, x, w_in, attn_sinks, dn_a_log, w_ffn_up):
        self.S, self.D = x.shape[1], x.shape[2]
        self.L = w_in.shape[0]
        self.HA, self.HD = attn_sinks.shape[1], dn_a_log.shape[1]
        self.IN = 4 * w_in.shape[2]
        kvw = self.IN - self.HA * HEAD - 4 * self.HD * HEAD - 2 * self.HD - 2 * self.D
        self.KV = kvw // (2 * HEAD)
        self.G = self.HA // self.KV
        self.FF = 2 * w_ffn_up.shape[2]
        self.KA = self.HA * HEAD
        self.VA = self.KA + self.KV * HEAD
        self.QD = self.VA + self.KV * HEAD
        self.Z = self.QD + 3 * self.HD * HEAD
        self.GA = self.Z + self.HD * HEAD
        self.GD = self.GA + self.D
        self.NB = self.GD + self.D
        assert self.NB + 2 * self.HD == self.IN and self.G <= 8 and 2 * self.HD <= LANES


def kernel(x, positions, norm_mix_pre, w_in, attn_sinks, dn_conv_w, dn_a_log, dn_dt_bias, dn_norm_w, w_branch_attn, w_branch_dn, w_out, norm_mix_post, norm_ffn_pre, w_ffn_up, ffn_conv_w, ffn_conv_b, w_ffn_down, norm_ffn_post, loss_target, m_norm_mix_pre, m_w_in, m_attn_sinks, m_dn_conv_w, m_dn_a_log, m_dn_dt_bias, m_dn_norm_w, m_w_branch_attn, m_w_branch_dn, m_w_out, m_norm_mix_post, m_norm_ffn_pre, m_w_ffn_up, m_ffn_conv_w, m_ffn_conv_b, m_w_ffn_down, m_norm_ffn_post, v_norm_mix_pre, v_w_in, v_attn_sinks, v_dn_conv_w, v_dn_a_log, v_dn_dt_bias, v_dn_norm_w, v_w_branch_attn, v_w_branch_dn, v_w_out, v_norm_mix_post, v_norm_ffn_pre, v_w_ffn_up, v_ffn_conv_w, v_ffn_conv_b, v_w_ffn_down, v_norm_ffn_post):
    cfg = _Cfg(x, w_in, attn_sinks, dn_a_log, w_ffn_up)
    S, D, L, HD = cfg.S, cfg.D, cfg.L, cfg.HD
    big = [w_in, w_branch_attn, w_branch_dn, w_out, w_ffn_up, w_ffn_down]
    big_shapes = [a.shape[1:] for a in big]
    conv_shapes = [dn_conv_w.shape[1:], ffn_conv_w.shape[1:]]
    o1 = cfg.Z

    def full_weights(l):
        parts = [a[l].astype(BF16) for a in big] + [_f32_as_bf16_pairs(dn_conv_w[l]), _f32_as_bf16_pairs(ffn_conv_w[l])]
        got = _gather_chips(_pack(parts), name="gather_weights")
        per_chip = [_unpack(got[j], big_shapes + [sh + (2,) for sh in conv_shapes]) for j in range(4)]
        cat = lambda i, axis: jnp.concatenate([per_chip[j][i] for j in range(4)], axis=axis)
        win = cat(0, 1)
        return dict(
            wbig=jnp.concatenate([win[:, :o1], win[:, o1 + 2 * HD:]], axis=1),
            wba=jnp.pad(win[:, o1:o1 + 2 * HD], ((0, 0), (0, LANES - 2 * HD))),
            wbra=cat(1, 0), wbrd=cat(2, 0), wout=cat(3, 0), wup=cat(4, 1), wdown=cat(5, 0),
            dnw=_bf16_pairs_as_f32(cat(6, 1)), fcw=_bf16_pairs_as_f32(cat(7, 1)))

    weights = [full_weights(l) for l in range(L)]

    inv_freq = ROPE_THETA ** (-jnp.arange(0, 2 * ROT_HALF, 2, dtype=F32) / (2 * ROT_HALF))
    ang = positions.reshape(S, 1).astype(F32) * inv_freq
    cosf = jnp.concatenate([jnp.cos(ang), jnp.cos(ang), jnp.ones((S, HEAD - 2 * ROT_HALF), F32)], axis=1)
    sinf = jnp.concatenate([-jnp.sin(ang), jnp.sin(ang), jnp.zeros((S, HEAD - 2 * ROT_HALF), F32)], axis=1)

    h = x.reshape(S, D)
    saved = []
    for l in range(L):
        wt = weights[l]
        t = dict(h0=h)
        t["xn"] = _rms_fwd(h, norm_mix_pre[l], out_dtype=BF16, name="norm_mix_pre")
        t["proj"] = _mm(t["xn"], wt["wbig"], name="proj")
        t["pba"] = _mm(t["xn"], wt["wba"], name="proj_ba")
        t["oa"] = _attn_fwd(t["proj"], cosf, sinf, attn_sinks[l], cfg)
        t["qkv"] = _dnconv_fwd(t["proj"], wt["dnw"], cfg)
        t["od"], t["st"] = _gdn_fwd(t["qkv"], t["proj"], t["pba"], dn_a_log[l], dn_dt_bias[l], dn_norm_w[l], cfg)
        t["pa"] = _mm(t["oa"], wt["wbra"], name="branch_attn")
        t["pd"] = _mm(t["od"], wt["wbrd"], name="branch_dn")
        t["y"] = _merge_fwd(t["proj"], t["pa"], t["pd"], cfg)
        t["mix"] = _mm(t["y"], wt["wout"], name="mix_out")
        t["h1"] = _rms_fwd(t["mix"], norm_mix_post[l], res=h, out_dtype=F32, name="norm_mix_post")
        t["xf"] = _rms_fwd(t["h1"], norm_ffn_pre[l], out_dtype=BF16, name="norm_ffn_pre")
        t["up"] = _mm(t["xf"], wt["wup"], name="ffn_up")
        t["act"] = _glu_fwd(t["up"], wt["fcw"], ffn_conv_b[l], cfg)
        t["f"] = _mm(t["act"], wt["wdown"], name="ffn_down")
        h = _rms_fwd(t["f"], norm_ffn_post[l], res=t["h1"], out_dtype=F32, name="norm_ffn_post")
        saved.append(t)

    loss_local, dh = _loss_head(h, loss_target.reshape(S, D))
    loss = lax.psum(loss_local, ("x", "y", "c"))

    cidx = lax.axis_index("c").astype(jnp.int32).reshape(1)
    big_grads, small = [None] * L, [None] * L
    for l in reversed(range(L)):
        wt, t = weights[l], saved[l]
        df, g_nfpost = _rms_bwd(t["f"], norm_ffn_post[l], dh, name="norm_ffn_post_bwd")
        dact = _mm(df, wt["wdown"], tb=True, name="ffn_down_dx")
        g_wdown = _mm(t["act"], df, ta=True, name="ffn_down_dw")
        dup, g_fcw, g_fcb = _glu_bwd(t["up"], dact, wt["fcw"], ffn_conv_b[l], cfg)
        dxf = _mm(dup, wt["wup"], tb=True, name="ffn_up_dx")
        g_wup = _mm(t["xf"], dup, ta=True, name="ffn_up_dw")
        dh1, g_nfp = _rms_bwd(t["h1"], norm_ffn_pre[l], dxf, add=dh, name="norm_ffn_pre_bwd")
        dmix, g_nmpost = _rms_bwd(t["mix"], norm_mix_post[l], dh1, name="norm_mix_post_bwd")
        dy = _mm(dmix, wt["wout"], tb=True, name="mix_out_dx")
        g_wout = _mm(t["y"], dmix, ta=True, name="mix_out_dw")
        dpa, dpd, dga, dgd = _merge_bwd(t["proj"], t["pa"], t["pd"], dy, cfg)
        doa = _mm(dpa, wt["wbra"], tb=True, out_dtype=BF16, name="branch_attn_dx")
        g_wbra = _mm(t["oa"], dpa, ta=True, name="branch_attn_dw")
        dod = _mm(dpd, wt["wbrd"], tb=True, out_dtype=BF16, name="branch_dn_dx")
        g_wbrd = _mm(t["od"], dpd, ta=True, name="branch_dn_dw")
        dqkv_c, dz, dba, g_alog, g_dt, g_dnnorm = _gdn_bwd(t["qkv"], t["proj"], t["pba"], dn_a_log[l], dn_dt_bias[l],
                                                            dn_norm_w[l], t["st"], dod, cfg)
        dqkv_d, g_dnw = _dnconv_bwd(t["proj"], dqkv_c, wt["dnw"], cfg)
        dqkv_a, g_sinks = _attn_bwd(t["proj"], doa, cosf, sinf, attn_sinks[l], cfg)
        dproj = jnp.concatenate([dqkv_a, dqkv_d, dz, dga, dgd], axis=1)
        dxn = _mm(dba, wt["wba"], tb=True, name="proj_ba_dx")
        dxn = _mm(dproj, wt["wbig"], tb=True, add=dxn, name="proj_dx")
        g_wbig = _mm(t["xn"], dproj, ta=True, name="proj_dw")
        g_wba = _mm(t["xn"], dba, ta=True, name="proj_ba_dw")
        dh, g_nmp = _rms_bwd(t["h0"], norm_mix_pre[l], dxn, add=dh1, name="norm_mix_pre_bwd")

        g_win = jnp.concatenate([g_wbig[:, :o1], g_wba[:, :2 * HD], g_wbig[:, o1:]], axis=1)
        full = [g_win, g_wbra, g_wbrd, g_wout, g_wup, g_wdown]
        split_axis = [1, 0, 0, 0, 1, 0]
        by_chip = jnp.stack([_pack([jnp.split(a, 4, axis=ax)[j].astype(BF16) for a, ax in zip(full, split_axis)])
                             for j in range(4)])
        pair = _pair_sum(by_chip, _swap_halves(by_chip), cidx)
        mine = _sum_leading(_scatter_chips(pair), name="rs_sum_chips")
        big_grads[l] = _unpack(_share_halves(mine), big_shapes)
        small[l] = [g_nmp, g_sinks, g_alog, g_dt, g_dnnorm, g_nmpost, g_nfp, g_fcb, g_nfpost, g_dnw, g_fcw]

    grad_x = dh.reshape(1, S, D)

    small_shapes = [a.shape for a in small[0]]
    flat = jnp.concatenate([a.reshape(-1) for l in range(L) for a in small[l]])
    rows = -(-flat.shape[0] // (SMALL_W * 8)) * 8
    packed = jnp.pad(flat, (0, rows * SMALL_W - flat.shape[0])).reshape(rows, SMALL_W)
    summed = _sum_leading(_gather_all(packed), name="sum_devices")
    per_layer = _unpack(summed, small_shapes * L)
    red = [jnp.stack([per_layer[l * len(small_shapes) + i] for l in range(L)]) for i in range(len(small_shapes))]
    chip = 2 * lax.axis_index("x") + lax.axis_index("y")
    g_dn_conv = lax.dynamic_slice_in_dim(red[9], chip * dn_conv_w.shape[2], dn_conv_w.shape[2], axis=2)
    g_ffn_conv = lax.dynamic_slice_in_dim(red[10], chip * ffn_conv_w.shape[2], ffn_conv_w.shape[2], axis=2)

    grads = dict(
        norm_mix_pre=red[0], w_in=jnp.stack([big_grads[l][0] for l in range(L)]), attn_sinks=red[1],
        dn_conv_w=g_dn_conv, dn_a_log=red[2], dn_dt_bias=red[3], dn_norm_w=red[4],
        w_branch_attn=jnp.stack([big_grads[l][1] for l in range(L)]),
        w_branch_dn=jnp.stack([big_grads[l][2] for l in range(L)]),
        w_out=jnp.stack([big_grads[l][3] for l in range(L)]), norm_mix_post=red[5], norm_ffn_pre=red[6],
        w_ffn_up=jnp.stack([big_grads[l][4] for l in range(L)]), ffn_conv_w=g_ffn_conv, ffn_conv_b=red[7],
        w_ffn_down=jnp.stack([big_grads[l][5] for l in range(L)]), norm_ffn_post=red[8])

    params = dict(
        norm_mix_pre=(norm_mix_pre, m_norm_mix_pre, v_norm_mix_pre), w_in=(w_in, m_w_in, v_w_in),
        attn_sinks=(attn_sinks, m_attn_sinks, v_attn_sinks), dn_conv_w=(dn_conv_w, m_dn_conv_w, v_dn_conv_w),
        dn_a_log=(dn_a_log, m_dn_a_log, v_dn_a_log), dn_dt_bias=(dn_dt_bias, m_dn_dt_bias, v_dn_dt_bias),
        dn_norm_w=(dn_norm_w, m_dn_norm_w, v_dn_norm_w),
        w_branch_attn=(w_branch_attn, m_w_branch_attn, v_w_branch_attn),
        w_branch_dn=(w_branch_dn, m_w_branch_dn, v_w_branch_dn), w_out=(w_out, m_w_out, v_w_out),
        norm_mix_post=(norm_mix_post, m_norm_mix_post, v_norm_mix_post),
        norm_ffn_pre=(norm_ffn_pre, m_norm_ffn_pre, v_norm_ffn_pre), w_ffn_up=(w_ffn_up, m_w_ffn_up, v_w_ffn_up),
        ffn_conv_w=(ffn_conv_w, m_ffn_conv_w, v_ffn_conv_w), ffn_conv_b=(ffn_conv_b, m_ffn_conv_b, v_ffn_conv_b),
        w_ffn_down=(w_ffn_down, m_w_ffn_down, v_w_ffn_down),
        norm_ffn_post=(norm_ffn_post, m_norm_ffn_post, v_norm_ffn_post))
    names = list(params)

    delta, new_m, new_v = {}, {}, {}
    small_names = [n for n in names if params[n][0].ndim == 2]
    for n in names:
        if n in small_names:
            continue
        w, m, v = params[n]
        two_d = (w.shape[0] * w.shape[1], w.shape[2])
        outs = _adamw(w.reshape(two_d), grads[n].reshape(two_d), m.reshape(two_d), v.reshape(two_d), name="adamw_" + n)
        delta[n], new_m[n], new_v[n] = (o.reshape(w.shape) for o in outs)

    def pack_small(arrs):
        fl = jnp.concatenate([a.reshape(-1) for a in arrs])
        r = -(-fl.shape[0] // (SMALL_W * 8)) * 8
        return jnp.pad(fl, (0, r * SMALL_W - fl.shape[0])).reshape(r, SMALL_W)

    outs = _adamw(pack_small([params[n][0] for n in small_names]), pack_small([grads[n] for n in small_names]),
                  pack_small([params[n][1] for n in small_names]), pack_small([params[n][2] for n in small_names]),
                  name="adamw_small")
    shapes = [params[n][0].shape for n in small_names]
    for dst, o in zip((delta, new_m, new_v), outs):
        for n, a in zip(small_names, _unpack(o, shapes)):
            dst[n] = a

    return (loss, grad_x, *[grads[n] for n in names], *[delta[n] for n in names],
            *[new_m[n] for n in names], *[new_v[n] for n in names])
```

```python
import functools
import math

import jax
import jax.numpy as jnp
from jax import lax
from jax.experimental import pallas as pl
from jax.experimental.pallas import tpu as pltpu

F32, BF16 = jnp.float32, jnp.bfloat16
EPS = 1e-6
HEAD = 128
ATTN_BLOCK = 128
ROT_HALF = 16
ROPE_THETA = 500000.0
CHUNK = 64
DN_CONV, FFN_CONV = 4, 3
GDN_HEADS_PER_STEP = 4
MM_TILES = (512, 1024, 3072)
HALO = 8
LANES = 128
PACK_W = 512
PACK_TR = 512
SMALL_W = 1024
ADAM_LR, ADAM_B1, ADAM_B2, ADAM_EPS, ADAM_WD, ADAM_STEP = 0.001, 0.9, 0.999, 1e-08, 0.01, 10
VMEM_LIMIT = 56 * 1024 * 1024
MESH = pl.DeviceIdType.MESH
ANY = pl.BlockSpec(memory_space=pl.ANY)


def _cp(*sem):
    return pltpu.CompilerParams(dimension_semantics=sem, vmem_limit_bytes=VMEM_LIMIT)


def _tile(n, pref, unit=LANES):
    if n <= pref:
        return n
    t = (pref // unit) * unit
    while t >= unit:
        if n % t == 0:
            return t
        t -= unit
    raise ValueError((n, pref, unit))


def _sigmoid(x):
    return 1.0 / (1.0 + jnp.exp(-x))


def _softplus(x):
    return jnp.maximum(x, 0.0) + jnp.log(1.0 + jnp.exp(-jnp.abs(x)))


def _dsilu(y):
    s = _sigmoid(y)
    return s * (1.0 + y * (1.0 - s))


def _mm(a, b, *, ta=False, tb=False, add=None, out_dtype=F32, name, tiles=MM_TILES):
    if ta:
        kdim, m = a.shape
    else:
        m, kdim = a.shape
    if tb:
        n, k2 = b.shape
    else:
        k2, n = b.shape
    assert kdim == k2, (a.shape, b.shape, ta, tb)
    tm, tn, tk = _tile(m, tiles[0]), _tile(n, tiles[1]), _tile(kdim, tiles[2])
    nk = kdim // tk
    dims = (((0 if ta else 1,), (1 if tb else 0,)), ((), ()))

    def body(*refs):
        if add is None:
            a_ref, b_ref, o_ref, acc_ref = refs
        else:
            a_ref, b_ref, add_ref, o_ref, acc_ref = refs
        k = pl.program_id(2)

        @pl.when(k == 0)
        def _():
            acc_ref[...] = jnp.zeros_like(acc_ref)

        acc_ref[...] += lax.dot_general(a_ref[...].astype(BF16), b_ref[...].astype(BF16), dims,
                                        preferred_element_type=F32)

        @pl.when(k == nk - 1)
        def _():
            r = acc_ref[...]
            if add is not None:
                r = r + add_ref[...]
            o_ref[...] = r.astype(o_ref.dtype)

    a_spec = pl.BlockSpec((tk, tm), lambda i, j, k: (k, i)) if ta else pl.BlockSpec((tm, tk), lambda i, j, k: (i, k))
    b_spec = pl.BlockSpec((tn, tk), lambda i, j, k: (j, k)) if tb else pl.BlockSpec((tk, tn), lambda i, j, k: (k, j))
    o_spec = pl.BlockSpec((tm, tn), lambda i, j, k: (i, j))
    ins, specs = [a, b], [a_spec, b_spec]
    if add is not None:
        ins.append(add)
        specs.append(o_spec)
    return pl.pallas_call(
        body, grid=(m // tm, n // tn, nk), in_specs=specs, out_specs=o_spec,
        out_shape=jax.ShapeDtypeStruct((m, n), out_dtype),
        scratch_shapes=[pltpu.VMEM((tm, tn), F32)],
        compiler_params=_cp("parallel", "parallel", "arbitrary"), name=name,
    )(*ins)


def _rms_fwd(x, w, res=None, *, out_dtype, name):
    s, d = x.shape
    tr = _tile(s, 256, 8)

    def body(*refs):
        if res is None:
            x_ref, w_ref, o_ref = refs
        else:
            x_ref, w_ref, r_ref, o_ref = refs
        xv = x_ref[...]
        y = xv * lax.rsqrt(jnp.mean(xv * xv, axis=-1, keepdims=True) + EPS) * w_ref[...]
        if res is not None:
            y = y + r_ref[...]
        o_ref[...] = y.astype(o_ref.dtype)

    row = pl.BlockSpec((tr, d), lambda i: (i, 0))
    par = pl.BlockSpec((1, d), lambda i: (0, 0))
    ins, specs = [x, w.reshape(1, d)], [row, par]
    if res is not None:
        ins.append(res)
        specs.append(row)
    return pl.pallas_call(body, grid=(s // tr,), in_specs=specs, out_specs=row,
                          out_shape=jax.ShapeDtypeStruct((s, d), out_dtype),
                          compiler_params=_cp("parallel"), name=name)(*ins)


def _rms_bwd(x, w, dy, add=None, *, out_dtype=F32, name):
    s, d = x.shape
    tr = _tile(s, 256, 8)

    def body(*refs):
        if add is None:
            x_ref, w_ref, dy_ref, dx_ref, dw_ref = refs
        else:
            x_ref, w_ref, dy_ref, add_ref, dx_ref, dw_ref = refs
        xv = x_ref[...]
        dyv = dy_ref[...].astype(F32)
        r = lax.rsqrt(jnp.mean(xv * xv, axis=-1, keepdims=True) + EPS)
        xh = xv * r
        dyw = dyv * w_ref[...]
        dx = r * (dyw - xh * jnp.mean(dyw * xh, axis=-1, keepdims=True))
        if add is not None:
            dx = dx + add_ref[...]
        dx_ref[...] = dx.astype(dx_ref.dtype)

        @pl.when(pl.program_id(0) == 0)
        def _():
            dw_ref[...] = jnp.zeros_like(dw_ref)

        dw_ref[...] += jnp.sum(dyv * xh, axis=0, keepdims=True)

    row = pl.BlockSpec((tr, d), lambda i: (i, 0))
    par = pl.BlockSpec((1, d), lambda i: (0, 0))
    ins, specs = [x, w.reshape(1, d), dy], [row, par, row]
    if add is not None:
        ins.append(add)
        specs.append(row)
    dx, dw = pl.pallas_call(
        body, grid=(s // tr,), in_specs=specs, out_specs=(row, par),
        out_shape=(jax.ShapeDtypeStruct((s, d), out_dtype), jax.ShapeDtypeStruct((1, d), F32)),
        compiler_params=_cp("arbitrary"), name=name)(*ins)
    return dx, dw.reshape(d)


def _loss_head(h, tgt):
    s, d = h.shape
    tr = _tile(s, 256, 8)

    def body(h_ref, t_ref, dh_ref, l_ref):
        e = h_ref[...] - t_ref[...]
        dh_ref[...] = e * (1.0 / d)

        @pl.when(pl.program_id(0) == 0)
        def _():
            l_ref[...] = jnp.zeros_like(l_ref)

        l_ref[...] += 0.5 * jnp.sum(jnp.mean(e * e, axis=-1, keepdims=True), axis=0, keepdims=True)

    row = pl.BlockSpec((tr, d), lambda i: (i, 0))
    dh, l = pl.pallas_call(
        body, grid=(s // tr,), in_specs=[row, row], out_specs=(row, pl.BlockSpec((1, 1), lambda i: (0, 0))),
        out_shape=(jax.ShapeDtypeStruct((s, d), F32), jax.ShapeDtypeStruct((1, 1), F32)),
        compiler_params=_cp("arbitrary"), name="loss_head")(h, tgt)
    return l[0, 0], dh


def _conv_rows(xe, w, width):
    y = None
    for j in range(width):
        s = width - 1 - j
        term = (pltpu.roll(xe, s, 0) if s else xe)[HALO:] * w[j:j + 1, :]
        y = term if y is None else y + term
    return y


def _conv_rows_t(dy_ext, w, width, tr):
    n = dy_ext.shape[0]
    dx = None
    for j in range(width):
        s = width - 1 - j
        term = (pltpu.roll(dy_ext, n - s, 0) if s else dy_ext)[:tr] * w[j:j + 1, :]
        dx = term if dx is None else dx + term
    return dx


def _conv_dw(dy, xe, width):
    rows = []
    for j in range(width):
        s = width - 1 - j
        rows.append(jnp.sum(dy * (pltpu.roll(xe, s, 0) if s else xe)[HALO:], axis=0, keepdims=True))
    return jnp.concatenate(rows, axis=0)


def _halo_specs(tr, tc, coff, nrow_blocks):
    per = tr // HALO
    last = nrow_blocks * per - 1
    cur = pl.BlockSpec((tr, tc), lambda j, r: (r, coff + j))
    prev = pl.BlockSpec((HALO, tc), lambda j, r: (jnp.maximum(r * per - 1, 0), coff + j))
    nxt = pl.BlockSpec((HALO, tc), lambda j, r: (jnp.minimum((r + 1) * per, last), coff + j))
    return cur, prev, nxt


def _dnconv_fwd(proj, w, cfg):
    s, width = cfg.S, 3 * cfg.HD * HEAD
    tc = _tile(math.gcd(cfg.QD, width), 512)
    tr = _tile(s, 512, 8)
    nr = s // tr
    cur, prev, _ = _halo_specs(tr, tc, cfg.QD // tc, nr)

    def body(x_ref, xp_ref, w_ref, o_ref):
        hist = jnp.where(pl.program_id(1) > 0, xp_ref[...], 0.0)
        y = _conv_rows(jnp.concatenate([hist, x_ref[...]], axis=0), w_ref[...], DN_CONV)
        o_ref[...] = y * _sigmoid(y)

    return pl.pallas_call(
        body, grid=(width // tc, nr),
        in_specs=[cur, prev, pl.BlockSpec((DN_CONV, tc), lambda j, r: (0, j))],
        out_specs=pl.BlockSpec((tr, tc), lambda j, r: (r, j)),
        out_shape=jax.ShapeDtypeStruct((s, width), F32),
        compiler_params=_cp("parallel", "parallel"), name="dnconv_fwd")(proj, proj, w)


def _dnconv_bwd(proj, du3, w, cfg):
    s, hw = cfg.S, cfg.HD * HEAD
    width = 3 * hw
    tc = _tile(math.gcd(cfg.QD, hw), 512)
    tr = _tile(s, 512, 8)
    nr = s // tr
    per, last = tr // HALO, s // HALO - 1
    cur, prev, nxt = _halo_specs(tr, tc, cfg.QD // tc, nr)
    pc = hw // tc
    du_cur = pl.BlockSpec((1, tr, tc), lambda j, r: (j // pc, r, j % pc))
    du_nxt = pl.BlockSpec((1, HALO, tc), lambda j, r: (j // pc, jnp.minimum((r + 1) * per, last), j % pc))

    def body(x_ref, xp_ref, xn_ref, du_ref, dun_ref, w_ref, dx_ref, dw_ref):
        r = pl.program_id(1)
        wv = w_ref[...]
        hist = jnp.where(r > 0, xp_ref[...], 0.0)
        xe = jnp.concatenate([hist, x_ref[...]], axis=0)
        y_ext = _conv_rows(jnp.concatenate([xe, xn_ref[...]], axis=0), wv, DN_CONV)
        du_ext = jnp.concatenate([du_ref[0], jnp.where(r < nr - 1, dun_ref[0], 0.0)], axis=0)
        dy_ext = du_ext * _dsilu(y_ext)
        dx_ref[...] = _conv_rows_t(dy_ext, wv, DN_CONV, tr).astype(dx_ref.dtype)

        @pl.when(r == 0)
        def _():
            dw_ref[...] = jnp.zeros_like(dw_ref)

        dw_ref[...] += _conv_dw(dy_ext[:tr], xe, DN_CONV)

    wspec = pl.BlockSpec((DN_CONV, tc), lambda j, r: (0, j))
    return pl.pallas_call(
        body, grid=(width // tc, nr),
        in_specs=[cur, prev, nxt, du_cur, du_nxt, wspec],
        out_specs=(pl.BlockSpec((tr, tc), lambda j, r: (r, j)), wspec),
        out_shape=(jax.ShapeDtypeStruct((s, width), BF16), jax.ShapeDtypeStruct((DN_CONV, width), F32)),
        compiler_params=_cp("parallel", "arbitrary"), name="dnconv_bwd")(proj, proj, proj, du3, du3, w)


def _glu_fwd(up, w, b, cfg):
    s, ff = cfg.S, cfg.FF
    tc = _tile(ff, 512)
    tr = _tile(s, 512, 8)
    nr, nc = s // tr, ff // tc
    g_cur, g_prev, _ = _halo_specs(tr, tc, 0, nr)
    v_cur, v_prev, _ = _halo_specs(tr, tc, nc, nr)

    def wspec(rows, off):
        return pl.BlockSpec((rows, tc), lambda j, r: (0, off + j))

    def body(g_ref, gp_ref, v_ref, vp_ref, wg_ref, wv_ref, bg_ref, bv_ref, o_ref):
        first = pl.program_id(1) > 0
        ug = _conv_rows(jnp.concatenate([jnp.where(first, gp_ref[...], 0.0), g_ref[...]], axis=0), wg_ref[...], FFN_CONV) + bg_ref[...]
        uv = _conv_rows(jnp.concatenate([jnp.where(first, vp_ref[...], 0.0), v_ref[...]], axis=0), wv_ref[...], FFN_CONV) + bv_ref[...]
        o_ref[...] = (ug * _sigmoid(ug) * uv).astype(o_ref.dtype)

    b2 = b.reshape(1, 2 * ff)
    return pl.pallas_call(
        body, grid=(nc, nr),
        in_specs=[g_cur, g_prev, v_cur, v_prev, wspec(FFN_CONV, 0), wspec(FFN_CONV, nc), wspec(1, 0), wspec(1, nc)],
        out_specs=pl.BlockSpec((tr, tc), lambda j, r: (r, j)),
        out_shape=jax.ShapeDtypeStruct((s, ff), BF16),
        compiler_params=_cp("parallel", "parallel"), name="glu_fwd")(up, up, up, up, w, w, b2, b2)


def _glu_bwd(up, dact, w, b, cfg):
    s, ff = cfg.S, cfg.FF
    tc = _tile(ff, 512)
    tr = _tile(s, 512, 8)
    nr, nc = s // tr, ff // tc
    g_cur, g_prev, g_nxt = _halo_specs(tr, tc, 0, nr)
    v_cur, v_prev, v_nxt = _halo_specs(tr, tc, nc, nr)
    d_cur, _, d_nxt = _halo_specs(tr, tc, 0, nr)

    def wspec(rows, off):
        return pl.BlockSpec((rows, tc), lambda j, r: (0, off + j))

    def body(g_ref, gp_ref, gn_ref, v_ref, vp_ref, vn_ref, d_ref, dn_ref, wg_ref, wv_ref, bg_ref, bv_ref,
             dup_ref, dw_ref, db_ref):
        r = pl.program_id(1)
        wg, wv = wg_ref[...], wv_ref[...]
        ge = jnp.concatenate([jnp.where(r > 0, gp_ref[...], 0.0), g_ref[...]], axis=0)
        ve = jnp.concatenate([jnp.where(r > 0, vp_ref[...], 0.0), v_ref[...]], axis=0)
        ug = _conv_rows(jnp.concatenate([ge, gn_ref[...]], axis=0), wg, FFN_CONV) + bg_ref[...]
        uv = _conv_rows(jnp.concatenate([ve, vn_ref[...]], axis=0), wv, FFN_CONV) + bv_ref[...]
        da = jnp.concatenate([d_ref[...].astype(F32), jnp.where(r < nr - 1, dn_ref[...].astype(F32), 0.0)], axis=0)
        dug = da * uv * _dsilu(ug)
        duv = da * ug * _sigmoid(ug)
        dup_ref[0] = _conv_rows_t(dug, wg, FFN_CONV, tr).astype(dup_ref.dtype)
        dup_ref[1] = _conv_rows_t(duv, wv, FFN_CONV, tr).astype(dup_ref.dtype)

        @pl.when(r == 0)
        def _():
            dw_ref[...] = jnp.zeros_like(dw_ref)
            db_ref[...] = jnp.zeros_like(db_ref)

        dw_ref[0] += _conv_dw(dug[:tr], ge, FFN_CONV)
        dw_ref[1] += _conv_dw(duv[:tr], ve, FFN_CONV)
        db_ref[0] += jnp.sum(dug[:tr], axis=0, keepdims=True)
        db_ref[1] += jnp.sum(duv[:tr], axis=0, keepdims=True)

    b2 = b.reshape(1, 2 * ff)
    dup, dw, db = pl.pallas_call(
        body, grid=(nc, nr),
        in_specs=[g_cur, g_prev, g_nxt, v_cur, v_prev, v_nxt, d_cur, d_nxt,
                  wspec(FFN_CONV, 0), wspec(FFN_CONV, nc), wspec(1, 0), wspec(1, nc)],
        out_specs=(pl.BlockSpec((2, tr, tc), lambda j, r: (0, r, j)),
                   pl.BlockSpec((2, FFN_CONV, tc), lambda j, r: (0, 0, j)),
                   pl.BlockSpec((2, 1, tc), lambda j, r: (0, 0, j))),
        out_shape=(jax.ShapeDtypeStruct((2, s, ff), BF16), jax.ShapeDtypeStruct((2, FFN_CONV, ff), F32),
                   jax.ShapeDtypeStruct((2, 1, ff), F32)),
        compiler_params=_cp("parallel", "arbitrary"), name="glu_bwd",
    )(up, up, up, up, up, up, dact, dact, w, w, b2, b2)
    dup = jnp.concatenate([dup[0], dup[1]], axis=1)
    dw = jnp.transpose(dw, (1, 0, 2)).reshape(FFN_CONV, 2 * ff)
    return dup, dw, db.reshape(2 * ff)


def _merge_specs(cfg):
    tc = _tile(math.gcd(math.gcd(cfg.GA, cfg.GD), cfg.D), 512)
    tr = _tile(cfg.S, 512, 8)
    ga = pl.BlockSpec((tr, tc), lambda r, j: (r, cfg.GA // tc + j))
    gd = pl.BlockSpec((tr, tc), lambda r, j: (r, cfg.GD // tc + j))
    pd = pl.BlockSpec((tr, tc), lambda r, j: (r, j))
    return (cfg.S // tr, cfg.D // tc), ga, gd, pd


def _merge_fwd(proj, pa, pdn, cfg):
    grid, ga, gd, pd = _merge_specs(cfg)

    def body(ga_ref, gd_ref, pa_ref, pd_ref, y_ref):
        y_ref[...] = (_sigmoid(ga_ref[...]) * pa_ref[...] + _sigmoid(gd_ref[...]) * pd_ref[...]).astype(y_ref.dtype)

    return pl.pallas_call(body, grid=grid, in_specs=[ga, gd, pd, pd], out_specs=pd,
                          out_shape=jax.ShapeDtypeStruct((cfg.S, cfg.D), BF16),
                          compiler_params=_cp("parallel", "parallel"), name="merge_fwd")(proj, proj, pa, pdn)


def _merge_bwd(proj, pa, pdn, dy, cfg):
    grid, ga, gd, pd = _merge_specs(cfg)

    def body(ga_ref, gd_ref, pa_ref, pd_ref, dy_ref, dpa_ref, dpd_ref, dga_ref, dgd_ref):
        dyv = dy_ref[...]
        sa, sd = _sigmoid(ga_ref[...]), _sigmoid(gd_ref[...])
        dpa_ref[...] = (dyv * sa).astype(BF16)
        dpd_ref[...] = (dyv * sd).astype(BF16)
        dga_ref[...] = (dyv * pa_ref[...] * sa * (1.0 - sa)).astype(BF16)
        dgd_ref[...] = (dyv * pd_ref[...] * sd * (1.0 - sd)).astype(BF16)

    out = jax.ShapeDtypeStruct((cfg.S, cfg.D), BF16)
    return pl.pallas_call(body, grid=grid, in_specs=[ga, gd, pd, pd, pd], out_specs=(pd,) * 4,
                          out_shape=(out,) * 4, compiler_params=_cp("parallel", "parallel"),
                          name="merge_bwd")(proj, proj, pa, pdn, dy)


def _rope(x, cos, sin):
    lane = lax.broadcasted_iota(jnp.int32, x.shape, 1)
    swapped = jnp.where(lane < ROT_HALF, pltpu.roll(x, LANES - ROT_HALF, 1), pltpu.roll(x, ROT_HALF, 1))
    return x * cos + swapped * sin


def _rope_t(dx, cos, sin):
    t = dx * sin
    lane = lax.broadcasted_iota(jnp.int32, dx.shape, 1)
    swapped = jnp.where(lane < ROT_HALF, pltpu.roll(t, LANES - ROT_HALF, 1), pltpu.roll(t, ROT_HALF, 1))
    return dx * cos + swapped


def _band_mask(i):
    r = lax.broadcasted_iota(jnp.int32, (ATTN_BLOCK, 2 * ATTN_BLOCK), 0)
    c = lax.broadcasted_iota(jnp.int32, (ATTN_BLOCK, 2 * ATTN_BLOCK), 1)
    dist = r + ATTN_BLOCK - c
    return (dist >= 0) & (dist < ATTN_BLOCK) & ((c >= ATTN_BLOCK) | (i > 0))


def _dot(a, b, ca, cb):
    return lax.dot_general(a.astype(BF16), b.astype(BF16), (((ca,), (cb,)), ((), ())), preferred_element_type=F32)


def _attn_probs(qg, k_cat, sink, valid):
    logits = jnp.where(valid, _dot(qg, k_cat, 1, 1) * (HEAD ** -0.5), -1e30)
    m = jnp.maximum(jnp.max(logits, axis=-1, keepdims=True), sink)
    p = jnp.exp(logits - m)
    es = jnp.exp(sink - m)
    inv = 1.0 / (jnp.sum(p, axis=-1, keepdims=True) + es)
    return p * inv, es * inv


def _attn_specs(cfg, nb, clamp):
    gw = cfg.G * HEAD
    qi = (lambda kv, i: (jnp.minimum(i, nb - 1), kv)) if clamp else (lambda kv, i: (i, kv))
    ci = (lambda i: jnp.minimum(i, nb - 1)) if clamp else (lambda i: i)
    pi = lambda i: jnp.maximum(ci(i) - 1, 0)
    ko, vo = cfg.KA // HEAD, cfg.VA // HEAD
    blk = (ATTN_BLOCK, HEAD)
    specs = [
        pl.BlockSpec((ATTN_BLOCK, gw), qi),
        pl.BlockSpec(blk, lambda kv, i: (ci(i), ko + kv)), pl.BlockSpec(blk, lambda kv, i: (pi(i), ko + kv)),
        pl.BlockSpec(blk, lambda kv, i: (ci(i), vo + kv)), pl.BlockSpec(blk, lambda kv, i: (pi(i), vo + kv)),
        pl.BlockSpec(blk, lambda kv, i: (ci(i), 0)), pl.BlockSpec(blk, lambda kv, i: (pi(i), 0)),
        pl.BlockSpec(blk, lambda kv, i: (ci(i), 0)), pl.BlockSpec(blk, lambda kv, i: (pi(i), 0)),
        pl.BlockSpec((1, 8, LANES), lambda kv, i: (kv, 0, 0)),
    ]
    return specs, pl.BlockSpec((ATTN_BLOCK, gw), qi)


def _sink_rows(sinks, cfg):
    sk = jnp.broadcast_to(sinks.reshape(cfg.KV, cfg.G, 1), (cfg.KV, cfg.G, LANES))
    return jnp.pad(sk, ((0, 0), (0, 8 - cfg.G), (0, 0)))


def _attn_fwd(proj, cosf, sinf, sinks, cfg):
    nb = cfg.S // ATTN_BLOCK
    specs, qspec = _attn_specs(cfg, nb, False)

    def body(q_ref, kc_ref, kp_ref, vc_ref, vp_ref, cc_ref, cp_ref, sc_ref, sp_ref, sk_ref, o_ref):
        i = pl.program_id(1)
        cc, sc = cc_ref[...], sc_ref[...]
        k_cat = jnp.concatenate([_rope(kp_ref[...], cp_ref[...], sp_ref[...]), _rope(kc_ref[...], cc, sc)], axis=0).astype(BF16)
        v_cat = jnp.concatenate([vp_ref[...], vc_ref[...]], axis=0).astype(BF16)
        valid = _band_mask(i)
        for g in range(cfg.G):
            qg = _rope(q_ref[:, g * HEAD:(g + 1) * HEAD], cc, sc)
            p, _ = _attn_probs(qg, k_cat, sk_ref[0, g:g + 1, 0:1], valid)
            o_ref[:, g * HEAD:(g + 1) * HEAD] = _dot(p, v_cat, 1, 0).astype(o_ref.dtype)

    return pl.pallas_call(
        body, grid=(cfg.KV, nb), in_specs=specs, out_specs=qspec,
        out_shape=jax.ShapeDtypeStruct((cfg.S, cfg.HA * HEAD), BF16),
        compiler_params=_cp("parallel", "parallel"), name="attn_fwd",
    )(proj, proj, proj, proj, proj, cosf, cosf, sinf, sinf, _sink_rows(sinks, cfg))


def _attn_bwd(proj, do, cosf, sinf, sinks, cfg):
    nb = cfg.S // ATTN_BLOCK
    specs, qspec = _attn_specs(cfg, nb, True)
    kv_out = pl.BlockSpec((ATTN_BLOCK, HEAD), lambda kv, i: (jnp.maximum(i - 1, 0), kv))

    def body(q_ref, kc_ref, kp_ref, vc_ref, vp_ref, cc_ref, cp_ref, sc_ref, sp_ref, sk_ref, do_ref,
             dq_ref, dk_ref, dv_ref, dsk_ref, ck_ref, cv_ref):
        i = pl.program_id(1)

        @pl.when(i == 0)
        def _():
            ck_ref[...] = jnp.zeros_like(ck_ref)
            cv_ref[...] = jnp.zeros_like(cv_ref)
            dsk_ref[...] = jnp.zeros_like(dsk_ref)

        @pl.when(i < nb)
        def _():
            cc, sc, cp, sp = cc_ref[...], sc_ref[...], cp_ref[...], sp_ref[...]
            k_cat = jnp.concatenate([_rope(kp_ref[...], cp, sp), _rope(kc_ref[...], cc, sc)], axis=0).astype(BF16)
            v_cat = jnp.concatenate([vp_ref[...], vc_ref[...]], axis=0).astype(BF16)
            valid = _band_mask(i)
            dk_cat = jnp.zeros((2 * ATTN_BLOCK, HEAD), F32)
            dv_cat = jnp.zeros((2 * ATTN_BLOCK, HEAD), F32)
            for g in range(cfg.G):
                sl = slice(g * HEAD, (g + 1) * HEAD)
                qg = _rope(q_ref[:, sl], cc, sc)
                dog = do_ref[:, sl]
                p, ps = _attn_probs(qg, k_cat, sk_ref[0, g:g + 1, 0:1], valid)
                dp = _dot(dog, v_cat, 1, 1)
                delta = jnp.sum(p * dp, axis=-1, keepdims=True)
                ds = p * (dp - delta) * (HEAD ** -0.5)
                dq_ref[:, sl] = _rope_t(_dot(ds, k_cat, 1, 0), cc, sc).astype(dq_ref.dtype)
                dk_cat = dk_cat + _dot(ds, qg, 0, 0)
                dv_cat = dv_cat + _dot(p, dog, 0, 0)
                dsk_ref[0, g:g + 1, :] += jnp.broadcast_to(-jnp.sum(ps * delta, axis=0, keepdims=True), (1, LANES))
            dk_ref[...] = (ck_ref[...] + _rope_t(dk_cat[:ATTN_BLOCK], cp, sp)).astype(dk_ref.dtype)
            dv_ref[...] = (cv_ref[...] + dv_cat[:ATTN_BLOCK]).astype(dv_ref.dtype)
            ck_ref[...] = _rope_t(dk_cat[ATTN_BLOCK:], cc, sc)
            cv_ref[...] = dv_cat[ATTN_BLOCK:]

        @pl.when(i == nb)
        def _():
            dk_ref[...] = ck_ref[...].astype(dk_ref.dtype)
            dv_ref[...] = cv_ref[...].astype(dv_ref.dtype)

    kvw = cfg.KV * HEAD
    dq, dk, dv, dsk = pl.pallas_call(
        body, grid=(cfg.KV, nb + 1), in_specs=specs + [qspec],
        out_specs=(qspec, kv_out, kv_out, pl.BlockSpec((1, 8, LANES), lambda kv, i: (kv, 0, 0))),
        out_shape=(jax.ShapeDtypeStruct((cfg.S, cfg.HA * HEAD), BF16), jax.ShapeDtypeStruct((cfg.S, kvw), BF16),
                   jax.ShapeDtypeStruct((cfg.S, kvw), BF16), jax.ShapeDtypeStruct((cfg.KV, 8, LANES), F32)),
        scratch_shapes=[pltpu.VMEM((ATTN_BLOCK, HEAD), F32), pltpu.VMEM((ATTN_BLOCK, HEAD), F32)],
        compiler_params=_cp("parallel", "arbitrary"), name="attn_bwd",
    )(proj, proj, proj, proj, proj, cosf, cosf, sinf, sinf, _sink_rows(sinks, cfg), do)
    return jnp.concatenate([dq, dk, dv], axis=1), dsk[:, :cfg.G, 0].reshape(cfg.HA)


@jax.custom_vjp
def _nn(a, b):
    return _dot(a, b, 1, 0)


@jax.custom_vjp
def _nt(a, b):
    return _dot(a, b, 1, 1)


@jax.custom_vjp
def _tn(a, b):
    return _dot(a, b, 0, 0)


_nn.defvjp(lambda a, b: (_nn(a, b), (a, b)), lambda r, g: (_nt(g, r[1]), _tn(r[0], g)))
_nt.defvjp(lambda a, b: (_nt(a, b), (a, b)), lambda r, g: (_nn(g, r[1]), _tn(g, r[0])))
_tn.defvjp(lambda a, b: (_tn(a, b), (a, b)), lambda r, g: (_nt(r[1], g), _nn(r[0], g)))


def _dgx(a, b, ca, cb):
    return lax.dot_general(a, b, (((ca,), (cb,)), ((), ())), precision=lax.Precision.HIGH, preferred_element_type=F32)


def _unit_lower_inverse(a):
    ii = lax.broadcasted_iota(jnp.int32, a.shape, 0)
    jj = lax.broadcasted_iota(jnp.int32, a.shape, 1)
    t = jnp.where(ii == jj, 1.0, 0.0) - a
    p = a
    for _ in range(int(math.log2(CHUNK)) - 1):
        p = _dgx(p, p, 1, 0)
        t = t + _dgx(t, p, 1, 0)
    return t


@jax.custom_vjp
def _tri_inv(a):
    return _unit_lower_inverse(a)


def _tri_inv_fwd(a):
    t = _unit_lower_inverse(a)
    return t, t


def _tri_inv_bwd(t, g):
    return (-_dgx(t, _dgx(g, t, 1, 1), 0, 0),)


_tri_inv.defvjp(_tri_inv_fwd, _tri_inv_bwd)


def _gdn_chunk(q, k, v, z, b_col, a_col, a_row, al_col, dt_col, al_row, dt_row, nw, s):
    rows = q.shape[0]
    hb = rows // CHUNK
    ii = lax.broadcasted_iota(jnp.int32, (rows, rows), 0)
    jj = lax.broadcasted_iota(jnp.int32, (rows, rows), 1)
    same = (ii // CHUNK) == (jj // CHUNK)
    tri, strict, upper = same & (ii >= jj), same & (ii > jj), same & (ii <= jj)
    beta = _sigmoid(b_col)
    g_col = -jnp.exp(al_col) * _softplus(a_col + dt_col)
    g_row = -jnp.exp(al_row) * _softplus(a_row + dt_row)
    gc_col = jnp.sum(jnp.where(tri, g_row, 0.0), axis=1, keepdims=True)
    gc_row = jnp.sum(jnp.where(upper, g_col, 0.0), axis=0, keepdims=True)
    last = jj == (ii // CHUNK) * CHUNK + (CHUNK - 1)
    gl_col = jnp.sum(jnp.where(last, gc_row, 0.0), axis=1, keepdims=True)
    decay = jnp.where(tri, jnp.exp(jnp.where(tri, gc_col - gc_row, 0.0)), 0.0)
    qn = q * lax.rsqrt(jnp.sum(q * q, axis=-1, keepdims=True) + EPS) * (HEAD ** -0.5)
    kn = k * lax.rsqrt(jnp.sum(k * k, axis=-1, keepdims=True) + EPS)
    t_inv = _tri_inv(jnp.where(strict, beta * _nt(kn, kn) * decay, 0.0))
    eg = jnp.exp(gc_col)
    u = _nn(t_inv, v * beta)
    w = _nn(t_inv, kn * (beta * eg))
    qk = jnp.where(tri, _nt(qn, kn) * decay, 0.0)
    ri = lax.broadcasted_iota(jnp.int32, (rows, hb * HEAD), 0)
    ci = lax.broadcasted_iota(jnp.int32, (rows, hb * HEAD), 1)
    own = (ri // CHUNK) == (ci // HEAD)

    def spread(x):
        return jnp.where(own, jnp.concatenate([x] * hb, axis=1), 0.0)

    v_new = u - _nn(spread(w), s)
    o = _nn(spread(qn * eg), s) + _nn(qk, v_new)
    si = lax.broadcasted_iota(jnp.int32, (hb * HEAD, rows), 0)
    sj = lax.broadcasted_iota(jnp.int32, (hb * HEAD, rows), 1)
    gl_s = jnp.sum(jnp.where(sj == (si // HEAD) * CHUNK + (CHUNK - 1), gc_row, 0.0), axis=1, keepdims=True)
    s1 = s * jnp.exp(gl_s) + _tn(spread(kn * jnp.exp(gl_col - gc_col)), v_new)
    od = o * lax.rsqrt(jnp.mean(o * o, axis=-1, keepdims=True) + EPS) * nw * (z * _sigmoid(z))
    return od, s1


def _gdn_heads_per_step(cfg):
    for hb in (GDN_HEADS_PER_STEP, 2, 1):
        if cfg.HD % hb == 0 and cfg.Z % (hb * HEAD) == 0:
            return hb


def _gdn_in_specs(cfg, nc, rev):
    hb = _gdn_heads_per_step(cfg)
    ng, rows = cfg.HD // hb, hb * CHUNK
    ci = (lambda n: nc - 1 - n) if rev else (lambda n: n)
    zo = cfg.Z // (hb * HEAD)
    blk = (CHUNK, hb * HEAD)
    col = pl.BlockSpec((1, 1, rows, 1), lambda g, n: (g, ci(n), 0, 0))
    row = pl.BlockSpec((1, 1, 1, rows), lambda g, n: (g, ci(n), 0, 0))
    hcol = pl.BlockSpec((1, rows, 1), lambda g, n: (g, 0, 0))
    hrow = pl.BlockSpec((1, 1, rows), lambda g, n: (g, 0, 0))
    return [
        pl.BlockSpec(blk, lambda g, n: (ci(n), g)), pl.BlockSpec(blk, lambda g, n: (ci(n), ng + g)),
        pl.BlockSpec(blk, lambda g, n: (ci(n), 2 * ng + g)), pl.BlockSpec(blk, lambda g, n: (ci(n), zo + g)),
        col, col, row, hcol, hcol, hrow, hrow, pl.BlockSpec((1, HEAD), lambda g, n: (0, 0)),
    ]


def _gdn_args(qkv, proj, pba, a_log, dt_bias, nw, cfg):
    hb, hd, s = _gdn_heads_per_step(cfg), cfg.HD, cfg.S
    ng, nc, rows = hd // hb, s // CHUNK, hb * CHUNK

    def grouped(x):
        return jnp.transpose(x.reshape(nc, CHUNK, ng, hb), (2, 0, 3, 1)).reshape(ng, nc, rows)

    b, a = grouped(pba[:, :hd]), grouped(pba[:, hd:2 * hd])
    per_row = lambda x: jnp.repeat(x.reshape(ng, hb), CHUNK, axis=1)
    al, dt = per_row(a_log), per_row(dt_bias)
    return (qkv, qkv, qkv, proj, b.reshape(ng, nc, rows, 1), a.reshape(ng, nc, rows, 1), a.reshape(ng, nc, 1, rows),
            al.reshape(ng, rows, 1), dt.reshape(ng, rows, 1), al.reshape(ng, 1, rows), dt.reshape(ng, 1, rows),
            nw.reshape(1, HEAD))


def _stack_heads(x, hb):
    return jnp.concatenate([x[:, i * HEAD:(i + 1) * HEAD] for i in range(hb)], axis=0)


def _gdn_load(refs, hb):
    q, k, v, z, b, a, ar, alc, dtc, alr, dtr, nw = refs
    return (_stack_heads(q[...], hb), _stack_heads(k[...], hb), _stack_heads(v[...], hb), _stack_heads(z[...], hb),
            b[0, 0], a[0, 0], ar[0, 0], alc[0], dtc[0], alr[0], dtr[0], nw[...])


def _gdn_fwd(qkv, proj, pba, a_log, dt_bias, nw, cfg):
    nc, hb = cfg.S // CHUNK, _gdn_heads_per_step(cfg)
    ng = cfg.HD // hb

    def body(*refs):
        o_ref, st_ref, s_ref = refs[12:]

        @pl.when(pl.program_id(1) == 0)
        def _():
            s_ref[...] = jnp.zeros_like(s_ref)

        s0 = s_ref[...]
        st_ref[0, 0] = s0
        od, s1 = _gdn_chunk(*_gdn_load(refs[:12], hb), s0)
        for i in range(hb):
            o_ref[:, i * HEAD:(i + 1) * HEAD] = od[i * CHUNK:(i + 1) * CHUNK].astype(o_ref.dtype)
        s_ref[...] = s1

    return pl.pallas_call(
        body, grid=(ng, nc), in_specs=_gdn_in_specs(cfg, nc, False),
        out_specs=(pl.BlockSpec((CHUNK, hb * HEAD), lambda g, n: (n, g)),
                   pl.BlockSpec((1, 1, hb * HEAD, HEAD), lambda g, n: (g, n, 0, 0))),
        out_shape=(jax.ShapeDtypeStruct((cfg.S, cfg.HD * HEAD), BF16),
                   jax.ShapeDtypeStruct((ng, nc, hb * HEAD, HEAD), F32)),
        scratch_shapes=[pltpu.VMEM((hb * HEAD, HEAD), F32)],
        compiler_params=_cp("parallel", "arbitrary"), name="gdn_fwd",
    )(*_gdn_args(qkv, proj, pba, a_log, dt_bias, nw, cfg))


def _gdn_bwd(qkv, proj, pba, a_log, dt_bias, nw, states, dod, cfg):
    nc, hd, s, hb = cfg.S // CHUNK, cfg.HD, cfg.S, _gdn_heads_per_step(cfg)
    ng, rows = hd // hb, hb * CHUNK
    rv = lambda n: nc - 1 - n
    col = pl.BlockSpec((1, 1, rows, 1), lambda g, n: (g, rv(n), 0, 0))
    row = pl.BlockSpec((1, 1, 1, rows), lambda g, n: (g, rv(n), 0, 0))
    hcol = pl.BlockSpec((1, rows, 1), lambda g, n: (g, 0, 0))
    hrow = pl.BlockSpec((1, 1, rows), lambda g, n: (g, 0, 0))

    def body(*refs):
        st_ref, dod_ref = refs[12:14]
        (dqkv_ref, dz_ref, db_ref, da_ref, dar_ref, dalc_ref, ddtc_ref, dalr_ref, ddtr_ref, dnw_ref, ds_ref) = refs[14:]
        g, n = pl.program_id(0), pl.program_id(1)

        @pl.when(n == 0)
        def _():
            ds_ref[...] = jnp.zeros_like(ds_ref)
            for r in (dalc_ref, ddtc_ref, dalr_ref, ddtr_ref):
                r[...] = jnp.zeros_like(r)

        @pl.when((n == 0) & (g == 0))
        def _():
            dnw_ref[...] = jnp.zeros_like(dnw_ref)

        _, vjp = jax.vjp(_gdn_chunk, *_gdn_load(refs[:12], hb), st_ref[0, 0])
        dq, dk, dv, dz, db, da, dar, dalc, ddtc, dalr, ddtr, dnw, ds0 = vjp(
            (_stack_heads(dod_ref[...].astype(F32), hb), ds_ref[...]))
        for i in range(hb):
            sl, rs = slice(i * HEAD, (i + 1) * HEAD), slice(i * CHUNK, (i + 1) * CHUNK)
            dqkv_ref[0, :, sl], dqkv_ref[1, :, sl], dqkv_ref[2, :, sl] = dq[rs], dk[rs], dv[rs]
            dz_ref[:, sl] = dz[rs].astype(dz_ref.dtype)
        db_ref[0, 0], da_ref[0, 0], dar_ref[0, 0] = db, da, dar
        dalc_ref[0] += dalc
        ddtc_ref[0] += ddtc
        dalr_ref[0] += dalr
        ddtr_ref[0] += ddtr
        dnw_ref[...] += dnw
        ds_ref[...] = ds0

    blk = pl.BlockSpec((CHUNK, hb * HEAD), lambda g, n: (rv(n), g))
    f32 = lambda *sh: jax.ShapeDtypeStruct(sh, F32)
    outs = pl.pallas_call(
        body, grid=(ng, nc),
        in_specs=_gdn_in_specs(cfg, nc, True) + [pl.BlockSpec((1, 1, hb * HEAD, HEAD), lambda g, n: (g, rv(n), 0, 0)), blk],
        out_specs=(pl.BlockSpec((3, CHUNK, hb * HEAD), lambda g, n: (0, rv(n), g)), blk, col, col, row,
                   hcol, hcol, hrow, hrow, pl.BlockSpec((1, HEAD), lambda g, n: (0, 0))),
        out_shape=(f32(3, s, hd * HEAD), jax.ShapeDtypeStruct((s, hd * HEAD), BF16),
                   f32(ng, nc, rows, 1), f32(ng, nc, rows, 1), f32(ng, nc, 1, rows),
                   f32(ng, rows, 1), f32(ng, rows, 1), f32(ng, 1, rows), f32(ng, 1, rows), f32(1, HEAD)),
        scratch_shapes=[pltpu.VMEM((hb * HEAD, HEAD), F32)],
        compiler_params=_cp("arbitrary", "arbitrary"), name="gdn_bwd",
    )(*_gdn_args(qkv, proj, pba, a_log, dt_bias, nw, cfg), states, dod)
    dqkv, dz, db, da, dar, dalc, ddtc, dalr, ddtr, dnw = outs

    def ungrouped(x):
        return jnp.transpose(x.reshape(ng, nc, hb, CHUNK), (1, 3, 0, 2)).reshape(s, hd)

    per_head = lambda c, r: (c.reshape(ng, hb, CHUNK) + r.reshape(ng, hb, CHUNK)).sum(axis=-1).reshape(hd)
    d_b = ungrouped(db.reshape(ng, nc, rows))
    d_a = ungrouped(da.reshape(ng, nc, rows) + dar.reshape(ng, nc, rows))
    dba = jnp.concatenate([d_b, d_a, jnp.zeros((s, LANES - 2 * hd), F32)], axis=1).astype(BF16)
    return dqkv, dz, dba, per_head(dalc, dalr), per_head(ddtc, ddtr), dnw.reshape(HEAD)


def _adamw(w, g, m, v, *, name):
    r, c = w.shape
    tr = _tile(r, max(8, (1 << 18) // c // 8 * 8), 8) if r % 8 == 0 else r
    c1, c2 = 1.0 - ADAM_B1 ** ADAM_STEP, 1.0 - ADAM_B2 ** ADAM_STEP

    def body(w_ref, g_ref, m_ref, v_ref, d_ref, nm_ref, nv_ref):
        gv = g_ref[...]
        nm = ADAM_B1 * m_ref[...] + (1.0 - ADAM_B1) * gv
        nv = ADAM_B2 * v_ref[...] + (1.0 - ADAM_B2) * (gv * gv)
        d_ref[...] = -ADAM_LR * ((nm / c1) / (jnp.sqrt(nv / c2) + ADAM_EPS) + ADAM_WD * w_ref[...])
        nm_ref[...] = nm
        nv_ref[...] = nv

    spec = pl.BlockSpec((tr, c), lambda i: (i, 0))
    out = jax.ShapeDtypeStruct((r, c), F32)
    return pl.pallas_call(body, grid=(r // tr,), in_specs=[spec] * 4, out_specs=(spec,) * 3, out_shape=(out,) * 3,
                          compiler_params=_cp("parallel"), name=name)(w, g, m, v)


def _sum_leading(a, *, name):
    n, r, c = a.shape
    tr = _tile(r, PACK_TR, 8)

    def body(a_ref, o_ref):
        acc = a_ref[0].astype(F32)
        for i in range(1, n):
            acc = acc + a_ref[i].astype(F32)
        o_ref[...] = acc

    return pl.pallas_call(body, grid=(r // tr,), in_specs=[pl.BlockSpec((n, tr, c), lambda i: (0, i, 0))],
                          out_specs=pl.BlockSpec((tr, c), lambda i: (i, 0)),
                          out_shape=jax.ShapeDtypeStruct((r, c), F32), compiler_params=_cp("parallel"), name=name)(a)


def _pair_sum(gp, b1, cidx):
    n, r, c = gp.shape
    half = r // 2
    tr = _tile(half, PACK_TR, 16)
    nh = half // tr

    def body(c_ref, a_ref, b_ref, o_ref):
        o_ref[...] = (a_ref[...].astype(F32) + b_ref[...].astype(F32)).astype(o_ref.dtype)

    return pl.pallas_call(
        body,
        grid_spec=pltpu.PrefetchScalarGridSpec(
            num_scalar_prefetch=1, grid=(n, nh),
            in_specs=[pl.BlockSpec((1, tr, c), lambda j, i, c_ref: (j, c_ref[0] * nh + i, 0)),
                      pl.BlockSpec((1, tr, c), lambda j, i, c_ref: (j, i, 0))],
            out_specs=pl.BlockSpec((1, tr, c), lambda j, i, c_ref: (j, i, 0))),
        out_shape=jax.ShapeDtypeStruct((n, half, c), BF16),
        compiler_params=_cp("parallel", "parallel"), name="rs_pair_sum")(cidx, gp, b1)


def _place():
    x, y, c = lax.axis_index("x"), lax.axis_index("y"), lax.axis_index("c")
    chips = [(1 - x, y), (x, 1 - y), (1 - x, 1 - y)]
    return x, y, c, chips


def _remote(src, dst, send_sems, recv_sems, k, to):
    return pltpu.make_async_remote_copy(src_ref=src, dst_ref=dst, send_sem=send_sems.at[k], recv_sem=recv_sems.at[k],
                                        device_id=to, device_id_type=MESH)


def _comm_call(body, out_shape, n_sems, name):
    return pl.pallas_call(
        body, out_shape=out_shape, in_specs=[ANY], out_specs=ANY,
        scratch_shapes=[pltpu.SemaphoreType.DMA((n_sems,)), pltpu.SemaphoreType.DMA((n_sems,))],
        compiler_params=pltpu.CompilerParams(has_side_effects=True), name=name)


def _chip_index():
    return 2 * lax.axis_index("x") + lax.axis_index("y")


def _gather_chips(p, *, name):
    r, w = p.shape
    half = r // 2

    def body(p_ref, g_ref, send_sems, recv_sems):
        x, y, c, chips = _place()
        me = 2 * x + y
        mine = pl.ds(pl.multiple_of(c * half, 16), half)
        theirs = pl.ds(pl.multiple_of((1 - c) * half, 16), half)
        first = [_remote(p_ref.at[mine], g_ref.at[me, mine], send_sems, recv_sems, j, (cx, cy, c))
                 for j, (cx, cy) in enumerate(chips)]
        for cp in first:
            cp.start()
        passed = []
        for j, (cx, cy) in enumerate(chips):
            rows = g_ref.at[2 * cx + cy, mine]
            _remote(rows, rows, send_sems, recv_sems, j, (cx, cy, c)).wait_recv()
            cp = _remote(rows, rows, send_sems, recv_sems, 3 + j, (x, y, 1 - c))
            cp.start()
            passed.append(cp)
        for j, (cx, cy) in enumerate(chips):
            rows = g_ref.at[2 * cx + cy, theirs]
            _remote(rows, rows, send_sems, recv_sems, 3 + j, (x, y, 1 - c)).wait_recv()
        for cp in first + passed:
            cp.wait_send()

    got = _comm_call(body, jax.ShapeDtypeStruct((4, r, w), p.dtype), 6, name)(p)
    return lax.dynamic_update_slice(got, p[None], (_chip_index(), 0, 0))


def _swap_halves(gp):
    n, r, w = gp.shape
    half = r // 2

    def body(g_ref, b_ref, send_sems, recv_sems):
        x, y, c, _ = _place()
        theirs = pl.ds(pl.multiple_of((1 - c) * half, 16), half)
        cps = [_remote(g_ref.at[j, theirs], b_ref.at[j], send_sems, recv_sems, j, (x, y, 1 - c)) for j in range(n)]
        for cp in cps:
            cp.start()
        for cp in cps:
            cp.wait()

    return _comm_call(body, jax.ShapeDtypeStruct((n, half, w), gp.dtype), n, "rs_swap_halves")(gp)


def _scatter_chips(sp):
    n, h, w = sp.shape

    def body(s_ref, b_ref, send_sems, recv_sems):
        x, y, c, chips = _place()
        me = 2 * x + y
        cps = [_remote(s_ref.at[2 * cx + cy], b_ref.at[me], send_sems, recv_sems, j, (cx, cy, c))
               for j, (cx, cy) in enumerate(chips)]
        for cp in cps:
            cp.start()
        for j, (cx, cy) in enumerate(chips):
            slot = b_ref.at[2 * cx + cy]
            _remote(slot, slot, send_sems, recv_sems, j, (cx, cy, c)).wait_recv()
        for cp in cps:
            cp.wait_send()

    got = _comm_call(body, jax.ShapeDtypeStruct((n, h, w), sp.dtype), 3, "rs_scatter_chips")(sp)
    me = _chip_index()
    return lax.dynamic_update_slice(got, lax.dynamic_slice_in_dim(sp, me, 1, axis=0), (me, 0, 0))


def _share_halves(t):
    h, w = t.shape

    def body(t_ref, f_ref, send_sems, recv_sems):
        x, y, c, _ = _place()
        mine = pl.ds(pl.multiple_of(c * h, 16), h)
        theirs = pl.ds(pl.multiple_of((1 - c) * h, 16), h)
        cp = _remote(t_ref, f_ref.at[mine], send_sems, recv_sems, 0, (x, y, 1 - c))
        cp.start()
        _remote(t_ref, f_ref.at[theirs], send_sems, recv_sems, 0, (x, y, 1 - c)).wait_recv()
        cp.wait_send()

    got = _comm_call(body, jax.ShapeDtypeStruct((2 * h, w), t.dtype), 1, "rs_share_halves")(t)
    return lax.dynamic_update_slice(got, t, (lax.axis_index("c") * h, 0))


def _gather_all(p):
    r, w = p.shape
    rel = [(dx, dy, dc) for dx in (0, 1) for dy in (0, 1) for dc in (0, 1)][1:]

    def body(p_ref, g_ref, send_sems, recv_sems):
        x, y, c = lax.axis_index("x"), lax.axis_index("y"), lax.axis_index("c")
        me = 4 * x + 2 * y + c
        g_ref[me] = p_ref[...]
        peers = [(jnp.bitwise_xor(x, dx), jnp.bitwise_xor(y, dy), jnp.bitwise_xor(c, dc)) for dx, dy, dc in rel]
        cps = [_remote(p_ref, g_ref.at[me], send_sems, recv_sems, k, peer) for k, peer in enumerate(peers)]
        for cp in cps:
            cp.start()
        for k, (px, py, pc) in enumerate(peers):
            slot = g_ref.at[4 * px + 2 * py + pc]
            _remote(slot, slot, send_sems, recv_sems, k, (px, py, pc)).wait_recv()
        for cp in cps:
            cp.wait_send()

    return pl.pallas_call(
        body, out_shape=jax.ShapeDtypeStruct((8, r, w), p.dtype),
        in_specs=[pl.BlockSpec(memory_space=pltpu.VMEM)], out_specs=pl.BlockSpec(memory_space=pltpu.VMEM),
        scratch_shapes=[pltpu.SemaphoreType.DMA((7,)), pltpu.SemaphoreType.DMA((7,))],
        compiler_params=pltpu.CompilerParams(has_side_effects=True, vmem_limit_bytes=VMEM_LIMIT),
        name="gather_all")(p)


def _pack_rows(n):
    unit = 2 * PACK_TR
    return -(-n // (PACK_W * unit)) * unit


def _pack(parts):
    flat = jnp.concatenate([p.reshape(-1) for p in parts])
    rows = _pack_rows(flat.shape[0])
    return jnp.pad(flat, (0, rows * PACK_W - flat.shape[0])).reshape(rows, PACK_W)


def _unpack(buf, shapes):
    flat = buf.reshape(-1)
    out, off = [], 0
    for sh in shapes:
        n = math.prod(sh)
        out.append(flat[off:off + n].reshape(sh))
        off += n
    return out


def _f32_as_bf16_pairs(a):
    return lax.bitcast_convert_type(a, BF16)


def _bf16_pairs_as_f32(a):
    return lax.bitcast_convert_type(a, F32)


class _Cfg:
    def __init__(self, x, w_in, attn_sinks, dn_a_log, w_ffn_up):
        self.S, self.D = x.shape[1], x.shape[2]
        self.L = w_in.shape[0]
        self.HA, self.HD = attn_sinks.shape[1], dn_a_log.shape[1]
        self.IN = 4 * w_in.shape[2]
        kvw = self.IN - self.HA * HEAD - 4 * self.HD * HEAD - 2 * self.HD - 2 * self.D
        self.KV = kvw // (2 * HEAD)
        self.G = self.HA // self.KV
        self.FF = 2 * w_ffn_up.shape[2]
        self.KA = self.HA * HEAD
        self.VA = self.KA + self.KV * HEAD
        self.QD = self.VA + self.KV * HEAD
        self.Z = self.QD + 3 * self.HD * HEAD
        self.GA = self.Z + self.HD * HEAD
        self.GD = self.GA + self.D
        self.NB = self.GD + self.D
        assert self.NB + 2 * self.HD == self.IN and self.G <= 8 and 2 * self.HD <= LANES


def kernel(x, positions, norm_mix_pre, w_in, attn_sinks, dn_conv_w, dn_a_log, dn_dt_bias, dn_norm_w, w_branch_attn, w_branch_dn, w_out, norm_mix_post, norm_ffn_pre, w_ffn_up, ffn_conv_w, ffn_conv_b, w_ffn_down, norm_ffn_post, loss_target, m_norm_mix_pre, m_w_in, m_attn_sinks, m_dn_conv_w, m_dn_a_log, m_dn_dt_bias, m_dn_norm_w, m_w_branch_attn, m_w_branch_dn, m_w_out, m_norm_mix_post, m_norm_ffn_pre, m_w_ffn_up, m_ffn_conv_w, m_ffn_conv_b, m_w_ffn_down, m_norm_ffn_post, v_norm_mix_pre, v_w_in, v_attn_sinks, v_dn_conv_w, v_dn_a_log, v_dn_dt_bias, v_dn_norm_w, v_w_branch_attn, v_w_branch_dn, v_w_out, v_norm_mix_post, v_norm_ffn_pre, v_w_ffn_up, v_ffn_conv_w, v_ffn_conv_b, v_w_ffn_down, v_norm_ffn_post):
    cfg = _Cfg(x, w_in, attn_sinks, dn_a_log, w_ffn_up)
    S, D, L, HD = cfg.S, cfg.D, cfg.L, cfg.HD
    big = [w_in, w_branch_attn, w_branch_dn, w_out, w_ffn_up, w_ffn_down]
    big_shapes = [a.shape[1:] for a in big]
    conv_shapes = [dn_conv_w.shape[1:], ffn_conv_w.shape[1:]]
    o1 = cfg.Z

    def full_weights(l):
        parts = [a[l].astype(BF16) for a in big] + [_f32_as_bf16_pairs(dn_conv_w[l]), _f32_as_bf16_pairs(ffn_conv_w[l])]
        got = _gather_chips(_pack(parts), name="gather_weights")
        per_chip = [_unpack(got[j], big_shapes + [sh + (2,) for sh in conv_shapes]) for j in range(4)]
        cat = lambda i, axis: jnp.concatenate([per_chip[j][i] for j in range(4)], axis=axis)
        win = cat(0, 1)
        return dict(
            wbig=jnp.concatenate([win[:, :o1], win[:, o1 + 2 * HD:]], axis=1),
            wba=jnp.pad(win[:, o1:o1 + 2 * HD], ((0, 0), (0, LANES - 2 * HD))),
            wbra=cat(1, 0), wbrd=cat(2, 0), wout=cat(3, 0), wup=cat(4, 1), wdown=cat(5, 0),
            dnw=_bf16_pairs_as_f32(cat(6, 1)), fcw=_bf16_pairs_as_f32(cat(7, 1)))

    weights = [full_weights(l) for l in range(L)]

    inv_freq = ROPE_THETA ** (-jnp.arange(0, 2 * ROT_HALF, 2, dtype=F32) / (2 * ROT_HALF))
    ang = positions.reshape(S, 1).astype(F32) * inv_freq
    cosf = jnp.concatenate([jnp.cos(ang), jnp.cos(ang), jnp.ones((S, HEAD - 2 * ROT_HALF), F32)], axis=1)
    sinf = jnp.concatenate([-jnp.sin(ang), jnp.sin(ang), jnp.zeros((S, HEAD - 2 * ROT_HALF), F32)], axis=1)

    h = x.reshape(S, D)
    saved = []
    for l in range(L):
        wt = weights[l]
        t = dict(h0=h)
        t["xn"] = _rms_fwd(h, norm_mix_pre[l], out_dtype=BF16, name="norm_mix_pre")
        t["proj"] = _mm(t["xn"], wt["wbig"], name="proj")
        t["pba"] = _mm(t["xn"], wt["wba"], name="proj_ba")
        t["oa"] = _attn_fwd(t["proj"], cosf, sinf, attn_sinks[l], cfg)
        t["qkv"] = _dnconv_fwd(t["proj"], wt["dnw"], cfg)
        t["od"], t["st"] = _gdn_fwd(t["qkv"], t["proj"], t["pba"], dn_a_log[l], dn_dt_bias[l], dn_norm_w[l], cfg)
        t["pa"] = _mm(t["oa"], wt["wbra"], name="branch_attn")
        t["pd"] = _mm(t["od"], wt["wbrd"], name="branch_dn")
        t["y"] = _merge_fwd(t["proj"], t["pa"], t["pd"], cfg)
        t["mix"] = _mm(t["y"], wt["wout"], name="mix_out")
        t["h1"] = _rms_fwd(t["mix"], norm_mix_post[l], res=h, out_dtype=F32, name="norm_mix_post")
        t["xf"] = _rms_fwd(t["h1"], norm_ffn_pre[l], out_dtype=BF16, name="norm_ffn_pre")
        t["up"] = _mm(t["xf"], wt["wup"], name="ffn_up")
        t["act"] = _glu_fwd(t["up"], wt["fcw"], ffn_conv_b[l], cfg)
        t["f"] = _mm(t["act"], wt["wdown"], name="ffn_down")
        h = _rms_fwd(t["f"], norm_ffn_post[l], res=t["h1"], out_dtype=F32, name="norm_ffn_post")
        saved.append(t)

    loss_local, dh = _loss_head(h, loss_target.reshape(S, D))
    loss = lax.psum(loss_local, ("x", "y", "c"))

    cidx = lax.axis_index("c").astype(jnp.int32).reshape(1)
    big_grads, small = [None] * L, [None] * L
    for l in reversed(range(L)):
        wt, t = weights[l], saved[l]
        df, g_nfpost = _rms_bwd(t["f"], norm_ffn_post[l], dh, out_dtype=BF16, name="norm_ffn_post_bwd")
        dact = _mm(df, wt["wdown"], tb=True, name="ffn_down_dx")
        g_wdown = _mm(t["act"], df, ta=True, name="ffn_down_dw")
        dup, g_fcw, g_fcb = _glu_bwd(t["up"], dact, wt["fcw"], ffn_conv_b[l], cfg)
        dxf = _mm(dup, wt["wup"], tb=True, name="ffn_up_dx")
        g_wup = _mm(t["xf"], dup, ta=True, name="ffn_up_dw")
        dh1, g_nfp = _rms_bwd(t["h1"], norm_ffn_pre[l], dxf, add=dh, name="norm_ffn_pre_bwd")
        dmix, g_nmpost = _rms_bwd(t["mix"], norm_mix_post[l], dh1, out_dtype=BF16, name="norm_mix_post_bwd")
        dy = _mm(dmix, wt["wout"], tb=True, name="mix_out_dx")
        g_wout = _mm(t["y"], dmix, ta=True, name="mix_out_dw")
        dpa, dpd, dga, dgd = _merge_bwd(t["proj"], t["pa"], t["pd"], dy, cfg)
        doa = _mm(dpa, wt["wbra"], tb=True, out_dtype=BF16, name="branch_attn_dx")
        g_wbra = _mm(t["oa"], dpa, ta=True, name="branch_attn_dw")
        dod = _mm(dpd, wt["wbrd"], tb=True, out_dtype=BF16, name="branch_dn_dx")
        g_wbrd = _mm(t["od"], dpd, ta=True, name="branch_dn_dw")
        dqkv_c, dz, dba, g_alog, g_dt, g_dnnorm = _gdn_bwd(t["qkv"], t["proj"], t["pba"], dn_a_log[l], dn_dt_bias[l],
                                                            dn_norm_w[l], t["st"], dod, cfg)
        dqkv_d, g_dnw = _dnconv_bwd(t["proj"], dqkv_c, wt["dnw"], cfg)
        dqkv_a, g_sinks = _attn_bwd(t["proj"], doa, cosf, sinf, attn_sinks[l], cfg)
        dproj = jnp.concatenate([dqkv_a, dqkv_d, dz, dga, dgd], axis=1)
        dxn = _mm(dba, wt["wba"], tb=True, name="proj_ba_dx")
        dxn = _mm(dproj, wt["wbig"], tb=True, add=dxn, name="proj_dx")
        g_wbig = _mm(t["xn"], dproj, ta=True, name="proj_dw")
        g_wba = _mm(t["xn"], dba, ta=True, name="proj_ba_dw")
        dh, g_nmp = _rms_bwd(t["h0"], norm_mix_pre[l], dxn, add=dh1, name="norm_mix_pre_bwd")

        g_win = jnp.concatenate([g_wbig[:, :o1], g_wba[:, :2 * HD], g_wbig[:, o1:]], axis=1)
        full = [g_win, g_wbra, g_wbrd, g_wout, g_wup, g_wdown]
        split_axis = [1, 0, 0, 0, 1, 0]
        by_chip = jnp.stack([_pack([jnp.split(a, 4, axis=ax)[j].astype(BF16) for a, ax in zip(full, split_axis)])
                             for j in range(4)])
        pair = _pair_sum(by_chip, _swap_halves(by_chip), cidx)
        mine = _sum_leading(_scatter_chips(pair), name="rs_sum_chips")
        big_grads[l] = _unpack(_share_halves(mine), big_shapes)
        small[l] = [g_nmp, g_sinks, g_alog, g_dt, g_dnnorm, g_nmpost, g_nfp, g_fcb, g_nfpost, g_dnw, g_fcw]

    grad_x = dh.reshape(1, S, D)

    small_shapes = [a.shape for a in small[0]]
    flat = jnp.concatenate([a.reshape(-1) for l in range(L) for a in small[l]])
    rows = -(-flat.shape[0] // (SMALL_W * 8)) * 8
    packed = jnp.pad(flat, (0, rows * SMALL_W - flat.shape[0])).reshape(rows, SMALL_W)
    summed = _sum_leading(_gather_all(packed), name="sum_devices")
    per_layer = _unpack(summed, small_shapes * L)
    red = [jnp.stack([per_layer[l * len(small_shapes) + i] for l in range(L)]) for i in range(len(small_shapes))]
    chip = 2 * lax.axis_index("x") + lax.axis_index("y")
    g_dn_conv = lax.dynamic_slice_in_dim(red[9], chip * dn_conv_w.shape[2], dn_conv_w.shape[2], axis=2)
    g_ffn_conv = lax.dynamic_slice_in_dim(red[10], chip * ffn_conv_w.shape[2], ffn_conv_w.shape[2], axis=2)

    grads = dict(
        norm_mix_pre=red[0], w_in=jnp.stack([big_grads[l][0] for l in range(L)]), attn_sinks=red[1],
        dn_conv_w=g_dn_conv, dn_a_log=red[2], dn_dt_bias=red[3], dn_norm_w=red[4],
        w_branch_attn=jnp.stack([big_grads[l][1] for l in range(L)]),
        w_branch_dn=jnp.stack([big_grads[l][2] for l in range(L)]),
        w_out=jnp.stack([big_grads[l][3] for l in range(L)]), norm_mix_post=red[5], norm_ffn_pre=red[6],
        w_ffn_up=jnp.stack([big_grads[l][4] for l in range(L)]), ffn_conv_w=g_ffn_conv, ffn_conv_b=red[7],
        w_ffn_down=jnp.stack([big_grads[l][5] for l in range(L)]), norm_ffn_post=red[8])

    params = dict(
        norm_mix_pre=(norm_mix_pre, m_norm_mix_pre, v_norm_mix_pre), w_in=(w_in, m_w_in, v_w_in),
        attn_sinks=(attn_sinks, m_attn_sinks, v_attn_sinks), dn_conv_w=(dn_conv_w, m_dn_conv_w, v_dn_conv_w),
        dn_a_log=(dn_a_log, m_dn_a_log, v_dn_a_log), dn_dt_bias=(dn_dt_bias, m_dn_dt_bias, v_dn_dt_bias),
        dn_norm_w=(dn_norm_w, m_dn_norm_w, v_dn_norm_w),
        w_branch_attn=(w_branch_attn, m_w_branch_attn, v_w_branch_attn),
        w_branch_dn=(w_branch_dn, m_w_branch_dn, v_w_branch_dn), w_out=(w_out, m_w_out, v_w_out),
        norm_mix_post=(norm_mix_post, m_norm_mix_post, v_norm_mix_post),
        norm_ffn_pre=(norm_ffn_pre, m_norm_ffn_pre, v_norm_ffn_pre), w_ffn_up=(w_ffn_up, m_w_ffn_up, v_w_ffn_up),
        ffn_conv_w=(ffn_conv_w, m_ffn_conv_w, v_ffn_conv_w), ffn_conv_b=(ffn_conv_b, m_ffn_conv_b, v_ffn_conv_b),
        w_ffn_down=(w_ffn_down, m_w_ffn_down, v_w_ffn_down),
        norm_ffn_post=(norm_ffn_post, m_norm_ffn_post, v_norm_ffn_post))
    names = list(params)

    delta, new_m, new_v = {}, {}, {}
    small_names = [n for n in names if params[n][0].ndim == 2]
    for n in names:
        if n in small_names:
            continue
        w, m, v = params[n]
        two_d = (w.shape[0] * w.shape[1], w.shape[2])
        outs = _adamw(w.reshape(two_d), grads[n].reshape(two_d), m.reshape(two_d), v.reshape(two_d), name="adamw_" + n)
        delta[n], new_m[n], new_v[n] = (o.reshape(w.shape) for o in outs)

    def pack_small(arrs):
        fl = jnp.concatenate([a.reshape(-1) for a in arrs])
        r = -(-fl.shape[0] // (SMALL_W * 8)) * 8
        return jnp.pad(fl, (0, r * SMALL_W - fl.shape[0])).reshape(r, SMALL_W)

    outs = _adamw(pack_small([params[n][0] for n in small_names]), pack_small([grads[n] for n in small_names]),
                  pack_small([params[n][1] for n in small_names]), pack_small([params[n][2] for n in small_names]),
                  name="adamw_small")
    shapes = [params[n][0].shape for n in small_names]
    for dst, o in zip((delta, new_m, new_v), outs):
        for n, a in zip(small_names, _unpack(o, shapes)):
            dst[n] = a

    return (loss, grad_x, *[grads[n] for n in names], *[delta[n] for n in names],
            *[new_m[n] for n in names], *[new_v[n] for n in names])
```

```python
import functools
import math

import jax
import jax.numpy as jnp
from jax import lax
from jax.experimental import pallas as pl
from jax.experimental.pallas import tpu as pltpu

F32, BF16 = jnp.float32, jnp.bfloat16
EPS = 1e-6
HEAD = 128
ATTN_BLOCK = 128
ROT_HALF = 16
ROPE_THETA = 500000.0
CHUNK = 64
DN_CONV, FFN_CONV = 4, 3
GDN_HEADS_PER_STEP = 4
MM_TILES = (512, 1024, 3968)
HALO = 8
LANES = 128
SUM_TR = 512
SHARD_PAD = 128
SMALL_W = 1024
ADAM_LR, ADAM_B1, ADAM_B2, ADAM_EPS, ADAM_WD, ADAM_STEP = 0.001, 0.9, 0.999, 1e-08, 0.01, 10
VMEM_LIMIT = 56 * 1024 * 1024
MESH = pl.DeviceIdType.MESH
ANY = pl.BlockSpec(memory_space=pl.ANY)


def _cp(*sem):
    return pltpu.CompilerParams(dimension_semantics=sem, vmem_limit_bytes=VMEM_LIMIT)


def _tile(n, pref, unit=LANES):
    if n <= pref:
        return n
    t = (pref // unit) * unit
    while t >= unit:
        if n % t == 0:
            return t
        t -= unit
    raise ValueError((n, pref, unit))


def _sigmoid(x):
    return 1.0 / (1.0 + jnp.exp(-x))


def _softplus(x):
    return jnp.maximum(x, 0.0) + jnp.log(1.0 + jnp.exp(-jnp.abs(x)))


def _dsilu(y):
    s = _sigmoid(y)
    return s * (1.0 + y * (1.0 - s))


def _mm(a, b, *, ta=False, tb=False, add=None, out_dtype=F32, name, tiles=MM_TILES):
    if ta:
        kdim, m = a.shape
    else:
        m, kdim = a.shape
    if tb:
        n, k2 = b.shape
    else:
        k2, n = b.shape
    assert kdim == k2, (a.shape, b.shape, ta, tb)
    tm, tn, tk = _tile(m, tiles[0]), _tile(n, tiles[1]), _tile(kdim, tiles[2])
    nk = kdim // tk
    dims = (((0 if ta else 1,), (1 if tb else 0,)), ((), ()))

    def body(*refs):
        if add is None:
            a_ref, b_ref, o_ref, acc_ref = refs
        else:
            a_ref, b_ref, add_ref, o_ref, acc_ref = refs
        k = pl.program_id(2)

        @pl.when(k == 0)
        def _():
            acc_ref[...] = jnp.zeros_like(acc_ref)

        acc_ref[...] += lax.dot_general(a_ref[...].astype(BF16), b_ref[...].astype(BF16), dims,
                                        preferred_element_type=F32)

        @pl.when(k == nk - 1)
        def _():
            r = acc_ref[...]
            if add is not None:
                r = r + add_ref[...]
            o_ref[...] = r.astype(o_ref.dtype)

    a_spec = pl.BlockSpec((tk, tm), lambda i, j, k: (k, i)) if ta else pl.BlockSpec((tm, tk), lambda i, j, k: (i, k))
    b_spec = pl.BlockSpec((tn, tk), lambda i, j, k: (j, k)) if tb else pl.BlockSpec((tk, tn), lambda i, j, k: (k, j))
    o_spec = pl.BlockSpec((tm, tn), lambda i, j, k: (i, j))
    ins, specs = [a, b], [a_spec, b_spec]
    if add is not None:
        ins.append(add)
        specs.append(o_spec)
    return pl.pallas_call(
        body, grid=(m // tm, n // tn, nk), in_specs=specs, out_specs=o_spec,
        out_shape=jax.ShapeDtypeStruct((m, n), out_dtype),
        scratch_shapes=[pltpu.VMEM((tm, tn), F32)],
        compiler_params=_cp("parallel", "parallel", "arbitrary"), name=name,
    )(*ins)


def _rms_fwd(x, w, res=None, *, out_dtype, name):
    s, d = x.shape
    tr = _tile(s, 256, 8)

    def body(*refs):
        if res is None:
            x_ref, w_ref, o_ref = refs
        else:
            x_ref, w_ref, r_ref, o_ref = refs
        xv = x_ref[...]
        y = xv * lax.rsqrt(jnp.mean(xv * xv, axis=-1, keepdims=True) + EPS) * w_ref[...]
        if res is not None:
            y = y + r_ref[...]
        o_ref[...] = y.astype(o_ref.dtype)

    row = pl.BlockSpec((tr, d), lambda i: (i, 0))
    par = pl.BlockSpec((1, d), lambda i: (0, 0))
    ins, specs = [x, w.reshape(1, d)], [row, par]
    if res is not None:
        ins.append(res)
        specs.append(row)
    return pl.pallas_call(body, grid=(s // tr,), in_specs=specs, out_specs=row,
                          out_shape=jax.ShapeDtypeStruct((s, d), out_dtype),
                          compiler_params=_cp("parallel"), name=name)(*ins)


def _rms_bwd(x, w, dy, add=None, *, out_dtype=F32, name):
    s, d = x.shape
    tr = _tile(s, 256, 8)

    def body(*refs):
        if add is None:
            x_ref, w_ref, dy_ref, dx_ref, dw_ref = refs
        else:
            x_ref, w_ref, dy_ref, add_ref, dx_ref, dw_ref = refs
        xv = x_ref[...]
        dyv = dy_ref[...].astype(F32)
        r = lax.rsqrt(jnp.mean(xv * xv, axis=-1, keepdims=True) + EPS)
        xh = xv * r
        dyw = dyv * w_ref[...]
        dx = r * (dyw - xh * jnp.mean(dyw * xh, axis=-1, keepdims=True))
        if add is not None:
            dx = dx + add_ref[...]
        dx_ref[...] = dx.astype(dx_ref.dtype)

        @pl.when(pl.program_id(0) == 0)
        def _():
            dw_ref[...] = jnp.zeros_like(dw_ref)

        dw_ref[...] += jnp.sum(dyv * xh, axis=0, keepdims=True)

    row = pl.BlockSpec((tr, d), lambda i: (i, 0))
    par = pl.BlockSpec((1, d), lambda i: (0, 0))
    ins, specs = [x, w.reshape(1, d), dy], [row, par, row]
    if add is not None:
        ins.append(add)
        specs.append(row)
    dx, dw = pl.pallas_call(
        body, grid=(s // tr,), in_specs=specs, out_specs=(row, par),
        out_shape=(jax.ShapeDtypeStruct((s, d), out_dtype), jax.ShapeDtypeStruct((1, d), F32)),
        compiler_params=_cp("arbitrary"), name=name)(*ins)
    return dx, dw.reshape(d)


def _loss_head(h, tgt):
    s, d = h.shape
    tr = _tile(s, 256, 8)

    def body(h_ref, t_ref, dh_ref, l_ref):
        e = h_ref[...] - t_ref[...]
        dh_ref[...] = e * (1.0 / d)

        @pl.when(pl.program_id(0) == 0)
        def _():
            l_ref[...] = jnp.zeros_like(l_ref)

        l_ref[...] += 0.5 * jnp.sum(jnp.mean(e * e, axis=-1, keepdims=True), axis=0, keepdims=True)

    row = pl.BlockSpec((tr, d), lambda i: (i, 0))
    dh, l = pl.pallas_call(
        body, grid=(s // tr,), in_specs=[row, row], out_specs=(row, pl.BlockSpec((1, 1), lambda i: (0, 0))),
        out_shape=(jax.ShapeDtypeStruct((s, d), F32), jax.ShapeDtypeStruct((1, 1), F32)),
        compiler_params=_cp("arbitrary"), name="loss_head")(h, tgt)
    return l[0, 0], dh


def _conv_rows(xe, w, width):
    y = None
    for j in range(width):
        s = width - 1 - j
        term = (pltpu.roll(xe, s, 0) if s else xe)[HALO:] * w[j:j + 1, :]
        y = term if y is None else y + term
    return y


def _conv_rows_t(dy_ext, w, width, tr):
    n = dy_ext.shape[0]
    dx = None
    for j in range(width):
        s = width - 1 - j
        term = (pltpu.roll(dy_ext, n - s, 0) if s else dy_ext)[:tr] * w[j:j + 1, :]
        dx = term if dx is None else dx + term
    return dx


def _conv_dw(dy, xe, width):
    rows = []
    for j in range(width):
        s = width - 1 - j
        rows.append(jnp.sum(dy * (pltpu.roll(xe, s, 0) if s else xe)[HALO:], axis=0, keepdims=True))
    return jnp.concatenate(rows, axis=0)


def _halo_specs(tr, tc, coff, nrow_blocks):
    per = tr // HALO
    last = nrow_blocks * per - 1
    cur = pl.BlockSpec((tr, tc), lambda j, r: (r, coff + j))
    prev = pl.BlockSpec((HALO, tc), lambda j, r: (jnp.maximum(r * per - 1, 0), coff + j))
    nxt = pl.BlockSpec((HALO, tc), lambda j, r: (jnp.minimum((r + 1) * per, last), coff + j))
    return cur, prev, nxt


def _dnconv_fwd(proj, w, cfg):
    s, width = cfg.S, 3 * cfg.HD * HEAD
    tc = _tile(math.gcd(cfg.QD, width), 512)
    tr = _tile(s, 512, 8)
    nr = s // tr
    cur, prev, _ = _halo_specs(tr, tc, cfg.QD // tc, nr)

    def body(x_ref, xp_ref, w_ref, o_ref):
        hist = jnp.where(pl.program_id(1) > 0, xp_ref[...], 0.0)
        y = _conv_rows(jnp.concatenate([hist, x_ref[...]], axis=0), w_ref[...], DN_CONV)
        o_ref[...] = y * _sigmoid(y)

    return pl.pallas_call(
        body, grid=(width // tc, nr),
        in_specs=[cur, prev, pl.BlockSpec((DN_CONV, tc), lambda j, r: (0, j))],
        out_specs=pl.BlockSpec((tr, tc), lambda j, r: (r, j)),
        out_shape=jax.ShapeDtypeStruct((s, width), F32),
        compiler_params=_cp("parallel", "parallel"), name="dnconv_fwd")(proj, proj, w)


def _dnconv_bwd(proj, du3, w, cfg):
    s, hw = cfg.S, cfg.HD * HEAD
    width = 3 * hw
    tc = _tile(math.gcd(cfg.QD, hw), 512)
    tr = _tile(s, 512, 8)
    nr = s // tr
    per, last = tr // HALO, s // HALO - 1
    cur, prev, nxt = _halo_specs(tr, tc, cfg.QD // tc, nr)
    pc = hw // tc
    du_cur = pl.BlockSpec((1, tr, tc), lambda j, r: (j // pc, r, j % pc))
    du_nxt = pl.BlockSpec((1, HALO, tc), lambda j, r: (j // pc, jnp.minimum((r + 1) * per, last), j % pc))

    def body(x_ref, xp_ref, xn_ref, du_ref, dun_ref, w_ref, dx_ref, dw_ref):
        r = pl.program_id(1)
        wv = w_ref[...]
        hist = jnp.where(r > 0, xp_ref[...], 0.0)
        xe = jnp.concatenate([hist, x_ref[...]], axis=0)
        y_ext = _conv_rows(jnp.concatenate([xe, xn_ref[...]], axis=0), wv, DN_CONV)
        du_ext = jnp.concatenate([du_ref[0], jnp.where(r < nr - 1, dun_ref[0], 0.0)], axis=0)
        dy_ext = du_ext * _dsilu(y_ext)
        dx_ref[...] = _conv_rows_t(dy_ext, wv, DN_CONV, tr).astype(dx_ref.dtype)

        @pl.when(r == 0)
        def _():
            dw_ref[...] = jnp.zeros_like(dw_ref)

        dw_ref[...] += _conv_dw(dy_ext[:tr], xe, DN_CONV)

    wspec = pl.BlockSpec((DN_CONV, tc), lambda j, r: (0, j))
    return pl.pallas_call(
        body, grid=(width // tc, nr),
        in_specs=[cur, prev, nxt, du_cur, du_nxt, wspec],
        out_specs=(pl.BlockSpec((tr, tc), lambda j, r: (r, j)), wspec),
        out_shape=(jax.ShapeDtypeStruct((s, width), BF16), jax.ShapeDtypeStruct((DN_CONV, width), F32)),
        compiler_params=_cp("parallel", "arbitrary"), name="dnconv_bwd")(proj, proj, proj, du3, du3, w)


def _glu_fwd(up, w, b, cfg):
    s, ff = cfg.S, cfg.FF
    tc = _tile(ff, 512)
    tr = _tile(s, 512, 8)
    nr, nc = s // tr, ff // tc
    g_cur, g_prev, _ = _halo_specs(tr, tc, 0, nr)
    v_cur, v_prev, _ = _halo_specs(tr, tc, nc, nr)

    def wspec(rows, off):
        return pl.BlockSpec((rows, tc), lambda j, r: (0, off + j))

    def body(g_ref, gp_ref, v_ref, vp_ref, wg_ref, wv_ref, bg_ref, bv_ref, o_ref):
        first = pl.program_id(1) > 0
        ug = _conv_rows(jnp.concatenate([jnp.where(first, gp_ref[...], 0.0), g_ref[...]], axis=0), wg_ref[...], FFN_CONV) + bg_ref[...]
        uv = _conv_rows(jnp.concatenate([jnp.where(first, vp_ref[...], 0.0), v_ref[...]], axis=0), wv_ref[...], FFN_CONV) + bv_ref[...]
        o_ref[...] = (ug * _sigmoid(ug) * uv).astype(o_ref.dtype)

    b2 = b.reshape(1, 2 * ff)
    return pl.pallas_call(
        body, grid=(nc, nr),
        in_specs=[g_cur, g_prev, v_cur, v_prev, wspec(FFN_CONV, 0), wspec(FFN_CONV, nc), wspec(1, 0), wspec(1, nc)],
        out_specs=pl.BlockSpec((tr, tc), lambda j, r: (r, j)),
        out_shape=jax.ShapeDtypeStruct((s, ff), BF16),
        compiler_params=_cp("parallel", "parallel"), name="glu_fwd")(up, up, up, up, w, w, b2, b2)


def _glu_bwd(up, dact, w, b, cfg):
    s, ff = cfg.S, cfg.FF
    tc = _tile(ff, 512)
    tr = _tile(s, 512, 8)
    nr, nc = s // tr, ff // tc
    g_cur, g_prev, g_nxt = _halo_specs(tr, tc, 0, nr)
    v_cur, v_prev, v_nxt = _halo_specs(tr, tc, nc, nr)
    d_cur, _, d_nxt = _halo_specs(tr, tc, 0, nr)

    def wspec(rows, off):
        return pl.BlockSpec((rows, tc), lambda j, r: (0, off + j))

    def body(g_ref, gp_ref, gn_ref, v_ref, vp_ref, vn_ref, d_ref, dn_ref, wg_ref, wv_ref, bg_ref, bv_ref,
             dup_ref, dw_ref, db_ref):
        r = pl.program_id(1)
        wg, wv = wg_ref[...], wv_ref[...]
        ge = jnp.concatenate([jnp.where(r > 0, gp_ref[...], 0.0), g_ref[...]], axis=0)
        ve = jnp.concatenate([jnp.where(r > 0, vp_ref[...], 0.0), v_ref[...]], axis=0)
        ug = _conv_rows(jnp.concatenate([ge, gn_ref[...]], axis=0), wg, FFN_CONV) + bg_ref[...]
        uv = _conv_rows(jnp.concatenate([ve, vn_ref[...]], axis=0), wv, FFN_CONV) + bv_ref[...]
        da = jnp.concatenate([d_ref[...].astype(F32), jnp.where(r < nr - 1, dn_ref[...].astype(F32), 0.0)], axis=0)
        dug = da * uv * _dsilu(ug)
        duv = da * ug * _sigmoid(ug)
        dup_ref[0] = _conv_rows_t(dug, wg, FFN_CONV, tr).astype(dup_ref.dtype)
        dup_ref[1] = _conv_rows_t(duv, wv, FFN_CONV, tr).astype(dup_ref.dtype)

        @pl.when(r == 0)
        def _():
            dw_ref[...] = jnp.zeros_like(dw_ref)
            db_ref[...] = jnp.zeros_like(db_ref)

        dw_ref[0] += _conv_dw(dug[:tr], ge, FFN_CONV)
        dw_ref[1] += _conv_dw(duv[:tr], ve, FFN_CONV)
        db_ref[0] += jnp.sum(dug[:tr], axis=0, keepdims=True)
        db_ref[1] += jnp.sum(duv[:tr], axis=0, keepdims=True)

    b2 = b.reshape(1, 2 * ff)
    dup, dw, db = pl.pallas_call(
        body, grid=(nc, nr),
        in_specs=[g_cur, g_prev, g_nxt, v_cur, v_prev, v_nxt, d_cur, d_nxt,
                  wspec(FFN_CONV, 0), wspec(FFN_CONV, nc), wspec(1, 0), wspec(1, nc)],
        out_specs=(pl.BlockSpec((2, tr, tc), lambda j, r: (0, r, j)),
                   pl.BlockSpec((2, FFN_CONV, tc), lambda j, r: (0, 0, j)),
                   pl.BlockSpec((2, 1, tc), lambda j, r: (0, 0, j))),
        out_shape=(jax.ShapeDtypeStruct((2, s, ff), BF16), jax.ShapeDtypeStruct((2, FFN_CONV, ff), F32),
                   jax.ShapeDtypeStruct((2, 1, ff), F32)),
        compiler_params=_cp("parallel", "arbitrary"), name="glu_bwd",
    )(up, up, up, up, up, up, dact, dact, w, w, b2, b2)
    dup = jnp.concatenate([dup[0], dup[1]], axis=1)
    dw = jnp.transpose(dw, (1, 0, 2)).reshape(FFN_CONV, 2 * ff)
    return dup, dw, db.reshape(2 * ff)


def _merge_specs(cfg):
    tc = _tile(math.gcd(math.gcd(cfg.GA, cfg.GD), cfg.D), 512)
    tr = _tile(cfg.S, 512, 8)
    ga = pl.BlockSpec((tr, tc), lambda r, j: (r, cfg.GA // tc + j))
    gd = pl.BlockSpec((tr, tc), lambda r, j: (r, cfg.GD // tc + j))
    pd = pl.BlockSpec((tr, tc), lambda r, j: (r, j))
    return (cfg.S // tr, cfg.D // tc), ga, gd, pd


def _merge_fwd(proj, pa, pdn, cfg):
    grid, ga, gd, pd = _merge_specs(cfg)

    def body(ga_ref, gd_ref, pa_ref, pd_ref, y_ref):
        y_ref[...] = (_sigmoid(ga_ref[...]) * pa_ref[...] + _sigmoid(gd_ref[...]) * pd_ref[...]).astype(y_ref.dtype)

    return pl.pallas_call(body, grid=grid, in_specs=[ga, gd, pd, pd], out_specs=pd,
                          out_shape=jax.ShapeDtypeStruct((cfg.S, cfg.D), BF16),
                          compiler_params=_cp("parallel", "parallel"), name="merge_fwd")(proj, proj, pa, pdn)


def _merge_bwd(proj, pa, pdn, dy, cfg):
    grid, ga, gd, pd = _merge_specs(cfg)

    def body(ga_ref, gd_ref, pa_ref, pd_ref, dy_ref, dpa_ref, dpd_ref, dga_ref, dgd_ref):
        dyv = dy_ref[...]
        sa, sd = _sigmoid(ga_ref[...]), _sigmoid(gd_ref[...])
        dpa_ref[...] = (dyv * sa).astype(BF16)
        dpd_ref[...] = (dyv * sd).astype(BF16)
        dga_ref[...] = (dyv * pa_ref[...] * sa * (1.0 - sa)).astype(BF16)
        dgd_ref[...] = (dyv * pd_ref[...] * sd * (1.0 - sd)).astype(BF16)

    out = jax.ShapeDtypeStruct((cfg.S, cfg.D), BF16)
    return pl.pallas_call(body, grid=grid, in_specs=[ga, gd, pd, pd, pd], out_specs=(pd,) * 4,
                          out_shape=(out,) * 4, compiler_params=_cp("parallel", "parallel"),
                          name="merge_bwd")(proj, proj, pa, pdn, dy)


def _rope(x, cos, sin):
    lane = lax.broadcasted_iota(jnp.int32, x.shape, 1)
    swapped = jnp.where(lane < ROT_HALF, pltpu.roll(x, LANES - ROT_HALF, 1), pltpu.roll(x, ROT_HALF, 1))
    return x * cos + swapped * sin


def _rope_t(dx, cos, sin):
    t = dx * sin
    lane = lax.broadcasted_iota(jnp.int32, dx.shape, 1)
    swapped = jnp.where(lane < ROT_HALF, pltpu.roll(t, LANES - ROT_HALF, 1), pltpu.roll(t, ROT_HALF, 1))
    return dx * cos + swapped


def _band_mask(i):
    r = lax.broadcasted_iota(jnp.int32, (ATTN_BLOCK, 2 * ATTN_BLOCK), 0)
    c = lax.broadcasted_iota(jnp.int32, (ATTN_BLOCK, 2 * ATTN_BLOCK), 1)
    dist = r + ATTN_BLOCK - c
    return (dist >= 0) & (dist < ATTN_BLOCK) & ((c >= ATTN_BLOCK) | (i > 0))


def _dot(a, b, ca, cb):
    return lax.dot_general(a.astype(BF16), b.astype(BF16), (((ca,), (cb,)), ((), ())), preferred_element_type=F32)


def _attn_probs(qg, k_cat, sink, valid):
    logits = jnp.where(valid, _dot(qg, k_cat, 1, 1) * (HEAD ** -0.5), -1e30)
    m = jnp.maximum(jnp.max(logits, axis=-1, keepdims=True), sink)
    p = jnp.exp(logits - m)
    es = jnp.exp(sink - m)
    inv = 1.0 / (jnp.sum(p, axis=-1, keepdims=True) + es)
    return p * inv, es * inv


def _attn_specs(cfg, nb, clamp):
    gw = cfg.G * HEAD
    qi = (lambda kv, i: (jnp.minimum(i, nb - 1), kv)) if clamp else (lambda kv, i: (i, kv))
    ci = (lambda i: jnp.minimum(i, nb - 1)) if clamp else (lambda i: i)
    pi = lambda i: jnp.maximum(ci(i) - 1, 0)
    ko, vo = cfg.KA // HEAD, cfg.VA // HEAD
    blk = (ATTN_BLOCK, HEAD)
    specs = [
        pl.BlockSpec((ATTN_BLOCK, gw), qi),
        pl.BlockSpec(blk, lambda kv, i: (ci(i), ko + kv)), pl.BlockSpec(blk, lambda kv, i: (pi(i), ko + kv)),
        pl.BlockSpec(blk, lambda kv, i: (ci(i), vo + kv)), pl.BlockSpec(blk, lambda kv, i: (pi(i), vo + kv)),
        pl.BlockSpec(blk, lambda kv, i: (ci(i), 0)), pl.BlockSpec(blk, lambda kv, i: (pi(i), 0)),
        pl.BlockSpec(blk, lambda kv, i: (ci(i), 0)), pl.BlockSpec(blk, lambda kv, i: (pi(i), 0)),
        pl.BlockSpec((1, 8, LANES), lambda kv, i: (kv, 0, 0)),
    ]
    return specs, pl.BlockSpec((ATTN_BLOCK, gw), qi)


def _sink_rows(sinks, cfg):
    sk = jnp.broadcast_to(sinks.reshape(cfg.KV, cfg.G, 1), (cfg.KV, cfg.G, LANES))
    return jnp.pad(sk, ((0, 0), (0, 8 - cfg.G), (0, 0)))


def _attn_fwd(proj, cosf, sinf, sinks, cfg):
    nb = cfg.S // ATTN_BLOCK
    specs, qspec = _attn_specs(cfg, nb, False)

    def body(q_ref, kc_ref, kp_ref, vc_ref, vp_ref, cc_ref, cp_ref, sc_ref, sp_ref, sk_ref, o_ref):
        i = pl.program_id(1)
        cc, sc = cc_ref[...], sc_ref[...]
        k_cat = jnp.concatenate([_rope(kp_ref[...], cp_ref[...], sp_ref[...]), _rope(kc_ref[...], cc, sc)], axis=0).astype(BF16)
        v_cat = jnp.concatenate([vp_ref[...], vc_ref[...]], axis=0).astype(BF16)
        valid = _band_mask(i)
        for g in range(cfg.G):
            qg = _rope(q_ref[:, g * HEAD:(g + 1) * HEAD], cc, sc)
            p, _ = _attn_probs(qg, k_cat, sk_ref[0, g:g + 1, 0:1], valid)
            o_ref[:, g * HEAD:(g + 1) * HEAD] = _dot(p, v_cat, 1, 0).astype(o_ref.dtype)

    return pl.pallas_call(
        body, grid=(cfg.KV, nb), in_specs=specs, out_specs=qspec,
        out_shape=jax.ShapeDtypeStruct((cfg.S, cfg.HA * HEAD), BF16),
        compiler_params=_cp("parallel", "parallel"), name="attn_fwd",
    )(proj, proj, proj, proj, proj, cosf, cosf, sinf, sinf, _sink_rows(sinks, cfg))


def _attn_bwd(proj, do, cosf, sinf, sinks, cfg):
    nb = cfg.S // ATTN_BLOCK
    specs, qspec = _attn_specs(cfg, nb, True)
    kv_out = pl.BlockSpec((ATTN_BLOCK, HEAD), lambda kv, i: (jnp.maximum(i - 1, 0), kv))

    def body(q_ref, kc_ref, kp_ref, vc_ref, vp_ref, cc_ref, cp_ref, sc_ref, sp_ref, sk_ref, do_ref,
             dq_ref, dk_ref, dv_ref, dsk_ref, ck_ref, cv_ref):
        i = pl.program_id(1)

        @pl.when(i == 0)
        def _():
            ck_ref[...] = jnp.zeros_like(ck_ref)
            cv_ref[...] = jnp.zeros_like(cv_ref)
            dsk_ref[...] = jnp.zeros_like(dsk_ref)

        @pl.when(i < nb)
        def _():
            cc, sc, cp, sp = cc_ref[...], sc_ref[...], cp_ref[...], sp_ref[...]
            k_cat = jnp.concatenate([_rope(kp_ref[...], cp, sp), _rope(kc_ref[...], cc, sc)], axis=0).astype(BF16)
            v_cat = jnp.concatenate([vp_ref[...], vc_ref[...]], axis=0).astype(BF16)
            valid = _band_mask(i)
            dk_cat = jnp.zeros((2 * ATTN_BLOCK, HEAD), F32)
            dv_cat = jnp.zeros((2 * ATTN_BLOCK, HEAD), F32)
            for g in range(cfg.G):
                sl = slice(g * HEAD, (g + 1) * HEAD)
                qg = _rope(q_ref[:, sl], cc, sc)
                dog = do_ref[:, sl]
                p, ps = _attn_probs(qg, k_cat, sk_ref[0, g:g + 1, 0:1], valid)
                dp = _dot(dog, v_cat, 1, 1)
                delta = jnp.sum(p * dp, axis=-1, keepdims=True)
                ds = p * (dp - delta) * (HEAD ** -0.5)
                dq_ref[:, sl] = _rope_t(_dot(ds, k_cat, 1, 0), cc, sc).astype(dq_ref.dtype)
                dk_cat = dk_cat + _dot(ds, qg, 0, 0)
                dv_cat = dv_cat + _dot(p, dog, 0, 0)
                dsk_ref[0, g:g + 1, :] += jnp.broadcast_to(-jnp.sum(ps * delta, axis=0, keepdims=True), (1, LANES))
            dk_ref[...] = (ck_ref[...] + _rope_t(dk_cat[:ATTN_BLOCK], cp, sp)).astype(dk_ref.dtype)
            dv_ref[...] = (cv_ref[...] + dv_cat[:ATTN_BLOCK]).astype(dv_ref.dtype)
            ck_ref[...] = _rope_t(dk_cat[ATTN_BLOCK:], cc, sc)
            cv_ref[...] = dv_cat[ATTN_BLOCK:]

        @pl.when(i == nb)
        def _():
            dk_ref[...] = ck_ref[...].astype(dk_ref.dtype)
            dv_ref[...] = cv_ref[...].astype(dv_ref.dtype)

    kvw = cfg.KV * HEAD
    dq, dk, dv, dsk = pl.pallas_call(
        body, grid=(cfg.KV, nb + 1), in_specs=specs + [qspec],
        out_specs=(qspec, kv_out, kv_out, pl.BlockSpec((1, 8, LANES), lambda kv, i: (kv, 0, 0))),
        out_shape=(jax.ShapeDtypeStruct((cfg.S, cfg.HA * HEAD), BF16), jax.ShapeDtypeStruct((cfg.S, kvw), BF16),
                   jax.ShapeDtypeStruct((cfg.S, kvw), BF16), jax.ShapeDtypeStruct((cfg.KV, 8, LANES), F32)),
        scratch_shapes=[pltpu.VMEM((ATTN_BLOCK, HEAD), F32), pltpu.VMEM((ATTN_BLOCK, HEAD), F32)],
        compiler_params=_cp("parallel", "arbitrary"), name="attn_bwd",
    )(proj, proj, proj, proj, proj, cosf, cosf, sinf, sinf, _sink_rows(sinks, cfg), do)
    return jnp.concatenate([dq, dk, dv], axis=1), dsk[:, :cfg.G, 0].reshape(cfg.HA)


@jax.custom_vjp
def _nn(a, b):
    return _dot(a, b, 1, 0)


@jax.custom_vjp
def _nt(a, b):
    return _dot(a, b, 1, 1)


@jax.custom_vjp
def _tn(a, b):
    return _dot(a, b, 0, 0)


_nn.defvjp(lambda a, b: (_nn(a, b), (a, b)), lambda r, g: (_nt(g, r[1]), _tn(r[0], g)))
_nt.defvjp(lambda a, b: (_nt(a, b), (a, b)), lambda r, g: (_nn(g, r[1]), _tn(g, r[0])))
_tn.defvjp(lambda a, b: (_tn(a, b), (a, b)), lambda r, g: (_nt(r[1], g), _nn(r[0], g)))


def _dgx(a, b, ca, cb):
    return lax.dot_general(a, b, (((ca,), (cb,)), ((), ())), precision=lax.Precision.HIGH, preferred_element_type=F32)


def _unit_lower_inverse(a):
    ii = lax.broadcasted_iota(jnp.int32, a.shape, 0)
    jj = lax.broadcasted_iota(jnp.int32, a.shape, 1)
    t = jnp.where(ii == jj, 1.0, 0.0) - a
    p = a
    for _ in range(int(math.log2(CHUNK)) - 1):
        p = _dgx(p, p, 1, 0)
        t = t + _dgx(t, p, 1, 0)
    return t


@jax.custom_vjp
def _tri_inv(a):
    return _unit_lower_inverse(a)


def _tri_inv_fwd(a):
    t = _unit_lower_inverse(a)
    return t, t


def _tri_inv_bwd(t, g):
    return (-_dgx(t, _dgx(g, t, 1, 1), 0, 0),)


_tri_inv.defvjp(_tri_inv_fwd, _tri_inv_bwd)


def _gdn_chunk(q, k, v, z, b_col, a_col, a_row, al_col, dt_col, al_row, dt_row, nw, s):
    rows = q.shape[0]
    hb = rows // CHUNK
    ii = lax.broadcasted_iota(jnp.int32, (rows, rows), 0)
    jj = lax.broadcasted_iota(jnp.int32, (rows, rows), 1)
    same = (ii // CHUNK) == (jj // CHUNK)
    tri, strict, upper = same & (ii >= jj), same & (ii > jj), same & (ii <= jj)
    beta = _sigmoid(b_col)
    g_col = -jnp.exp(al_col) * _softplus(a_col + dt_col)
    g_row = -jnp.exp(al_row) * _softplus(a_row + dt_row)
    gc_col = jnp.sum(jnp.where(tri, g_row, 0.0), axis=1, keepdims=True)
    gc_row = jnp.sum(jnp.where(upper, g_col, 0.0), axis=0, keepdims=True)
    last = jj == (ii // CHUNK) * CHUNK + (CHUNK - 1)
    gl_col = jnp.sum(jnp.where(last, gc_row, 0.0), axis=1, keepdims=True)
    decay = jnp.where(tri, jnp.exp(jnp.where(tri, gc_col - gc_row, 0.0)), 0.0)
    qn = q * lax.rsqrt(jnp.sum(q * q, axis=-1, keepdims=True) + EPS) * (HEAD ** -0.5)
    kn = k * lax.rsqrt(jnp.sum(k * k, axis=-1, keepdims=True) + EPS)
    t_inv = _tri_inv(jnp.where(strict, beta * _nt(kn, kn) * decay, 0.0))
    eg = jnp.exp(gc_col)
    u = _nn(t_inv, v * beta)
    w = _nn(t_inv, kn * (beta * eg))
    qk = jnp.where(tri, _nt(qn, kn) * decay, 0.0)
    ri = lax.broadcasted_iota(jnp.int32, (rows, hb * HEAD), 0)
    ci = lax.broadcasted_iota(jnp.int32, (rows, hb * HEAD), 1)
    own = (ri // CHUNK) == (ci // HEAD)

    def spread(x):
        return jnp.where(own, jnp.concatenate([x] * hb, axis=1), 0.0)

    v_new = u - _nn(spread(w), s)
    o = _nn(spread(qn * eg), s) + _nn(qk, v_new)
    si = lax.broadcasted_iota(jnp.int32, (hb * HEAD, rows), 0)
    sj = lax.broadcasted_iota(jnp.int32, (hb * HEAD, rows), 1)
    gl_s = jnp.sum(jnp.where(sj == (si // HEAD) * CHUNK + (CHUNK - 1), gc_row, 0.0), axis=1, keepdims=True)
    s1 = s * jnp.exp(gl_s) + _tn(spread(kn * jnp.exp(gl_col - gc_col)), v_new)
    od = o * lax.rsqrt(jnp.mean(o * o, axis=-1, keepdims=True) + EPS) * nw * (z * _sigmoid(z))
    return od, s1


def _gdn_heads_per_step(cfg):
    for hb in (GDN_HEADS_PER_STEP, 2, 1):
        if cfg.HD % hb == 0 and cfg.Z % (hb * HEAD) == 0:
            return hb


def _gdn_in_specs(cfg, nc, rev):
    hb = _gdn_heads_per_step(cfg)
    ng, rows = cfg.HD // hb, hb * CHUNK
    ci = (lambda n: nc - 1 - n) if rev else (lambda n: n)
    zo = cfg.Z // (hb * HEAD)
    blk = (CHUNK, hb * HEAD)
    col = pl.BlockSpec((1, 1, rows, 1), lambda g, n: (g, ci(n), 0, 0))
    row = pl.BlockSpec((1, 1, 1, rows), lambda g, n: (g, ci(n), 0, 0))
    hcol = pl.BlockSpec((1, rows, 1), lambda g, n: (g, 0, 0))
    hrow = pl.BlockSpec((1, 1, rows), lambda g, n: (g, 0, 0))
    return [
        pl.BlockSpec(blk, lambda g, n: (ci(n), g)), pl.BlockSpec(blk, lambda g, n: (ci(n), ng + g)),
        pl.BlockSpec(blk, lambda g, n: (ci(n), 2 * ng + g)), pl.BlockSpec(blk, lambda g, n: (ci(n), zo + g)),
        col, col, row, hcol, hcol, hrow, hrow, pl.BlockSpec((1, HEAD), lambda g, n: (0, 0)),
    ]


def _gdn_args(qkv, proj, pba, a_log, dt_bias, nw, cfg):
    hb, hd, s = _gdn_heads_per_step(cfg), cfg.HD, cfg.S
    ng, nc, rows = hd // hb, s // CHUNK, hb * CHUNK

    def grouped(x):
        return jnp.transpose(x.reshape(nc, CHUNK, ng, hb), (2, 0, 3, 1)).reshape(ng, nc, rows)

    b, a = grouped(pba[:, :hd]), grouped(pba[:, hd:2 * hd])
    per_row = lambda x: jnp.repeat(x.reshape(ng, hb), CHUNK, axis=1)
    al, dt = per_row(a_log), per_row(dt_bias)
    return (qkv, qkv, qkv, proj, b.reshape(ng, nc, rows, 1), a.reshape(ng, nc, rows, 1), a.reshape(ng, nc, 1, rows),
            al.reshape(ng, rows, 1), dt.reshape(ng, rows, 1), al.reshape(ng, 1, rows), dt.reshape(ng, 1, rows),
            nw.reshape(1, HEAD))


def _stack_heads(x, hb):
    return jnp.concatenate([x[:, i * HEAD:(i + 1) * HEAD] for i in range(hb)], axis=0)


def _gdn_load(refs, hb):
    q, k, v, z, b, a, ar, alc, dtc, alr, dtr, nw = refs
    return (_stack_heads(q[...], hb), _stack_heads(k[...], hb), _stack_heads(v[...], hb), _stack_heads(z[...], hb),
            b[0, 0], a[0, 0], ar[0, 0], alc[0], dtc[0], alr[0], dtr[0], nw[...])


def _gdn_fwd(qkv, proj, pba, a_log, dt_bias, nw, cfg):
    nc, hb = cfg.S // CHUNK, _gdn_heads_per_step(cfg)
    ng = cfg.HD // hb

    def body(*refs):
        o_ref, st_ref, s_ref = refs[12:]

        @pl.when(pl.program_id(1) == 0)
        def _():
            s_ref[...] = jnp.zeros_like(s_ref)

        s0 = s_ref[...]
        st_ref[0, 0] = s0
        od, s1 = _gdn_chunk(*_gdn_load(refs[:12], hb), s0)
        for i in range(hb):
            o_ref[:, i * HEAD:(i + 1) * HEAD] = od[i * CHUNK:(i + 1) * CHUNK].astype(o_ref.dtype)
        s_ref[...] = s1

    return pl.pallas_call(
        body, grid=(ng, nc), in_specs=_gdn_in_specs(cfg, nc, False),
        out_specs=(pl.BlockSpec((CHUNK, hb * HEAD), lambda g, n: (n, g)),
                   pl.BlockSpec((1, 1, hb * HEAD, HEAD), lambda g, n: (g, n, 0, 0))),
        out_shape=(jax.ShapeDtypeStruct((cfg.S, cfg.HD * HEAD), BF16),
                   jax.ShapeDtypeStruct((ng, nc, hb * HEAD, HEAD), F32)),
        scratch_shapes=[pltpu.VMEM((hb * HEAD, HEAD), F32)],
        compiler_params=_cp("parallel", "arbitrary"), name="gdn_fwd",
    )(*_gdn_args(qkv, proj, pba, a_log, dt_bias, nw, cfg))


def _gdn_bwd(qkv, proj, pba, a_log, dt_bias, nw, states, dod, cfg):
    nc, hd, s, hb = cfg.S // CHUNK, cfg.HD, cfg.S, _gdn_heads_per_step(cfg)
    ng, rows = hd // hb, hb * CHUNK
    rv = lambda n: nc - 1 - n
    col = pl.BlockSpec((1, 1, rows, 1), lambda g, n: (g, rv(n), 0, 0))
    row = pl.BlockSpec((1, 1, 1, rows), lambda g, n: (g, rv(n), 0, 0))
    hcol = pl.BlockSpec((1, rows, 1), lambda g, n: (g, 0, 0))
    hrow = pl.BlockSpec((1, 1, rows), lambda g, n: (g, 0, 0))

    def body(*refs):
        st_ref, dod_ref = refs[12:14]
        (dqkv_ref, dz_ref, db_ref, da_ref, dar_ref, dalc_ref, ddtc_ref, dalr_ref, ddtr_ref, dnw_ref, ds_ref) = refs[14:]
        g, n = pl.program_id(0), pl.program_id(1)

        @pl.when(n == 0)
        def _():
            ds_ref[...] = jnp.zeros_like(ds_ref)
            for r in (dalc_ref, ddtc_ref, dalr_ref, ddtr_ref):
                r[...] = jnp.zeros_like(r)

        @pl.when((n == 0) & (g == 0))
        def _():
            dnw_ref[...] = jnp.zeros_like(dnw_ref)

        _, vjp = jax.vjp(_gdn_chunk, *_gdn_load(refs[:12], hb), st_ref[0, 0])
        dq, dk, dv, dz, db, da, dar, dalc, ddtc, dalr, ddtr, dnw, ds0 = vjp(
            (_stack_heads(dod_ref[...].astype(F32), hb), ds_ref[...]))
        for i in range(hb):
            sl, rs = slice(i * HEAD, (i + 1) * HEAD), slice(i * CHUNK, (i + 1) * CHUNK)
            dqkv_ref[0, :, sl], dqkv_ref[1, :, sl], dqkv_ref[2, :, sl] = dq[rs], dk[rs], dv[rs]
            dz_ref[:, sl] = dz[rs].astype(dz_ref.dtype)
        db_ref[0, 0], da_ref[0, 0], dar_ref[0, 0] = db, da, dar
        dalc_ref[0] += dalc
        ddtc_ref[0] += ddtc
        dalr_ref[0] += dalr
        ddtr_ref[0] += ddtr
        dnw_ref[...] += dnw
        ds_ref[...] = ds0

    blk = pl.BlockSpec((CHUNK, hb * HEAD), lambda g, n: (rv(n), g))
    f32 = lambda *sh: jax.ShapeDtypeStruct(sh, F32)
    outs = pl.pallas_call(
        body, grid=(ng, nc),
        in_specs=_gdn_in_specs(cfg, nc, True) + [pl.BlockSpec((1, 1, hb * HEAD, HEAD), lambda g, n: (g, rv(n), 0, 0)), blk],
        out_specs=(pl.BlockSpec((3, CHUNK, hb * HEAD), lambda g, n: (0, rv(n), g)), blk, col, col, row,
                   hcol, hcol, hrow, hrow, pl.BlockSpec((1, HEAD), lambda g, n: (0, 0))),
        out_shape=(f32(3, s, hd * HEAD), jax.ShapeDtypeStruct((s, hd * HEAD), BF16),
                   f32(ng, nc, rows, 1), f32(ng, nc, rows, 1), f32(ng, nc, 1, rows),
                   f32(ng, rows, 1), f32(ng, rows, 1), f32(ng, 1, rows), f32(ng, 1, rows), f32(1, HEAD)),
        scratch_shapes=[pltpu.VMEM((hb * HEAD, HEAD), F32)],
        compiler_params=_cp("arbitrary", "arbitrary"), name="gdn_bwd",
    )(*_gdn_args(qkv, proj, pba, a_log, dt_bias, nw, cfg), states, dod)
    dqkv, dz, db, da, dar, dalc, ddtc, dalr, ddtr, dnw = outs

    def ungrouped(x):
        return jnp.transpose(x.reshape(ng, nc, hb, CHUNK), (1, 3, 0, 2)).reshape(s, hd)

    per_head = lambda c, r: (c.reshape(ng, hb, CHUNK) + r.reshape(ng, hb, CHUNK)).sum(axis=-1).reshape(hd)
    d_b = ungrouped(db.reshape(ng, nc, rows))
    d_a = ungrouped(da.reshape(ng, nc, rows) + dar.reshape(ng, nc, rows))
    dba = jnp.concatenate([d_b, d_a, jnp.zeros((s, LANES - 2 * hd), F32)], axis=1).astype(BF16)
    return dqkv, dz, dba, per_head(dalc, dalr), per_head(ddtc, ddtr), dnw.reshape(HEAD)


def _adamw(w, g, m, v, *, name):
    r, c = w.shape
    gc = g.shape[1]
    tr = _tile(r, max(8, (1 << 18) // c // 8 * 8), 8) if r % 8 == 0 else r
    c1, c2 = 1.0 - ADAM_B1 ** ADAM_STEP, 1.0 - ADAM_B2 ** ADAM_STEP

    def body(w_ref, g_ref, m_ref, v_ref, go_ref, d_ref, nm_ref, nv_ref):
        gv = g_ref[:, :c]
        nm = ADAM_B1 * m_ref[...] + (1.0 - ADAM_B1) * gv
        nv = ADAM_B2 * v_ref[...] + (1.0 - ADAM_B2) * (gv * gv)
        go_ref[...] = gv
        d_ref[...] = -ADAM_LR * ((nm / c1) / (jnp.sqrt(nv / c2) + ADAM_EPS) + ADAM_WD * w_ref[...])
        nm_ref[...] = nm
        nv_ref[...] = nv

    spec = pl.BlockSpec((tr, c), lambda i: (i, 0))
    gspec = pl.BlockSpec((tr, gc), lambda i: (i, 0))
    out = jax.ShapeDtypeStruct((r, c), F32)
    return pl.pallas_call(body, grid=(r // tr,), in_specs=[spec, gspec, spec, spec], out_specs=(spec,) * 4,
                          out_shape=(out,) * 4, compiler_params=_cp("parallel"), name=name)(w, g, m, v)


def _sum_leading(a, *, name):
    n, r, c = a.shape
    tr = _tile(r, SUM_TR, 8)

    def body(a_ref, o_ref):
        acc = a_ref[0].astype(F32)
        for i in range(1, n):
            acc = acc + a_ref[i].astype(F32)
        o_ref[...] = acc

    return pl.pallas_call(body, grid=(r // tr,), in_specs=[pl.BlockSpec((n, tr, c), lambda i: (0, i, 0))],
                          out_specs=pl.BlockSpec((tr, c), lambda i: (i, 0)),
                          out_shape=jax.ShapeDtypeStruct((r, c), F32), compiler_params=_cp("parallel"), name=name)(a)


class _Mat:
    def __init__(self, kind, rows, cols):
        self.kind, self.rows, self.cols = kind, rows, cols
        self.full = (rows, 4 * cols) if kind == "col" else (4 * rows, cols)
        self.half = rows // 2

    def block(self, ref, j, c=None):
        r0 = (0 if self.kind == "col" else j * self.rows) + (0 if c is None else c * self.half)
        rows = pl.ds(_aligned(r0, 16), self.rows if c is None else self.half)
        return ref.at[rows, pl.ds(_aligned(j * self.cols, LANES), self.cols)] if self.kind == "col" else ref.at[rows, :]

    def own_half(self, ref, c):
        return ref.at[pl.ds(_aligned(c * self.half, 16), self.half), :]


def _aligned(v, unit):
    return v if isinstance(v, int) else pl.multiple_of(v, unit)


def _place():
    x, y, c = lax.axis_index("x"), lax.axis_index("y"), lax.axis_index("c")
    chips = [(1 - x, y), (x, 1 - y), (1 - x, 1 - y)]
    return x, y, c, chips


def _chip_index():
    return 2 * lax.axis_index("x") + lax.axis_index("y")


def _remote(src, dst, send_sems, recv_sems, k, to):
    return pltpu.make_async_remote_copy(src_ref=src, dst_ref=dst, send_sem=send_sems.at[k], recv_sem=recv_sems.at[k],
                                        device_id=to, device_id_type=MESH)


def _comm_call(body, ins, out_shapes, n_sems, name):
    return pl.pallas_call(
        body, out_shape=tuple(out_shapes), in_specs=[ANY] * len(ins), out_specs=tuple([ANY] * len(out_shapes)),
        scratch_shapes=[pltpu.SemaphoreType.DMA((n_sems,)), pltpu.SemaphoreType.DMA((n_sems,))],
        compiler_params=pltpu.CompilerParams(has_side_effects=True), name=name)(*ins)


def _gather_weights(own, mats):
    n = len(mats)

    def body(*refs):
        own_refs, full_refs, (send_sems, recv_sems) = refs[:n], refs[n:2 * n], refs[2 * n:]
        x, y, c, chips = _place()
        me = 2 * x + y
        first, passed = [], []
        for k, m in enumerate(mats):
            for j, (cx, cy) in enumerate(chips):
                first.append(_remote(m.own_half(own_refs[k], c), m.block(full_refs[k], me, c), send_sems, recv_sems,
                                     3 * k + j, (cx, cy, c)))
        for cp in first:
            cp.start()
        for k, m in enumerate(mats):
            for j, (cx, cy) in enumerate(chips):
                got = m.block(full_refs[k], 2 * cx + cy, c)
                _remote(got, got, send_sems, recv_sems, 3 * k + j, (cx, cy, c)).wait_recv()
                cp = _remote(got, got, send_sems, recv_sems, 3 * n + 3 * k + j, (x, y, 1 - c))
                cp.start()
                passed.append(cp)
        for k, m in enumerate(mats):
            for j, (cx, cy) in enumerate(chips):
                got = m.block(full_refs[k], 2 * cx + cy, 1 - c)
                _remote(got, got, send_sems, recv_sems, 3 * n + 3 * k + j, (x, y, 1 - c)).wait_recv()
        for cp in first + passed:
            cp.wait_send()

    outs = _comm_call(body, own, [jax.ShapeDtypeStruct(m.full, a.dtype) for m, a in zip(mats, own)], 6 * n,
                      "gather_weights")
    me = _chip_index()
    return [lax.dynamic_update_slice(o, a, (0, me * m.cols) if m.kind == "col" else (me * m.rows, 0))
            for o, a, m in zip(outs, own, mats)]


def _swap_halves(grads, mats):
    def body(*refs):
        n = len(mats)
        g_refs, b_refs, (send_sems, recv_sems) = refs[:n], refs[n:2 * n], refs[2 * n:]
        x, y, c, _ = _place()
        cps, k = [], 0
        for g, b, m in zip(g_refs, b_refs, mats):
            if m.kind == "col":
                cps.append(_remote(m.own_half(g, 1 - c), b, send_sems, recv_sems, k, (x, y, 1 - c)))
                k += 1
            else:
                for j in range(4):
                    cps.append(_remote(m.block(g, j, 1 - c), b.at[j], send_sems, recv_sems, k, (x, y, 1 - c)))
                    k += 1
        for cp in cps:
            cp.start()
        for cp in cps:
            cp.wait()

    shapes = [jax.ShapeDtypeStruct((m.half, 4 * m.cols) if m.kind == "col" else (4, m.half, m.cols), F32) for m in mats]
    return _comm_call(body, grads, shapes, sum(1 if m.kind == "col" else 4 for m in mats), "rs_swap_halves")


def _pair_sum(own, recv, m, cidx):
    tr = _tile(m.half, 256, 16)
    nh = m.half // tr
    if m.kind == "col":
        own_spec = pl.BlockSpec((tr, m.cols), lambda j, i, c: (c[0] * nh + i, j))
        recv_spec = pl.BlockSpec((tr, m.cols), lambda j, i, c: (i, j))
    else:
        own_spec = pl.BlockSpec((tr, m.cols), lambda j, i, c: ((2 * j + c[0]) * nh + i, 0))
        recv_spec = pl.BlockSpec((None, tr, m.cols), lambda j, i, c: (j, i, 0))

    def body(c_ref, a_ref, b_ref, o_ref):
        o_ref[...] = (a_ref[...] + b_ref[...]).astype(o_ref.dtype)

    return pl.pallas_call(
        body,
        grid_spec=pltpu.PrefetchScalarGridSpec(
            num_scalar_prefetch=1, grid=(4, nh), in_specs=[own_spec, recv_spec],
            out_specs=pl.BlockSpec((None, tr, m.cols), lambda j, i, c: (j, i, 0))),
        out_shape=jax.ShapeDtypeStruct((4, m.half, m.cols), BF16),
        compiler_params=_cp("parallel", "parallel"), name="rs_pair_sum")(cidx, own, recv)


def _scatter_chips(sps):
    def body(*refs):
        n = len(sps)
        s_refs, b_refs, (send_sems, recv_sems) = refs[:n], refs[n:2 * n], refs[2 * n:]
        x, y, c, chips = _place()
        me = 2 * x + y
        cps = [_remote(s.at[2 * cx + cy], b.at[me], send_sems, recv_sems, 3 * k + j, (cx, cy, c))
               for k, (s, b) in enumerate(zip(s_refs, b_refs)) for j, (cx, cy) in enumerate(chips)]
        for cp in cps:
            cp.start()
        for k, b in enumerate(b_refs):
            for j, (cx, cy) in enumerate(chips):
                slot = b.at[2 * cx + cy]
                _remote(slot, slot, send_sems, recv_sems, 3 * k + j, (cx, cy, c)).wait_recv()
        for cp in cps:
            cp.wait_send()

    return _comm_call(body, sps, [jax.ShapeDtypeStruct(s.shape, s.dtype) for s in sps], 3 * len(sps), "rs_scatter_chips")


def _sum_chips(got, sp, me):
    _, h, w = got.shape
    tr = _tile(h, 256, 16)

    def body(me_ref, g_ref, s_ref, o_ref):
        acc = None
        for s in range(4):
            term = jnp.where(me_ref[0] == s, s_ref[...], g_ref[s]).astype(F32)
            acc = term if acc is None else acc + term
        o_ref[...] = acc

    return pl.pallas_call(
        body,
        grid_spec=pltpu.PrefetchScalarGridSpec(
            num_scalar_prefetch=1, grid=(h // tr,),
            in_specs=[pl.BlockSpec((4, tr, w), lambda i, me: (0, i, 0)),
                      pl.BlockSpec((None, tr, w), lambda i, me: (me[0], i, 0))],
            out_specs=pl.BlockSpec((tr, w), lambda i, me: (i, 0))),
        out_shape=jax.ShapeDtypeStruct((h, w), F32), compiler_params=_cp("parallel"), name="rs_sum_chips")(me, got, sp)


def _share_halves(ts):
    def body(*refs):
        n = len(ts)
        t_refs, f_refs, (send_sems, recv_sems) = refs[:n], refs[n:2 * n], refs[2 * n:]
        x, y, c, _ = _place()
        cps = []
        for k, (t, f) in enumerate(zip(t_refs, f_refs)):
            h = t.shape[0]
            cps.append(_remote(t, f.at[pl.ds(_aligned(c * h, 16), h), :], send_sems, recv_sems, k, (x, y, 1 - c)))
        for cp in cps:
            cp.start()
        for k, (t, f) in enumerate(zip(t_refs, f_refs)):
            h = t.shape[0]
            _remote(t, f.at[pl.ds(_aligned((1 - c) * h, 16), h), :], send_sems, recv_sems, k, (x, y, 1 - c)).wait_recv()
        for cp in cps:
            cp.wait_send()

    outs = _comm_call(body, ts, [jax.ShapeDtypeStruct((2 * t.shape[0], t.shape[1]), F32) for t in ts], len(ts),
                      "rs_share_halves")
    c = lax.axis_index("c")
    return [lax.dynamic_update_slice(o, t, (c * t.shape[0], 0)) for o, t in zip(outs, ts)]


def _gather_all(p):
    r, w = p.shape
    rel = [(dx, dy, dc) for dx in (0, 1) for dy in (0, 1) for dc in (0, 1)][1:]

    def body(p_ref, g_ref, send_sems, recv_sems):
        x, y, c = lax.axis_index("x"), lax.axis_index("y"), lax.axis_index("c")
        me = 4 * x + 2 * y + c
        g_ref[me] = p_ref[...]
        peers = [(jnp.bitwise_xor(x, dx), jnp.bitwise_xor(y, dy), jnp.bitwise_xor(c, dc)) for dx, dy, dc in rel]
        cps = [_remote(p_ref, g_ref.at[me], send_sems, recv_sems, k, peer) for k, peer in enumerate(peers)]
        for cp in cps:
            cp.start()
        for k, (px, py, pc) in enumerate(peers):
            slot = g_ref.at[4 * px + 2 * py + pc]
            _remote(slot, slot, send_sems, recv_sems, k, (px, py, pc)).wait_recv()
        for cp in cps:
            cp.wait_send()

    return pl.pallas_call(
        body, out_shape=jax.ShapeDtypeStruct((8, r, w), p.dtype),
        in_specs=[pl.BlockSpec(memory_space=pltpu.VMEM)], out_specs=pl.BlockSpec(memory_space=pltpu.VMEM),
        scratch_shapes=[pltpu.SemaphoreType.DMA((7,)), pltpu.SemaphoreType.DMA((7,))],
        compiler_params=pltpu.CompilerParams(has_side_effects=True, vmem_limit_bytes=VMEM_LIMIT),
        name=("gather_all_%dx%d" % (r, w)))(p)


def _pack_small(arrs):
    flat = jnp.concatenate([a.reshape(-1) for a in arrs])
    rows = -(-flat.shape[0] // (SMALL_W * 8)) * 8
    return jnp.pad(flat, (0, rows * SMALL_W - flat.shape[0])).reshape(rows, SMALL_W)


def _unpack(buf, shapes):
    flat = buf.reshape(-1)
    out, off = [], 0
    for sh in shapes:
        n = math.prod(sh)
        out.append(flat[off:off + n].reshape(sh))
        off += n
    return out


class _Cfg:
    def __init__(self, x, w_in, attn_sinks, dn_a_log, w_ffn_up):
        self.S, self.D = x.shape[1], x.shape[2]
        self.L = w_in.shape[0]
        self.HA, self.HD = attn_sinks.shape[1], dn_a_log.shape[1]
        self.IN = 4 * w_in.shape[2]
        kvw = self.IN - self.HA * HEAD - 4 * self.HD * HEAD - 2 * self.HD - 2 * self.D
        self.KV = kvw // (2 * HEAD)
        self.G = self.HA // self.KV
        self.FF = 2 * w_ffn_up.shape[2]
        self.KA = self.HA * HEAD
        self.VA = self.KA + self.KV * HEAD
        self.QD = self.VA + self.KV * HEAD
        self.Z = self.QD + 3 * self.HD * HEAD
        self.GA = self.Z + self.HD * HEAD
        self.GD = self.GA + self.D
        self.NB = self.GD + self.D
        assert self.NB + 2 * self.HD == self.IN and self.G <= 8 and 2 * self.HD <= LANES


def kernel(x, positions, norm_mix_pre, w_in, attn_sinks, dn_conv_w, dn_a_log, dn_dt_bias, dn_norm_w, w_branch_attn, w_branch_dn, w_out, norm_mix_post, norm_ffn_pre, w_ffn_up, ffn_conv_w, ffn_conv_b, w_ffn_down, norm_ffn_post, loss_target, m_norm_mix_pre, m_w_in, m_attn_sinks, m_dn_conv_w, m_dn_a_log, m_dn_dt_bias, m_dn_norm_w, m_w_branch_attn, m_w_branch_dn, m_w_out, m_norm_mix_post, m_norm_ffn_pre, m_w_ffn_up, m_ffn_conv_w, m_ffn_conv_b, m_w_ffn_down, m_norm_ffn_post, v_norm_mix_pre, v_w_in, v_attn_sinks, v_dn_conv_w, v_dn_a_log, v_dn_dt_bias, v_dn_norm_w, v_w_branch_attn, v_w_branch_dn, v_w_out, v_norm_mix_post, v_norm_ffn_pre, v_w_ffn_up, v_ffn_conv_w, v_ffn_conv_b, v_w_ffn_down, v_norm_ffn_post):
    cfg = _Cfg(x, w_in, attn_sinks, dn_a_log, w_ffn_up)
    S, D, L, HD = cfg.S, cfg.D, cfg.L, cfg.HD
    o1 = cfg.Z
    ins = w_in.shape[2]
    insp = -(-ins // SHARD_PAD) * SHARD_PAD
    mats = [_Mat("col", D, insp), _Mat("row", *w_branch_attn.shape[1:]), _Mat("row", *w_branch_dn.shape[1:]),
            _Mat("row", *w_out.shape[1:]), _Mat("col", *w_ffn_up.shape[1:]), _Mat("row", *w_ffn_down.shape[1:])]
    me = _chip_index()

    taps = _gather_all(_pack_small([dn_conv_w, ffn_conv_w]))
    tap_shapes = [dn_conv_w.shape, ffn_conv_w.shape]
    per_chip = [_unpack(taps[2 * j], tap_shapes) for j in range(4)]
    dnw_all = jnp.concatenate([per_chip[j][0] for j in range(4)], axis=2)
    fcw_all = jnp.concatenate([per_chip[j][1] for j in range(4)], axis=2)

    def full_weights(l):
        own = [jnp.pad(w_in[l].astype(BF16), ((0, 0), (0, insp - ins))), w_branch_attn[l].astype(BF16),
               w_branch_dn[l].astype(BF16), w_out[l].astype(BF16), w_ffn_up[l].astype(BF16), w_ffn_down[l].astype(BF16)]
        win_sm, wbra, wbrd, wout, wup, wdown = _gather_weights(own, mats)
        win = jnp.concatenate([win_sm[:, j * insp:j * insp + ins] for j in range(4)], axis=1)
        return dict(
            win_sm=win_sm, wbig=jnp.concatenate([win[:, :o1], win[:, o1 + 2 * HD:]], axis=1),
            wba=jnp.pad(win[:, o1:o1 + 2 * HD], ((0, 0), (0, LANES - 2 * HD))),
            wbra=wbra, wbrd=wbrd, wout=wout, wup=wup, wdown=wdown, dnw=dnw_all[l], fcw=fcw_all[l])

    weights = [full_weights(l) for l in range(L)]

    inv_freq = ROPE_THETA ** (-jnp.arange(0, 2 * ROT_HALF, 2, dtype=F32) / (2 * ROT_HALF))
    ang = positions.reshape(S, 1).astype(F32) * inv_freq
    cosf = jnp.concatenate([jnp.cos(ang), jnp.cos(ang), jnp.ones((S, HEAD - 2 * ROT_HALF), F32)], axis=1)
    sinf = jnp.concatenate([-jnp.sin(ang), jnp.sin(ang), jnp.zeros((S, HEAD - 2 * ROT_HALF), F32)], axis=1)

    h = x.reshape(S, D)
    saved = []
    for l in range(L):
        wt = weights[l]
        t = dict(h0=h)
        t["xn"] = _rms_fwd(h, norm_mix_pre[l], out_dtype=BF16, name="norm_mix_pre")
        t["proj"] = _mm(t["xn"], wt["wbig"], name="proj")
        t["pba"] = _mm(t["xn"], wt["wba"], name="proj_ba")
        t["oa"] = _attn_fwd(t["proj"], cosf, sinf, attn_sinks[l], cfg)
        t["qkv"] = _dnconv_fwd(t["proj"], wt["dnw"], cfg)
        t["od"], t["st"] = _gdn_fwd(t["qkv"], t["proj"], t["pba"], dn_a_log[l], dn_dt_bias[l], dn_norm_w[l], cfg)
        t["pa"] = _mm(t["oa"], wt["wbra"], name="branch_attn")
        t["pd"] = _mm(t["od"], wt["wbrd"], name="branch_dn")
        t["y"] = _merge_fwd(t["proj"], t["pa"], t["pd"], cfg)
        t["mix"] = _mm(t["y"], wt["wout"], name="mix_out")
        t["h1"] = _rms_fwd(t["mix"], norm_mix_post[l], res=h, out_dtype=F32, name="norm_mix_post")
        t["xf"] = _rms_fwd(t["h1"], norm_ffn_pre[l], out_dtype=BF16, name="norm_ffn_pre")
        t["up"] = _mm(t["xf"], wt["wup"], name="ffn_up")
        t["act"] = _glu_fwd(t["up"], wt["fcw"], ffn_conv_b[l], cfg)
        t["f"] = _mm(t["act"], wt["wdown"], name="ffn_down")
        h = _rms_fwd(t["f"], norm_ffn_post[l], res=t["h1"], out_dtype=F32, name="norm_ffn_post")
        saved.append(t)

    loss_local, dh = _loss_head(h, loss_target.reshape(S, D))
    loss = lax.psum(loss_local, ("x", "y", "c"))

    cidx = lax.axis_index("c").astype(jnp.int32).reshape(1)
    midx = me.astype(jnp.int32).reshape(1)
    big_grads, small = [None] * L, [None] * L
    for l in reversed(range(L)):
        wt, t = weights[l], saved[l]
        df, g_nfpost = _rms_bwd(t["f"], norm_ffn_post[l], dh, out_dtype=BF16, name="norm_ffn_post_bwd")
        dact = _mm(df, wt["wdown"], tb=True, name="ffn_down_dx")
        g_wdown = _mm(t["act"], df, ta=True, name="ffn_down_dw")
        dup, g_fcw, g_fcb = _glu_bwd(t["up"], dact, wt["fcw"], ffn_conv_b[l], cfg)
        dxf = _mm(dup, wt["wup"], tb=True, name="ffn_up_dx")
        g_wup = _mm(t["xf"], dup, ta=True, name="ffn_up_dw")
        dh1, g_nfp = _rms_bwd(t["h1"], norm_ffn_pre[l], dxf, add=dh, name="norm_ffn_pre_bwd")
        dmix, g_nmpost = _rms_bwd(t["mix"], norm_mix_post[l], dh1, out_dtype=BF16, name="norm_mix_post_bwd")
        dy = _mm(dmix, wt["wout"], tb=True, name="mix_out_dx")
        g_wout = _mm(t["y"], dmix, ta=True, name="mix_out_dw")
        dpa, dpd, dga, dgd = _merge_bwd(t["proj"], t["pa"], t["pd"], dy, cfg)
        doa = _mm(dpa, wt["wbra"], tb=True, out_dtype=BF16, name="branch_attn_dx")
        g_wbra = _mm(t["oa"], dpa, ta=True, name="branch_attn_dw")
        dod = _mm(dpd, wt["wbrd"], tb=True, out_dtype=BF16, name="branch_dn_dx")
        g_wbrd = _mm(t["od"], dpd, ta=True, name="branch_dn_dw")
        dqkv_c, dz, dba, g_alog, g_dt, g_dnnorm = _gdn_bwd(t["qkv"], t["proj"], t["pba"], dn_a_log[l], dn_dt_bias[l],
                                                            dn_norm_w[l], t["st"], dod, cfg)
        dqkv_d, g_dnw = _dnconv_bwd(t["proj"], dqkv_c, wt["dnw"], cfg)
        dqkv_a, g_sinks = _attn_bwd(t["proj"], doa, cosf, sinf, attn_sinks[l], cfg)
        dproj = jnp.concatenate([dqkv_a, dqkv_d, dba[:, :2 * HD], dz, dga, dgd], axis=1)
        dproj = jnp.pad(dproj.reshape(S, 4, ins), ((0, 0), (0, 0), (0, insp - ins))).reshape(S, 4 * insp)
        dxn = _mm(dproj, wt["win_sm"], tb=True, name="proj_dx")
        g_win = _mm(t["xn"], dproj, ta=True, name="proj_dw")
        dh, g_nmp = _rms_bwd(t["h0"], norm_mix_pre[l], dxn, add=dh1, name="norm_mix_pre_bwd")

        full = [g_win, g_wbra, g_wbrd, g_wout, g_wup, g_wdown]
        theirs = _swap_halves(full, mats)
        pair = [_pair_sum(a, b, m, cidx) for a, b, m in zip(full, theirs, mats)]
        got = _scatter_chips(pair)
        big_grads[l] = _share_halves([_sum_chips(a, b, midx) for a, b in zip(got, pair)])
        small[l] = [g_nmp, g_sinks, g_alog, g_dt, g_dnnorm, g_nmpost, g_nfp, g_fcb, g_nfpost, g_dnw, g_fcw]

    grad_x = dh.reshape(1, S, D)

    small_shapes = [a.shape for a in small[0]]
    summed = _sum_leading(_gather_all(_pack_small([a for l in range(L) for a in small[l]])), name="sum_devices")
    per_layer = _unpack(summed, small_shapes * L)
    red = [jnp.stack([per_layer[l * len(small_shapes) + i] for l in range(L)]) for i in range(len(small_shapes))]
    g_dn_conv = lax.dynamic_slice_in_dim(red[9], me * dn_conv_w.shape[2], dn_conv_w.shape[2], axis=2)
    g_ffn_conv = lax.dynamic_slice_in_dim(red[10], me * ffn_conv_w.shape[2], ffn_conv_w.shape[2], axis=2)
    stacked = lambda i: jnp.stack([big_grads[l][i] for l in range(L)])

    grads = dict(
        norm_mix_pre=red[0], w_in=stacked(0), attn_sinks=red[1], dn_conv_w=g_dn_conv, dn_a_log=red[2],
        dn_dt_bias=red[3], dn_norm_w=red[4], w_branch_attn=stacked(1), w_branch_dn=stacked(2), w_out=stacked(3),
        norm_mix_post=red[5], norm_ffn_pre=red[6], w_ffn_up=stacked(4), ffn_conv_w=g_ffn_conv, ffn_conv_b=red[7],
        w_ffn_down=stacked(5), norm_ffn_post=red[8])

    params = dict(
        norm_mix_pre=(norm_mix_pre, m_norm_mix_pre, v_norm_mix_pre), w_in=(w_in, m_w_in, v_w_in),
        attn_sinks=(attn_sinks, m_attn_sinks, v_attn_sinks), dn_conv_w=(dn_conv_w, m_dn_conv_w, v_dn_conv_w),
        dn_a_log=(dn_a_log, m_dn_a_log, v_dn_a_log), dn_dt_bias=(dn_dt_bias, m_dn_dt_bias, v_dn_dt_bias),
        dn_norm_w=(dn_norm_w, m_dn_norm_w, v_dn_norm_w),
        w_branch_attn=(w_branch_attn, m_w_branch_attn, v_w_branch_attn),
        w_branch_dn=(w_branch_dn, m_w_branch_dn, v_w_branch_dn), w_out=(w_out, m_w_out, v_w_out),
        norm_mix_post=(norm_mix_post, m_norm_mix_post, v_norm_mix_post),
        norm_ffn_pre=(norm_ffn_pre, m_norm_ffn_pre, v_norm_ffn_pre), w_ffn_up=(w_ffn_up, m_w_ffn_up, v_w_ffn_up),
        ffn_conv_w=(ffn_conv_w, m_ffn_conv_w, v_ffn_conv_w), ffn_conv_b=(ffn_conv_b, m_ffn_conv_b, v_ffn_conv_b),
        w_ffn_down=(w_ffn_down, m_w_ffn_down, v_w_ffn_down),
        norm_ffn_post=(norm_ffn_post, m_norm_ffn_post, v_norm_ffn_post))
    names = list(params)

    grad_out, delta, new_m, new_v = {}, {}, {}, {}
    small_names = [n for n in names if params[n][0].ndim == 2]
    for n in names:
        if n in small_names:
            continue
        w, m, v = params[n]
        two_d = (w.shape[0] * w.shape[1], w.shape[2])
        g = grads[n].reshape(two_d[0], grads[n].shape[2])
        outs = _adamw(w.reshape(two_d), g, m.reshape(two_d), v.reshape(two_d), name="adamw_" + n)
        grad_out[n], delta[n], new_m[n], new_v[n] = (o.reshape(w.shape) for o in outs)

    outs = _adamw(_pack_small([params[n][0] for n in small_names]), _pack_small([grads[n] for n in small_names]),
                  _pack_small([params[n][1] for n in small_names]), _pack_small([params[n][2] for n in small_names]),
                  name="adamw_small")
    shapes = [params[n][0].shape for n in small_names]
    for dst, o in zip((grad_out, delta, new_m, new_v), outs):
        for n, a in zip(small_names, _unpack(o, shapes)):
            dst[n] = a

    return (loss, grad_x, *[grad_out[n] for n in names], *[delta[n] for n in names],
            *[new_m[n] for n in names], *[new_v[n] for n in names])
```

```python
import functools
import math

import jax
import jax.numpy as jnp
from jax import lax
from jax.experimental import pallas as pl
from jax.experimental.pallas import tpu as pltpu

F32, BF16 = jnp.float32, jnp.bfloat16
EPS = 1e-6
HEAD = 128
ATTN_BLOCK = 128
ROT_HALF = 16
ROPE_THETA = 500000.0
CHUNK = 64
DN_CONV, FFN_CONV = 4, 3
GDN_HEADS_PER_STEP = 4
MM_TILES = (512, 1024, 3968)
HALO = 8
LANES = 128
SUM_TR = 512
SHARD_PAD = 1024
SMALL_W = 1024
ADAM_LR, ADAM_B1, ADAM_B2, ADAM_EPS, ADAM_WD, ADAM_STEP = 0.001, 0.9, 0.999, 1e-08, 0.01, 10
VMEM_LIMIT = 56 * 1024 * 1024
MESH = pl.DeviceIdType.MESH
ANY = pl.BlockSpec(memory_space=pl.ANY)


def _cp(*sem):
    return pltpu.CompilerParams(dimension_semantics=sem, vmem_limit_bytes=VMEM_LIMIT)


def _tile(n, pref, unit=LANES):
    if n <= pref:
        return n
    t = (pref // unit) * unit
    while t >= unit:
        if n % t == 0:
            return t
        t -= unit
    raise ValueError((n, pref, unit))


def _sigmoid(x):
    return 1.0 / (1.0 + jnp.exp(-x))


def _softplus(x):
    return jnp.maximum(x, 0.0) + jnp.log(1.0 + jnp.exp(-jnp.abs(x)))


def _dsilu(y):
    s = _sigmoid(y)
    return s * (1.0 + y * (1.0 - s))


class _Side:
    def __init__(self, ins, out_shapes, n_sems, start, finish, in_place=False):
        self.ins, self.out_shapes, self.n_sems = list(ins), list(out_shapes), n_sems
        self.start, self.finish, self.in_place = start, finish, in_place


def _side_split(refs, n_in, n_out, side):
    si, so = (len(side.ins), len(side.out_shapes)) if side else (0, 0)
    ins, sins = refs[:n_in], refs[n_in:n_in + si]
    outs, souts = refs[n_in + si:n_in + si + n_out], refs[n_in + si + n_out:n_in + si + n_out + so]
    rest = refs[n_in + si + n_out + so:]
    scratch, sems = (rest[:-2], rest[-2:]) if side else (rest, ())
    return ins, outs, scratch, (sins, souts, sems)


def _side_steps(side, parts, grid):
    if side is None:
        return (lambda: None), (lambda: None)
    sins, souts, sems = parts
    ids = [pl.program_id(a) for a in range(len(grid))]
    first = functools.reduce(jnp.logical_and, [i == 0 for i in ids])
    last = functools.reduce(jnp.logical_and, [i == n - 1 for i, n in zip(ids, grid)])

    def begin():
        @pl.when(first)
        def _():
            side.start(sins, souts, *sems)

    def end():
        @pl.when(last)
        def _():
            side.finish(sins, souts, *sems)

    return begin, end


def _side_call_args(side, n_in, n_out):
    if side is None:
        return [], [], [], [], [], {}
    sems = [pltpu.SemaphoreType.DMA((side.n_sems,)), pltpu.SemaphoreType.DMA((side.n_sems,))]
    aliases = {n_in + k: n_out + k for k in range(len(side.ins))} if side.in_place else {}
    return [ANY] * len(side.ins), [ANY] * len(side.out_shapes), side.out_shapes, sems, side.ins, aliases


def _mm(a, b, *, ta=False, tb=False, add=None, out_dtype=F32, name, tiles=MM_TILES, side=None):
    if ta:
        kdim, m = a.shape
    else:
        m, kdim = a.shape
    if tb:
        n, k2 = b.shape
    else:
        k2, n = b.shape
    assert kdim == k2, (a.shape, b.shape, ta, tb)
    tm, tn, tk = _tile(m, tiles[0]), _tile(n, tiles[1]), _tile(kdim, tiles[2])
    nk = kdim // tk
    grid = (m // tm, n // tn, nk)
    dims = (((0 if ta else 1,), (1 if tb else 0,)), ((), ()))
    n_in = 2 if add is None else 3

    def body(*refs):
        ins, (o_ref,), (acc_ref,), parts = _side_split(refs, n_in, 1, side)
        begin, end = _side_steps(side, parts, grid)
        begin()
        a_ref, b_ref = ins[:2]
        k = pl.program_id(2)

        @pl.when(k == 0)
        def _():
            acc_ref[...] = jnp.zeros_like(acc_ref)

        acc_ref[...] += lax.dot_general(a_ref[...].astype(BF16), b_ref[...].astype(BF16), dims,
                                        preferred_element_type=F32)

        @pl.when(k == nk - 1)
        def _():
            r = acc_ref[...]
            if add is not None:
                r = r + ins[2][...]
            o_ref[...] = r.astype(o_ref.dtype)

        end()

    a_spec = pl.BlockSpec((tk, tm), lambda i, j, k: (k, i)) if ta else pl.BlockSpec((tm, tk), lambda i, j, k: (i, k))
    b_spec = pl.BlockSpec((tn, tk), lambda i, j, k: (j, k)) if tb else pl.BlockSpec((tk, tn), lambda i, j, k: (k, j))
    o_spec = pl.BlockSpec((tm, tn), lambda i, j, k: (i, j))
    ins, specs = [a, b], [a_spec, b_spec]
    if add is not None:
        ins.append(add)
        specs.append(o_spec)
    s_in, s_out, s_shapes, s_sems, s_ops, aliases = _side_call_args(side, n_in, 1)
    sem = ("arbitrary",) * 3 if side else ("parallel", "parallel", "arbitrary")
    outs = pl.pallas_call(
        body, grid=grid, in_specs=specs + s_in, out_specs=(o_spec, *s_out),
        out_shape=(jax.ShapeDtypeStruct((m, n), out_dtype), *s_shapes),
        scratch_shapes=[pltpu.VMEM((tm, tn), F32)] + s_sems, input_output_aliases=aliases,
        compiler_params=_cp(*sem), name=name,
    )(*ins, *s_ops)
    return outs if side else outs[0]


def _rms_fwd(x, w, res=None, *, out_dtype, name):
    s, d = x.shape
    tr = _tile(s, 256, 8)

    def body(*refs):
        if res is None:
            x_ref, w_ref, o_ref = refs
        else:
            x_ref, w_ref, r_ref, o_ref = refs
        xv = x_ref[...]
        y = xv * lax.rsqrt(jnp.mean(xv * xv, axis=-1, keepdims=True) + EPS) * w_ref[...]
        if res is not None:
            y = y + r_ref[...]
        o_ref[...] = y.astype(o_ref.dtype)

    row = pl.BlockSpec((tr, d), lambda i: (i, 0))
    par = pl.BlockSpec((1, d), lambda i: (0, 0))
    ins, specs = [x, w.reshape(1, d)], [row, par]
    if res is not None:
        ins.append(res)
        specs.append(row)
    return pl.pallas_call(body, grid=(s // tr,), in_specs=specs, out_specs=row,
                          out_shape=jax.ShapeDtypeStruct((s, d), out_dtype),
                          compiler_params=_cp("parallel"), name=name)(*ins)


def _rms_bwd(x, w, dy, add=None, *, out_dtype=F32, name):
    s, d = x.shape
    tr = _tile(s, 256, 8)

    def body(*refs):
        if add is None:
            x_ref, w_ref, dy_ref, dx_ref, dw_ref = refs
        else:
            x_ref, w_ref, dy_ref, add_ref, dx_ref, dw_ref = refs
        xv = x_ref[...]
        dyv = dy_ref[...].astype(F32)
        r = lax.rsqrt(jnp.mean(xv * xv, axis=-1, keepdims=True) + EPS)
        xh = xv * r
        dyw = dyv * w_ref[...]
        dx = r * (dyw - xh * jnp.mean(dyw * xh, axis=-1, keepdims=True))
        if add is not None:
            dx = dx + add_ref[...]
        dx_ref[...] = dx.astype(dx_ref.dtype)

        @pl.when(pl.program_id(0) == 0)
        def _():
            dw_ref[...] = jnp.zeros_like(dw_ref)

        dw_ref[...] += jnp.sum(dyv * xh, axis=0, keepdims=True)

    row = pl.BlockSpec((tr, d), lambda i: (i, 0))
    par = pl.BlockSpec((1, d), lambda i: (0, 0))
    ins, specs = [x, w.reshape(1, d), dy], [row, par, row]
    if add is not None:
        ins.append(add)
        specs.append(row)
    dx, dw = pl.pallas_call(
        body, grid=(s // tr,), in_specs=specs, out_specs=(row, par),
        out_shape=(jax.ShapeDtypeStruct((s, d), out_dtype), jax.ShapeDtypeStruct((1, d), F32)),
        compiler_params=_cp("arbitrary"), name=name)(*ins)
    return dx, dw.reshape(d)


def _loss_head(h, tgt):
    s, d = h.shape
    tr = _tile(s, 256, 8)

    def body(h_ref, t_ref, dh_ref, l_ref):
        e = h_ref[...] - t_ref[...]
        dh_ref[...] = e * (1.0 / d)

        @pl.when(pl.program_id(0) == 0)
        def _():
            l_ref[...] = jnp.zeros_like(l_ref)

        l_ref[...] += 0.5 * jnp.sum(jnp.mean(e * e, axis=-1, keepdims=True), axis=0, keepdims=True)

    row = pl.BlockSpec((tr, d), lambda i: (i, 0))
    dh, l = pl.pallas_call(
        body, grid=(s // tr,), in_specs=[row, row], out_specs=(row, pl.BlockSpec((1, 1), lambda i: (0, 0))),
        out_shape=(jax.ShapeDtypeStruct((s, d), F32), jax.ShapeDtypeStruct((1, 1), F32)),
        compiler_params=_cp("arbitrary"), name="loss_head")(h, tgt)
    return l[0, 0], dh


def _conv_rows(xe, w, width):
    y = None
    for j in range(width):
        s = width - 1 - j
        term = (pltpu.roll(xe, s, 0) if s else xe)[HALO:] * w[j:j + 1, :]
        y = term if y is None else y + term
    return y


def _conv_rows_t(dy_ext, w, width, tr):
    n = dy_ext.shape[0]
    dx = None
    for j in range(width):
        s = width - 1 - j
        term = (pltpu.roll(dy_ext, n - s, 0) if s else dy_ext)[:tr] * w[j:j + 1, :]
        dx = term if dx is None else dx + term
    return dx


def _conv_dw(dy, xe, width):
    rows = []
    for j in range(width):
        s = width - 1 - j
        rows.append(jnp.sum(dy * (pltpu.roll(xe, s, 0) if s else xe)[HALO:], axis=0, keepdims=True))
    return jnp.concatenate(rows, axis=0)


def _halo_specs(tr, tc, coff, nrow_blocks):
    per = tr // HALO
    last = nrow_blocks * per - 1
    cur = pl.BlockSpec((tr, tc), lambda j, r: (r, coff + j))
    prev = pl.BlockSpec((HALO, tc), lambda j, r: (jnp.maximum(r * per - 1, 0), coff + j))
    nxt = pl.BlockSpec((HALO, tc), lambda j, r: (jnp.minimum((r + 1) * per, last), coff + j))
    return cur, prev, nxt


def _dnconv_fwd(proj, w, cfg):
    s, width = cfg.S, 3 * cfg.HD * HEAD
    tc = _tile(math.gcd(cfg.QD, width), 512)
    tr = _tile(s, 512, 8)
    nr = s // tr
    cur, prev, _ = _halo_specs(tr, tc, cfg.QD // tc, nr)

    def body(x_ref, xp_ref, w_ref, o_ref):
        hist = jnp.where(pl.program_id(1) > 0, xp_ref[...], 0.0)
        y = _conv_rows(jnp.concatenate([hist, x_ref[...]], axis=0), w_ref[...], DN_CONV)
        o_ref[...] = y * _sigmoid(y)

    return pl.pallas_call(
        body, grid=(width // tc, nr),
        in_specs=[cur, prev, pl.BlockSpec((DN_CONV, tc), lambda j, r: (0, j))],
        out_specs=pl.BlockSpec((tr, tc), lambda j, r: (r, j)),
        out_shape=jax.ShapeDtypeStruct((s, width), F32),
        compiler_params=_cp("parallel", "parallel"), name="dnconv_fwd")(proj, proj, w)


def _dnconv_bwd(proj, du3, w, cfg):
    s, hw = cfg.S, cfg.HD * HEAD
    width = 3 * hw
    tc = _tile(math.gcd(cfg.QD, hw), 512)
    tr = _tile(s, 512, 8)
    nr = s // tr
    per, last = tr // HALO, s // HALO - 1
    cur, prev, nxt = _halo_specs(tr, tc, cfg.QD // tc, nr)
    pc = hw // tc
    du_cur = pl.BlockSpec((1, tr, tc), lambda j, r: (j // pc, r, j % pc))
    du_nxt = pl.BlockSpec((1, HALO, tc), lambda j, r: (j // pc, jnp.minimum((r + 1) * per, last), j % pc))

    def body(x_ref, xp_ref, xn_ref, du_ref, dun_ref, w_ref, dx_ref, dw_ref):
        r = pl.program_id(1)
        wv = w_ref[...]
        hist = jnp.where(r > 0, xp_ref[...], 0.0)
        xe = jnp.concatenate([hist, x_ref[...]], axis=0)
        y_ext = _conv_rows(jnp.concatenate([xe, xn_ref[...]], axis=0), wv, DN_CONV)
        du_ext = jnp.concatenate([du_ref[0], jnp.where(r < nr - 1, dun_ref[0], 0.0)], axis=0)
        dy_ext = du_ext * _dsilu(y_ext)
        dx_ref[...] = _conv_rows_t(dy_ext, wv, DN_CONV, tr).astype(dx_ref.dtype)

        @pl.when(r == 0)
        def _():
            dw_ref[...] = jnp.zeros_like(dw_ref)

        dw_ref[...] += _conv_dw(dy_ext[:tr], xe, DN_CONV)

    wspec = pl.BlockSpec((DN_CONV, tc), lambda j, r: (0, j))
    return pl.pallas_call(
        body, grid=(width // tc, nr),
        in_specs=[cur, prev, nxt, du_cur, du_nxt, wspec],
        out_specs=(pl.BlockSpec((tr, tc), lambda j, r: (r, j)), wspec),
        out_shape=(jax.ShapeDtypeStruct((s, width), BF16), jax.ShapeDtypeStruct((DN_CONV, width), F32)),
        compiler_params=_cp("parallel", "arbitrary"), name="dnconv_bwd")(proj, proj, proj, du3, du3, w)


def _glu_fwd(up, w, b, cfg):
    s, ff = cfg.S, cfg.FF
    tc = _tile(ff, 512)
    tr = _tile(s, 512, 8)
    nr, nc = s // tr, ff // tc
    g_cur, g_prev, _ = _halo_specs(tr, tc, 0, nr)
    v_cur, v_prev, _ = _halo_specs(tr, tc, nc, nr)

    def wspec(rows, off):
        return pl.BlockSpec((rows, tc), lambda j, r: (0, off + j))

    def body(g_ref, gp_ref, v_ref, vp_ref, wg_ref, wv_ref, bg_ref, bv_ref, o_ref):
        first = pl.program_id(1) > 0
        ug = _conv_rows(jnp.concatenate([jnp.where(first, gp_ref[...], 0.0), g_ref[...]], axis=0), wg_ref[...], FFN_CONV) + bg_ref[...]
        uv = _conv_rows(jnp.concatenate([jnp.where(first, vp_ref[...], 0.0), v_ref[...]], axis=0), wv_ref[...], FFN_CONV) + bv_ref[...]
        o_ref[...] = (ug * _sigmoid(ug) * uv).astype(o_ref.dtype)

    b2 = b.reshape(1, 2 * ff)
    return pl.pallas_call(
        body, grid=(nc, nr),
        in_specs=[g_cur, g_prev, v_cur, v_prev, wspec(FFN_CONV, 0), wspec(FFN_CONV, nc), wspec(1, 0), wspec(1, nc)],
        out_specs=pl.BlockSpec((tr, tc), lambda j, r: (r, j)),
        out_shape=jax.ShapeDtypeStruct((s, ff), BF16),
        compiler_params=_cp("parallel", "parallel"), name="glu_fwd")(up, up, up, up, w, w, b2, b2)


def _glu_bwd(up, dact, w, b, cfg):
    s, ff = cfg.S, cfg.FF
    tc = _tile(ff, 512)
    tr = _tile(s, 512, 8)
    nr, nc = s // tr, ff // tc
    g_cur, g_prev, g_nxt = _halo_specs(tr, tc, 0, nr)
    v_cur, v_prev, v_nxt = _halo_specs(tr, tc, nc, nr)
    d_cur, _, d_nxt = _halo_specs(tr, tc, 0, nr)

    def wspec(rows, off):
        return pl.BlockSpec((rows, tc), lambda j, r: (0, off + j))

    def body(g_ref, gp_ref, gn_ref, v_ref, vp_ref, vn_ref, d_ref, dn_ref, wg_ref, wv_ref, bg_ref, bv_ref,
             dup_ref, dw_ref, db_ref):
        r = pl.program_id(1)
        wg, wv = wg_ref[...], wv_ref[...]
        ge = jnp.concatenate([jnp.where(r > 0, gp_ref[...], 0.0), g_ref[...]], axis=0)
        ve = jnp.concatenate([jnp.where(r > 0, vp_ref[...], 0.0), v_ref[...]], axis=0)
        ug = _conv_rows(jnp.concatenate([ge, gn_ref[...]], axis=0), wg, FFN_CONV) + bg_ref[...]
        uv = _conv_rows(jnp.concatenate([ve, vn_ref[...]], axis=0), wv, FFN_CONV) + bv_ref[...]
        da = jnp.concatenate([d_ref[...].astype(F32), jnp.where(r < nr - 1, dn_ref[...].astype(F32), 0.0)], axis=0)
        dug = da * uv * _dsilu(ug)
        duv = da * ug * _sigmoid(ug)
        dup_ref[0] = _conv_rows_t(dug, wg, FFN_CONV, tr).astype(dup_ref.dtype)
        dup_ref[1] = _conv_rows_t(duv, wv, FFN_CONV, tr).astype(dup_ref.dtype)

        @pl.when(r == 0)
        def _():
            dw_ref[...] = jnp.zeros_like(dw_ref)
            db_ref[...] = jnp.zeros_like(db_ref)

        dw_ref[0] += _conv_dw(dug[:tr], ge, FFN_CONV)
        dw_ref[1] += _conv_dw(duv[:tr], ve, FFN_CONV)
        db_ref[0] += jnp.sum(dug[:tr], axis=0, keepdims=True)
        db_ref[1] += jnp.sum(duv[:tr], axis=0, keepdims=True)

    b2 = b.reshape(1, 2 * ff)
    dup, dw, db = pl.pallas_call(
        body, grid=(nc, nr),
        in_specs=[g_cur, g_prev, g_nxt, v_cur, v_prev, v_nxt, d_cur, d_nxt,
                  wspec(FFN_CONV, 0), wspec(FFN_CONV, nc), wspec(1, 0), wspec(1, nc)],
        out_specs=(pl.BlockSpec((2, tr, tc), lambda j, r: (0, r, j)),
                   pl.BlockSpec((2, FFN_CONV, tc), lambda j, r: (0, 0, j)),
                   pl.BlockSpec((2, 1, tc), lambda j, r: (0, 0, j))),
        out_shape=(jax.ShapeDtypeStruct((2, s, ff), BF16), jax.ShapeDtypeStruct((2, FFN_CONV, ff), F32),
                   jax.ShapeDtypeStruct((2, 1, ff), F32)),
        compiler_params=_cp("parallel", "arbitrary"), name="glu_bwd",
    )(up, up, up, up, up, up, dact, dact, w, w, b2, b2)
    dup = jnp.concatenate([dup[0], dup[1]], axis=1)
    dw = jnp.transpose(dw, (1, 0, 2)).reshape(FFN_CONV, 2 * ff)
    return dup, dw, db.reshape(2 * ff)


def _merge_specs(cfg):
    tc = _tile(math.gcd(math.gcd(cfg.GA, cfg.GD), cfg.D), 512)
    tr = _tile(cfg.S, 512, 8)
    ga = pl.BlockSpec((tr, tc), lambda r, j: (r, cfg.GA // tc + j))
    gd = pl.BlockSpec((tr, tc), lambda r, j: (r, cfg.GD // tc + j))
    pd = pl.BlockSpec((tr, tc), lambda r, j: (r, j))
    return (cfg.S // tr, cfg.D // tc), ga, gd, pd


def _merge_fwd(proj, pa, pdn, cfg):
    grid, ga, gd, pd = _merge_specs(cfg)

    def body(ga_ref, gd_ref, pa_ref, pd_ref, y_ref):
        y_ref[...] = (_sigmoid(ga_ref[...]) * pa_ref[...] + _sigmoid(gd_ref[...]) * pd_ref[...]).astype(y_ref.dtype)

    return pl.pallas_call(body, grid=grid, in_specs=[ga, gd, pd, pd], out_specs=pd,
                          out_shape=jax.ShapeDtypeStruct((cfg.S, cfg.D), BF16),
                          compiler_params=_cp("parallel", "parallel"), name="merge_fwd")(proj, proj, pa, pdn)


def _merge_bwd(proj, pa, pdn, dy, cfg):
    grid, ga, gd, pd = _merge_specs(cfg)

    def body(ga_ref, gd_ref, pa_ref, pd_ref, dy_ref, dpa_ref, dpd_ref, dga_ref, dgd_ref):
        dyv = dy_ref[...]
        sa, sd = _sigmoid(ga_ref[...]), _sigmoid(gd_ref[...])
        dpa_ref[...] = (dyv * sa).astype(BF16)
        dpd_ref[...] = (dyv * sd).astype(BF16)
        dga_ref[...] = (dyv * pa_ref[...] * sa * (1.0 - sa)).astype(BF16)
        dgd_ref[...] = (dyv * pd_ref[...] * sd * (1.0 - sd)).astype(BF16)

    out = jax.ShapeDtypeStruct((cfg.S, cfg.D), BF16)
    return pl.pallas_call(body, grid=grid, in_specs=[ga, gd, pd, pd, pd], out_specs=(pd,) * 4,
                          out_shape=(out,) * 4, compiler_params=_cp("parallel", "parallel"),
                          name="merge_bwd")(proj, proj, pa, pdn, dy)


def _rope(x, cos, sin):
    lane = lax.broadcasted_iota(jnp.int32, x.shape, 1)
    swapped = jnp.where(lane < ROT_HALF, pltpu.roll(x, LANES - ROT_HALF, 1), pltpu.roll(x, ROT_HALF, 1))
    return x * cos + swapped * sin


def _rope_t(dx, cos, sin):
    t = dx * sin
    lane = lax.broadcasted_iota(jnp.int32, dx.shape, 1)
    swapped = jnp.where(lane < ROT_HALF, pltpu.roll(t, LANES - ROT_HALF, 1), pltpu.roll(t, ROT_HALF, 1))
    return dx * cos + swapped


def _band_mask(i):
    r = lax.broadcasted_iota(jnp.int32, (ATTN_BLOCK, 2 * ATTN_BLOCK), 0)
    c = lax.broadcasted_iota(jnp.int32, (ATTN_BLOCK, 2 * ATTN_BLOCK), 1)
    dist = r + ATTN_BLOCK - c
    return (dist >= 0) & (dist < ATTN_BLOCK) & ((c >= ATTN_BLOCK) | (i > 0))


def _dot(a, b, ca, cb):
    return lax.dot_general(a.astype(BF16), b.astype(BF16), (((ca,), (cb,)), ((), ())), preferred_element_type=F32)


def _attn_probs(qg, k_cat, sink, valid):
    logits = jnp.where(valid, _dot(qg, k_cat, 1, 1) * (HEAD ** -0.5), -1e30)
    m = jnp.maximum(jnp.max(logits, axis=-1, keepdims=True), sink)
    p = jnp.exp(logits - m)
    es = jnp.exp(sink - m)
    inv = 1.0 / (jnp.sum(p, axis=-1, keepdims=True) + es)
    return p * inv, es * inv


def _attn_specs(cfg, nb, clamp):
    gw = cfg.G * HEAD
    qi = (lambda kv, i: (jnp.minimum(i, nb - 1), kv)) if clamp else (lambda kv, i: (i, kv))
    ci = (lambda i: jnp.minimum(i, nb - 1)) if clamp else (lambda i: i)
    pi = lambda i: jnp.maximum(ci(i) - 1, 0)
    ko, vo = cfg.KA // HEAD, cfg.VA // HEAD
    blk = (ATTN_BLOCK, HEAD)
    specs = [
        pl.BlockSpec((ATTN_BLOCK, gw), qi),
        pl.BlockSpec(blk, lambda kv, i: (ci(i), ko + kv)), pl.BlockSpec(blk, lambda kv, i: (pi(i), ko + kv)),
        pl.BlockSpec(blk, lambda kv, i: (ci(i), vo + kv)), pl.BlockSpec(blk, lambda kv, i: (pi(i), vo + kv)),
        pl.BlockSpec(blk, lambda kv, i: (ci(i), 0)), pl.BlockSpec(blk, lambda kv, i: (pi(i), 0)),
        pl.BlockSpec(blk, lambda kv, i: (ci(i), 0)), pl.BlockSpec(blk, lambda kv, i: (pi(i), 0)),
        pl.BlockSpec((1, 8, LANES), lambda kv, i: (kv, 0, 0)),
    ]
    return specs, pl.BlockSpec((ATTN_BLOCK, gw), qi)


def _sink_rows(sinks, cfg):
    sk = jnp.broadcast_to(sinks.reshape(cfg.KV, cfg.G, 1), (cfg.KV, cfg.G, LANES))
    return jnp.pad(sk, ((0, 0), (0, 8 - cfg.G), (0, 0)))


def _attn_fwd(proj, cosf, sinf, sinks, cfg):
    nb = cfg.S // ATTN_BLOCK
    specs, qspec = _attn_specs(cfg, nb, False)

    def body(q_ref, kc_ref, kp_ref, vc_ref, vp_ref, cc_ref, cp_ref, sc_ref, sp_ref, sk_ref, o_ref):
        i = pl.program_id(1)
        cc, sc = cc_ref[...], sc_ref[...]
        k_cat = jnp.concatenate([_rope(kp_ref[...], cp_ref[...], sp_ref[...]), _rope(kc_ref[...], cc, sc)], axis=0).astype(BF16)
        v_cat = jnp.concatenate([vp_ref[...], vc_ref[...]], axis=0).astype(BF16)
        valid = _band_mask(i)
        for g in range(cfg.G):
            qg = _rope(q_ref[:, g * HEAD:(g + 1) * HEAD], cc, sc)
            p, _ = _attn_probs(qg, k_cat, sk_ref[0, g:g + 1, 0:1], valid)
            o_ref[:, g * HEAD:(g + 1) * HEAD] = _dot(p, v_cat, 1, 0).astype(o_ref.dtype)

    return pl.pallas_call(
        body, grid=(cfg.KV, nb), in_specs=specs, out_specs=qspec,
        out_shape=jax.ShapeDtypeStruct((cfg.S, cfg.HA * HEAD), BF16),
        compiler_params=_cp("parallel", "parallel"), name="attn_fwd",
    )(proj, proj, proj, proj, proj, cosf, cosf, sinf, sinf, _sink_rows(sinks, cfg))


def _attn_bwd(proj, do, cosf, sinf, sinks, cfg):
    nb = cfg.S // ATTN_BLOCK
    specs, qspec = _attn_specs(cfg, nb, True)
    kv_out = pl.BlockSpec((ATTN_BLOCK, HEAD), lambda kv, i: (jnp.maximum(i - 1, 0), kv))

    def body(q_ref, kc_ref, kp_ref, vc_ref, vp_ref, cc_ref, cp_ref, sc_ref, sp_ref, sk_ref, do_ref,
             dq_ref, dk_ref, dv_ref, dsk_ref, ck_ref, cv_ref):
        i = pl.program_id(1)

        @pl.when(i == 0)
        def _():
            ck_ref[...] = jnp.zeros_like(ck_ref)
            cv_ref[...] = jnp.zeros_like(cv_ref)
            dsk_ref[...] = jnp.zeros_like(dsk_ref)

        @pl.when(i < nb)
        def _():
            cc, sc, cp, sp = cc_ref[...], sc_ref[...], cp_ref[...], sp_ref[...]
            k_cat = jnp.concatenate([_rope(kp_ref[...], cp, sp), _rope(kc_ref[...], cc, sc)], axis=0).astype(BF16)
            v_cat = jnp.concatenate([vp_ref[...], vc_ref[...]], axis=0).astype(BF16)
            valid = _band_mask(i)
            dk_cat = jnp.zeros((2 * ATTN_BLOCK, HEAD), F32)
            dv_cat = jnp.zeros((2 * ATTN_BLOCK, HEAD), F32)
            for g in range(cfg.G):
                sl = slice(g * HEAD, (g + 1) * HEAD)
                qg = _rope(q_ref[:, sl], cc, sc)
                dog = do_ref[:, sl]
                p, ps = _attn_probs(qg, k_cat, sk_ref[0, g:g + 1, 0:1], valid)
                dp = _dot(dog, v_cat, 1, 1)
                delta = jnp.sum(p * dp, axis=-1, keepdims=True)
                ds = p * (dp - delta) * (HEAD ** -0.5)
                dq_ref[:, sl] = _rope_t(_dot(ds, k_cat, 1, 0), cc, sc).astype(dq_ref.dtype)
                dk_cat = dk_cat + _dot(ds, qg, 0, 0)
                dv_cat = dv_cat + _dot(p, dog, 0, 0)
                dsk_ref[0, g:g + 1, :] += jnp.broadcast_to(-jnp.sum(ps * delta, axis=0, keepdims=True), (1, LANES))
            dk_ref[...] = (ck_ref[...] + _rope_t(dk_cat[:ATTN_BLOCK], cp, sp)).astype(dk_ref.dtype)
            dv_ref[...] = (cv_ref[...] + dv_cat[:ATTN_BLOCK]).astype(dv_ref.dtype)
            ck_ref[...] = _rope_t(dk_cat[ATTN_BLOCK:], cc, sc)
            cv_ref[...] = dv_cat[ATTN_BLOCK:]

        @pl.when(i == nb)
        def _():
            dk_ref[...] = ck_ref[...].astype(dk_ref.dtype)
            dv_ref[...] = cv_ref[...].astype(dv_ref.dtype)

    kvw = cfg.KV * HEAD
    dq, dk, dv, dsk = pl.pallas_call(
        body, grid=(cfg.KV, nb + 1), in_specs=specs + [qspec],
        out_specs=(qspec, kv_out, kv_out, pl.BlockSpec((1, 8, LANES), lambda kv, i: (kv, 0, 0))),
        out_shape=(jax.ShapeDtypeStruct((cfg.S, cfg.HA * HEAD), BF16), jax.ShapeDtypeStruct((cfg.S, kvw), BF16),
                   jax.ShapeDtypeStruct((cfg.S, kvw), BF16), jax.ShapeDtypeStruct((cfg.KV, 8, LANES), F32)),
        scratch_shapes=[pltpu.VMEM((ATTN_BLOCK, HEAD), F32), pltpu.VMEM((ATTN_BLOCK, HEAD), F32)],
        compiler_params=_cp("parallel", "arbitrary"), name="attn_bwd",
    )(proj, proj, proj, proj, proj, cosf, cosf, sinf, sinf, _sink_rows(sinks, cfg), do)
    return jnp.concatenate([dq, dk, dv], axis=1), dsk[:, :cfg.G, 0].reshape(cfg.HA)


@jax.custom_vjp
def _nn(a, b):
    return _dot(a, b, 1, 0)


@jax.custom_vjp
def _nt(a, b):
    return _dot(a, b, 1, 1)


@jax.custom_vjp
def _tn(a, b):
    return _dot(a, b, 0, 0)


_nn.defvjp(lambda a, b: (_nn(a, b), (a, b)), lambda r, g: (_nt(g, r[1]), _tn(r[0], g)))
_nt.defvjp(lambda a, b: (_nt(a, b), (a, b)), lambda r, g: (_nn(g, r[1]), _tn(g, r[0])))
_tn.defvjp(lambda a, b: (_tn(a, b), (a, b)), lambda r, g: (_nt(r[1], g), _nn(r[0], g)))


def _dgx(a, b, ca, cb):
    return lax.dot_general(a, b, (((ca,), (cb,)), ((), ())), precision=lax.Precision.HIGH, preferred_element_type=F32)


def _unit_lower_inverse(a):
    ii = lax.broadcasted_iota(jnp.int32, a.shape, 0)
    jj = lax.broadcasted_iota(jnp.int32, a.shape, 1)
    t = jnp.where(ii == jj, 1.0, 0.0) - a
    p = a
    for _ in range(int(math.log2(CHUNK)) - 1):
        p = _dgx(p, p, 1, 0)
        t = t + _dgx(t, p, 1, 0)
    return t


@jax.custom_vjp
def _tri_inv(a):
    return _unit_lower_inverse(a)


def _tri_inv_fwd(a):
    t = _unit_lower_inverse(a)
    return t, t


def _tri_inv_bwd(t, g):
    return (-_dgx(t, _dgx(g, t, 1, 1), 0, 0),)


_tri_inv.defvjp(_tri_inv_fwd, _tri_inv_bwd)


def _gdn_chunk(q, k, v, z, b_col, a_col, a_row, al_col, dt_col, al_row, dt_row, nw, s):
    rows = q.shape[0]
    hb = rows // CHUNK
    ii = lax.broadcasted_iota(jnp.int32, (rows, rows), 0)
    jj = lax.broadcasted_iota(jnp.int32, (rows, rows), 1)
    same = (ii // CHUNK) == (jj // CHUNK)
    tri, strict, upper = same & (ii >= jj), same & (ii > jj), same & (ii <= jj)
    beta = _sigmoid(b_col)
    g_col = -jnp.exp(al_col) * _softplus(a_col + dt_col)
    g_row = -jnp.exp(al_row) * _softplus(a_row + dt_row)
    gc_col = jnp.sum(jnp.where(tri, g_row, 0.0), axis=1, keepdims=True)
    gc_row = jnp.sum(jnp.where(upper, g_col, 0.0), axis=0, keepdims=True)
    last = jj == (ii // CHUNK) * CHUNK + (CHUNK - 1)
    gl_col = jnp.sum(jnp.where(last, gc_row, 0.0), axis=1, keepdims=True)
    decay = jnp.where(tri, jnp.exp(jnp.where(tri, gc_col - gc_row, 0.0)), 0.0)
    qn = q * lax.rsqrt(jnp.sum(q * q, axis=-1, keepdims=True) + EPS) * (HEAD ** -0.5)
    kn = k * lax.rsqrt(jnp.sum(k * k, axis=-1, keepdims=True) + EPS)
    t_inv = _tri_inv(jnp.where(strict, beta * _nt(kn, kn) * decay, 0.0))
    eg = jnp.exp(gc_col)
    u = _nn(t_inv, v * beta)
    w = _nn(t_inv, kn * (beta * eg))
    qk = jnp.where(tri, _nt(qn, kn) * decay, 0.0)
    ri = lax.broadcasted_iota(jnp.int32, (rows, hb * HEAD), 0)
    ci = lax.broadcasted_iota(jnp.int32, (rows, hb * HEAD), 1)
    own = (ri // CHUNK) == (ci // HEAD)

    def spread(x):
        return jnp.where(own, jnp.concatenate([x] * hb, axis=1), 0.0)

    v_new = u - _nn(spread(w), s)
    o = _nn(spread(qn * eg), s) + _nn(qk, v_new)
    si = lax.broadcasted_iota(jnp.int32, (hb * HEAD, rows), 0)
    sj = lax.broadcasted_iota(jnp.int32, (hb * HEAD, rows), 1)
    gl_s = jnp.sum(jnp.where(sj == (si // HEAD) * CHUNK + (CHUNK - 1), gc_row, 0.0), axis=1, keepdims=True)
    s1 = s * jnp.exp(gl_s) + _tn(spread(kn * jnp.exp(gl_col - gc_col)), v_new)
    od = o * lax.rsqrt(jnp.mean(o * o, axis=-1, keepdims=True) + EPS) * nw * (z * _sigmoid(z))
    return od, s1


def _gdn_heads_per_step(cfg):
    for hb in (GDN_HEADS_PER_STEP, 2, 1):
        if cfg.HD % hb == 0 and cfg.Z % (hb * HEAD) == 0:
            return hb


def _gdn_in_specs(cfg, nc, rev):
    hb = _gdn_heads_per_step(cfg)
    ng, rows = cfg.HD // hb, hb * CHUNK
    ci = (lambda n: nc - 1 - n) if rev else (lambda n: n)
    zo = cfg.Z // (hb * HEAD)
    blk = (CHUNK, hb * HEAD)
    col = pl.BlockSpec((1, 1, rows, 1), lambda g, n: (g, ci(n), 0, 0))
    row = pl.BlockSpec((1, 1, 1, rows), lambda g, n: (g, ci(n), 0, 0))
    hcol = pl.BlockSpec((1, rows, 1), lambda g, n: (g, 0, 0))
    hrow = pl.BlockSpec((1, 1, rows), lambda g, n: (g, 0, 0))
    return [
        pl.BlockSpec(blk, lambda g, n: (ci(n), g)), pl.BlockSpec(blk, lambda g, n: (ci(n), ng + g)),
        pl.BlockSpec(blk, lambda g, n: (ci(n), 2 * ng + g)), pl.BlockSpec(blk, lambda g, n: (ci(n), zo + g)),
        col, col, row, hcol, hcol, hrow, hrow, pl.BlockSpec((1, HEAD), lambda g, n: (0, 0)),
    ]


def _gdn_args(qkv, proj, pba, a_log, dt_bias, nw, cfg):
    hb, hd, s = _gdn_heads_per_step(cfg), cfg.HD, cfg.S
    ng, nc, rows = hd // hb, s // CHUNK, hb * CHUNK

    def grouped(x):
        return jnp.transpose(x.reshape(nc, CHUNK, ng, hb), (2, 0, 3, 1)).reshape(ng, nc, rows)

    b, a = grouped(pba[:, :hd]), grouped(pba[:, hd:2 * hd])
    per_row = lambda x: jnp.repeat(x.reshape(ng, hb), CHUNK, axis=1)
    al, dt = per_row(a_log), per_row(dt_bias)
    return (qkv, qkv, qkv, proj, b.reshape(ng, nc, rows, 1), a.reshape(ng, nc, rows, 1), a.reshape(ng, nc, 1, rows),
            al.reshape(ng, rows, 1), dt.reshape(ng, rows, 1), al.reshape(ng, 1, rows), dt.reshape(ng, 1, rows),
            nw.reshape(1, HEAD))


def _stack_heads(x, hb):
    return jnp.concatenate([x[:, i * HEAD:(i + 1) * HEAD] for i in range(hb)], axis=0)


def _gdn_load(refs, hb):
    q, k, v, z, b, a, ar, alc, dtc, alr, dtr, nw = refs
    return (_stack_heads(q[...], hb), _stack_heads(k[...], hb), _stack_heads(v[...], hb), _stack_heads(z[...], hb),
            b[0, 0], a[0, 0], ar[0, 0], alc[0], dtc[0], alr[0], dtr[0], nw[...])


def _gdn_fwd(qkv, proj, pba, a_log, dt_bias, nw, cfg, side=None):
    nc, hb = cfg.S // CHUNK, _gdn_heads_per_step(cfg)
    ng = cfg.HD // hb
    grid = (ng, nc)

    def body(*refs):
        ins, (o_ref, st_ref), (s_ref,), parts = _side_split(refs, 12, 2, side)
        begin, end = _side_steps(side, parts, grid)
        begin()

        @pl.when(pl.program_id(1) == 0)
        def _():
            s_ref[...] = jnp.zeros_like(s_ref)

        s0 = s_ref[...]
        st_ref[0, 0] = s0
        od, s1 = _gdn_chunk(*_gdn_load(ins, hb), s0)
        for i in range(hb):
            o_ref[:, i * HEAD:(i + 1) * HEAD] = od[i * CHUNK:(i + 1) * CHUNK].astype(o_ref.dtype)
        s_ref[...] = s1
        end()

    s_in, s_out, s_shapes, s_sems, s_ops, _ = _side_call_args(side, 12, 2)
    return pl.pallas_call(
        body, grid=grid, in_specs=_gdn_in_specs(cfg, nc, False) + s_in,
        out_specs=(pl.BlockSpec((CHUNK, hb * HEAD), lambda g, n: (n, g)),
                   pl.BlockSpec((1, 1, hb * HEAD, HEAD), lambda g, n: (g, n, 0, 0)), *s_out),
        out_shape=(jax.ShapeDtypeStruct((cfg.S, cfg.HD * HEAD), BF16),
                   jax.ShapeDtypeStruct((ng, nc, hb * HEAD, HEAD), F32), *s_shapes),
        scratch_shapes=[pltpu.VMEM((hb * HEAD, HEAD), F32)] + s_sems,
        compiler_params=_cp("arbitrary" if side else "parallel", "arbitrary"),
        name="gdn_fwd_gather" if side else "gdn_fwd",
    )(*_gdn_args(qkv, proj, pba, a_log, dt_bias, nw, cfg), *s_ops)


def _gdn_bwd(qkv, proj, pba, a_log, dt_bias, nw, states, dod, cfg, side=None):
    nc, hd, s, hb = cfg.S // CHUNK, cfg.HD, cfg.S, _gdn_heads_per_step(cfg)
    ng, rows = hd // hb, hb * CHUNK
    rv = lambda n: nc - 1 - n
    col = pl.BlockSpec((1, 1, rows, 1), lambda g, n: (g, rv(n), 0, 0))
    row = pl.BlockSpec((1, 1, 1, rows), lambda g, n: (g, rv(n), 0, 0))
    hcol = pl.BlockSpec((1, rows, 1), lambda g, n: (g, 0, 0))
    hrow = pl.BlockSpec((1, 1, rows), lambda g, n: (g, 0, 0))

    grid = (ng, nc)

    def body(*refs):
        ins, outs, (ds_ref,), parts = _side_split(refs, 14, 10, side)
        begin, end = _side_steps(side, parts, grid)
        begin()
        st_ref, dod_ref = ins[12:14]
        dqkv_ref, dz_ref, db_ref, da_ref, dar_ref, dalc_ref, ddtc_ref, dalr_ref, ddtr_ref, dnw_ref = outs
        g, n = pl.program_id(0), pl.program_id(1)

        @pl.when(n == 0)
        def _():
            ds_ref[...] = jnp.zeros_like(ds_ref)
            for r in (dalc_ref, ddtc_ref, dalr_ref, ddtr_ref):
                r[...] = jnp.zeros_like(r)

        @pl.when((n == 0) & (g == 0))
        def _():
            dnw_ref[...] = jnp.zeros_like(dnw_ref)

        _, vjp = jax.vjp(_gdn_chunk, *_gdn_load(ins[:12], hb), st_ref[0, 0])
        dq, dk, dv, dz, db, da, dar, dalc, ddtc, dalr, ddtr, dnw, ds0 = vjp(
            (_stack_heads(dod_ref[...].astype(F32), hb), ds_ref[...]))
        for i in range(hb):
            sl, rs = slice(i * HEAD, (i + 1) * HEAD), slice(i * CHUNK, (i + 1) * CHUNK)
            dqkv_ref[0, :, sl], dqkv_ref[1, :, sl], dqkv_ref[2, :, sl] = dq[rs], dk[rs], dv[rs]
            dz_ref[:, sl] = dz[rs].astype(dz_ref.dtype)
        db_ref[0, 0], da_ref[0, 0], dar_ref[0, 0] = db, da, dar
        dalc_ref[0] += dalc
        ddtc_ref[0] += ddtc
        dalr_ref[0] += dalr
        ddtr_ref[0] += ddtr
        dnw_ref[...] += dnw
        ds_ref[...] = ds0
        end()

    blk = pl.BlockSpec((CHUNK, hb * HEAD), lambda g, n: (rv(n), g))
    f32 = lambda *sh: jax.ShapeDtypeStruct(sh, F32)
    s_in, s_out, s_shapes, s_sems, s_ops, _ = _side_call_args(side, 14, 10)
    outs = pl.pallas_call(
        body, grid=grid,
        in_specs=_gdn_in_specs(cfg, nc, True) + [pl.BlockSpec((1, 1, hb * HEAD, HEAD), lambda g, n: (g, rv(n), 0, 0)), blk] + s_in,
        out_specs=(pl.BlockSpec((3, CHUNK, hb * HEAD), lambda g, n: (0, rv(n), g)), blk, col, col, row,
                   hcol, hcol, hrow, hrow, pl.BlockSpec((1, HEAD), lambda g, n: (0, 0)), *s_out),
        out_shape=(f32(3, s, hd * HEAD), jax.ShapeDtypeStruct((s, hd * HEAD), BF16),
                   f32(ng, nc, rows, 1), f32(ng, nc, rows, 1), f32(ng, nc, 1, rows),
                   f32(ng, rows, 1), f32(ng, rows, 1), f32(ng, 1, rows), f32(ng, 1, rows), f32(1, HEAD), *s_shapes),
        scratch_shapes=[pltpu.VMEM((hb * HEAD, HEAD), F32)] + s_sems,
        compiler_params=_cp("arbitrary", "arbitrary"), name="gdn_bwd_scatter" if side else "gdn_bwd",
    )(*_gdn_args(qkv, proj, pba, a_log, dt_bias, nw, cfg), states, dod, *s_ops)
    dqkv, dz, db, da, dar, dalc, ddtc, dalr, ddtr, dnw = outs[:10]
    side_outs = list(outs[10:])

    def ungrouped(x):
        return jnp.transpose(x.reshape(ng, nc, hb, CHUNK), (1, 3, 0, 2)).reshape(s, hd)

    per_head = lambda c, r: (c.reshape(ng, hb, CHUNK) + r.reshape(ng, hb, CHUNK)).sum(axis=-1).reshape(hd)
    d_b = ungrouped(db.reshape(ng, nc, rows))
    d_a = ungrouped(da.reshape(ng, nc, rows) + dar.reshape(ng, nc, rows))
    dba = jnp.concatenate([d_b, d_a, jnp.zeros((s, LANES - 2 * hd), F32)], axis=1).astype(BF16)
    return dqkv, dz, dba, per_head(dalc, dalr), per_head(ddtc, ddtr), dnw.reshape(HEAD), side_outs


def _adamw(w, g, m, v, *, name):
    r, c = w.shape
    gc = g.shape[1]
    tr = _tile(r, max(8, (1 << 18) // c // 8 * 8), 8) if r % 8 == 0 else r
    c1, c2 = 1.0 - ADAM_B1 ** ADAM_STEP, 1.0 - ADAM_B2 ** ADAM_STEP

    def body(w_ref, g_ref, m_ref, v_ref, go_ref, d_ref, nm_ref, nv_ref):
        gv = g_ref[:, :c]
        nm = ADAM_B1 * m_ref[...] + (1.0 - ADAM_B1) * gv
        nv = ADAM_B2 * v_ref[...] + (1.0 - ADAM_B2) * (gv * gv)
        go_ref[...] = gv
        d_ref[...] = -ADAM_LR * ((nm / c1) / (jnp.sqrt(nv / c2) + ADAM_EPS) + ADAM_WD * w_ref[...])
        nm_ref[...] = nm
        nv_ref[...] = nv

    spec = pl.BlockSpec((tr, c), lambda i: (i, 0))
    gspec = pl.BlockSpec((tr, gc), lambda i: (i, 0))
    out = jax.ShapeDtypeStruct((r, c), F32)
    return pl.pallas_call(body, grid=(r // tr,), in_specs=[spec, gspec, spec, spec], out_specs=(spec,) * 4,
                          out_shape=(out,) * 4, compiler_params=_cp("parallel"), name=name)(w, g, m, v)


def _sum_leading(a, *, name):
    n, r, c = a.shape
    tr = _tile(r, SUM_TR, 8)

    def body(a_ref, o_ref):
        acc = a_ref[0].astype(F32)
        for i in range(1, n):
            acc = acc + a_ref[i].astype(F32)
        o_ref[...] = acc

    return pl.pallas_call(body, grid=(r // tr,), in_specs=[pl.BlockSpec((n, tr, c), lambda i: (0, i, 0))],
                          out_specs=pl.BlockSpec((tr, c), lambda i: (i, 0)),
                          out_shape=jax.ShapeDtypeStruct((r, c), F32), compiler_params=_cp("parallel"), name=name)(a)


class _Mat:
    def __init__(self, kind, rows, cols):
        self.kind, self.rows, self.cols = kind, rows, cols
        self.full = (rows, 4 * cols) if kind == "col" else (4 * rows, cols)
        self.half = rows // 2

    def block(self, ref, j, c=None):
        r0 = (0 if self.kind == "col" else j * self.rows) + (0 if c is None else c * self.half)
        rows = pl.ds(_aligned(r0, 16), self.rows if c is None else self.half)
        return ref.at[rows, pl.ds(_aligned(j * self.cols, LANES), self.cols)] if self.kind == "col" else ref.at[rows, :]

    def own_half(self, ref, c):
        return ref.at[pl.ds(_aligned(c * self.half, 16), self.half), :]


def _aligned(v, unit):
    return v if isinstance(v, int) else pl.multiple_of(v, unit)


def _place():
    x, y, c = lax.axis_index("x"), lax.axis_index("y"), lax.axis_index("c")
    chips = [(1 - x, y), (x, 1 - y), (1 - x, 1 - y)]
    return x, y, c, chips


def _chip_index():
    return 2 * lax.axis_index("x") + lax.axis_index("y")


def _remote(src, dst, send_sems, recv_sems, k, to):
    return pltpu.make_async_remote_copy(src_ref=src, dst_ref=dst, send_sem=send_sems.at[k], recv_sem=recv_sems.at[k],
                                        device_id=to, device_id_type=MESH)


def _comm_call(body, ins, out_shapes, n_sems, name):
    return pl.pallas_call(
        body, out_shape=tuple(out_shapes), in_specs=[ANY] * len(ins), out_specs=tuple([ANY] * len(out_shapes)),
        scratch_shapes=[pltpu.SemaphoreType.DMA((n_sems,)), pltpu.SemaphoreType.DMA((n_sems,))],
        compiler_params=pltpu.CompilerParams(has_side_effects=True), name=name)(*ins)


def _gather_copies(own_refs, full_refs, send_sems, recv_sems, mats, base=0):
    x, y, c, chips = _place()
    me = 2 * x + y
    return [(_remote(m.own_half(own_refs[k], c), m.block(full_refs[k], me, c), send_sems, recv_sems, base + 3 * k + j, (cx, cy, c)),
             _remote(m.block(full_refs[k], 2 * cx + cy, c), m.block(full_refs[k], 2 * cx + cy, c), send_sems, recv_sems,
                     base + 3 * k + j, (cx, cy, c)))
            for k, m in enumerate(mats) for j, (cx, cy) in enumerate(chips)]


def _pass_copies(full_refs, send_sems, recv_sems, mats, base=0):
    x, y, c, chips = _place()
    sib = (x, y, 1 - c)
    return [(_remote(m.block(full_refs[k], 2 * cx + cy, c), m.block(full_refs[k], 2 * cx + cy, c), send_sems, recv_sems,
                     base + 3 * k + j, sib),
             _remote(m.block(full_refs[k], 2 * cx + cy, 1 - c), m.block(full_refs[k], 2 * cx + cy, 1 - c), send_sems,
                     recv_sems, base + 3 * k + j, sib))
            for k, m in enumerate(mats) for j, (cx, cy) in enumerate(chips)]


def _start_all(pairs):
    for cp, _ in pairs:
        cp.start()


def _finish_all(pairs):
    for _, landing in pairs:
        landing.wait_recv()
    for cp, _ in pairs:
        cp.wait_send()


def _gather_ici_side(own, mats):
    return _Side(own, [jax.ShapeDtypeStruct(m.full, a.dtype) for m, a in zip(mats, own)], 3 * len(mats),
                 lambda i, o, ss, rs: _start_all(_gather_copies(i, o, ss, rs, mats)),
                 lambda i, o, ss, rs: _finish_all(_gather_copies(i, o, ss, rs, mats)))


def _gather_pass_side(fulls, mats):
    return _Side(fulls, [jax.ShapeDtypeStruct(f.shape, f.dtype) for f in fulls], 3 * len(mats),
                 lambda i, o, ss, rs: _start_all(_pass_copies(o, ss, rs, mats)),
                 lambda i, o, ss, rs: _finish_all(_pass_copies(o, ss, rs, mats)), in_place=True)


def _place_own(fulls, own, mats):
    me = _chip_index()
    return [lax.dynamic_update_slice(o, a, (0, me * m.cols) if m.kind == "col" else (me * m.rows, 0))
            for o, a, m in zip(fulls, own, mats)]


def _gather_weights(own, mats):
    n = len(mats)

    def body(*refs):
        own_refs, full_refs, (send_sems, recv_sems) = refs[:n], refs[n:2 * n], refs[2 * n:]
        first = _gather_copies(own_refs, full_refs, send_sems, recv_sems, mats)
        second = _pass_copies(full_refs, send_sems, recv_sems, mats, base=3 * n)
        _start_all(first)
        for (_, landed), (fwd, _) in zip(first, second):
            landed.wait_recv()
            fwd.start()
        for _, landing in second:
            landing.wait_recv()
        for cp, _ in first + second:
            cp.wait_send()

    outs = _comm_call(body, own, [jax.ShapeDtypeStruct(m.full, a.dtype) for m, a in zip(mats, own)], 6 * n,
                      "gather_weights")
    return _place_own(outs, own, mats)


def _swap_halves(grads, mats):
    def body(*refs):
        n = len(mats)
        g_refs, b_refs, (send_sems, recv_sems) = refs[:n], refs[n:2 * n], refs[2 * n:]
        x, y, c, _ = _place()
        cps, k = [], 0
        for g, b, m in zip(g_refs, b_refs, mats):
            if m.kind == "col":
                cps.append(_remote(m.own_half(g, 1 - c), b, send_sems, recv_sems, k, (x, y, 1 - c)))
                k += 1
            else:
                for j in range(4):
                    cps.append(_remote(m.block(g, j, 1 - c), b.at[j], send_sems, recv_sems, k, (x, y, 1 - c)))
                    k += 1
        for cp in cps:
            cp.start()
        for cp in cps:
            cp.wait()

    shapes = [jax.ShapeDtypeStruct((m.half, 4 * m.cols) if m.kind == "col" else (4, m.half, m.cols), F32) for m in mats]
    return _comm_call(body, grads, shapes, sum(1 if m.kind == "col" else 4 for m in mats), "rs_swap_halves")


def _pair_sum(own, recv, m, cidx):
    tr = _tile(m.half, 256, 16)
    nh = m.half // tr
    if m.kind == "col":
        own_spec = pl.BlockSpec((tr, m.cols), lambda j, i, c: (c[0] * nh + i, j))
        recv_spec = pl.BlockSpec((tr, m.cols), lambda j, i, c: (i, j))
    else:
        own_spec = pl.BlockSpec((tr, m.cols), lambda j, i, c: ((2 * j + c[0]) * nh + i, 0))
        recv_spec = pl.BlockSpec((None, tr, m.cols), lambda j, i, c: (j, i, 0))

    def body(c_ref, a_ref, b_ref, o_ref):
        o_ref[...] = (a_ref[...] + b_ref[...]).astype(o_ref.dtype)

    return pl.pallas_call(
        body,
        grid_spec=pltpu.PrefetchScalarGridSpec(
            num_scalar_prefetch=1, grid=(4, nh), in_specs=[own_spec, recv_spec],
            out_specs=pl.BlockSpec((None, tr, m.cols), lambda j, i, c: (j, i, 0))),
        out_shape=jax.ShapeDtypeStruct((4, m.half, m.cols), BF16),
        compiler_params=_cp("parallel", "parallel"), name="rs_pair_sum")(cidx, own, recv)


def _scatter_copies(s_refs, b_refs, send_sems, recv_sems):
    x, y, c, chips = _place()
    me = 2 * x + y
    return [(_remote(s.at[2 * cx + cy], b.at[me], send_sems, recv_sems, 3 * k + j, (cx, cy, c)),
             _remote(b.at[2 * cx + cy], b.at[2 * cx + cy], send_sems, recv_sems, 3 * k + j, (cx, cy, c)))
            for k, (s, b) in enumerate(zip(s_refs, b_refs)) for j, (cx, cy) in enumerate(chips)]


def _scatter_side(sps):
    return _Side(sps, [jax.ShapeDtypeStruct(s.shape, s.dtype) for s in sps], 3 * len(sps),
                 lambda i, o, ss, rs: _start_all(_scatter_copies(i, o, ss, rs)),
                 lambda i, o, ss, rs: _finish_all(_scatter_copies(i, o, ss, rs)))


def _scatter_chips(sps):
    def body(*refs):
        n = len(sps)
        pairs = _scatter_copies(refs[:n], refs[n:2 * n], *refs[2 * n:])
        _start_all(pairs)
        _finish_all(pairs)

    return _comm_call(body, sps, [jax.ShapeDtypeStruct(s.shape, s.dtype) for s in sps], 3 * len(sps), "rs_scatter_chips")


def _sum_chips(got, sp, me):
    _, h, w = got.shape
    tr = _tile(h, 256, 16)

    def body(me_ref, g_ref, s_ref, o_ref):
        acc = None
        for s in range(4):
            term = jnp.where(me_ref[0] == s, s_ref[...], g_ref[s]).astype(F32)
            acc = term if acc is None else acc + term
        o_ref[...] = acc

    return pl.pallas_call(
        body,
        grid_spec=pltpu.PrefetchScalarGridSpec(
            num_scalar_prefetch=1, grid=(h // tr,),
            in_specs=[pl.BlockSpec((4, tr, w), lambda i, me: (0, i, 0)),
                      pl.BlockSpec((None, tr, w), lambda i, me: (me[0], i, 0))],
            out_specs=pl.BlockSpec((tr, w), lambda i, me: (i, 0))),
        out_shape=jax.ShapeDtypeStruct((h, w), F32), compiler_params=_cp("parallel"), name="rs_sum_chips")(me, got, sp)


def _share_halves(ts):
    def body(*refs):
        n = len(ts)
        t_refs, f_refs, (send_sems, recv_sems) = refs[:n], refs[n:2 * n], refs[2 * n:]
        x, y, c, _ = _place()
        cps = []
        for k, (t, f) in enumerate(zip(t_refs, f_refs)):
            h = t.shape[0]
            cps.append(_remote(t, f.at[pl.ds(_aligned(c * h, 16), h), :], send_sems, recv_sems, k, (x, y, 1 - c)))
        for cp in cps:
            cp.start()
        for k, (t, f) in enumerate(zip(t_refs, f_refs)):
            h = t.shape[0]
            _remote(t, f.at[pl.ds(_aligned((1 - c) * h, 16), h), :], send_sems, recv_sems, k, (x, y, 1 - c)).wait_recv()
        for cp in cps:
            cp.wait_send()

    outs = _comm_call(body, ts, [jax.ShapeDtypeStruct((2 * t.shape[0], t.shape[1]), F32) for t in ts], len(ts),
                      "rs_share_halves")
    c = lax.axis_index("c")
    return [lax.dynamic_update_slice(o, t, (c * t.shape[0], 0)) for o, t in zip(outs, ts)]


def _gather_all(p):
    r, w = p.shape
    rel = [(dx, dy, dc) for dx in (0, 1) for dy in (0, 1) for dc in (0, 1)][1:]

    def body(p_ref, g_ref, send_sems, recv_sems):
        x, y, c = lax.axis_index("x"), lax.axis_index("y"), lax.axis_index("c")
        me = 4 * x + 2 * y + c
        g_ref[me] = p_ref[...]
        peers = [(jnp.bitwise_xor(x, dx), jnp.bitwise_xor(y, dy), jnp.bitwise_xor(c, dc)) for dx, dy, dc in rel]
        cps = [_remote(p_ref, g_ref.at[me], send_sems, recv_sems, k, peer) for k, peer in enumerate(peers)]
        for cp in cps:
            cp.start()
        for k, (px, py, pc) in enumerate(peers):
            slot = g_ref.at[4 * px + 2 * py + pc]
            _remote(slot, slot, send_sems, recv_sems, k, (px, py, pc)).wait_recv()
        for cp in cps:
            cp.wait_send()

    return pl.pallas_call(
        body, out_shape=jax.ShapeDtypeStruct((8, r, w), p.dtype),
        in_specs=[pl.BlockSpec(memory_space=pltpu.VMEM)], out_specs=pl.BlockSpec(memory_space=pltpu.VMEM),
        scratch_shapes=[pltpu.SemaphoreType.DMA((7,)), pltpu.SemaphoreType.DMA((7,))],
        compiler_params=pltpu.CompilerParams(has_side_effects=True, vmem_limit_bytes=VMEM_LIMIT),
        name=("gather_all_%dx%d" % (r, w)))(p)


def _pack_small(arrs):
    flat = jnp.concatenate([a.reshape(-1) for a in arrs])
    rows = -(-flat.shape[0] // (SMALL_W * 8)) * 8
    return jnp.pad(flat, (0, rows * SMALL_W - flat.shape[0])).reshape(rows, SMALL_W)


def _unpack(buf, shapes):
    flat = buf.reshape(-1)
    out, off = [], 0
    for sh in shapes:
        n = math.prod(sh)
        out.append(flat[off:off + n].reshape(sh))
        off += n
    return out


class _Cfg:
    def __init__(self, x, w_in, attn_sinks, dn_a_log, w_ffn_up):
        self.S, self.D = x.shape[1], x.shape[2]
        self.L = w_in.shape[0]
        self.HA, self.HD = attn_sinks.shape[1], dn_a_log.shape[1]
        self.IN = 4 * w_in.shape[2]
        kvw = self.IN - self.HA * HEAD - 4 * self.HD * HEAD - 2 * self.HD - 2 * self.D
        self.KV = kvw // (2 * HEAD)
        self.G = self.HA // self.KV
        self.FF = 2 * w_ffn_up.shape[2]
        self.KA = self.HA * HEAD
        self.VA = self.KA + self.KV * HEAD
        self.QD = self.VA + self.KV * HEAD
        self.Z = self.QD + 3 * self.HD * HEAD
        self.GA = self.Z + self.HD * HEAD
        self.GD = self.GA + self.D
        self.NB = self.GD + self.D
        assert self.NB + 2 * self.HD == self.IN and self.G <= 8 and 2 * self.HD <= LANES


def kernel(x, positions, norm_mix_pre, w_in, attn_sinks, dn_conv_w, dn_a_log, dn_dt_bias, dn_norm_w, w_branch_attn, w_branch_dn, w_out, norm_mix_post, norm_ffn_pre, w_ffn_up, ffn_conv_w, ffn_conv_b, w_ffn_down, norm_ffn_post, loss_target, m_norm_mix_pre, m_w_in, m_attn_sinks, m_dn_conv_w, m_dn_a_log, m_dn_dt_bias, m_dn_norm_w, m_w_branch_attn, m_w_branch_dn, m_w_out, m_norm_mix_post, m_norm_ffn_pre, m_w_ffn_up, m_ffn_conv_w, m_ffn_conv_b, m_w_ffn_down, m_norm_ffn_post, v_norm_mix_pre, v_w_in, v_attn_sinks, v_dn_conv_w, v_dn_a_log, v_dn_dt_bias, v_dn_norm_w, v_w_branch_attn, v_w_branch_dn, v_w_out, v_norm_mix_post, v_norm_ffn_pre, v_w_ffn_up, v_ffn_conv_w, v_ffn_conv_b, v_w_ffn_down, v_norm_ffn_post):
    cfg = _Cfg(x, w_in, attn_sinks, dn_a_log, w_ffn_up)
    S, D, L, HD = cfg.S, cfg.D, cfg.L, cfg.HD
    o1 = cfg.Z
    ins = w_in.shape[2]
    insp = -(-ins // SHARD_PAD) * SHARD_PAD
    mats = [_Mat("col", D, insp), _Mat("row", *w_branch_attn.shape[1:]), _Mat("row", *w_branch_dn.shape[1:]),
            _Mat("row", *w_out.shape[1:]), _Mat("col", *w_ffn_up.shape[1:]), _Mat("row", *w_ffn_down.shape[1:])]
    me = _chip_index()

    taps = _gather_all(_pack_small([dn_conv_w, ffn_conv_w]))
    tap_shapes = [dn_conv_w.shape, ffn_conv_w.shape]
    per_chip = [_unpack(taps[2 * j], tap_shapes) for j in range(4)]
    dnw_all = jnp.concatenate([per_chip[j][0] for j in range(4)], axis=2)
    fcw_all = jnp.concatenate([per_chip[j][1] for j in range(4)], axis=2)

    def own_blocks(l):
        return [jnp.pad(w_in[l].astype(BF16), ((0, 0), (0, insp - ins))), w_branch_attn[l].astype(BF16),
                w_branch_dn[l].astype(BF16), w_out[l].astype(BF16), w_ffn_up[l].astype(BF16), w_ffn_down[l].astype(BF16)]

    def layer_weights(l, fulls):
        win_sm, wbra, wbrd, wout, wup, wdown = fulls
        win = jnp.concatenate([win_sm[:, j * insp:j * insp + ins] for j in range(4)], axis=1)
        return dict(
            win_sm=win_sm, wbig=jnp.concatenate([win[:, :o1], win[:, o1 + 2 * HD:]], axis=1),
            wba=jnp.pad(win[:, o1:o1 + 2 * HD], ((0, 0), (0, LANES - 2 * HD))),
            wbra=wbra, wbrd=wbrd, wout=wout, wup=wup, wdown=wdown, dnw=dnw_all[l], fcw=fcw_all[l])

    weights = [layer_weights(0, _gather_weights(own_blocks(0), mats))] + [None] * (L - 1)

    inv_freq = ROPE_THETA ** (-jnp.arange(0, 2 * ROT_HALF, 2, dtype=F32) / (2 * ROT_HALF))
    ang = positions.reshape(S, 1).astype(F32) * inv_freq
    cosf = jnp.concatenate([jnp.cos(ang), jnp.cos(ang), jnp.ones((S, HEAD - 2 * ROT_HALF), F32)], axis=1)
    sinf = jnp.concatenate([-jnp.sin(ang), jnp.sin(ang), jnp.zeros((S, HEAD - 2 * ROT_HALF), F32)], axis=1)

    h = x.reshape(S, D)
    saved = []
    for l in range(L):
        wt = weights[l]
        t = dict(h0=h)
        t["xn"] = _rms_fwd(h, norm_mix_pre[l], out_dtype=BF16, name="norm_mix_pre")
        t["proj"] = _mm(t["xn"], wt["wbig"], name="proj")
        t["pba"] = _mm(t["xn"], wt["wba"], name="proj_ba")
        t["oa"] = _attn_fwd(t["proj"], cosf, sinf, attn_sinks[l], cfg)
        t["qkv"] = _dnconv_fwd(t["proj"], wt["dnw"], cfg)
        nxt = own_blocks(l + 1) if l + 1 < L else None
        t["od"], t["st"], *landed = _gdn_fwd(t["qkv"], t["proj"], t["pba"], dn_a_log[l], dn_dt_bias[l], dn_norm_w[l], cfg,
                                             side=_gather_ici_side(nxt, mats) if nxt else None)
        t["pa"] = _mm(t["oa"], wt["wbra"], name="branch_attn")
        t["pd"] = _mm(t["od"], wt["wbrd"], name="branch_dn")
        t["y"] = _merge_fwd(t["proj"], t["pa"], t["pd"], cfg)
        t["mix"] = _mm(t["y"], wt["wout"], name="mix_out")
        t["h1"] = _rms_fwd(t["mix"], norm_mix_post[l], res=h, out_dtype=F32, name="norm_mix_post")
        t["xf"] = _rms_fwd(t["h1"], norm_ffn_pre[l], out_dtype=BF16, name="norm_ffn_pre")
        if nxt:
            t["up"], *landed = _mm(t["xf"], wt["wup"], name="ffn_up_pass", side=_gather_pass_side(landed, mats))
            weights[l + 1] = layer_weights(l + 1, _place_own(landed, nxt, mats))
        else:
            t["up"] = _mm(t["xf"], wt["wup"], name="ffn_up")
        t["act"] = _glu_fwd(t["up"], wt["fcw"], ffn_conv_b[l], cfg)
        t["f"] = _mm(t["act"], wt["wdown"], name="ffn_down")
        h = _rms_fwd(t["f"], norm_ffn_post[l], res=t["h1"], out_dtype=F32, name="norm_ffn_post")
        saved.append(t)

    loss_local, dh = _loss_head(h, loss_target.reshape(S, D))
    loss = lax.psum(loss_local, ("x", "y", "c"))

    cidx = lax.axis_index("c").astype(jnp.int32).reshape(1)
    midx = me.astype(jnp.int32).reshape(1)
    big_grads, small = [None] * L, [None] * L
    reduce_blocks = lambda got, pair: _share_halves([_sum_chips(a, b, midx) for a, b in zip(got, pair)])
    waiting = None
    for l in reversed(range(L)):
        wt, t = weights[l], saved[l]
        df, g_nfpost = _rms_bwd(t["f"], norm_ffn_post[l], dh, out_dtype=BF16, name="norm_ffn_post_bwd")
        dact = _mm(df, wt["wdown"], tb=True, name="ffn_down_dx")
        g_wdown = _mm(t["act"], df, ta=True, name="ffn_down_dw")
        dup, g_fcw, g_fcb = _glu_bwd(t["up"], dact, wt["fcw"], ffn_conv_b[l], cfg)
        dxf = _mm(dup, wt["wup"], tb=True, name="ffn_up_dx")
        g_wup = _mm(t["xf"], dup, ta=True, name="ffn_up_dw")
        dh1, g_nfp = _rms_bwd(t["h1"], norm_ffn_pre[l], dxf, add=dh, name="norm_ffn_pre_bwd")
        dmix, g_nmpost = _rms_bwd(t["mix"], norm_mix_post[l], dh1, out_dtype=BF16, name="norm_mix_post_bwd")
        dy = _mm(dmix, wt["wout"], tb=True, name="mix_out_dx")
        g_wout = _mm(t["y"], dmix, ta=True, name="mix_out_dw")
        dpa, dpd, dga, dgd = _merge_bwd(t["proj"], t["pa"], t["pd"], dy, cfg)
        doa = _mm(dpa, wt["wbra"], tb=True, out_dtype=BF16, name="branch_attn_dx")
        g_wbra = _mm(t["oa"], dpa, ta=True, name="branch_attn_dw")
        dod = _mm(dpd, wt["wbrd"], tb=True, out_dtype=BF16, name="branch_dn_dx")
        g_wbrd = _mm(t["od"], dpd, ta=True, name="branch_dn_dw")
        dqkv_c, dz, dba, g_alog, g_dt, g_dnnorm, got = _gdn_bwd(
            t["qkv"], t["proj"], t["pba"], dn_a_log[l], dn_dt_bias[l], dn_norm_w[l], t["st"], dod, cfg,
            side=_scatter_side(waiting[1]) if waiting else None)
        if waiting:
            big_grads[waiting[0]] = reduce_blocks(got, waiting[1])
        dqkv_d, g_dnw = _dnconv_bwd(t["proj"], dqkv_c, wt["dnw"], cfg)
        dqkv_a, g_sinks = _attn_bwd(t["proj"], doa, cosf, sinf, attn_sinks[l], cfg)
        dproj = jnp.concatenate([dqkv_a, dqkv_d, dba[:, :2 * HD], dz, dga, dgd], axis=1)
        dproj = jnp.pad(dproj.reshape(S, 4, ins), ((0, 0), (0, 0), (0, insp - ins))).reshape(S, 4 * insp)
        dxn = _mm(dproj, wt["win_sm"], tb=True, name="proj_dx")
        g_win = _mm(t["xn"], dproj, ta=True, name="proj_dw")
        dh, g_nmp = _rms_bwd(t["h0"], norm_mix_pre[l], dxn, add=dh1, name="norm_mix_pre_bwd")

        full = [g_win, g_wbra, g_wbrd, g_wout, g_wup, g_wdown]
        theirs = _swap_halves(full, mats)
        waiting = (l, [_pair_sum(a, b, m, cidx) for a, b, m in zip(full, theirs, mats)])
        small[l] = [g_nmp, g_sinks, g_alog, g_dt, g_dnnorm, g_nmpost, g_nfp, g_fcb, g_nfpost, g_dnw, g_fcw]

    big_grads[waiting[0]] = reduce_blocks(_scatter_chips(waiting[1]), waiting[1])
    grad_x = dh.reshape(1, S, D)

    small_shapes = [a.shape for a in small[0]]
    summed = _sum_leading(_gather_all(_pack_small([a for l in range(L) for a in small[l]])), name="sum_devices")
    per_layer = _unpack(summed, small_shapes * L)
    red = [jnp.stack([per_layer[l * len(small_shapes) + i] for l in range(L)]) for i in range(len(small_shapes))]
    g_dn_conv = lax.dynamic_slice_in_dim(red[9], me * dn_conv_w.shape[2], dn_conv_w.shape[2], axis=2)
    g_ffn_conv = lax.dynamic_slice_in_dim(red[10], me * ffn_conv_w.shape[2], ffn_conv_w.shape[2], axis=2)
    stacked = lambda i: jnp.stack([big_grads[l][i] for l in range(L)])

    grads = dict(
        norm_mix_pre=red[0], w_in=stacked(0), attn_sinks=red[1], dn_conv_w=g_dn_conv, dn_a_log=red[2],
        dn_dt_bias=red[3], dn_norm_w=red[4], w_branch_attn=stacked(1), w_branch_dn=stacked(2), w_out=stacked(3),
        norm_mix_post=red[5], norm_ffn_pre=red[6], w_ffn_up=stacked(4), ffn_conv_w=g_ffn_conv, ffn_conv_b=red[7],
        w_ffn_down=stacked(5), norm_ffn_post=red[8])

    params = dict(
        norm_mix_pre=(norm_mix_pre, m_norm_mix_pre, v_norm_mix_pre), w_in=(w_in, m_w_in, v_w_in),
        attn_sinks=(attn_sinks, m_attn_sinks, v_attn_sinks), dn_conv_w=(dn_conv_w, m_dn_conv_w, v_dn_conv_w),
        dn_a_log=(dn_a_log, m_dn_a_log, v_dn_a_log), dn_dt_bias=(dn_dt_bias, m_dn_dt_bias, v_dn_dt_bias),
        dn_norm_w=(dn_norm_w, m_dn_norm_w, v_dn_norm_w),
        w_branch_attn=(w_branch_attn, m_w_branch_attn, v_w_branch_attn),
        w_branch_dn=(w_branch_dn, m_w_branch_dn, v_w_branch_dn), w_out=(w_out, m_w_out, v_w_out),
        norm_mix_post=(norm_mix_post, m_norm_mix_post, v_norm_mix_post),
        norm_ffn_pre=(norm_ffn_pre, m_norm_ffn_pre, v_norm_ffn_pre), w_ffn_up=(w_ffn_up, m_w_ffn_up, v_w_ffn_up),
        ffn_conv_w=(ffn_conv_w, m_ffn_conv_w, v_ffn_conv_w), ffn_conv_b=(ffn_conv_b, m_ffn_conv_b, v_ffn_conv_b),
        w_ffn_down=(w_ffn_down, m_w_ffn_down, v_w_ffn_down),
        norm_ffn_post=(norm_ffn_post, m_norm_ffn_post, v_norm_ffn_post))
    names = list(params)

    grad_out, delta, new_m, new_v = {}, {}, {}, {}
    small_names = [n for n in names if params[n][0].ndim == 2]
    for n in names:
        if n in small_names:
            continue
        w, m, v = params[n]
        two_d = (w.shape[0] * w.shape[1], w.shape[2])
        g = grads[n].reshape(two_d[0], grads[n].shape[2])
        outs = _adamw(w.reshape(two_d), g, m.reshape(two_d), v.reshape(two_d), name="adamw_" + n)
        grad_out[n], delta[n], new_m[n], new_v[n] = (o.reshape(w.shape) for o in outs)

    outs = _adamw(_pack_small([params[n][0] for n in small_names]), _pack_small([grads[n] for n in small_names]),
                  _pack_small([params[n][1] for n in small_names]), _pack_small([params[n][2] for n in small_names]),
                  name="adamw_small")
    shapes = [params[n][0].shape for n in small_names]
    for dst, o in zip((grad_out, delta, new_m, new_v), outs):
        for n, a in zip(small_names, _unpack(o, shapes)):
            dst[n] = a

    return (loss, grad_x, *[grad_out[n] for n in names], *[delta[n] for n in names],
            *[new_m[n] for n in names], *[new_v[n] for n in names])
```

```python
import functools
import math

import jax
import jax.numpy as jnp
from jax import lax
from jax.experimental import pallas as pl
from jax.experimental.pallas import tpu as pltpu

F32, BF16 = jnp.float32, jnp.bfloat16
EPS = 1e-6
HEAD = 128
ATTN_BLOCK = 128
ROT_HALF = 16
ROPE_THETA = 500000.0
CHUNK = 64
DN_CONV, FFN_CONV = 4, 3
GDN_HEADS_PER_STEP = 4
MM_TILES = (512, 1024, 3968)
HALO = 8
LANES = 128
SUM_TR = 512
SHARD_PAD = 1024
SMALL_W = 1024
ADAM_LR, ADAM_B1, ADAM_B2, ADAM_EPS, ADAM_WD, ADAM_STEP = 0.001, 0.9, 0.999, 1e-08, 0.01, 10
VMEM_LIMIT = 56 * 1024 * 1024
MESH = pl.DeviceIdType.MESH
ANY = pl.BlockSpec(memory_space=pl.ANY)


def _cp(*sem):
    return pltpu.CompilerParams(dimension_semantics=sem, vmem_limit_bytes=VMEM_LIMIT)


def _tile(n, pref, unit=LANES):
    if n <= pref:
        return n
    t = (pref // unit) * unit
    while t >= unit:
        if n % t == 0:
            return t
        t -= unit
    raise ValueError((n, pref, unit))


def _sigmoid(x):
    return 1.0 / (1.0 + jnp.exp(-x))


def _softplus(x):
    return jnp.maximum(x, 0.0) + jnp.log(1.0 + jnp.exp(-jnp.abs(x)))


def _dsilu(y):
    s = _sigmoid(y)
    return s * (1.0 + y * (1.0 - s))


class _Side:
    def __init__(self, ins, out_shapes, n_sems, start, finish, in_place=False):
        self.ins, self.out_shapes, self.n_sems = list(ins), list(out_shapes), n_sems
        self.start, self.finish, self.in_place = start, finish, in_place


def _side_split(refs, n_in, n_out, side):
    si, so = (len(side.ins), len(side.out_shapes)) if side else (0, 0)
    ins, sins = refs[:n_in], refs[n_in:n_in + si]
    outs, souts = refs[n_in + si:n_in + si + n_out], refs[n_in + si + n_out:n_in + si + n_out + so]
    rest = refs[n_in + si + n_out + so:]
    scratch, sems = (rest[:-2], rest[-2:]) if side else (rest, ())
    return ins, outs, scratch, (sins, souts, sems)


def _side_steps(side, parts, grid):
    if side is None:
        return (lambda: None), (lambda: None)
    sins, souts, sems = parts
    ids = [pl.program_id(a) for a in range(len(grid))]
    first = functools.reduce(jnp.logical_and, [i == 0 for i in ids])
    last = functools.reduce(jnp.logical_and, [i == n - 1 for i, n in zip(ids, grid)])

    def begin():
        @pl.when(first)
        def _():
            side.start(sins, souts, *sems)

    def end():
        @pl.when(last)
        def _():
            side.finish(sins, souts, *sems)

    return begin, end


def _side_call_args(side, n_in, n_out):
    if side is None:
        return [], [], [], [], [], {}
    sems = [pltpu.SemaphoreType.DMA((side.n_sems,)), pltpu.SemaphoreType.DMA((side.n_sems,))]
    aliases = {n_in + k: n_out + k for k in range(len(side.ins))} if side.in_place else {}
    return [ANY] * len(side.ins), [ANY] * len(side.out_shapes), side.out_shapes, sems, side.ins, aliases


def _mm(a, b, *, ta=False, tb=False, add=None, out_dtype=F32, name, tiles=MM_TILES, side=None):
    if ta:
        kdim, m = a.shape
    else:
        m, kdim = a.shape
    if tb:
        n, k2 = b.shape
    else:
        k2, n = b.shape
    assert kdim == k2, (a.shape, b.shape, ta, tb)
    tm, tn, tk = _tile(m, tiles[0]), _tile(n, tiles[1]), _tile(kdim, tiles[2])
    nk = kdim // tk
    grid = (m // tm, n // tn, nk)
    dims = (((0 if ta else 1,), (1 if tb else 0,)), ((), ()))
    n_in = 2 if add is None else 3

    def body(*refs):
        ins, (o_ref,), (acc_ref,), parts = _side_split(refs, n_in, 1, side)
        begin, end = _side_steps(side, parts, grid)
        begin()
        a_ref, b_ref = ins[:2]
        k = pl.program_id(2)

        @pl.when(k == 0)
        def _():
            acc_ref[...] = jnp.zeros_like(acc_ref)

        acc_ref[...] += lax.dot_general(a_ref[...].astype(BF16), b_ref[...].astype(BF16), dims,
                                        preferred_element_type=F32)

        @pl.when(k == nk - 1)
        def _():
            r = acc_ref[...]
            if add is not None:
                r = r + ins[2][...]
            o_ref[...] = r.astype(o_ref.dtype)

        end()

    a_spec = pl.BlockSpec((tk, tm), lambda i, j, k: (k, i)) if ta else pl.BlockSpec((tm, tk), lambda i, j, k: (i, k))
    b_spec = pl.BlockSpec((tn, tk), lambda i, j, k: (j, k)) if tb else pl.BlockSpec((tk, tn), lambda i, j, k: (k, j))
    o_spec = pl.BlockSpec((tm, tn), lambda i, j, k: (i, j))
    ins, specs = [a, b], [a_spec, b_spec]
    if add is not None:
        ins.append(add)
        specs.append(o_spec)
    s_in, s_out, s_shapes, s_sems, s_ops, aliases = _side_call_args(side, n_in, 1)
    sem = ("arbitrary",) * 3 if side else ("parallel", "parallel", "arbitrary")
    outs = pl.pallas_call(
        body, grid=grid, in_specs=specs + s_in, out_specs=(o_spec, *s_out),
        out_shape=(jax.ShapeDtypeStruct((m, n), out_dtype), *s_shapes),
        scratch_shapes=[pltpu.VMEM((tm, tn), F32)] + s_sems, input_output_aliases=aliases,
        compiler_params=_cp(*sem), name=name,
    )(*ins, *s_ops)
    return outs if side else outs[0]


def _rms_fwd(x, w, res=None, *, out_dtype, name):
    s, d = x.shape
    tr = _tile(s, 256, 8)

    def body(*refs):
        if res is None:
            x_ref, w_ref, o_ref = refs
        else:
            x_ref, w_ref, r_ref, o_ref = refs
        xv = x_ref[...]
        y = xv * lax.rsqrt(jnp.mean(xv * xv, axis=-1, keepdims=True) + EPS) * w_ref[...]
        if res is not None:
            y = y + r_ref[...]
        o_ref[...] = y.astype(o_ref.dtype)

    row = pl.BlockSpec((tr, d), lambda i: (i, 0))
    par = pl.BlockSpec((1, d), lambda i: (0, 0))
    ins, specs = [x, w.reshape(1, d)], [row, par]
    if res is not None:
        ins.append(res)
        specs.append(row)
    return pl.pallas_call(body, grid=(s // tr,), in_specs=specs, out_specs=row,
                          out_shape=jax.ShapeDtypeStruct((s, d), out_dtype),
                          compiler_params=_cp("parallel"), name=name)(*ins)


def _rms_bwd(x, w, dy, add=None, *, out_dtype=F32, name):
    s, d = x.shape
    tr = _tile(s, 256, 8)

    def body(*refs):
        if add is None:
            x_ref, w_ref, dy_ref, dx_ref, dw_ref = refs
        else:
            x_ref, w_ref, dy_ref, add_ref, dx_ref, dw_ref = refs
        xv = x_ref[...]
        dyv = dy_ref[...].astype(F32)
        r = lax.rsqrt(jnp.mean(xv * xv, axis=-1, keepdims=True) + EPS)
        xh = xv * r
        dyw = dyv * w_ref[...]
        dx = r * (dyw - xh * jnp.mean(dyw * xh, axis=-1, keepdims=True))
        if add is not None:
            dx = dx + add_ref[...]
        dx_ref[...] = dx.astype(dx_ref.dtype)

        @pl.when(pl.program_id(0) == 0)
        def _():
            dw_ref[...] = jnp.zeros_like(dw_ref)

        dw_ref[...] += jnp.sum(dyv * xh, axis=0, keepdims=True)

    row = pl.BlockSpec((tr, d), lambda i: (i, 0))
    par = pl.BlockSpec((1, d), lambda i: (0, 0))
    ins, specs = [x, w.reshape(1, d), dy], [row, par, row]
    if add is not None:
        ins.append(add)
        specs.append(row)
    dx, dw = pl.pallas_call(
        body, grid=(s // tr,), in_specs=specs, out_specs=(row, par),
        out_shape=(jax.ShapeDtypeStruct((s, d), out_dtype), jax.ShapeDtypeStruct((1, d), F32)),
        compiler_params=_cp("arbitrary"), name=name)(*ins)
    return dx, dw.reshape(d)


def _loss_head(h, tgt):
    s, d = h.shape
    tr = _tile(s, 256, 8)

    def body(h_ref, t_ref, dh_ref, l_ref):
        e = h_ref[...] - t_ref[...]
        dh_ref[...] = e * (1.0 / d)

        @pl.when(pl.program_id(0) == 0)
        def _():
            l_ref[...] = jnp.zeros_like(l_ref)

        l_ref[...] += 0.5 * jnp.sum(jnp.mean(e * e, axis=-1, keepdims=True), axis=0, keepdims=True)

    row = pl.BlockSpec((tr, d), lambda i: (i, 0))
    dh, l = pl.pallas_call(
        body, grid=(s // tr,), in_specs=[row, row], out_specs=(row, pl.BlockSpec((1, 1), lambda i: (0, 0))),
        out_shape=(jax.ShapeDtypeStruct((s, d), F32), jax.ShapeDtypeStruct((1, 1), F32)),
        compiler_params=_cp("arbitrary"), name="loss_head")(h, tgt)
    return l[0, 0], dh


def _conv_rows(xe, w, width):
    y = None
    for j in range(width):
        s = width - 1 - j
        term = (pltpu.roll(xe, s, 0) if s else xe)[HALO:] * w[j:j + 1, :]
        y = term if y is None else y + term
    return y


def _conv_rows_t(dy_ext, w, width, tr):
    n = dy_ext.shape[0]
    dx = None
    for j in range(width):
        s = width - 1 - j
        term = (pltpu.roll(dy_ext, n - s, 0) if s else dy_ext)[:tr] * w[j:j + 1, :]
        dx = term if dx is None else dx + term
    return dx


def _conv_dw(dy, xe, width):
    rows = []
    for j in range(width):
        s = width - 1 - j
        rows.append(jnp.sum(dy * (pltpu.roll(xe, s, 0) if s else xe)[HALO:], axis=0, keepdims=True))
    return jnp.concatenate(rows, axis=0)


def _halo_specs(tr, tc, coff, nrow_blocks):
    per = tr // HALO
    last = nrow_blocks * per - 1
    cur = pl.BlockSpec((tr, tc), lambda j, r: (r, coff + j))
    prev = pl.BlockSpec((HALO, tc), lambda j, r: (jnp.maximum(r * per - 1, 0), coff + j))
    nxt = pl.BlockSpec((HALO, tc), lambda j, r: (jnp.minimum((r + 1) * per, last), coff + j))
    return cur, prev, nxt


def _dnconv_fwd(proj, w, cfg):
    s, width = cfg.S, 3 * cfg.HD * HEAD
    tc = _tile(math.gcd(cfg.QD, width), 512)
    tr = _tile(s, 512, 8)
    nr = s // tr
    cur, prev, _ = _halo_specs(tr, tc, cfg.QD // tc, nr)

    def body(x_ref, xp_ref, w_ref, o_ref):
        hist = jnp.where(pl.program_id(1) > 0, xp_ref[...], 0.0)
        y = _conv_rows(jnp.concatenate([hist, x_ref[...]], axis=0), w_ref[...], DN_CONV)
        o_ref[...] = y * _sigmoid(y)

    return pl.pallas_call(
        body, grid=(width // tc, nr),
        in_specs=[cur, prev, pl.BlockSpec((DN_CONV, tc), lambda j, r: (0, j))],
        out_specs=pl.BlockSpec((tr, tc), lambda j, r: (r, j)),
        out_shape=jax.ShapeDtypeStruct((s, width), F32),
        compiler_params=_cp("parallel", "parallel"), name="dnconv_fwd")(proj, proj, w)


def _dnconv_bwd(proj, du3, w, cfg):
    s, hw = cfg.S, cfg.HD * HEAD
    width = 3 * hw
    tc = _tile(math.gcd(cfg.QD, hw), 512)
    tr = _tile(s, 512, 8)
    nr = s // tr
    per, last = tr // HALO, s // HALO - 1
    cur, prev, nxt = _halo_specs(tr, tc, cfg.QD // tc, nr)
    pc = hw // tc
    du_cur = pl.BlockSpec((1, tr, tc), lambda j, r: (j // pc, r, j % pc))
    du_nxt = pl.BlockSpec((1, HALO, tc), lambda j, r: (j // pc, jnp.minimum((r + 1) * per, last), j % pc))

    def body(x_ref, xp_ref, xn_ref, du_ref, dun_ref, w_ref, dx_ref, dw_ref):
        r = pl.program_id(1)
        wv = w_ref[...]
        hist = jnp.where(r > 0, xp_ref[...], 0.0)
        xe = jnp.concatenate([hist, x_ref[...]], axis=0)
        y_ext = _conv_rows(jnp.concatenate([xe, xn_ref[...]], axis=0), wv, DN_CONV)
        du_ext = jnp.concatenate([du_ref[0], jnp.where(r < nr - 1, dun_ref[0], 0.0)], axis=0)
        dy_ext = du_ext * _dsilu(y_ext)
        dx_ref[...] = _conv_rows_t(dy_ext, wv, DN_CONV, tr).astype(dx_ref.dtype)

        @pl.when(r == 0)
        def _():
            dw_ref[...] = jnp.zeros_like(dw_ref)

        dw_ref[...] += _conv_dw(dy_ext[:tr], xe, DN_CONV)

    wspec = pl.BlockSpec((DN_CONV, tc), lambda j, r: (0, j))
    return pl.pallas_call(
        body, grid=(width // tc, nr),
        in_specs=[cur, prev, nxt, du_cur, du_nxt, wspec],
        out_specs=(pl.BlockSpec((tr, tc), lambda j, r: (r, j)), wspec),
        out_shape=(jax.ShapeDtypeStruct((s, width), BF16), jax.ShapeDtypeStruct((DN_CONV, width), F32)),
        compiler_params=_cp("parallel", "arbitrary"), name="dnconv_bwd")(proj, proj, proj, du3, du3, w)


def _glu_fwd(up, w, b, cfg):
    s, ff = cfg.S, cfg.FF
    tc = _tile(ff, 512)
    tr = _tile(s, 512, 8)
    nr, nc = s // tr, ff // tc
    g_cur, g_prev, _ = _halo_specs(tr, tc, 0, nr)
    v_cur, v_prev, _ = _halo_specs(tr, tc, nc, nr)

    def wspec(rows, off):
        return pl.BlockSpec((rows, tc), lambda j, r: (0, off + j))

    def body(g_ref, gp_ref, v_ref, vp_ref, wg_ref, wv_ref, bg_ref, bv_ref, o_ref):
        first = pl.program_id(1) > 0
        ug = _conv_rows(jnp.concatenate([jnp.where(first, gp_ref[...], 0.0), g_ref[...]], axis=0), wg_ref[...], FFN_CONV) + bg_ref[...]
        uv = _conv_rows(jnp.concatenate([jnp.where(first, vp_ref[...], 0.0), v_ref[...]], axis=0), wv_ref[...], FFN_CONV) + bv_ref[...]
        o_ref[...] = (ug * _sigmoid(ug) * uv).astype(o_ref.dtype)

    b2 = b.reshape(1, 2 * ff)
    return pl.pallas_call(
        body, grid=(nc, nr),
        in_specs=[g_cur, g_prev, v_cur, v_prev, wspec(FFN_CONV, 0), wspec(FFN_CONV, nc), wspec(1, 0), wspec(1, nc)],
        out_specs=pl.BlockSpec((tr, tc), lambda j, r: (r, j)),
        out_shape=jax.ShapeDtypeStruct((s, ff), BF16),
        compiler_params=_cp("parallel", "parallel"), name="glu_fwd")(up, up, up, up, w, w, b2, b2)


def _glu_bwd(up, dact, w, b, cfg):
    s, ff = cfg.S, cfg.FF
    tc = _tile(ff, 512)
    tr = _tile(s, 512, 8)
    nr, nc = s // tr, ff // tc
    g_cur, g_prev, g_nxt = _halo_specs(tr, tc, 0, nr)
    v_cur, v_prev, v_nxt = _halo_specs(tr, tc, nc, nr)
    d_cur, _, d_nxt = _halo_specs(tr, tc, 0, nr)

    def wspec(rows, off):
        return pl.BlockSpec((rows, tc), lambda j, r: (0, off + j))

    def body(g_ref, gp_ref, gn_ref, v_ref, vp_ref, vn_ref, d_ref, dn_ref, wg_ref, wv_ref, bg_ref, bv_ref,
             dup_ref, dw_ref, db_ref):
        r = pl.program_id(1)
        wg, wv = wg_ref[...], wv_ref[...]
        ge = jnp.concatenate([jnp.where(r > 0, gp_ref[...], 0.0), g_ref[...]], axis=0)
        ve = jnp.concatenate([jnp.where(r > 0, vp_ref[...], 0.0), v_ref[...]], axis=0)
        ug = _conv_rows(jnp.concatenate([ge, gn_ref[...]], axis=0), wg, FFN_CONV) + bg_ref[...]
        uv = _conv_rows(jnp.concatenate([ve, vn_ref[...]], axis=0), wv, FFN_CONV) + bv_ref[...]
        da = jnp.concatenate([d_ref[...].astype(F32), jnp.where(r < nr - 1, dn_ref[...].astype(F32), 0.0)], axis=0)
        dug = da * uv * _dsilu(ug)
        duv = da * ug * _sigmoid(ug)
        dup_ref[0] = _conv_rows_t(dug, wg, FFN_CONV, tr).astype(dup_ref.dtype)
        dup_ref[1] = _conv_rows_t(duv, wv, FFN_CONV, tr).astype(dup_ref.dtype)

        @pl.when(r == 0)
        def _():
            dw_ref[...] = jnp.zeros_like(dw_ref)
            db_ref[...] = jnp.zeros_like(db_ref)

        dw_ref[0] += _conv_dw(dug[:tr], ge, FFN_CONV)
        dw_ref[1] += _conv_dw(duv[:tr], ve, FFN_CONV)
        db_ref[0] += jnp.sum(dug[:tr], axis=0, keepdims=True)
        db_ref[1] += jnp.sum(duv[:tr], axis=0, keepdims=True)

    b2 = b.reshape(1, 2 * ff)
    dup, dw, db = pl.pallas_call(
        body, grid=(nc, nr),
        in_specs=[g_cur, g_prev, g_nxt, v_cur, v_prev, v_nxt, d_cur, d_nxt,
                  wspec(FFN_CONV, 0), wspec(FFN_CONV, nc), wspec(1, 0), wspec(1, nc)],
        out_specs=(pl.BlockSpec((2, tr, tc), lambda j, r: (0, r, j)),
                   pl.BlockSpec((2, FFN_CONV, tc), lambda j, r: (0, 0, j)),
                   pl.BlockSpec((2, 1, tc), lambda j, r: (0, 0, j))),
        out_shape=(jax.ShapeDtypeStruct((2, s, ff), BF16), jax.ShapeDtypeStruct((2, FFN_CONV, ff), F32),
                   jax.ShapeDtypeStruct((2, 1, ff), F32)),
        compiler_params=_cp("parallel", "arbitrary"), name="glu_bwd",
    )(up, up, up, up, up, up, dact, dact, w, w, b2, b2)
    dup = jnp.concatenate([dup[0], dup[1]], axis=1)
    dw = jnp.transpose(dw, (1, 0, 2)).reshape(FFN_CONV, 2 * ff)
    return dup, dw, db.reshape(2 * ff)


def _merge_specs(cfg):
    tc = _tile(math.gcd(math.gcd(cfg.GA, cfg.GD), cfg.D), 512)
    tr = _tile(cfg.S, 512, 8)
    ga = pl.BlockSpec((tr, tc), lambda r, j: (r, cfg.GA // tc + j))
    gd = pl.BlockSpec((tr, tc), lambda r, j: (r, cfg.GD // tc + j))
    pd = pl.BlockSpec((tr, tc), lambda r, j: (r, j))
    return (cfg.S // tr, cfg.D // tc), ga, gd, pd


def _merge_fwd(proj, pa, pdn, cfg):
    grid, ga, gd, pd = _merge_specs(cfg)

    def body(ga_ref, gd_ref, pa_ref, pd_ref, y_ref):
        y_ref[...] = (_sigmoid(ga_ref[...]) * pa_ref[...] + _sigmoid(gd_ref[...]) * pd_ref[...]).astype(y_ref.dtype)

    return pl.pallas_call(body, grid=grid, in_specs=[ga, gd, pd, pd], out_specs=pd,
                          out_shape=jax.ShapeDtypeStruct((cfg.S, cfg.D), BF16),
                          compiler_params=_cp("parallel", "parallel"), name="merge_fwd")(proj, proj, pa, pdn)


def _merge_bwd(proj, pa, pdn, dy, cfg):
    grid, ga, gd, pd = _merge_specs(cfg)

    def body(ga_ref, gd_ref, pa_ref, pd_ref, dy_ref, dpa_ref, dpd_ref, dga_ref, dgd_ref):
        dyv = dy_ref[...]
        sa, sd = _sigmoid(ga_ref[...]), _sigmoid(gd_ref[...])
        dpa_ref[...] = (dyv * sa).astype(BF16)
        dpd_ref[...] = (dyv * sd).astype(BF16)
        dga_ref[...] = (dyv * pa_ref[...] * sa * (1.0 - sa)).astype(BF16)
        dgd_ref[...] = (dyv * pd_ref[...] * sd * (1.0 - sd)).astype(BF16)

    out = jax.ShapeDtypeStruct((cfg.S, cfg.D), BF16)
    return pl.pallas_call(body, grid=grid, in_specs=[ga, gd, pd, pd, pd], out_specs=(pd,) * 4,
                          out_shape=(out,) * 4, compiler_params=_cp("parallel", "parallel"),
                          name="merge_bwd")(proj, proj, pa, pdn, dy)


def _rope(x, cos, sin):
    lane = lax.broadcasted_iota(jnp.int32, x.shape, 1)
    swapped = jnp.where(lane < ROT_HALF, pltpu.roll(x, LANES - ROT_HALF, 1), pltpu.roll(x, ROT_HALF, 1))
    return x * cos + swapped * sin


def _rope_t(dx, cos, sin):
    t = dx * sin
    lane = lax.broadcasted_iota(jnp.int32, dx.shape, 1)
    swapped = jnp.where(lane < ROT_HALF, pltpu.roll(t, LANES - ROT_HALF, 1), pltpu.roll(t, ROT_HALF, 1))
    return dx * cos + swapped


def _band_mask(i):
    r = lax.broadcasted_iota(jnp.int32, (ATTN_BLOCK, 2 * ATTN_BLOCK), 0)
    c = lax.broadcasted_iota(jnp.int32, (ATTN_BLOCK, 2 * ATTN_BLOCK), 1)
    dist = r + ATTN_BLOCK - c
    return (dist >= 0) & (dist < ATTN_BLOCK) & ((c >= ATTN_BLOCK) | (i > 0))


def _dot(a, b, ca, cb):
    return lax.dot_general(a.astype(BF16), b.astype(BF16), (((ca,), (cb,)), ((), ())), preferred_element_type=F32)


def _attn_probs(qg, k_cat, sink, valid):
    logits = jnp.where(valid, _dot(qg, k_cat, 1, 1) * (HEAD ** -0.5), -1e30)
    m = jnp.maximum(jnp.max(logits, axis=-1, keepdims=True), sink)
    p = jnp.exp(logits - m)
    es = jnp.exp(sink - m)
    inv = 1.0 / (jnp.sum(p, axis=-1, keepdims=True) + es)
    return p * inv, es * inv


def _attn_specs(cfg, nb, clamp):
    gw = cfg.G * HEAD
    qi = (lambda kv, i: (jnp.minimum(i, nb - 1), kv)) if clamp else (lambda kv, i: (i, kv))
    ci = (lambda i: jnp.minimum(i, nb - 1)) if clamp else (lambda i: i)
    pi = lambda i: jnp.maximum(ci(i) - 1, 0)
    ko, vo = cfg.KA // HEAD, cfg.VA // HEAD
    blk = (ATTN_BLOCK, HEAD)
    specs = [
        pl.BlockSpec((ATTN_BLOCK, gw), qi),
        pl.BlockSpec(blk, lambda kv, i: (ci(i), ko + kv)), pl.BlockSpec(blk, lambda kv, i: (pi(i), ko + kv)),
        pl.BlockSpec(blk, lambda kv, i: (ci(i), vo + kv)), pl.BlockSpec(blk, lambda kv, i: (pi(i), vo + kv)),
        pl.BlockSpec(blk, lambda kv, i: (ci(i), 0)), pl.BlockSpec(blk, lambda kv, i: (pi(i), 0)),
        pl.BlockSpec(blk, lambda kv, i: (ci(i), 0)), pl.BlockSpec(blk, lambda kv, i: (pi(i), 0)),
        pl.BlockSpec((1, 8, LANES), lambda kv, i: (kv, 0, 0)),
    ]
    return specs, pl.BlockSpec((ATTN_BLOCK, gw), qi)


def _sink_rows(sinks, cfg):
    sk = jnp.broadcast_to(sinks.reshape(cfg.KV, cfg.G, 1), (cfg.KV, cfg.G, LANES))
    return jnp.pad(sk, ((0, 0), (0, 8 - cfg.G), (0, 0)))


def _attn_fwd(proj, cosf, sinf, sinks, cfg):
    nb = cfg.S // ATTN_BLOCK
    specs, qspec = _attn_specs(cfg, nb, False)

    def body(q_ref, kc_ref, kp_ref, vc_ref, vp_ref, cc_ref, cp_ref, sc_ref, sp_ref, sk_ref, o_ref):
        i = pl.program_id(1)
        cc, sc = cc_ref[...], sc_ref[...]
        k_cat = jnp.concatenate([_rope(kp_ref[...], cp_ref[...], sp_ref[...]), _rope(kc_ref[...], cc, sc)], axis=0).astype(BF16)
        v_cat = jnp.concatenate([vp_ref[...], vc_ref[...]], axis=0).astype(BF16)
        valid = _band_mask(i)
        for g in range(cfg.G):
            qg = _rope(q_ref[:, g * HEAD:(g + 1) * HEAD], cc, sc)
            p, _ = _attn_probs(qg, k_cat, sk_ref[0, g:g + 1, 0:1], valid)
            o_ref[:, g * HEAD:(g + 1) * HEAD] = _dot(p, v_cat, 1, 0).astype(o_ref.dtype)

    return pl.pallas_call(
        body, grid=(cfg.KV, nb), in_specs=specs, out_specs=qspec,
        out_shape=jax.ShapeDtypeStruct((cfg.S, cfg.HA * HEAD), BF16),
        compiler_params=_cp("parallel", "parallel"), name="attn_fwd",
    )(proj, proj, proj, proj, proj, cosf, cosf, sinf, sinf, _sink_rows(sinks, cfg))


def _attn_bwd(proj, do, cosf, sinf, sinks, cfg):
    nb = cfg.S // ATTN_BLOCK
    specs, qspec = _attn_specs(cfg, nb, True)
    kv_out = pl.BlockSpec((ATTN_BLOCK, HEAD), lambda kv, i: (jnp.maximum(i - 1, 0), kv))

    def body(q_ref, kc_ref, kp_ref, vc_ref, vp_ref, cc_ref, cp_ref, sc_ref, sp_ref, sk_ref, do_ref,
             dq_ref, dk_ref, dv_ref, dsk_ref, ck_ref, cv_ref):
        i = pl.program_id(1)

        @pl.when(i == 0)
        def _():
            ck_ref[...] = jnp.zeros_like(ck_ref)
            cv_ref[...] = jnp.zeros_like(cv_ref)
            dsk_ref[...] = jnp.zeros_like(dsk_ref)

        @pl.when(i < nb)
        def _():
            cc, sc, cp, sp = cc_ref[...], sc_ref[...], cp_ref[...], sp_ref[...]
            k_cat = jnp.concatenate([_rope(kp_ref[...], cp, sp), _rope(kc_ref[...], cc, sc)], axis=0).astype(BF16)
            v_cat = jnp.concatenate([vp_ref[...], vc_ref[...]], axis=0).astype(BF16)
            valid = _band_mask(i)
            dk_cat = jnp.zeros((2 * ATTN_BLOCK, HEAD), F32)
            dv_cat = jnp.zeros((2 * ATTN_BLOCK, HEAD), F32)
            for g in range(cfg.G):
                sl = slice(g * HEAD, (g + 1) * HEAD)
                qg = _rope(q_ref[:, sl], cc, sc)
                dog = do_ref[:, sl]
                p, ps = _attn_probs(qg, k_cat, sk_ref[0, g:g + 1, 0:1], valid)
                dp = _dot(dog, v_cat, 1, 1)
                delta = jnp.sum(p * dp, axis=-1, keepdims=True)
                ds = p * (dp - delta) * (HEAD ** -0.5)
                dq_ref[:, sl] = _rope_t(_dot(ds, k_cat, 1, 0), cc, sc).astype(dq_ref.dtype)
                dk_cat = dk_cat + _dot(ds, qg, 0, 0)
                dv_cat = dv_cat + _dot(p, dog, 0, 0)
                dsk_ref[0, g:g + 1, :] += jnp.broadcast_to(-jnp.sum(ps * delta, axis=0, keepdims=True), (1, LANES))
            dk_ref[...] = (ck_ref[...] + _rope_t(dk_cat[:ATTN_BLOCK], cp, sp)).astype(dk_ref.dtype)
            dv_ref[...] = (cv_ref[...] + dv_cat[:ATTN_BLOCK]).astype(dv_ref.dtype)
            ck_ref[...] = _rope_t(dk_cat[ATTN_BLOCK:], cc, sc)
            cv_ref[...] = dv_cat[ATTN_BLOCK:]

        @pl.when(i == nb)
        def _():
            dk_ref[...] = ck_ref[...].astype(dk_ref.dtype)
            dv_ref[...] = cv_ref[...].astype(dv_ref.dtype)

    kvw = cfg.KV * HEAD
    dq, dk, dv, dsk = pl.pallas_call(
        body, grid=(cfg.KV, nb + 1), in_specs=specs + [qspec],
        out_specs=(qspec, kv_out, kv_out, pl.BlockSpec((1, 8, LANES), lambda kv, i: (kv, 0, 0))),
        out_shape=(jax.ShapeDtypeStruct((cfg.S, cfg.HA * HEAD), BF16), jax.ShapeDtypeStruct((cfg.S, kvw), BF16),
                   jax.ShapeDtypeStruct((cfg.S, kvw), BF16), jax.ShapeDtypeStruct((cfg.KV, 8, LANES), F32)),
        scratch_shapes=[pltpu.VMEM((ATTN_BLOCK, HEAD), F32), pltpu.VMEM((ATTN_BLOCK, HEAD), F32)],
        compiler_params=_cp("parallel", "arbitrary"), name="attn_bwd",
    )(proj, proj, proj, proj, proj, cosf, cosf, sinf, sinf, _sink_rows(sinks, cfg), do)
    return jnp.concatenate([dq, dk, dv], axis=1), dsk[:, :cfg.G, 0].reshape(cfg.HA)


@jax.custom_vjp
def _nn(a, b):
    return _dot(a, b, 1, 0)


@jax.custom_vjp
def _nt(a, b):
    return _dot(a, b, 1, 1)


@jax.custom_vjp
def _tn(a, b):
    return _dot(a, b, 0, 0)


_nn.defvjp(lambda a, b: (_nn(a, b), (a, b)), lambda r, g: (_nt(g, r[1]), _tn(r[0], g)))
_nt.defvjp(lambda a, b: (_nt(a, b), (a, b)), lambda r, g: (_nn(g, r[1]), _tn(g, r[0])))
_tn.defvjp(lambda a, b: (_tn(a, b), (a, b)), lambda r, g: (_nt(r[1], g), _nn(r[0], g)))


def _dgx(a, b, ca, cb):
    return lax.dot_general(a, b, (((ca,), (cb,)), ((), ())), precision=lax.Precision.HIGH, preferred_element_type=F32)


def _unit_lower_inverse(a):
    ii = lax.broadcasted_iota(jnp.int32, a.shape, 0)
    jj = lax.broadcasted_iota(jnp.int32, a.shape, 1)
    t = jnp.where(ii == jj, 1.0, 0.0) - a
    p = a
    for _ in range(int(math.log2(CHUNK)) - 1):
        p = _dgx(p, p, 1, 0)
        t = t + _dgx(t, p, 1, 0)
    return t


@jax.custom_vjp
def _tri_inv(a):
    return _unit_lower_inverse(a)


def _tri_inv_fwd(a):
    t = _unit_lower_inverse(a)
    return t, t


def _tri_inv_bwd(t, g):
    return (-_dgx(t, _dgx(g, t, 1, 1), 0, 0),)


_tri_inv.defvjp(_tri_inv_fwd, _tri_inv_bwd)


def _gdn_chunk(q, k, v, z, b_col, a_col, a_row, al_col, dt_col, al_row, dt_row, nw, s):
    rows = q.shape[0]
    hb = rows // CHUNK
    ii = lax.broadcasted_iota(jnp.int32, (rows, rows), 0)
    jj = lax.broadcasted_iota(jnp.int32, (rows, rows), 1)
    same = (ii // CHUNK) == (jj // CHUNK)
    tri, strict, upper = same & (ii >= jj), same & (ii > jj), same & (ii <= jj)
    beta = _sigmoid(b_col)
    g_col = -jnp.exp(al_col) * _softplus(a_col + dt_col)
    g_row = -jnp.exp(al_row) * _softplus(a_row + dt_row)
    gc_col = jnp.sum(jnp.where(tri, g_row, 0.0), axis=1, keepdims=True)
    gc_row = jnp.sum(jnp.where(upper, g_col, 0.0), axis=0, keepdims=True)
    last = jj == (ii // CHUNK) * CHUNK + (CHUNK - 1)
    gl_col = jnp.sum(jnp.where(last, gc_row, 0.0), axis=1, keepdims=True)
    decay = jnp.where(tri, jnp.exp(jnp.where(tri, gc_col - gc_row, 0.0)), 0.0)
    qn = q * lax.rsqrt(jnp.sum(q * q, axis=-1, keepdims=True) + EPS) * (HEAD ** -0.5)
    kn = k * lax.rsqrt(jnp.sum(k * k, axis=-1, keepdims=True) + EPS)
    t_inv = _tri_inv(jnp.where(strict, beta * _nt(kn, kn) * decay, 0.0))
    eg = jnp.exp(gc_col)
    u = _nn(t_inv, v * beta)
    w = _nn(t_inv, kn * (beta * eg))
    qk = jnp.where(tri, _nt(qn, kn) * decay, 0.0)
    ri = lax.broadcasted_iota(jnp.int32, (rows, hb * HEAD), 0)
    ci = lax.broadcasted_iota(jnp.int32, (rows, hb * HEAD), 1)
    own = (ri // CHUNK) == (ci // HEAD)

    def spread(x):
        return jnp.where(own, jnp.concatenate([x] * hb, axis=1), 0.0)

    v_new = u - _nn(spread(w), s)
    o = _nn(spread(qn * eg), s) + _nn(qk, v_new)
    si = lax.broadcasted_iota(jnp.int32, (hb * HEAD, rows), 0)
    sj = lax.broadcasted_iota(jnp.int32, (hb * HEAD, rows), 1)
    gl_s = jnp.sum(jnp.where(sj == (si // HEAD) * CHUNK + (CHUNK - 1), gc_row, 0.0), axis=1, keepdims=True)
    s1 = s * jnp.exp(gl_s) + _tn(spread(kn * jnp.exp(gl_col - gc_col)), v_new)
    od = o * lax.rsqrt(jnp.mean(o * o, axis=-1, keepdims=True) + EPS) * nw * (z * _sigmoid(z))
    return od, s1


def _gdn_heads_per_step(cfg):
    for hb in (GDN_HEADS_PER_STEP, 2, 1):
        if cfg.HD % hb == 0 and cfg.Z % (hb * HEAD) == 0:
            return hb


def _gdn_in_specs(cfg, nc, rev):
    hb = _gdn_heads_per_step(cfg)
    ng, rows = cfg.HD // hb, hb * CHUNK
    ci = (lambda n: nc - 1 - n) if rev else (lambda n: n)
    zo = cfg.Z // (hb * HEAD)
    blk = (CHUNK, hb * HEAD)
    col = pl.BlockSpec((1, 1, rows, 1), lambda g, n: (g, ci(n), 0, 0))
    row = pl.BlockSpec((1, 1, 1, rows), lambda g, n: (g, ci(n), 0, 0))
    hcol = pl.BlockSpec((1, rows, 1), lambda g, n: (g, 0, 0))
    hrow = pl.BlockSpec((1, 1, rows), lambda g, n: (g, 0, 0))
    return [
        pl.BlockSpec(blk, lambda g, n: (ci(n), g)), pl.BlockSpec(blk, lambda g, n: (ci(n), ng + g)),
        pl.BlockSpec(blk, lambda g, n: (ci(n), 2 * ng + g)), pl.BlockSpec(blk, lambda g, n: (ci(n), zo + g)),
        col, col, row, hcol, hcol, hrow, hrow, pl.BlockSpec((1, HEAD), lambda g, n: (0, 0)),
    ]


def _gdn_args(qkv, proj, pba, a_log, dt_bias, nw, cfg):
    hb, hd, s = _gdn_heads_per_step(cfg), cfg.HD, cfg.S
    ng, nc, rows = hd // hb, s // CHUNK, hb * CHUNK

    def grouped(x):
        return jnp.transpose(x.reshape(nc, CHUNK, ng, hb), (2, 0, 3, 1)).reshape(ng, nc, rows)

    b, a = grouped(pba[:, :hd]), grouped(pba[:, hd:2 * hd])
    per_row = lambda x: jnp.repeat(x.reshape(ng, hb), CHUNK, axis=1)
    al, dt = per_row(a_log), per_row(dt_bias)
    return (qkv, qkv, qkv, proj, b.reshape(ng, nc, rows, 1), a.reshape(ng, nc, rows, 1), a.reshape(ng, nc, 1, rows),
            al.reshape(ng, rows, 1), dt.reshape(ng, rows, 1), al.reshape(ng, 1, rows), dt.reshape(ng, 1, rows),
            nw.reshape(1, HEAD))


def _stack_heads(x, hb):
    return jnp.concatenate([x[:, i * HEAD:(i + 1) * HEAD] for i in range(hb)], axis=0)


def _gdn_load(refs, hb):
    q, k, v, z, b, a, ar, alc, dtc, alr, dtr, nw = refs
    return (_stack_heads(q[...], hb), _stack_heads(k[...], hb), _stack_heads(v[...], hb), _stack_heads(z[...], hb),
            b[0, 0], a[0, 0], ar[0, 0], alc[0], dtc[0], alr[0], dtr[0], nw[...])


def _gdn_fwd(qkv, proj, pba, a_log, dt_bias, nw, cfg, side=None):
    nc, hb = cfg.S // CHUNK, _gdn_heads_per_step(cfg)
    ng = cfg.HD // hb
    grid = (ng, nc)

    def body(*refs):
        ins, (o_ref, st_ref), (s_ref,), parts = _side_split(refs, 12, 2, side)
        begin, end = _side_steps(side, parts, grid)
        begin()

        @pl.when(pl.program_id(1) == 0)
        def _():
            s_ref[...] = jnp.zeros_like(s_ref)

        s0 = s_ref[...]
        st_ref[0, 0] = s0
        od, s1 = _gdn_chunk(*_gdn_load(ins, hb), s0)
        for i in range(hb):
            o_ref[:, i * HEAD:(i + 1) * HEAD] = od[i * CHUNK:(i + 1) * CHUNK].astype(o_ref.dtype)
        s_ref[...] = s1
        end()

    s_in, s_out, s_shapes, s_sems, s_ops, _ = _side_call_args(side, 12, 2)
    return pl.pallas_call(
        body, grid=grid, in_specs=_gdn_in_specs(cfg, nc, False) + s_in,
        out_specs=(pl.BlockSpec((CHUNK, hb * HEAD), lambda g, n: (n, g)),
                   pl.BlockSpec((1, 1, hb * HEAD, HEAD), lambda g, n: (g, n, 0, 0)), *s_out),
        out_shape=(jax.ShapeDtypeStruct((cfg.S, cfg.HD * HEAD), BF16),
                   jax.ShapeDtypeStruct((ng, nc, hb * HEAD, HEAD), F32), *s_shapes),
        scratch_shapes=[pltpu.VMEM((hb * HEAD, HEAD), F32)] + s_sems,
        compiler_params=_cp("arbitrary" if side else "parallel", "arbitrary"),
        name="gdn_fwd_gather" if side else "gdn_fwd",
    )(*_gdn_args(qkv, proj, pba, a_log, dt_bias, nw, cfg), *s_ops)


def _gdn_bwd(qkv, proj, pba, a_log, dt_bias, nw, states, dod, cfg, side=None):
    nc, hd, s, hb = cfg.S // CHUNK, cfg.HD, cfg.S, _gdn_heads_per_step(cfg)
    ng, rows = hd // hb, hb * CHUNK
    rv = lambda n: nc - 1 - n
    col = pl.BlockSpec((1, 1, rows, 1), lambda g, n: (g, rv(n), 0, 0))
    row = pl.BlockSpec((1, 1, 1, rows), lambda g, n: (g, rv(n), 0, 0))
    hcol = pl.BlockSpec((1, rows, 1), lambda g, n: (g, 0, 0))
    hrow = pl.BlockSpec((1, 1, rows), lambda g, n: (g, 0, 0))

    grid = (ng, nc)

    def body(*refs):
        ins, outs, (ds_ref,), parts = _side_split(refs, 14, 10, side)
        begin, end = _side_steps(side, parts, grid)
        begin()
        st_ref, dod_ref = ins[12:14]
        dqkv_ref, dz_ref, db_ref, da_ref, dar_ref, dalc_ref, ddtc_ref, dalr_ref, ddtr_ref, dnw_ref = outs
        g, n = pl.program_id(0), pl.program_id(1)

        @pl.when(n == 0)
        def _():
            ds_ref[...] = jnp.zeros_like(ds_ref)
            for r in (dalc_ref, ddtc_ref, dalr_ref, ddtr_ref):
                r[...] = jnp.zeros_like(r)

        @pl.when((n == 0) & (g == 0))
        def _():
            dnw_ref[...] = jnp.zeros_like(dnw_ref)

        _, vjp = jax.vjp(_gdn_chunk, *_gdn_load(ins[:12], hb), st_ref[0, 0])
        dq, dk, dv, dz, db, da, dar, dalc, ddtc, dalr, ddtr, dnw, ds0 = vjp(
            (_stack_heads(dod_ref[...].astype(F32), hb), ds_ref[...]))
        for i in range(hb):
            sl, rs = slice(i * HEAD, (i + 1) * HEAD), slice(i * CHUNK, (i + 1) * CHUNK)
            dqkv_ref[0, :, sl], dqkv_ref[1, :, sl], dqkv_ref[2, :, sl] = dq[rs], dk[rs], dv[rs]
            dz_ref[:, sl] = dz[rs].astype(dz_ref.dtype)
        db_ref[0, 0], da_ref[0, 0], dar_ref[0, 0] = db, da, dar
        dalc_ref[0] += dalc
        ddtc_ref[0] += ddtc
        dalr_ref[0] += dalr
        ddtr_ref[0] += ddtr
        dnw_ref[...] += dnw
        ds_ref[...] = ds0
        end()

    blk = pl.BlockSpec((CHUNK, hb * HEAD), lambda g, n: (rv(n), g))
    f32 = lambda *sh: jax.ShapeDtypeStruct(sh, F32)
    s_in, s_out, s_shapes, s_sems, s_ops, _ = _side_call_args(side, 14, 10)
    outs = pl.pallas_call(
        body, grid=grid,
        in_specs=_gdn_in_specs(cfg, nc, True) + [pl.BlockSpec((1, 1, hb * HEAD, HEAD), lambda g, n: (g, rv(n), 0, 0)), blk] + s_in,
        out_specs=(pl.BlockSpec((3, CHUNK, hb * HEAD), lambda g, n: (0, rv(n), g)), blk, col, col, row,
                   hcol, hcol, hrow, hrow, pl.BlockSpec((1, HEAD), lambda g, n: (0, 0)), *s_out),
        out_shape=(f32(3, s, hd * HEAD), jax.ShapeDtypeStruct((s, hd * HEAD), BF16),
                   f32(ng, nc, rows, 1), f32(ng, nc, rows, 1), f32(ng, nc, 1, rows),
                   f32(ng, rows, 1), f32(ng, rows, 1), f32(ng, 1, rows), f32(ng, 1, rows), f32(1, HEAD), *s_shapes),
        scratch_shapes=[pltpu.VMEM((hb * HEAD, HEAD), F32)] + s_sems,
        compiler_params=_cp("arbitrary", "arbitrary"), name="gdn_bwd_scatter" if side else "gdn_bwd",
    )(*_gdn_args(qkv, proj, pba, a_log, dt_bias, nw, cfg), states, dod, *s_ops)
    dqkv, dz, db, da, dar, dalc, ddtc, dalr, ddtr, dnw = outs[:10]
    side_outs = list(outs[10:])

    def ungrouped(x):
        return jnp.transpose(x.reshape(ng, nc, hb, CHUNK), (1, 3, 0, 2)).reshape(s, hd)

    per_head = lambda c, r: (c.reshape(ng, hb, CHUNK) + r.reshape(ng, hb, CHUNK)).sum(axis=-1).reshape(hd)
    d_b = ungrouped(db.reshape(ng, nc, rows))
    d_a = ungrouped(da.reshape(ng, nc, rows) + dar.reshape(ng, nc, rows))
    dba = jnp.concatenate([d_b, d_a, jnp.zeros((s, LANES - 2 * hd), F32)], axis=1).astype(BF16)
    return dqkv, dz, dba, per_head(dalc, dalr), per_head(ddtc, ddtr), dnw.reshape(HEAD), side_outs


def _adamw(w, g, m, v, *, name):
    nl, r, c = w.shape
    gc = g.shape[2]
    tr = _tile(r, max(8, (1 << 18) // c // 8 * 8), 8) if r % 8 == 0 else r
    c1, c2 = 1.0 - ADAM_B1 ** ADAM_STEP, 1.0 - ADAM_B2 ** ADAM_STEP

    def body(w_ref, g_ref, m_ref, v_ref, go_ref, d_ref, nm_ref, nv_ref):
        gv = g_ref[:, :c]
        nm = ADAM_B1 * m_ref[...] + (1.0 - ADAM_B1) * gv
        nv = ADAM_B2 * v_ref[...] + (1.0 - ADAM_B2) * (gv * gv)
        go_ref[...] = gv
        d_ref[...] = -ADAM_LR * ((nm / c1) / (jnp.sqrt(nv / c2) + ADAM_EPS) + ADAM_WD * w_ref[...])
        nm_ref[...] = nm
        nv_ref[...] = nv

    spec = pl.BlockSpec((None, tr, c), lambda l, i: (l, i, 0))
    gspec = pl.BlockSpec((None, tr, gc), lambda l, i: (l, i, 0))
    out = jax.ShapeDtypeStruct((nl, r, c), F32)
    return pl.pallas_call(body, grid=(nl, r // tr), in_specs=[spec, gspec, spec, spec], out_specs=(spec,) * 4,
                          out_shape=(out,) * 4, compiler_params=_cp("parallel", "parallel"), name=name)(w, g, m, v)


def _sum_leading(a, *, name):
    n, r, c = a.shape
    tr = _tile(r, SUM_TR, 8)

    def body(a_ref, o_ref):
        acc = a_ref[0].astype(F32)
        for i in range(1, n):
            acc = acc + a_ref[i].astype(F32)
        o_ref[...] = acc

    return pl.pallas_call(body, grid=(r // tr,), in_specs=[pl.BlockSpec((n, tr, c), lambda i: (0, i, 0))],
                          out_specs=pl.BlockSpec((tr, c), lambda i: (i, 0)),
                          out_shape=jax.ShapeDtypeStruct((r, c), F32), compiler_params=_cp("parallel"), name=name)(a)


class _Mat:
    def __init__(self, kind, rows, cols):
        self.kind, self.rows, self.cols = kind, rows, cols
        self.full = (rows, 4 * cols) if kind == "col" else (4 * rows, cols)
        self.half = rows // 2

    def block(self, ref, j, c=None):
        r0 = (0 if self.kind == "col" else j * self.rows) + (0 if c is None else c * self.half)
        rows = pl.ds(_aligned(r0, 16), self.rows if c is None else self.half)
        return ref.at[rows, pl.ds(_aligned(j * self.cols, LANES), self.cols)] if self.kind == "col" else ref.at[rows, :]

    def own_half(self, ref, c):
        return ref.at[pl.ds(_aligned(c * self.half, 16), self.half), :]


def _aligned(v, unit):
    return v if isinstance(v, int) else pl.multiple_of(v, unit)


def _place():
    x, y, c = lax.axis_index("x"), lax.axis_index("y"), lax.axis_index("c")
    chips = [(1 - x, y), (x, 1 - y), (1 - x, 1 - y)]
    return x, y, c, chips


def _chip_index():
    return 2 * lax.axis_index("x") + lax.axis_index("y")


def _remote(src, dst, send_sems, recv_sems, k, to):
    return pltpu.make_async_remote_copy(src_ref=src, dst_ref=dst, send_sem=send_sems.at[k], recv_sem=recv_sems.at[k],
                                        device_id=to, device_id_type=MESH)


def _comm_call(body, ins, out_shapes, n_sems, name):
    return pl.pallas_call(
        body, out_shape=tuple(out_shapes), in_specs=[ANY] * len(ins), out_specs=tuple([ANY] * len(out_shapes)),
        scratch_shapes=[pltpu.SemaphoreType.DMA((n_sems,)), pltpu.SemaphoreType.DMA((n_sems,))],
        compiler_params=pltpu.CompilerParams(has_side_effects=True), name=name)(*ins)


def _gather_copies(own_refs, full_refs, send_sems, recv_sems, mats, base=0):
    x, y, c, chips = _place()
    me = 2 * x + y
    return [(_remote(m.own_half(own_refs[k], c), m.block(full_refs[k], me, c), send_sems, recv_sems, base + 3 * k + j, (cx, cy, c)),
             _remote(m.block(full_refs[k], 2 * cx + cy, c), m.block(full_refs[k], 2 * cx + cy, c), send_sems, recv_sems,
                     base + 3 * k + j, (cx, cy, c)))
            for k, m in enumerate(mats) for j, (cx, cy) in enumerate(chips)]


def _pass_copies(full_refs, send_sems, recv_sems, mats, base=0):
    x, y, c, chips = _place()
    sib = (x, y, 1 - c)
    return [(_remote(m.block(full_refs[k], 2 * cx + cy, c), m.block(full_refs[k], 2 * cx + cy, c), send_sems, recv_sems,
                     base + 3 * k + j, sib),
             _remote(m.block(full_refs[k], 2 * cx + cy, 1 - c), m.block(full_refs[k], 2 * cx + cy, 1 - c), send_sems,
                     recv_sems, base + 3 * k + j, sib))
            for k, m in enumerate(mats) for j, (cx, cy) in enumerate(chips)]


def _start_all(pairs):
    for cp, _ in pairs:
        cp.start()


def _finish_all(pairs):
    for _, landing in pairs:
        landing.wait_recv()
    for cp, _ in pairs:
        cp.wait_send()


def _gather_ici_side(own, mats):
    return _Side(own, [jax.ShapeDtypeStruct(m.full, a.dtype) for m, a in zip(mats, own)], 3 * len(mats),
                 lambda i, o, ss, rs: _start_all(_gather_copies(i, o, ss, rs, mats)),
                 lambda i, o, ss, rs: _finish_all(_gather_copies(i, o, ss, rs, mats)))


def _gather_pass_side(fulls, mats):
    return _Side(fulls, [jax.ShapeDtypeStruct(f.shape, f.dtype) for f in fulls], 3 * len(mats),
                 lambda i, o, ss, rs: _start_all(_pass_copies(o, ss, rs, mats)),
                 lambda i, o, ss, rs: _finish_all(_pass_copies(o, ss, rs, mats)), in_place=True)


def _place_own(fulls, own, mats):
    me = _chip_index()
    return [lax.dynamic_update_slice(o, a, (0, me * m.cols) if m.kind == "col" else (me * m.rows, 0))
            for o, a, m in zip(fulls, own, mats)]


def _gather_weights(own, mats):
    n = len(mats)

    def body(*refs):
        own_refs, full_refs, (send_sems, recv_sems) = refs[:n], refs[n:2 * n], refs[2 * n:]
        first = _gather_copies(own_refs, full_refs, send_sems, recv_sems, mats)
        second = _pass_copies(full_refs, send_sems, recv_sems, mats, base=3 * n)
        _start_all(first)
        for (_, landed), (fwd, _) in zip(first, second):
            landed.wait_recv()
            fwd.start()
        for _, landing in second:
            landing.wait_recv()
        for cp, _ in first + second:
            cp.wait_send()

    outs = _comm_call(body, own, [jax.ShapeDtypeStruct(m.full, a.dtype) for m, a in zip(mats, own)], 6 * n,
                      "gather_weights")
    return _place_own(outs, own, mats)


def _swap_copies(g_refs, b_refs, send_sems, recv_sems, mats):
    x, y, c, _ = _place()
    cps = []
    for g, b, m in zip(g_refs, b_refs, mats):
        if m.kind == "col":
            cps.append(_remote(m.own_half(g, 1 - c), b, send_sems, recv_sems, len(cps), (x, y, 1 - c)))
        else:
            for j in range(4):
                cps.append(_remote(m.block(g, j, 1 - c), b.at[j], send_sems, recv_sems, len(cps), (x, y, 1 - c)))
    return [(cp, cp) for cp in cps]


def _swap_shapes(mats):
    return [jax.ShapeDtypeStruct((m.half, 4 * m.cols) if m.kind == "col" else (4, m.half, m.cols), F32) for m in mats]


def _swap_side(grads, mats):
    return _Side(grads, _swap_shapes(mats), sum(1 if m.kind == "col" else 4 for m in mats),
                 lambda i, o, ss, rs: _start_all(_swap_copies(i, o, ss, rs, mats)),
                 lambda i, o, ss, rs: _finish_all(_swap_copies(i, o, ss, rs, mats)))


def _swap_halves(grads, mats):
    def body(*refs):
        n = len(mats)
        pairs = _swap_copies(refs[:n], refs[n:2 * n], *refs[2 * n:], mats)
        _start_all(pairs)
        _finish_all(pairs)

    return _comm_call(body, grads, _swap_shapes(mats), sum(1 if m.kind == "col" else 4 for m in mats), "rs_swap_halves")


def _pair_sum(own, recv, m, cidx):
    tr = _tile(m.half, 256, 16)
    nh = m.half // tr
    if m.kind == "col":
        own_spec = pl.BlockSpec((tr, m.cols), lambda j, i, c: (c[0] * nh + i, j))
        recv_spec = pl.BlockSpec((tr, m.cols), lambda j, i, c: (i, j))
    else:
        own_spec = pl.BlockSpec((tr, m.cols), lambda j, i, c: ((2 * j + c[0]) * nh + i, 0))
        recv_spec = pl.BlockSpec((None, tr, m.cols), lambda j, i, c: (j, i, 0))

    def body(c_ref, a_ref, b_ref, o_ref):
        o_ref[...] = (a_ref[...] + b_ref[...]).astype(o_ref.dtype)

    return pl.pallas_call(
        body,
        grid_spec=pltpu.PrefetchScalarGridSpec(
            num_scalar_prefetch=1, grid=(4, nh), in_specs=[own_spec, recv_spec],
            out_specs=pl.BlockSpec((None, tr, m.cols), lambda j, i, c: (j, i, 0))),
        out_shape=jax.ShapeDtypeStruct((4, m.half, m.cols), BF16),
        compiler_params=_cp("parallel", "parallel"), name="rs_pair_sum")(cidx, own, recv)


def _scatter_copies(s_refs, b_refs, send_sems, recv_sems):
    x, y, c, chips = _place()
    me = 2 * x + y
    return [(_remote(s.at[2 * cx + cy], b.at[me], send_sems, recv_sems, 3 * k + j, (cx, cy, c)),
             _remote(b.at[2 * cx + cy], b.at[2 * cx + cy], send_sems, recv_sems, 3 * k + j, (cx, cy, c)))
            for k, (s, b) in enumerate(zip(s_refs, b_refs)) for j, (cx, cy) in enumerate(chips)]


def _scatter_side(sps):
    return _Side(sps, [jax.ShapeDtypeStruct(s.shape, s.dtype) for s in sps], 3 * len(sps),
                 lambda i, o, ss, rs: _start_all(_scatter_copies(i, o, ss, rs)),
                 lambda i, o, ss, rs: _finish_all(_scatter_copies(i, o, ss, rs)))


def _scatter_chips(sps):
    def body(*refs):
        n = len(sps)
        pairs = _scatter_copies(refs[:n], refs[n:2 * n], *refs[2 * n:])
        _start_all(pairs)
        _finish_all(pairs)

    return _comm_call(body, sps, [jax.ShapeDtypeStruct(s.shape, s.dtype) for s in sps], 3 * len(sps), "rs_scatter_chips")


def _sum_chips(got, sp, me, cidx):
    _, h, w = got.shape
    tr = _tile(h, 256, 16)
    nh = h // tr

    def body(me_ref, c_ref, g_ref, s_ref, o_ref):
        acc = None
        for s in range(4):
            term = jnp.where(me_ref[0] == s, s_ref[...], g_ref[s]).astype(F32)
            acc = term if acc is None else acc + term
        o_ref[...] = acc

    return pl.pallas_call(
        body,
        grid_spec=pltpu.PrefetchScalarGridSpec(
            num_scalar_prefetch=2, grid=(nh,),
            in_specs=[pl.BlockSpec((4, tr, w), lambda i, me, c: (0, i, 0)),
                      pl.BlockSpec((None, tr, w), lambda i, me, c: (me[0], i, 0))],
            out_specs=pl.BlockSpec((tr, w), lambda i, me, c: (c[0] * nh + i, 0))),
        out_shape=jax.ShapeDtypeStruct((2 * h, w), F32), compiler_params=_cp("parallel"), name="rs_sum_chips",
    )(me, cidx, got, sp)


def _share_halves(fs):
    n = len(fs)

    def body(*refs):
        f_refs, (send_sems, recv_sems) = refs[n:2 * n], refs[2 * n:]
        x, y, c, _ = _place()
        cps = []
        for k, f in enumerate(f_refs):
            h = f.shape[0] // 2
            mine, theirs = f.at[pl.ds(_aligned(c * h, 16), h), :], f.at[pl.ds(_aligned((1 - c) * h, 16), h), :]
            cps.append((_remote(mine, mine, send_sems, recv_sems, k, (x, y, 1 - c)),
                        _remote(theirs, theirs, send_sems, recv_sems, k, (x, y, 1 - c))))
        _start_all(cps)
        _finish_all(cps)

    return pl.pallas_call(
        body, out_shape=tuple(jax.ShapeDtypeStruct(f.shape, f.dtype) for f in fs), in_specs=[ANY] * n,
        out_specs=tuple([ANY] * n), input_output_aliases={k: k for k in range(n)},
        scratch_shapes=[pltpu.SemaphoreType.DMA((n,)), pltpu.SemaphoreType.DMA((n,))],
        compiler_params=pltpu.CompilerParams(has_side_effects=True), name="rs_share_halves")(*fs)


def _gather_all(p):
    r, w = p.shape
    rel = [(dx, dy, dc) for dx in (0, 1) for dy in (0, 1) for dc in (0, 1)][1:]

    def body(p_ref, g_ref, send_sems, recv_sems):
        x, y, c = lax.axis_index("x"), lax.axis_index("y"), lax.axis_index("c")
        me = 4 * x + 2 * y + c
        g_ref[me] = p_ref[...]
        peers = [(jnp.bitwise_xor(x, dx), jnp.bitwise_xor(y, dy), jnp.bitwise_xor(c, dc)) for dx, dy, dc in rel]
        cps = [_remote(p_ref, g_ref.at[me], send_sems, recv_sems, k, peer) for k, peer in enumerate(peers)]
        for cp in cps:
            cp.start()
        for k, (px, py, pc) in enumerate(peers):
            slot = g_ref.at[4 * px + 2 * py + pc]
            _remote(slot, slot, send_sems, recv_sems, k, (px, py, pc)).wait_recv()
        for cp in cps:
            cp.wait_send()

    return pl.pallas_call(
        body, out_shape=jax.ShapeDtypeStruct((8, r, w), p.dtype),
        in_specs=[pl.BlockSpec(memory_space=pltpu.VMEM)], out_specs=pl.BlockSpec(memory_space=pltpu.VMEM),
        scratch_shapes=[pltpu.SemaphoreType.DMA((7,)), pltpu.SemaphoreType.DMA((7,))],
        compiler_params=pltpu.CompilerParams(has_side_effects=True, vmem_limit_bytes=VMEM_LIMIT),
        name=("gather_all_%dx%d" % (r, w)))(p)


def _pack_small(arrs):
    flat = jnp.concatenate([a.reshape(-1) for a in arrs])
    rows = -(-flat.shape[0] // (SMALL_W * 8)) * 8
    return jnp.pad(flat, (0, rows * SMALL_W - flat.shape[0])).reshape(rows, SMALL_W)


def _unpack(buf, shapes):
    flat = buf.reshape(-1)
    out, off = [], 0
    for sh in shapes:
        n = math.prod(sh)
        out.append(flat[off:off + n].reshape(sh))
        off += n
    return out


def _chip_padded(pieces, width, padded):
    out, used = [], 0
    for p in pieces:
        at = 0
        while at < p.shape[1]:
            take = min(p.shape[1] - at, width - used)
            out.append(p[:, at:at + take])
            at, used = at + take, used + take
            if used == width:
                out.append(jnp.zeros((p.shape[0], padded - width), p.dtype))
                used = 0
    assert used == 0
    return jnp.concatenate(out, axis=1)


class _Cfg:
    def __init__(self, x, w_in, attn_sinks, dn_a_log, w_ffn_up):
        self.S, self.D = x.shape[1], x.shape[2]
        self.L = w_in.shape[0]
        self.HA, self.HD = attn_sinks.shape[1], dn_a_log.shape[1]
        self.IN = 4 * w_in.shape[2]
        kvw = self.IN - self.HA * HEAD - 4 * self.HD * HEAD - 2 * self.HD - 2 * self.D
        self.KV = kvw // (2 * HEAD)
        self.G = self.HA // self.KV
        self.FF = 2 * w_ffn_up.shape[2]
        self.KA = self.HA * HEAD
        self.VA = self.KA + self.KV * HEAD
        self.QD = self.VA + self.KV * HEAD
        self.Z = self.QD + 3 * self.HD * HEAD
        self.GA = self.Z + self.HD * HEAD
        self.GD = self.GA + self.D
        self.NB = self.GD + self.D
        assert self.NB + 2 * self.HD == self.IN and self.G <= 8 and 2 * self.HD <= LANES


def kernel(x, positions, norm_mix_pre, w_in, attn_sinks, dn_conv_w, dn_a_log, dn_dt_bias, dn_norm_w, w_branch_attn, w_branch_dn, w_out, norm_mix_post, norm_ffn_pre, w_ffn_up, ffn_conv_w, ffn_conv_b, w_ffn_down, norm_ffn_post, loss_target, m_norm_mix_pre, m_w_in, m_attn_sinks, m_dn_conv_w, m_dn_a_log, m_dn_dt_bias, m_dn_norm_w, m_w_branch_attn, m_w_branch_dn, m_w_out, m_norm_mix_post, m_norm_ffn_pre, m_w_ffn_up, m_ffn_conv_w, m_ffn_conv_b, m_w_ffn_down, m_norm_ffn_post, v_norm_mix_pre, v_w_in, v_attn_sinks, v_dn_conv_w, v_dn_a_log, v_dn_dt_bias, v_dn_norm_w, v_w_branch_attn, v_w_branch_dn, v_w_out, v_norm_mix_post, v_norm_ffn_pre, v_w_ffn_up, v_ffn_conv_w, v_ffn_conv_b, v_w_ffn_down, v_norm_ffn_post):
    cfg = _Cfg(x, w_in, attn_sinks, dn_a_log, w_ffn_up)
    S, D, L, HD = cfg.S, cfg.D, cfg.L, cfg.HD
    o1 = cfg.Z
    ins = w_in.shape[2]
    insp = -(-ins // SHARD_PAD) * SHARD_PAD
    mats = [_Mat("col", D, insp), _Mat("row", *w_branch_attn.shape[1:]), _Mat("row", *w_branch_dn.shape[1:]),
            _Mat("row", *w_out.shape[1:]), _Mat("col", *w_ffn_up.shape[1:]), _Mat("row", *w_ffn_down.shape[1:])]
    me = _chip_index()

    taps = _gather_all(_pack_small([dn_conv_w, ffn_conv_w]))
    tap_shapes = [dn_conv_w.shape, ffn_conv_w.shape]
    per_chip = [_unpack(taps[2 * j], tap_shapes) for j in range(4)]
    dnw_all = jnp.concatenate([per_chip[j][0] for j in range(4)], axis=2)
    fcw_all = jnp.concatenate([per_chip[j][1] for j in range(4)], axis=2)

    def own_blocks(l):
        return [jnp.pad(w_in[l].astype(BF16), ((0, 0), (0, insp - ins))), w_branch_attn[l].astype(BF16),
                w_branch_dn[l].astype(BF16), w_out[l].astype(BF16), w_ffn_up[l].astype(BF16), w_ffn_down[l].astype(BF16)]

    def layer_weights(l, fulls):
        win_sm, wbra, wbrd, wout, wup, wdown = fulls
        win = jnp.concatenate([win_sm[:, j * insp:j * insp + ins] for j in range(4)], axis=1)
        return dict(
            win_sm=win_sm, wbig=jnp.concatenate([win[:, :o1], win[:, o1 + 2 * HD:]], axis=1),
            wba=jnp.pad(win[:, o1:o1 + 2 * HD], ((0, 0), (0, LANES - 2 * HD))),
            wbra=wbra, wbrd=wbrd, wout=wout, wup=wup, wdown=wdown, dnw=dnw_all[l], fcw=fcw_all[l])

    weights = [layer_weights(0, _gather_weights(own_blocks(0), mats))] + [None] * (L - 1)

    inv_freq = ROPE_THETA ** (-jnp.arange(0, 2 * ROT_HALF, 2, dtype=F32) / (2 * ROT_HALF))
    ang = positions.reshape(S, 1).astype(F32) * inv_freq
    cosf = jnp.concatenate([jnp.cos(ang), jnp.cos(ang), jnp.ones((S, HEAD - 2 * ROT_HALF), F32)], axis=1)
    sinf = jnp.concatenate([-jnp.sin(ang), jnp.sin(ang), jnp.zeros((S, HEAD - 2 * ROT_HALF), F32)], axis=1)

    h = x.reshape(S, D)
    saved = []
    for l in range(L):
        wt = weights[l]
        t = dict(h0=h)
        t["xn"] = _rms_fwd(h, norm_mix_pre[l], out_dtype=BF16, name="norm_mix_pre")
        t["proj"] = _mm(t["xn"], wt["wbig"], name="proj")
        t["pba"] = _mm(t["xn"], wt["wba"], name="proj_ba")
        t["oa"] = _attn_fwd(t["proj"], cosf, sinf, attn_sinks[l], cfg)
        t["qkv"] = _dnconv_fwd(t["proj"], wt["dnw"], cfg)
        nxt = own_blocks(l + 1) if l + 1 < L else None
        t["od"], t["st"], *landed = _gdn_fwd(t["qkv"], t["proj"], t["pba"], dn_a_log[l], dn_dt_bias[l], dn_norm_w[l], cfg,
                                             side=_gather_ici_side(nxt, mats) if nxt else None)
        t["pa"] = _mm(t["oa"], wt["wbra"], name="branch_attn")
        t["pd"] = _mm(t["od"], wt["wbrd"], name="branch_dn")
        t["y"] = _merge_fwd(t["proj"], t["pa"], t["pd"], cfg)
        t["mix"] = _mm(t["y"], wt["wout"], name="mix_out")
        t["h1"] = _rms_fwd(t["mix"], norm_mix_post[l], res=h, out_dtype=F32, name="norm_mix_post")
        t["xf"] = _rms_fwd(t["h1"], norm_ffn_pre[l], out_dtype=BF16, name="norm_ffn_pre")
        if nxt:
            t["up"], *landed = _mm(t["xf"], wt["wup"], name="ffn_up_pass", side=_gather_pass_side(landed, mats))
            weights[l + 1] = layer_weights(l + 1, _place_own(landed, nxt, mats))
        else:
            t["up"] = _mm(t["xf"], wt["wup"], name="ffn_up")
        t["act"] = _glu_fwd(t["up"], wt["fcw"], ffn_conv_b[l], cfg)
        t["f"] = _mm(t["act"], wt["wdown"], name="ffn_down")
        h = _rms_fwd(t["f"], norm_ffn_post[l], res=t["h1"], out_dtype=F32, name="norm_ffn_post")
        saved.append(t)

    loss_local, dh = _loss_head(h, loss_target.reshape(S, D))
    loss = lax.psum(loss_local, ("x", "y", "c"))

    cidx = lax.axis_index("c").astype(jnp.int32).reshape(1)
    midx = me.astype(jnp.int32).reshape(1)
    big_grads, small = [None] * L, [None] * L
    reduce_blocks = lambda got, pair: _share_halves([_sum_chips(a, b, midx, cidx) for a, b in zip(got, pair)])
    pair_sums = lambda full, theirs: [_pair_sum(a, b, m, cidx) for a, b, m in zip(full, theirs, mats)]
    unswapped = waiting = None
    for l in reversed(range(L)):
        wt, t = weights[l], saved[l]
        df, g_nfpost = _rms_bwd(t["f"], norm_ffn_post[l], dh, out_dtype=BF16, name="norm_ffn_post_bwd")
        dact = _mm(df, wt["wdown"], tb=True, name="ffn_down_dx")
        g_wdown = _mm(t["act"], df, ta=True, name="ffn_down_dw")
        dup, g_fcw, g_fcb = _glu_bwd(t["up"], dact, wt["fcw"], ffn_conv_b[l], cfg)
        if unswapped:
            dxf, *theirs = _mm(dup, wt["wup"], tb=True, name="ffn_up_dx_swap", side=_swap_side(unswapped[1], mats))
            waiting = (unswapped[0], pair_sums(unswapped[1], theirs))
        else:
            dxf = _mm(dup, wt["wup"], tb=True, name="ffn_up_dx")
        g_wup = _mm(t["xf"], dup, ta=True, name="ffn_up_dw")
        dh1, g_nfp = _rms_bwd(t["h1"], norm_ffn_pre[l], dxf, add=dh, name="norm_ffn_pre_bwd")
        dmix, g_nmpost = _rms_bwd(t["mix"], norm_mix_post[l], dh1, out_dtype=BF16, name="norm_mix_post_bwd")
        dy = _mm(dmix, wt["wout"], tb=True, name="mix_out_dx")
        g_wout = _mm(t["y"], dmix, ta=True, name="mix_out_dw")
        dpa, dpd, dga, dgd = _merge_bwd(t["proj"], t["pa"], t["pd"], dy, cfg)
        doa = _mm(dpa, wt["wbra"], tb=True, out_dtype=BF16, name="branch_attn_dx")
        g_wbra = _mm(t["oa"], dpa, ta=True, name="branch_attn_dw")
        dod = _mm(dpd, wt["wbrd"], tb=True, out_dtype=BF16, name="branch_dn_dx")
        g_wbrd = _mm(t["od"], dpd, ta=True, name="branch_dn_dw")
        dqkv_c, dz, dba, g_alog, g_dt, g_dnnorm, got = _gdn_bwd(
            t["qkv"], t["proj"], t["pba"], dn_a_log[l], dn_dt_bias[l], dn_norm_w[l], t["st"], dod, cfg,
            side=_scatter_side(waiting[1]) if waiting else None)
        if waiting:
            big_grads[waiting[0]] = reduce_blocks(got, waiting[1])
            waiting = None
        dqkv_d, g_dnw = _dnconv_bwd(t["proj"], dqkv_c, wt["dnw"], cfg)
        dqkv_a, g_sinks = _attn_bwd(t["proj"], doa, cosf, sinf, attn_sinks[l], cfg)
        dproj = _chip_padded([dqkv_a, dqkv_d, dba[:, :2 * HD], dz, dga, dgd], ins, insp)
        dxn = _mm(dproj, wt["win_sm"], tb=True, name="proj_dx")
        g_win = _mm(t["xn"], dproj, ta=True, name="proj_dw")
        dh, g_nmp = _rms_bwd(t["h0"], norm_mix_pre[l], dxn, add=dh1, name="norm_mix_pre_bwd")

        unswapped = (l, [g_win, g_wbra, g_wbrd, g_wout, g_wup, g_wdown])
        small[l] = [g_nmp, g_sinks, g_alog, g_dt, g_dnnorm, g_nmpost, g_nfp, g_fcb, g_nfpost, g_dnw, g_fcw]

    last_pair = pair_sums(unswapped[1], _swap_halves(unswapped[1], mats))
    big_grads[unswapped[0]] = reduce_blocks(_scatter_chips(last_pair), last_pair)
    grad_x = dh.reshape(1, S, D)

    small_shapes = [a.shape for a in small[0]]
    summed = _sum_leading(_gather_all(_pack_small([a for l in range(L) for a in small[l]])), name="sum_devices")
    per_layer = _unpack(summed, small_shapes * L)
    red = [jnp.stack([per_layer[l * len(small_shapes) + i] for l in range(L)]) for i in range(len(small_shapes))]
    g_dn_conv = lax.dynamic_slice_in_dim(red[9], me * dn_conv_w.shape[2], dn_conv_w.shape[2], axis=2)
    g_ffn_conv = lax.dynamic_slice_in_dim(red[10], me * ffn_conv_w.shape[2], ffn_conv_w.shape[2], axis=2)
    stacked = lambda i: jnp.stack([big_grads[l][i] for l in range(L)])

    grads = dict(
        norm_mix_pre=red[0], w_in=stacked(0), attn_sinks=red[1], dn_conv_w=g_dn_conv, dn_a_log=red[2],
        dn_dt_bias=red[3], dn_norm_w=red[4], w_branch_attn=stacked(1), w_branch_dn=stacked(2), w_out=stacked(3),
        norm_mix_post=red[5], norm_ffn_pre=red[6], w_ffn_up=stacked(4), ffn_conv_w=g_ffn_conv, ffn_conv_b=red[7],
        w_ffn_down=stacked(5), norm_ffn_post=red[8])

    params = dict(
        norm_mix_pre=(norm_mix_pre, m_norm_mix_pre, v_norm_mix_pre), w_in=(w_in, m_w_in, v_w_in),
        attn_sinks=(attn_sinks, m_attn_sinks, v_attn_sinks), dn_conv_w=(dn_conv_w, m_dn_conv_w, v_dn_conv_w),
        dn_a_log=(dn_a_log, m_dn_a_log, v_dn_a_log), dn_dt_bias=(dn_dt_bias, m_dn_dt_bias, v_dn_dt_bias),
        dn_norm_w=(dn_norm_w, m_dn_norm_w, v_dn_norm_w),
        w_branch_attn=(w_branch_attn, m_w_branch_attn, v_w_branch_attn),
        w_branch_dn=(w_branch_dn, m_w_branch_dn, v_w_branch_dn), w_out=(w_out, m_w_out, v_w_out),
        norm_mix_post=(norm_mix_post, m_norm_mix_post, v_norm_mix_post),
        norm_ffn_pre=(norm_ffn_pre, m_norm_ffn_pre, v_norm_ffn_pre), w_ffn_up=(w_ffn_up, m_w_ffn_up, v_w_ffn_up),
        ffn_conv_w=(ffn_conv_w, m_ffn_conv_w, v_ffn_conv_w), ffn_conv_b=(ffn_conv_b, m_ffn_conv_b, v_ffn_conv_b),
        w_ffn_down=(w_ffn_down, m_w_ffn_down, v_w_ffn_down),
        norm_ffn_post=(norm_ffn_post, m_norm_ffn_post, v_norm_ffn_post))
    names = list(params)

    grad_out, delta, new_m, new_v = {}, {}, {}, {}
    small_names = [n for n in names if params[n][0].ndim == 2]
    for n in names:
        if n in small_names:
            continue
        w, m, v = params[n]
        grad_out[n], delta[n], new_m[n], new_v[n] = _adamw(w, grads[n], m, v, name="adamw_" + n)

    outs = _adamw(*[_pack_small([params[n][i] if i is not None else grads[n] for n in small_names])[None]
                    for i in (0, None, 1, 2)], name="adamw_small")
    shapes = [params[n][0].shape for n in small_names]
    for dst, o in zip((grad_out, delta, new_m, new_v), outs):
        for n, a in zip(small_names, _unpack(o, shapes)):
            dst[n] = a

    return (loss, grad_x, *[grad_out[n] for n in names], *[delta[n] for n in names],
            *[new_m[n] for n in names], *[new_v[n] for n in names])
```

```python
import functools
import math

import jax
import jax.numpy as jnp
from jax import lax
from jax.experimental import pallas as pl
from jax.experimental.pallas import tpu as pltpu

F32, BF16 = jnp.float32, jnp.bfloat16
EPS = 1e-6
HEAD = 128
ATTN_BLOCK = 128
ROT_HALF = 16
ROPE_THETA = 500000.0
CHUNK = 64
DN_CONV, FFN_CONV = 4, 3
GDN_HEADS_PER_STEP = 4
MM_TILES = (512, 1024, 3968)
HALO = 8
LANES = 128
SUM_TR = 512
SHARD_PAD = 1024
SMALL_W = 1024
ADAM_LR, ADAM_B1, ADAM_B2, ADAM_EPS, ADAM_WD, ADAM_STEP = 0.001, 0.9, 0.999, 1e-08, 0.01, 10
VMEM_LIMIT = 56 * 1024 * 1024
MESH = pl.DeviceIdType.MESH
ANY = pl.BlockSpec(memory_space=pl.ANY)


def _cp(*sem):
    return pltpu.CompilerParams(dimension_semantics=sem, vmem_limit_bytes=VMEM_LIMIT)


def _tile(n, pref, unit=LANES):
    if n <= pref:
        return n
    t = (pref // unit) * unit
    while t >= unit:
        if n % t == 0:
            return t
        t -= unit
    raise ValueError((n, pref, unit))


def _sigmoid(x):
    return 1.0 / (1.0 + jnp.exp(-x))


def _softplus(x):
    return jnp.maximum(x, 0.0) + jnp.log(1.0 + jnp.exp(-jnp.abs(x)))


def _dsilu(y):
    s = _sigmoid(y)
    return s * (1.0 + y * (1.0 - s))


class _Side:
    def __init__(self, ins, out_shapes, n_sems, start, finish, in_place=False):
        self.ins, self.out_shapes, self.n_sems = list(ins), list(out_shapes), n_sems
        self.start, self.finish, self.in_place = start, finish, in_place


def _side_split(refs, n_in, n_out, side):
    si, so = (len(side.ins), len(side.out_shapes)) if side else (0, 0)
    ins, sins = refs[:n_in], refs[n_in:n_in + si]
    outs, souts = refs[n_in + si:n_in + si + n_out], refs[n_in + si + n_out:n_in + si + n_out + so]
    rest = refs[n_in + si + n_out + so:]
    scratch, sems = (rest[:-2], rest[-2:]) if side else (rest, ())
    return ins, outs, scratch, (sins, souts, sems)


def _side_steps(side, parts, grid):
    if side is None:
        return (lambda: None), (lambda: None)
    sins, souts, sems = parts
    ids = [pl.program_id(a) for a in range(len(grid))]
    first = functools.reduce(jnp.logical_and, [i == 0 for i in ids])
    last = functools.reduce(jnp.logical_and, [i == n - 1 for i, n in zip(ids, grid)])

    def begin():
        @pl.when(first)
        def _():
            side.start(sins, souts, *sems)

    def end():
        @pl.when(last)
        def _():
            side.finish(sins, souts, *sems)

    return begin, end


def _side_call_args(side, n_in, n_out):
    if side is None:
        return [], [], [], [], [], {}
    sems = [pltpu.SemaphoreType.DMA((side.n_sems,)), pltpu.SemaphoreType.DMA((side.n_sems,))]
    aliases = {n_in + k: n_out + k for k in range(len(side.ins))} if side.in_place else {}
    return [ANY] * len(side.ins), [ANY] * len(side.out_shapes), side.out_shapes, sems, side.ins, aliases


def _mm(a, b, *, ta=False, tb=False, add=None, out_dtype=F32, name, tiles=MM_TILES, side=None):
    if ta:
        kdim, m = a.shape
    else:
        m, kdim = a.shape
    if tb:
        n, k2 = b.shape
    else:
        k2, n = b.shape
    assert kdim == k2, (a.shape, b.shape, ta, tb)
    tm, tn, tk = _tile(m, tiles[0]), _tile(n, tiles[1]), _tile(kdim, tiles[2])
    nk = kdim // tk
    grid = (m // tm, n // tn, nk)
    dims = (((0 if ta else 1,), (1 if tb else 0,)), ((), ()))
    n_in = 2 if add is None else 3

    def body(*refs):
        ins, (o_ref,), (acc_ref,), parts = _side_split(refs, n_in, 1, side)
        begin, end = _side_steps(side, parts, grid)
        begin()
        a_ref, b_ref = ins[:2]
        k = pl.program_id(2)

        @pl.when(k == 0)
        def _():
            acc_ref[...] = jnp.zeros_like(acc_ref)

        acc_ref[...] += lax.dot_general(a_ref[...].astype(BF16), b_ref[...].astype(BF16), dims,
                                        preferred_element_type=F32)

        @pl.when(k == nk - 1)
        def _():
            r = acc_ref[...]
            if add is not None:
                r = r + ins[2][...]
            o_ref[...] = r.astype(o_ref.dtype)

        end()

    a_spec = pl.BlockSpec((tk, tm), lambda i, j, k: (k, i)) if ta else pl.BlockSpec((tm, tk), lambda i, j, k: (i, k))
    b_spec = pl.BlockSpec((tn, tk), lambda i, j, k: (j, k)) if tb else pl.BlockSpec((tk, tn), lambda i, j, k: (k, j))
    o_spec = pl.BlockSpec((tm, tn), lambda i, j, k: (i, j))
    ins, specs = [a, b], [a_spec, b_spec]
    if add is not None:
        ins.append(add)
        specs.append(o_spec)
    s_in, s_out, s_shapes, s_sems, s_ops, aliases = _side_call_args(side, n_in, 1)
    sem = ("arbitrary",) * 3 if side else ("parallel", "parallel", "arbitrary")
    outs = pl.pallas_call(
        body, grid=grid, in_specs=specs + s_in, out_specs=(o_spec, *s_out),
        out_shape=(jax.ShapeDtypeStruct((m, n), out_dtype), *s_shapes),
        scratch_shapes=[pltpu.VMEM((tm, tn), F32)] + s_sems, input_output_aliases=aliases,
        compiler_params=_cp(*sem), name=name,
    )(*ins, *s_ops)
    return outs if side else outs[0]


def _rms_fwd(x, w, res=None, *, out_dtype, name):
    s, d = x.shape
    tr = _tile(s, 256, 8)

    def body(*refs):
        if res is None:
            x_ref, w_ref, o_ref = refs
        else:
            x_ref, w_ref, r_ref, o_ref = refs
        xv = x_ref[...]
        y = xv * lax.rsqrt(jnp.mean(xv * xv, axis=-1, keepdims=True) + EPS) * w_ref[...]
        if res is not None:
            y = y + r_ref[...]
        o_ref[...] = y.astype(o_ref.dtype)

    row = pl.BlockSpec((tr, d), lambda i: (i, 0))
    par = pl.BlockSpec((1, d), lambda i: (0, 0))
    ins, specs = [x, w.reshape(1, d)], [row, par]
    if res is not None:
        ins.append(res)
        specs.append(row)
    return pl.pallas_call(body, grid=(s // tr,), in_specs=specs, out_specs=row,
                          out_shape=jax.ShapeDtypeStruct((s, d), out_dtype),
                          compiler_params=_cp("parallel"), name=name)(*ins)


def _rms_bwd(x, w, dy, add=None, *, out_dtype=F32, name):
    s, d = x.shape
    tr = _tile(s, 256, 8)

    def body(*refs):
        if add is None:
            x_ref, w_ref, dy_ref, dx_ref, dw_ref = refs
        else:
            x_ref, w_ref, dy_ref, add_ref, dx_ref, dw_ref = refs
        xv = x_ref[...]
        dyv = dy_ref[...].astype(F32)
        r = lax.rsqrt(jnp.mean(xv * xv, axis=-1, keepdims=True) + EPS)
        xh = xv * r
        dyw = dyv * w_ref[...]
        dx = r * (dyw - xh * jnp.mean(dyw * xh, axis=-1, keepdims=True))
        if add is not None:
            dx = dx + add_ref[...]
        dx_ref[...] = dx.astype(dx_ref.dtype)

        @pl.when(pl.program_id(0) == 0)
        def _():
            dw_ref[...] = jnp.zeros_like(dw_ref)

        dw_ref[...] += jnp.sum(dyv * xh, axis=0, keepdims=True)

    row = pl.BlockSpec((tr, d), lambda i: (i, 0))
    par = pl.BlockSpec((1, d), lambda i: (0, 0))
    ins, specs = [x, w.reshape(1, d), dy], [row, par, row]
    if add is not None:
        ins.append(add)
        specs.append(row)
    dx, dw = pl.pallas_call(
        body, grid=(s // tr,), in_specs=specs, out_specs=(row, par),
        out_shape=(jax.ShapeDtypeStruct((s, d), out_dtype), jax.ShapeDtypeStruct((1, d), F32)),
        compiler_params=_cp("arbitrary"), name=name)(*ins)
    return dx, dw.reshape(d)


def _loss_head(h, tgt):
    s, d = h.shape
    tr = _tile(s, 256, 8)

    def body(h_ref, t_ref, dh_ref, l_ref):
        e = h_ref[...] - t_ref[...]
        dh_ref[...] = e * (1.0 / d)

        @pl.when(pl.program_id(0) == 0)
        def _():
            l_ref[...] = jnp.zeros_like(l_ref)

        l_ref[...] += 0.5 * jnp.sum(jnp.mean(e * e, axis=-1, keepdims=True), axis=0, keepdims=True)

    row = pl.BlockSpec((tr, d), lambda i: (i, 0))
    dh, l = pl.pallas_call(
        body, grid=(s // tr,), in_specs=[row, row], out_specs=(row, pl.BlockSpec((1, 1), lambda i: (0, 0))),
        out_shape=(jax.ShapeDtypeStruct((s, d), F32), jax.ShapeDtypeStruct((1, 1), F32)),
        compiler_params=_cp("arbitrary"), name="loss_head")(h, tgt)
    return l[0, 0], dh


def _conv_rows(xe, w, width):
    y = None
    for j in range(width):
        s = width - 1 - j
        term = (pltpu.roll(xe, s, 0) if s else xe)[HALO:] * w[j:j + 1, :]
        y = term if y is None else y + term
    return y


def _conv_rows_t(dy_ext, w, width, tr):
    n = dy_ext.shape[0]
    dx = None
    for j in range(width):
        s = width - 1 - j
        term = (pltpu.roll(dy_ext, n - s, 0) if s else dy_ext)[:tr] * w[j:j + 1, :]
        dx = term if dx is None else dx + term
    return dx


def _conv_dw(dy, xe, width):
    rows = []
    for j in range(width):
        s = width - 1 - j
        rows.append(jnp.sum(dy * (pltpu.roll(xe, s, 0) if s else xe)[HALO:], axis=0, keepdims=True))
    return jnp.concatenate(rows, axis=0)


def _halo_specs(tr, tc, coff, nrow_blocks):
    per = tr // HALO
    last = nrow_blocks * per - 1
    cur = pl.BlockSpec((tr, tc), lambda j, r: (r, coff + j))
    prev = pl.BlockSpec((HALO, tc), lambda j, r: (jnp.maximum(r * per - 1, 0), coff + j))
    nxt = pl.BlockSpec((HALO, tc), lambda j, r: (jnp.minimum((r + 1) * per, last), coff + j))
    return cur, prev, nxt


def _dnconv_fwd(proj, w, cfg):
    s, width = cfg.S, 3 * cfg.HD * HEAD
    tc = _tile(math.gcd(cfg.QD, width), 512)
    tr = _tile(s, 512, 8)
    nr = s // tr
    cur, prev, _ = _halo_specs(tr, tc, cfg.QD // tc, nr)

    def body(x_ref, xp_ref, w_ref, o_ref):
        hist = jnp.where(pl.program_id(1) > 0, xp_ref[...], 0.0)
        y = _conv_rows(jnp.concatenate([hist, x_ref[...]], axis=0), w_ref[...], DN_CONV)
        o_ref[...] = y * _sigmoid(y)

    return pl.pallas_call(
        body, grid=(width // tc, nr),
        in_specs=[cur, prev, pl.BlockSpec((DN_CONV, tc), lambda j, r: (0, j))],
        out_specs=pl.BlockSpec((tr, tc), lambda j, r: (r, j)),
        out_shape=jax.ShapeDtypeStruct((s, width), F32),
        compiler_params=_cp("parallel", "parallel"), name="dnconv_fwd")(proj, proj, w)


def _dnconv_bwd(proj, du3, w, cfg):
    s, hw = cfg.S, cfg.HD * HEAD
    width = 3 * hw
    tc = _tile(math.gcd(cfg.QD, hw), 512)
    tr = _tile(s, 512, 8)
    nr = s // tr
    per, last = tr // HALO, s // HALO - 1
    cur, prev, nxt = _halo_specs(tr, tc, cfg.QD // tc, nr)
    pc = hw // tc
    du_cur = pl.BlockSpec((1, tr, tc), lambda j, r: (j // pc, r, j % pc))
    du_nxt = pl.BlockSpec((1, HALO, tc), lambda j, r: (j // pc, jnp.minimum((r + 1) * per, last), j % pc))

    def body(x_ref, xp_ref, xn_ref, du_ref, dun_ref, w_ref, dx_ref, dw_ref):
        r = pl.program_id(1)
        wv = w_ref[...]
        hist = jnp.where(r > 0, xp_ref[...], 0.0)
        xe = jnp.concatenate([hist, x_ref[...]], axis=0)
        y_ext = _conv_rows(jnp.concatenate([xe, xn_ref[...]], axis=0), wv, DN_CONV)
        du_ext = jnp.concatenate([du_ref[0], jnp.where(r < nr - 1, dun_ref[0], 0.0)], axis=0)
        dy_ext = du_ext * _dsilu(y_ext)
        dx_ref[...] = _conv_rows_t(dy_ext, wv, DN_CONV, tr).astype(dx_ref.dtype)

        @pl.when(r == 0)
        def _():
            dw_ref[...] = jnp.zeros_like(dw_ref)

        dw_ref[...] += _conv_dw(dy_ext[:tr], xe, DN_CONV)

    wspec = pl.BlockSpec((DN_CONV, tc), lambda j, r: (0, j))
    return pl.pallas_call(
        body, grid=(width // tc, nr),
        in_specs=[cur, prev, nxt, du_cur, du_nxt, wspec],
        out_specs=(pl.BlockSpec((tr, tc), lambda j, r: (r, j)), wspec),
        out_shape=(jax.ShapeDtypeStruct((s, width), BF16), jax.ShapeDtypeStruct((DN_CONV, width), F32)),
        compiler_params=_cp("parallel", "arbitrary"), name="dnconv_bwd")(proj, proj, proj, du3, du3, w)


def _glu_fwd(up, w, b, cfg):
    s, ff = cfg.S, cfg.FF
    tc = _tile(ff, 512)
    tr = _tile(s, 512, 8)
    nr, nc = s // tr, ff // tc
    g_cur, g_prev, _ = _halo_specs(tr, tc, 0, nr)
    v_cur, v_prev, _ = _halo_specs(tr, tc, nc, nr)

    def wspec(rows, off):
        return pl.BlockSpec((rows, tc), lambda j, r: (0, off + j))

    def body(g_ref, gp_ref, v_ref, vp_ref, wg_ref, wv_ref, bg_ref, bv_ref, o_ref):
        first = pl.program_id(1) > 0
        ug = _conv_rows(jnp.concatenate([jnp.where(first, gp_ref[...], 0.0), g_ref[...]], axis=0), wg_ref[...], FFN_CONV) + bg_ref[...]
        uv = _conv_rows(jnp.concatenate([jnp.where(first, vp_ref[...], 0.0), v_ref[...]], axis=0), wv_ref[...], FFN_CONV) + bv_ref[...]
        o_ref[...] = (ug * _sigmoid(ug) * uv).astype(o_ref.dtype)

    b2 = b.reshape(1, 2 * ff)
    return pl.pallas_call(
        body, grid=(nc, nr),
        in_specs=[g_cur, g_prev, v_cur, v_prev, wspec(FFN_CONV, 0), wspec(FFN_CONV, nc), wspec(1, 0), wspec(1, nc)],
        out_specs=pl.BlockSpec((tr, tc), lambda j, r: (r, j)),
        out_shape=jax.ShapeDtypeStruct((s, ff), BF16),
        compiler_params=_cp("parallel", "parallel"), name="glu_fwd")(up, up, up, up, w, w, b2, b2)


def _glu_bwd(up, dact, w, b, cfg):
    s, ff = cfg.S, cfg.FF
    tc = _tile(ff, 512)
    tr = _tile(s, 512, 8)
    nr, nc = s // tr, ff // tc
    g_cur, g_prev, g_nxt = _halo_specs(tr, tc, 0, nr)
    v_cur, v_prev, v_nxt = _halo_specs(tr, tc, nc, nr)
    d_cur, _, d_nxt = _halo_specs(tr, tc, 0, nr)

    def wspec(rows, off):
        return pl.BlockSpec((rows, tc), lambda j, r: (0, off + j))

    def body(g_ref, gp_ref, gn_ref, v_ref, vp_ref, vn_ref, d_ref, dn_ref, wg_ref, wv_ref, bg_ref, bv_ref,
             dup_ref, dw_ref, db_ref):
        r = pl.program_id(1)
        wg, wv = wg_ref[...], wv_ref[...]
        ge = jnp.concatenate([jnp.where(r > 0, gp_ref[...], 0.0), g_ref[...]], axis=0)
        ve = jnp.concatenate([jnp.where(r > 0, vp_ref[...], 0.0), v_ref[...]], axis=0)
        ug = _conv_rows(jnp.concatenate([ge, gn_ref[...]], axis=0), wg, FFN_CONV) + bg_ref[...]
        uv = _conv_rows(jnp.concatenate([ve, vn_ref[...]], axis=0), wv, FFN_CONV) + bv_ref[...]
        da = jnp.concatenate([d_ref[...].astype(F32), jnp.where(r < nr - 1, dn_ref[...].astype(F32), 0.0)], axis=0)
        dug = da * uv * _dsilu(ug)
        duv = da * ug * _sigmoid(ug)
        dup_ref[0] = _conv_rows_t(dug, wg, FFN_CONV, tr).astype(dup_ref.dtype)
        dup_ref[1] = _conv_rows_t(duv, wv, FFN_CONV, tr).astype(dup_ref.dtype)

        @pl.when(r == 0)
        def _():
            dw_ref[...] = jnp.zeros_like(dw_ref)
            db_ref[...] = jnp.zeros_like(db_ref)

        dw_ref[0] += _conv_dw(dug[:tr], ge, FFN_CONV)
        dw_ref[1] += _conv_dw(duv[:tr], ve, FFN_CONV)
        db_ref[0] += jnp.sum(dug[:tr], axis=0, keepdims=True)
        db_ref[1] += jnp.sum(duv[:tr], axis=0, keepdims=True)

    b2 = b.reshape(1, 2 * ff)
    dup, dw, db = pl.pallas_call(
        body, grid=(nc, nr),
        in_specs=[g_cur, g_prev, g_nxt, v_cur, v_prev, v_nxt, d_cur, d_nxt,
                  wspec(FFN_CONV, 0), wspec(FFN_CONV, nc), wspec(1, 0), wspec(1, nc)],
        out_specs=(pl.BlockSpec((2, tr, tc), lambda j, r: (0, r, j)),
                   pl.BlockSpec((2, FFN_CONV, tc), lambda j, r: (0, 0, j)),
                   pl.BlockSpec((2, 1, tc), lambda j, r: (0, 0, j))),
        out_shape=(jax.ShapeDtypeStruct((2, s, ff), BF16), jax.ShapeDtypeStruct((2, FFN_CONV, ff), F32),
                   jax.ShapeDtypeStruct((2, 1, ff), F32)),
        compiler_params=_cp("parallel", "arbitrary"), name="glu_bwd",
    )(up, up, up, up, up, up, dact, dact, w, w, b2, b2)
    dup = jnp.concatenate([dup[0], dup[1]], axis=1)
    dw = jnp.transpose(dw, (1, 0, 2)).reshape(FFN_CONV, 2 * ff)
    return dup, dw, db.reshape(2 * ff)


def _merge_specs(cfg):
    tc = _tile(math.gcd(math.gcd(cfg.GA, cfg.GD), cfg.D), 512)
    tr = _tile(cfg.S, 512, 8)
    ga = pl.BlockSpec((tr, tc), lambda r, j: (r, cfg.GA // tc + j))
    gd = pl.BlockSpec((tr, tc), lambda r, j: (r, cfg.GD // tc + j))
    pd = pl.BlockSpec((tr, tc), lambda r, j: (r, j))
    return (cfg.S // tr, cfg.D // tc), ga, gd, pd


def _merge_fwd(proj, pa, pdn, cfg):
    grid, ga, gd, pd = _merge_specs(cfg)

    def body(ga_ref, gd_ref, pa_ref, pd_ref, y_ref):
        y_ref[...] = (_sigmoid(ga_ref[...]) * pa_ref[...] + _sigmoid(gd_ref[...]) * pd_ref[...]).astype(y_ref.dtype)

    return pl.pallas_call(body, grid=grid, in_specs=[ga, gd, pd, pd], out_specs=pd,
                          out_shape=jax.ShapeDtypeStruct((cfg.S, cfg.D), BF16),
                          compiler_params=_cp("parallel", "parallel"), name="merge_fwd")(proj, proj, pa, pdn)


def _merge_bwd(proj, pa, pdn, dy, cfg):
    grid, ga, gd, pd = _merge_specs(cfg)

    def body(ga_ref, gd_ref, pa_ref, pd_ref, dy_ref, dpa_ref, dpd_ref, dga_ref, dgd_ref):
        dyv = dy_ref[...]
        sa, sd = _sigmoid(ga_ref[...]), _sigmoid(gd_ref[...])
        dpa_ref[...] = (dyv * sa).astype(BF16)
        dpd_ref[...] = (dyv * sd).astype(BF16)
        dga_ref[...] = (dyv * pa_ref[...] * sa * (1.0 - sa)).astype(BF16)
        dgd_ref[...] = (dyv * pd_ref[...] * sd * (1.0 - sd)).astype(BF16)

    out = jax.ShapeDtypeStruct((cfg.S, cfg.D), BF16)
    return pl.pallas_call(body, grid=grid, in_specs=[ga, gd, pd, pd, pd], out_specs=(pd,) * 4,
                          out_shape=(out,) * 4, compiler_params=_cp("parallel", "parallel"),
                          name="merge_bwd")(proj, proj, pa, pdn, dy)


def _rope(x, cos, sin):
    lane = lax.broadcasted_iota(jnp.int32, x.shape, 1)
    swapped = jnp.where(lane < ROT_HALF, pltpu.roll(x, LANES - ROT_HALF, 1), pltpu.roll(x, ROT_HALF, 1))
    return x * cos + swapped * sin


def _rope_t(dx, cos, sin):
    t = dx * sin
    lane = lax.broadcasted_iota(jnp.int32, dx.shape, 1)
    swapped = jnp.where(lane < ROT_HALF, pltpu.roll(t, LANES - ROT_HALF, 1), pltpu.roll(t, ROT_HALF, 1))
    return dx * cos + swapped


def _band_mask(i):
    r = lax.broadcasted_iota(jnp.int32, (ATTN_BLOCK, 2 * ATTN_BLOCK), 0)
    c = lax.broadcasted_iota(jnp.int32, (ATTN_BLOCK, 2 * ATTN_BLOCK), 1)
    dist = r + ATTN_BLOCK - c
    return (dist >= 0) & (dist < ATTN_BLOCK) & ((c >= ATTN_BLOCK) | (i > 0))


def _dot(a, b, ca, cb):
    return lax.dot_general(a.astype(BF16), b.astype(BF16), (((ca,), (cb,)), ((), ())), preferred_element_type=F32)


def _attn_probs(qg, k_cat, sink, valid):
    logits = jnp.where(valid, _dot(qg, k_cat, 1, 1) * (HEAD ** -0.5), -1e30)
    m = jnp.maximum(jnp.max(logits, axis=-1, keepdims=True), sink)
    p = jnp.exp(logits - m)
    es = jnp.exp(sink - m)
    inv = 1.0 / (jnp.sum(p, axis=-1, keepdims=True) + es)
    return p * inv, es * inv


def _attn_specs(cfg, nb, clamp):
    gw = cfg.G * HEAD
    qi = (lambda kv, i: (jnp.minimum(i, nb - 1), kv)) if clamp else (lambda kv, i: (i, kv))
    ci = (lambda i: jnp.minimum(i, nb - 1)) if clamp else (lambda i: i)
    pi = lambda i: jnp.maximum(ci(i) - 1, 0)
    ko, vo = cfg.KA // HEAD, cfg.VA // HEAD
    blk = (ATTN_BLOCK, HEAD)
    specs = [
        pl.BlockSpec((ATTN_BLOCK, gw), qi),
        pl.BlockSpec(blk, lambda kv, i: (ci(i), ko + kv)), pl.BlockSpec(blk, lambda kv, i: (pi(i), ko + kv)),
        pl.BlockSpec(blk, lambda kv, i: (ci(i), vo + kv)), pl.BlockSpec(blk, lambda kv, i: (pi(i), vo + kv)),
        pl.BlockSpec(blk, lambda kv, i: (ci(i), 0)), pl.BlockSpec(blk, lambda kv, i: (pi(i), 0)),
        pl.BlockSpec(blk, lambda kv, i: (ci(i), 0)), pl.BlockSpec(blk, lambda kv, i: (pi(i), 0)),
        pl.BlockSpec((1, 8, LANES), lambda kv, i: (kv, 0, 0)),
    ]
    return specs, pl.BlockSpec((ATTN_BLOCK, gw), qi)


def _sink_rows(sinks, cfg):
    sk = jnp.broadcast_to(sinks.reshape(cfg.KV, cfg.G, 1), (cfg.KV, cfg.G, LANES))
    return jnp.pad(sk, ((0, 0), (0, 8 - cfg.G), (0, 0)))


def _attn_fwd(proj, cosf, sinf, sinks, cfg):
    nb = cfg.S // ATTN_BLOCK
    specs, qspec = _attn_specs(cfg, nb, False)

    def body(q_ref, kc_ref, kp_ref, vc_ref, vp_ref, cc_ref, cp_ref, sc_ref, sp_ref, sk_ref, o_ref):
        i = pl.program_id(1)
        cc, sc = cc_ref[...], sc_ref[...]
        k_cat = jnp.concatenate([_rope(kp_ref[...], cp_ref[...], sp_ref[...]), _rope(kc_ref[...], cc, sc)], axis=0).astype(BF16)
        v_cat = jnp.concatenate([vp_ref[...], vc_ref[...]], axis=0).astype(BF16)
        valid = _band_mask(i)
        for g in range(cfg.G):
            qg = _rope(q_ref[:, g * HEAD:(g + 1) * HEAD], cc, sc)
            p, _ = _attn_probs(qg, k_cat, sk_ref[0, g:g + 1, 0:1], valid)
            o_ref[:, g * HEAD:(g + 1) * HEAD] = _dot(p, v_cat, 1, 0).astype(o_ref.dtype)

    return pl.pallas_call(
        body, grid=(cfg.KV, nb), in_specs=specs, out_specs=qspec,
        out_shape=jax.ShapeDtypeStruct((cfg.S, cfg.HA * HEAD), BF16),
        compiler_params=_cp("parallel", "parallel"), name="attn_fwd",
    )(proj, proj, proj, proj, proj, cosf, cosf, sinf, sinf, _sink_rows(sinks, cfg))


def _attn_bwd(proj, do, cosf, sinf, sinks, cfg):
    nb = cfg.S // ATTN_BLOCK
    specs, qspec = _attn_specs(cfg, nb, True)
    kv_out = pl.BlockSpec((ATTN_BLOCK, HEAD), lambda kv, i: (jnp.maximum(i - 1, 0), kv))

    def body(q_ref, kc_ref, kp_ref, vc_ref, vp_ref, cc_ref, cp_ref, sc_ref, sp_ref, sk_ref, do_ref,
             dq_ref, dk_ref, dv_ref, dsk_ref, ck_ref, cv_ref):
        i = pl.program_id(1)

        @pl.when(i == 0)
        def _():
            ck_ref[...] = jnp.zeros_like(ck_ref)
            cv_ref[...] = jnp.zeros_like(cv_ref)
            dsk_ref[...] = jnp.zeros_like(dsk_ref)

        @pl.when(i < nb)
        def _():
            cc, sc, cp, sp = cc_ref[...], sc_ref[...], cp_ref[...], sp_ref[...]
            k_cat = jnp.concatenate([_rope(kp_ref[...], cp, sp), _rope(kc_ref[...], cc, sc)], axis=0).astype(BF16)
            v_cat = jnp.concatenate([vp_ref[...], vc_ref[...]], axis=0).astype(BF16)
            valid = _band_mask(i)
            dk_cat = jnp.zeros((2 * ATTN_BLOCK, HEAD), F32)
            dv_cat = jnp.zeros((2 * ATTN_BLOCK, HEAD), F32)
            for g in range(cfg.G):
                sl = slice(g * HEAD, (g + 1) * HEAD)
                qg = _rope(q_ref[:, sl], cc, sc)
                dog = do_ref[:, sl]
                p, ps = _attn_probs(qg, k_cat, sk_ref[0, g:g + 1, 0:1], valid)
                dp = _dot(dog, v_cat, 1, 1)
                delta = jnp.sum(p * dp, axis=-1, keepdims=True)
                ds = p * (dp - delta) * (HEAD ** -0.5)
                dq_ref[:, sl] = _rope_t(_dot(ds, k_cat, 1, 0), cc, sc).astype(dq_ref.dtype)
                dk_cat = dk_cat + _dot(ds, qg, 0, 0)
                dv_cat = dv_cat + _dot(p, dog, 0, 0)
                dsk_ref[0, g:g + 1, :] += jnp.broadcast_to(-jnp.sum(ps * delta, axis=0, keepdims=True), (1, LANES))
            dk_ref[...] = (ck_ref[...] + _rope_t(dk_cat[:ATTN_BLOCK], cp, sp)).astype(dk_ref.dtype)
            dv_ref[...] = (cv_ref[...] + dv_cat[:ATTN_BLOCK]).astype(dv_ref.dtype)
            ck_ref[...] = _rope_t(dk_cat[ATTN_BLOCK:], cc, sc)
            cv_ref[...] = dv_cat[ATTN_BLOCK:]

        @pl.when(i == nb)
        def _():
            dk_ref[...] = ck_ref[...].astype(dk_ref.dtype)
            dv_ref[...] = cv_ref[...].astype(dv_ref.dtype)

    kvw = cfg.KV * HEAD
    dq, dk, dv, dsk = pl.pallas_call(
        body, grid=(cfg.KV, nb + 1), in_specs=specs + [qspec],
        out_specs=(qspec, kv_out, kv_out, pl.BlockSpec((1, 8, LANES), lambda kv, i: (kv, 0, 0))),
        out_shape=(jax.ShapeDtypeStruct((cfg.S, cfg.HA * HEAD), BF16), jax.ShapeDtypeStruct((cfg.S, kvw), BF16),
                   jax.ShapeDtypeStruct((cfg.S, kvw), BF16), jax.ShapeDtypeStruct((cfg.KV, 8, LANES), F32)),
        scratch_shapes=[pltpu.VMEM((ATTN_BLOCK, HEAD), F32), pltpu.VMEM((ATTN_BLOCK, HEAD), F32)],
        compiler_params=_cp("parallel", "arbitrary"), name="attn_bwd",
    )(proj, proj, proj, proj, proj, cosf, cosf, sinf, sinf, _sink_rows(sinks, cfg), do)
    return jnp.concatenate([dq, dk, dv], axis=1), dsk[:, :cfg.G, 0].reshape(cfg.HA)


@jax.custom_vjp
def _nn(a, b):
    return _dot(a, b, 1, 0)


@jax.custom_vjp
def _nt(a, b):
    return _dot(a, b, 1, 1)


@jax.custom_vjp
def _tn(a, b):
    return _dot(a, b, 0, 0)


_nn.defvjp(lambda a, b: (_nn(a, b), (a, b)), lambda r, g: (_nt(g, r[1]), _tn(r[0], g)))
_nt.defvjp(lambda a, b: (_nt(a, b), (a, b)), lambda r, g: (_nn(g, r[1]), _tn(g, r[0])))
_tn.defvjp(lambda a, b: (_tn(a, b), (a, b)), lambda r, g: (_nt(r[1], g), _nn(r[0], g)))


def _unit_lower_inverse(a):
    ii = lax.broadcasted_iota(jnp.int32, a.shape, 0)
    jj = lax.broadcasted_iota(jnp.int32, a.shape, 1)
    t = jnp.where(ii == jj, 1.0, 0.0) - jnp.where((ii == jj + 1) & (ii % 2 == 1), a, 0.0)
    b = 2
    while b < CHUNK:
        lower_left = (ii // (2 * b) == jj // (2 * b)) & ((ii // b) % 2 == 1) & ((jj // b) % 2 == 0)
        t = t - _dot(_dot(t, jnp.where(lower_left, a, 0.0), 1, 0), t, 1, 0)
        b *= 2
    return t


@jax.custom_vjp
def _tri_inv(a):
    return _unit_lower_inverse(a)


def _tri_inv_fwd(a):
    t = _unit_lower_inverse(a)
    return t, t


def _tri_inv_bwd(t, g):
    return (-_dot(t, _dot(g, t, 1, 1), 0, 0),)


_tri_inv.defvjp(_tri_inv_fwd, _tri_inv_bwd)


def _gdn_chunk(q, k, v, z, b_col, a_col, a_row, al_col, dt_col, al_row, dt_row, nw, s):
    rows = q.shape[0]
    hb = rows // CHUNK
    ii = lax.broadcasted_iota(jnp.int32, (rows, rows), 0)
    jj = lax.broadcasted_iota(jnp.int32, (rows, rows), 1)
    same = (ii // CHUNK) == (jj // CHUNK)
    tri, strict, upper = same & (ii >= jj), same & (ii > jj), same & (ii <= jj)
    beta = _sigmoid(b_col)
    g_col = -jnp.exp(al_col) * _softplus(a_col + dt_col)
    g_row = -jnp.exp(al_row) * _softplus(a_row + dt_row)
    gc_col = jnp.sum(jnp.where(tri, g_row, 0.0), axis=1, keepdims=True)
    gc_row = jnp.sum(jnp.where(upper, g_col, 0.0), axis=0, keepdims=True)
    last = jj == (ii // CHUNK) * CHUNK + (CHUNK - 1)
    gl_col = jnp.sum(jnp.where(last, gc_row, 0.0), axis=1, keepdims=True)
    decay = jnp.where(tri, jnp.exp(jnp.where(tri, gc_col - gc_row, 0.0)), 0.0)
    qn = q * lax.rsqrt(jnp.sum(q * q, axis=-1, keepdims=True) + EPS) * (HEAD ** -0.5)
    kn = k * lax.rsqrt(jnp.sum(k * k, axis=-1, keepdims=True) + EPS)
    t_inv = _tri_inv(jnp.where(strict, beta * _nt(kn, kn) * decay, 0.0))
    eg = jnp.exp(gc_col)
    u = _nn(t_inv, v * beta)
    w = _nn(t_inv, kn * (beta * eg))
    qk = jnp.where(tri, _nt(qn, kn) * decay, 0.0)
    ri = lax.broadcasted_iota(jnp.int32, (rows, hb * HEAD), 0)
    ci = lax.broadcasted_iota(jnp.int32, (rows, hb * HEAD), 1)
    own = (ri // CHUNK) == (ci // HEAD)

    def spread(x):
        return jnp.where(own, jnp.concatenate([x] * hb, axis=1), 0.0)

    v_new = u - _nn(spread(w), s)
    o = _nn(spread(qn * eg), s) + _nn(qk, v_new)
    si = lax.broadcasted_iota(jnp.int32, (hb * HEAD, rows), 0)
    sj = lax.broadcasted_iota(jnp.int32, (hb * HEAD, rows), 1)
    gl_s = jnp.sum(jnp.where(sj == (si // HEAD) * CHUNK + (CHUNK - 1), gc_row, 0.0), axis=1, keepdims=True)
    s1 = s * jnp.exp(gl_s) + _tn(spread(kn * jnp.exp(gl_col - gc_col)), v_new)
    od = o * lax.rsqrt(jnp.mean(o * o, axis=-1, keepdims=True) + EPS) * nw * (z * _sigmoid(z))
    return od, s1


def _gdn_heads_per_step(cfg):
    for hb in (GDN_HEADS_PER_STEP, 2, 1):
        if cfg.HD % hb == 0 and cfg.Z % (hb * HEAD) == 0:
            return hb


def _gdn_in_specs(cfg, nc, rev):
    hb = _gdn_heads_per_step(cfg)
    ng, rows = cfg.HD // hb, hb * CHUNK
    ci = (lambda n: nc - 1 - n) if rev else (lambda n: n)
    zo = cfg.Z // (hb * HEAD)
    blk = (CHUNK, hb * HEAD)
    col = pl.BlockSpec((1, 1, rows, 1), lambda g, n: (g, ci(n), 0, 0))
    row = pl.BlockSpec((1, 1, 1, rows), lambda g, n: (g, ci(n), 0, 0))
    hcol = pl.BlockSpec((1, rows, 1), lambda g, n: (g, 0, 0))
    hrow = pl.BlockSpec((1, 1, rows), lambda g, n: (g, 0, 0))
    return [
        pl.BlockSpec(blk, lambda g, n: (ci(n), g)), pl.BlockSpec(blk, lambda g, n: (ci(n), ng + g)),
        pl.BlockSpec(blk, lambda g, n: (ci(n), 2 * ng + g)), pl.BlockSpec(blk, lambda g, n: (ci(n), zo + g)),
        col, col, row, hcol, hcol, hrow, hrow, pl.BlockSpec((1, HEAD), lambda g, n: (0, 0)),
    ]


def _gdn_args(qkv, proj, pba, a_log, dt_bias, nw, cfg):
    hb, hd, s = _gdn_heads_per_step(cfg), cfg.HD, cfg.S
    ng, nc, rows = hd // hb, s // CHUNK, hb * CHUNK

    def grouped(x):
        return jnp.transpose(x.reshape(nc, CHUNK, ng, hb), (2, 0, 3, 1)).reshape(ng, nc, rows)

    b, a = grouped(pba[:, :hd]), grouped(pba[:, hd:2 * hd])
    per_row = lambda x: jnp.repeat(x.reshape(ng, hb), CHUNK, axis=1)
    al, dt = per_row(a_log), per_row(dt_bias)
    return (qkv, qkv, qkv, proj, b.reshape(ng, nc, rows, 1), a.reshape(ng, nc, rows, 1), a.reshape(ng, nc, 1, rows),
            al.reshape(ng, rows, 1), dt.reshape(ng, rows, 1), al.reshape(ng, 1, rows), dt.reshape(ng, 1, rows),
            nw.reshape(1, HEAD))


def _stack_heads(x, hb):
    return jnp.concatenate([x[:, i * HEAD:(i + 1) * HEAD] for i in range(hb)], axis=0)


def _gdn_load(refs, hb):
    q, k, v, z, b, a, ar, alc, dtc, alr, dtr, nw = refs
    return (_stack_heads(q[...], hb), _stack_heads(k[...], hb), _stack_heads(v[...], hb), _stack_heads(z[...], hb),
            b[0, 0], a[0, 0], ar[0, 0], alc[0], dtc[0], alr[0], dtr[0], nw[...])


def _gdn_fwd(qkv, proj, pba, a_log, dt_bias, nw, cfg, side=None):
    nc, hb = cfg.S // CHUNK, _gdn_heads_per_step(cfg)
    ng = cfg.HD // hb
    grid = (ng, nc)

    def body(*refs):
        ins, (o_ref, st_ref), (s_ref,), parts = _side_split(refs, 12, 2, side)
        begin, end = _side_steps(side, parts, grid)
        begin()

        @pl.when(pl.program_id(1) == 0)
        def _():
            s_ref[...] = jnp.zeros_like(s_ref)

        s0 = s_ref[...]
        st_ref[0, 0] = s0
        od, s1 = _gdn_chunk(*_gdn_load(ins, hb), s0)
        for i in range(hb):
            o_ref[:, i * HEAD:(i + 1) * HEAD] = od[i * CHUNK:(i + 1) * CHUNK].astype(o_ref.dtype)
        s_ref[...] = s1
        end()

    s_in, s_out, s_shapes, s_sems, s_ops, _ = _side_call_args(side, 12, 2)
    return pl.pallas_call(
        body, grid=grid, in_specs=_gdn_in_specs(cfg, nc, False) + s_in,
        out_specs=(pl.BlockSpec((CHUNK, hb * HEAD), lambda g, n: (n, g)),
                   pl.BlockSpec((1, 1, hb * HEAD, HEAD), lambda g, n: (g, n, 0, 0)), *s_out),
        out_shape=(jax.ShapeDtypeStruct((cfg.S, cfg.HD * HEAD), BF16),
                   jax.ShapeDtypeStruct((ng, nc, hb * HEAD, HEAD), F32), *s_shapes),
        scratch_shapes=[pltpu.VMEM((hb * HEAD, HEAD), F32)] + s_sems,
        compiler_params=_cp("arbitrary" if side else "parallel", "arbitrary"),
        name="gdn_fwd_gather" if side else "gdn_fwd",
    )(*_gdn_args(qkv, proj, pba, a_log, dt_bias, nw, cfg), *s_ops)


def _gdn_bwd(qkv, proj, pba, a_log, dt_bias, nw, states, dod, cfg, side=None):
    nc, hd, s, hb = cfg.S // CHUNK, cfg.HD, cfg.S, _gdn_heads_per_step(cfg)
    ng, rows = hd // hb, hb * CHUNK
    rv = lambda n: nc - 1 - n
    col = pl.BlockSpec((1, 1, rows, 1), lambda g, n: (g, rv(n), 0, 0))
    row = pl.BlockSpec((1, 1, 1, rows), lambda g, n: (g, rv(n), 0, 0))
    hcol = pl.BlockSpec((1, rows, 1), lambda g, n: (g, 0, 0))
    hrow = pl.BlockSpec((1, 1, rows), lambda g, n: (g, 0, 0))

    grid = (ng, nc)

    def body(*refs):
        ins, outs, (ds_ref,), parts = _side_split(refs, 14, 10, side)
        begin, end = _side_steps(side, parts, grid)
        begin()
        st_ref, dod_ref = ins[12:14]
        dqkv_ref, dz_ref, db_ref, da_ref, dar_ref, dalc_ref, ddtc_ref, dalr_ref, ddtr_ref, dnw_ref = outs
        g, n = pl.program_id(0), pl.program_id(1)

        @pl.when(n == 0)
        def _():
            ds_ref[...] = jnp.zeros_like(ds_ref)
            for r in (dalc_ref, ddtc_ref, dalr_ref, ddtr_ref):
                r[...] = jnp.zeros_like(r)

        @pl.when((n == 0) & (g == 0))
        def _():
            dnw_ref[...] = jnp.zeros_like(dnw_ref)

        _, vjp = jax.vjp(_gdn_chunk, *_gdn_load(ins[:12], hb), st_ref[0, 0])
        dq, dk, dv, dz, db, da, dar, dalc, ddtc, dalr, ddtr, dnw, ds0 = vjp(
            (_stack_heads(dod_ref[...].astype(F32), hb), ds_ref[...]))
        for i in range(hb):
            sl, rs = slice(i * HEAD, (i + 1) * HEAD), slice(i * CHUNK, (i + 1) * CHUNK)
            dqkv_ref[0, :, sl], dqkv_ref[1, :, sl], dqkv_ref[2, :, sl] = dq[rs], dk[rs], dv[rs]
            dz_ref[:, sl] = dz[rs].astype(dz_ref.dtype)
        db_ref[0, 0], da_ref[0, 0], dar_ref[0, 0] = db, da, dar
        dalc_ref[0] += dalc
        ddtc_ref[0] += ddtc
        dalr_ref[0] += dalr
        ddtr_ref[0] += ddtr
        dnw_ref[...] += dnw
        ds_ref[...] = ds0
        end()

    blk = pl.BlockSpec((CHUNK, hb * HEAD), lambda g, n: (rv(n), g))
    f32 = lambda *sh: jax.ShapeDtypeStruct(sh, F32)
    s_in, s_out, s_shapes, s_sems, s_ops, _ = _side_call_args(side, 14, 10)
    outs = pl.pallas_call(
        body, grid=grid,
        in_specs=_gdn_in_specs(cfg, nc, True) + [pl.BlockSpec((1, 1, hb * HEAD, HEAD), lambda g, n: (g, rv(n), 0, 0)), blk] + s_in,
        out_specs=(pl.BlockSpec((3, CHUNK, hb * HEAD), lambda g, n: (0, rv(n), g)), blk, col, col, row,
                   hcol, hcol, hrow, hrow, pl.BlockSpec((1, HEAD), lambda g, n: (0, 0)), *s_out),
        out_shape=(f32(3, s, hd * HEAD), jax.ShapeDtypeStruct((s, hd * HEAD), BF16),
                   f32(ng, nc, rows, 1), f32(ng, nc, rows, 1), f32(ng, nc, 1, rows),
                   f32(ng, rows, 1), f32(ng, rows, 1), f32(ng, 1, rows), f32(ng, 1, rows), f32(1, HEAD), *s_shapes),
        scratch_shapes=[pltpu.VMEM((hb * HEAD, HEAD), F32)] + s_sems,
        compiler_params=_cp("arbitrary", "arbitrary"), name="gdn_bwd_scatter" if side else "gdn_bwd",
    )(*_gdn_args(qkv, proj, pba, a_log, dt_bias, nw, cfg), states, dod, *s_ops)
    dqkv, dz, db, da, dar, dalc, ddtc, dalr, ddtr, dnw = outs[:10]
    side_outs = list(outs[10:])

    def ungrouped(x):
        return jnp.transpose(x.reshape(ng, nc, hb, CHUNK), (1, 3, 0, 2)).reshape(s, hd)

    per_head = lambda c, r: (c.reshape(ng, hb, CHUNK) + r.reshape(ng, hb, CHUNK)).sum(axis=-1).reshape(hd)
    d_b = ungrouped(db.reshape(ng, nc, rows))
    d_a = ungrouped(da.reshape(ng, nc, rows) + dar.reshape(ng, nc, rows))
    dba = jnp.concatenate([d_b, d_a, jnp.zeros((s, LANES - 2 * hd), F32)], axis=1).astype(BF16)
    return dqkv, dz, dba, per_head(dalc, dalr), per_head(ddtc, ddtr), dnw.reshape(HEAD), side_outs


def _adamw(w, g, m, v, *, name):
    nl, r, c = w.shape
    gc = g.shape[2]
    tr = _tile(r, max(8, (1 << 18) // c // 8 * 8), 8) if r % 8 == 0 else r
    c1, c2 = 1.0 - ADAM_B1 ** ADAM_STEP, 1.0 - ADAM_B2 ** ADAM_STEP

    def body(w_ref, g_ref, m_ref, v_ref, go_ref, d_ref, nm_ref, nv_ref):
        gv = g_ref[:, :c]
        nm = ADAM_B1 * m_ref[...] + (1.0 - ADAM_B1) * gv
        nv = ADAM_B2 * v_ref[...] + (1.0 - ADAM_B2) * (gv * gv)
        go_ref[...] = gv
        d_ref[...] = -ADAM_LR * ((nm / c1) / (jnp.sqrt(nv / c2) + ADAM_EPS) + ADAM_WD * w_ref[...])
        nm_ref[...] = nm
        nv_ref[...] = nv

    spec = pl.BlockSpec((None, tr, c), lambda l, i: (l, i, 0))
    gspec = pl.BlockSpec((None, tr, gc), lambda l, i: (l, i, 0))
    out = jax.ShapeDtypeStruct((nl, r, c), F32)
    return pl.pallas_call(body, grid=(nl, r // tr), in_specs=[spec, gspec, spec, spec], out_specs=(spec,) * 4,
                          out_shape=(out,) * 4, compiler_params=_cp("parallel", "parallel"), name=name)(w, g, m, v)


def _sum_leading(a, *, name):
    n, r, c = a.shape
    tr = _tile(r, SUM_TR, 8)

    def body(a_ref, o_ref):
        acc = a_ref[0].astype(F32)
        for i in range(1, n):
            acc = acc + a_ref[i].astype(F32)
        o_ref[...] = acc

    return pl.pallas_call(body, grid=(r // tr,), in_specs=[pl.BlockSpec((n, tr, c), lambda i: (0, i, 0))],
                          out_specs=pl.BlockSpec((tr, c), lambda i: (i, 0)),
                          out_shape=jax.ShapeDtypeStruct((r, c), F32), compiler_params=_cp("parallel"), name=name)(a)


class _Mat:
    def __init__(self, kind, rows, cols):
        self.kind, self.rows, self.cols = kind, rows, cols
        self.full = (rows, 4 * cols) if kind == "col" else (4 * rows, cols)
        self.half = rows // 2

    def block(self, ref, j, c=None):
        r0 = (0 if self.kind == "col" else j * self.rows) + (0 if c is None else c * self.half)
        rows = pl.ds(_aligned(r0, 16), self.rows if c is None else self.half)
        return ref.at[rows, pl.ds(_aligned(j * self.cols, LANES), self.cols)] if self.kind == "col" else ref.at[rows, :]

    def own_half(self, ref, c):
        return ref.at[pl.ds(_aligned(c * self.half, 16), self.half), :]


def _aligned(v, unit):
    return v if isinstance(v, int) else pl.multiple_of(v, unit)


def _place():
    x, y, c = lax.axis_index("x"), lax.axis_index("y"), lax.axis_index("c")
    chips = [(1 - x, y), (x, 1 - y), (1 - x, 1 - y)]
    return x, y, c, chips


def _chip_index():
    return 2 * lax.axis_index("x") + lax.axis_index("y")


def _remote(src, dst, send_sems, recv_sems, k, to):
    return pltpu.make_async_remote_copy(src_ref=src, dst_ref=dst, send_sem=send_sems.at[k], recv_sem=recv_sems.at[k],
                                        device_id=to, device_id_type=MESH)


def _comm_call(body, ins, out_shapes, n_sems, name):
    return pl.pallas_call(
        body, out_shape=tuple(out_shapes), in_specs=[ANY] * len(ins), out_specs=tuple([ANY] * len(out_shapes)),
        scratch_shapes=[pltpu.SemaphoreType.DMA((n_sems,)), pltpu.SemaphoreType.DMA((n_sems,))],
        compiler_params=pltpu.CompilerParams(has_side_effects=True), name=name)(*ins)


def _gather_copies(own_refs, full_refs, send_sems, recv_sems, mats, base=0):
    x, y, c, chips = _place()
    me = 2 * x + y
    return [(_remote(m.own_half(own_refs[k], c), m.block(full_refs[k], me, c), send_sems, recv_sems, base + 3 * k + j, (cx, cy, c)),
             _remote(m.block(full_refs[k], 2 * cx + cy, c), m.block(full_refs[k], 2 * cx + cy, c), send_sems, recv_sems,
                     base + 3 * k + j, (cx, cy, c)))
            for k, m in enumerate(mats) for j, (cx, cy) in enumerate(chips)]


def _pass_copies(full_refs, send_sems, recv_sems, mats, base=0):
    x, y, c, chips = _place()
    sib = (x, y, 1 - c)
    return [(_remote(m.block(full_refs[k], 2 * cx + cy, c), m.block(full_refs[k], 2 * cx + cy, c), send_sems, recv_sems,
                     base + 3 * k + j, sib),
             _remote(m.block(full_refs[k], 2 * cx + cy, 1 - c), m.block(full_refs[k], 2 * cx + cy, 1 - c), send_sems,
                     recv_sems, base + 3 * k + j, sib))
            for k, m in enumerate(mats) for j, (cx, cy) in enumerate(chips)]


def _start_all(pairs):
    for cp, _ in pairs:
        cp.start()


def _finish_all(pairs):
    for _, landing in pairs:
        landing.wait_recv()
    for cp, _ in pairs:
        cp.wait_send()


def _gather_ici_side(own, mats):
    return _Side(own, [jax.ShapeDtypeStruct(m.full, a.dtype) for m, a in zip(mats, own)], 3 * len(mats),
                 lambda i, o, ss, rs: _start_all(_gather_copies(i, o, ss, rs, mats)),
                 lambda i, o, ss, rs: _finish_all(_gather_copies(i, o, ss, rs, mats)))


def _gather_pass_side(fulls, mats):
    return _Side(fulls, [jax.ShapeDtypeStruct(f.shape, f.dtype) for f in fulls], 3 * len(mats),
                 lambda i, o, ss, rs: _start_all(_pass_copies(o, ss, rs, mats)),
                 lambda i, o, ss, rs: _finish_all(_pass_copies(o, ss, rs, mats)), in_place=True)


def _place_own(fulls, own, mats):
    me = _chip_index().astype(jnp.int32).reshape(1)

    def place(full, blk, m):
        tr = _tile(m.rows, 512, 16)
        nr = m.rows // tr
        window = (lambda i, me: (i, me[0])) if m.kind == "col" else (lambda i, me: (me[0] * nr + i, 0))

        def body(me_ref, b_ref, f_ref, o_ref):
            o_ref[...] = b_ref[...]

        return pl.pallas_call(
            body,
            grid_spec=pltpu.PrefetchScalarGridSpec(
                num_scalar_prefetch=1, grid=(nr,),
                in_specs=[pl.BlockSpec((tr, m.cols), lambda i, me: (i, 0)), ANY],
                out_specs=pl.BlockSpec((tr, m.cols), window)),
            out_shape=jax.ShapeDtypeStruct(full.shape, full.dtype), input_output_aliases={2: 0},
            compiler_params=_cp("parallel"), name="place_own")(me, blk, full)

    return [place(o, a, m) for o, a, m in zip(fulls, own, mats)]


def _gather_weights(own, mats):
    n = len(mats)

    def body(*refs):
        own_refs, full_refs, (send_sems, recv_sems) = refs[:n], refs[n:2 * n], refs[2 * n:]
        first = _gather_copies(own_refs, full_refs, send_sems, recv_sems, mats)
        second = _pass_copies(full_refs, send_sems, recv_sems, mats, base=3 * n)
        _start_all(first)
        for (_, landed), (fwd, _) in zip(first, second):
            landed.wait_recv()
            fwd.start()
        for _, landing in second:
            landing.wait_recv()
        for cp, _ in first + second:
            cp.wait_send()

    outs = _comm_call(body, own, [jax.ShapeDtypeStruct(m.full, a.dtype) for m, a in zip(mats, own)], 6 * n,
                      "gather_weights")
    return _place_own(outs, own, mats)


def _swap_copies(g_refs, b_refs, send_sems, recv_sems, mats):
    x, y, c, _ = _place()
    cps = []
    for g, b, m in zip(g_refs, b_refs, mats):
        if m.kind == "col":
            cps.append(_remote(m.own_half(g, 1 - c), b, send_sems, recv_sems, len(cps), (x, y, 1 - c)))
        else:
            for j in range(4):
                cps.append(_remote(m.block(g, j, 1 - c), b.at[j], send_sems, recv_sems, len(cps), (x, y, 1 - c)))
    return [(cp, cp) for cp in cps]


def _swap_shapes(mats):
    return [jax.ShapeDtypeStruct((m.half, 4 * m.cols) if m.kind == "col" else (4, m.half, m.cols), F32) for m in mats]


def _swap_side(grads, mats):
    return _Side(grads, _swap_shapes(mats), sum(1 if m.kind == "col" else 4 for m in mats),
                 lambda i, o, ss, rs: _start_all(_swap_copies(i, o, ss, rs, mats)),
                 lambda i, o, ss, rs: _finish_all(_swap_copies(i, o, ss, rs, mats)))


def _swap_halves(grads, mats):
    def body(*refs):
        n = len(mats)
        pairs = _swap_copies(refs[:n], refs[n:2 * n], *refs[2 * n:], mats)
        _start_all(pairs)
        _finish_all(pairs)

    return _comm_call(body, grads, _swap_shapes(mats), sum(1 if m.kind == "col" else 4 for m in mats), "rs_swap_halves")


def _pair_sum(own, recv, m, cidx):
    tr = _tile(m.half, 256, 16)
    nh = m.half // tr
    if m.kind == "col":
        own_spec = pl.BlockSpec((tr, m.cols), lambda j, i, c: (c[0] * nh + i, j))
        recv_spec = pl.BlockSpec((tr, m.cols), lambda j, i, c: (i, j))
    else:
        own_spec = pl.BlockSpec((tr, m.cols), lambda j, i, c: ((2 * j + c[0]) * nh + i, 0))
        recv_spec = pl.BlockSpec((None, tr, m.cols), lambda j, i, c: (j, i, 0))

    def body(c_ref, a_ref, b_ref, o_ref):
        o_ref[...] = (a_ref[...] + b_ref[...]).astype(o_ref.dtype)

    return pl.pallas_call(
        body,
        grid_spec=pltpu.PrefetchScalarGridSpec(
            num_scalar_prefetch=1, grid=(4, nh), in_specs=[own_spec, recv_spec],
            out_specs=pl.BlockSpec((None, tr, m.cols), lambda j, i, c: (j, i, 0))),
        out_shape=jax.ShapeDtypeStruct((4, m.half, m.cols), BF16),
        compiler_params=_cp("parallel", "parallel"), name="rs_pair_sum")(cidx, own, recv)


def _scatter_copies(s_refs, b_refs, send_sems, recv_sems):
    x, y, c, chips = _place()
    me = 2 * x + y
    return [(_remote(s.at[2 * cx + cy], b.at[me], send_sems, recv_sems, 3 * k + j, (cx, cy, c)),
             _remote(b.at[2 * cx + cy], b.at[2 * cx + cy], send_sems, recv_sems, 3 * k + j, (cx, cy, c)))
            for k, (s, b) in enumerate(zip(s_refs, b_refs)) for j, (cx, cy) in enumerate(chips)]


def _scatter_side(sps):
    return _Side(sps, [jax.ShapeDtypeStruct(s.shape, s.dtype) for s in sps], 3 * len(sps),
                 lambda i, o, ss, rs: _start_all(_scatter_copies(i, o, ss, rs)),
                 lambda i, o, ss, rs: _finish_all(_scatter_copies(i, o, ss, rs)))


def _scatter_chips(sps):
    def body(*refs):
        n = len(sps)
        pairs = _scatter_copies(refs[:n], refs[n:2 * n], *refs[2 * n:])
        _start_all(pairs)
        _finish_all(pairs)

    return _comm_call(body, sps, [jax.ShapeDtypeStruct(s.shape, s.dtype) for s in sps], 3 * len(sps), "rs_scatter_chips")


def _sum_chips(got, sp, me, cidx):
    _, h, w = got.shape
    tr = _tile(h, 256, 16)
    nh = h // tr

    def body(me_ref, c_ref, g_ref, s_ref, o_ref):
        acc = None
        for s in range(4):
            term = jnp.where(me_ref[0] == s, s_ref[...], g_ref[s]).astype(F32)
            acc = term if acc is None else acc + term
        o_ref[...] = acc

    return pl.pallas_call(
        body,
        grid_spec=pltpu.PrefetchScalarGridSpec(
            num_scalar_prefetch=2, grid=(nh,),
            in_specs=[pl.BlockSpec((4, tr, w), lambda i, me, c: (0, i, 0)),
                      pl.BlockSpec((None, tr, w), lambda i, me, c: (me[0], i, 0))],
            out_specs=pl.BlockSpec((tr, w), lambda i, me, c: (c[0] * nh + i, 0))),
        out_shape=jax.ShapeDtypeStruct((2 * h, w), F32), compiler_params=_cp("parallel"), name="rs_sum_chips",
    )(me, cidx, got, sp)


def _share_halves(fs):
    n = len(fs)

    def body(*refs):
        f_refs, (send_sems, recv_sems) = refs[n:2 * n], refs[2 * n:]
        x, y, c, _ = _place()
        cps = []
        for k, f in enumerate(f_refs):
            h = f.shape[0] // 2
            mine, theirs = f.at[pl.ds(_aligned(c * h, 16), h), :], f.at[pl.ds(_aligned((1 - c) * h, 16), h), :]
            cps.append((_remote(mine, mine, send_sems, recv_sems, k, (x, y, 1 - c)),
                        _remote(theirs, theirs, send_sems, recv_sems, k, (x, y, 1 - c))))
        _start_all(cps)
        _finish_all(cps)

    return pl.pallas_call(
        body, out_shape=tuple(jax.ShapeDtypeStruct(f.shape, f.dtype) for f in fs), in_specs=[ANY] * n,
        out_specs=tuple([ANY] * n), input_output_aliases={k: k for k in range(n)},
        scratch_shapes=[pltpu.SemaphoreType.DMA((n,)), pltpu.SemaphoreType.DMA((n,))],
        compiler_params=pltpu.CompilerParams(has_side_effects=True), name="rs_share_halves")(*fs)


def _gather_all(p):
    r, w = p.shape
    rel = [(dx, dy, dc) for dx in (0, 1) for dy in (0, 1) for dc in (0, 1)][1:]

    def body(p_ref, g_ref, send_sems, recv_sems):
        x, y, c = lax.axis_index("x"), lax.axis_index("y"), lax.axis_index("c")
        me = 4 * x + 2 * y + c
        g_ref[me] = p_ref[...]
        peers = [(jnp.bitwise_xor(x, dx), jnp.bitwise_xor(y, dy), jnp.bitwise_xor(c, dc)) for dx, dy, dc in rel]
        cps = [_remote(p_ref, g_ref.at[me], send_sems, recv_sems, k, peer) for k, peer in enumerate(peers)]
        for cp in cps:
            cp.start()
        for k, (px, py, pc) in enumerate(peers):
            slot = g_ref.at[4 * px + 2 * py + pc]
            _remote(slot, slot, send_sems, recv_sems, k, (px, py, pc)).wait_recv()
        for cp in cps:
            cp.wait_send()

    return pl.pallas_call(
        body, out_shape=jax.ShapeDtypeStruct((8, r, w), p.dtype),
        in_specs=[pl.BlockSpec(memory_space=pltpu.VMEM)], out_specs=pl.BlockSpec(memory_space=pltpu.VMEM),
        scratch_shapes=[pltpu.SemaphoreType.DMA((7,)), pltpu.SemaphoreType.DMA((7,))],
        compiler_params=pltpu.CompilerParams(has_side_effects=True, vmem_limit_bytes=VMEM_LIMIT),
        name=("gather_all_%dx%d" % (r, w)))(p)


def _pack_small(arrs):
    flat = jnp.concatenate([a.reshape(-1) for a in arrs])
    rows = -(-flat.shape[0] // (SMALL_W * 8)) * 8
    return jnp.pad(flat, (0, rows * SMALL_W - flat.shape[0])).reshape(rows, SMALL_W)


def _unpack(buf, shapes):
    flat = buf.reshape(-1)
    out, off = [], 0
    for sh in shapes:
        n = math.prod(sh)
        out.append(flat[off:off + n].reshape(sh))
        off += n
    return out


def _chip_padded(pieces, width, padded):
    out, used = [], 0
    for p in pieces:
        at = 0
        while at < p.shape[1]:
            take = min(p.shape[1] - at, width - used)
            out.append(p[:, at:at + take])
            at, used = at + take, used + take
            if used == width:
                out.append(jnp.zeros((p.shape[0], padded - width), p.dtype))
                used = 0
    assert used == 0
    return jnp.concatenate(out, axis=1)


class _Cfg:
    def __init__(self, x, w_in, attn_sinks, dn_a_log, w_ffn_up):
        self.S, self.D = x.shape[1], x.shape[2]
        self.L = w_in.shape[0]
        self.HA, self.HD = attn_sinks.shape[1], dn_a_log.shape[1]
        self.IN = 4 * w_in.shape[2]
        kvw = self.IN - self.HA * HEAD - 4 * self.HD * HEAD - 2 * self.HD - 2 * self.D
        self.KV = kvw // (2 * HEAD)
        self.G = self.HA // self.KV
        self.FF = 2 * w_ffn_up.shape[2]
        self.KA = self.HA * HEAD
        self.VA = self.KA + self.KV * HEAD
        self.QD = self.VA + self.KV * HEAD
        self.Z = self.QD + 3 * self.HD * HEAD
        self.GA = self.Z + self.HD * HEAD
        self.GD = self.GA + self.D
        self.NB = self.GD + self.D
        assert self.NB + 2 * self.HD == self.IN and self.G <= 8 and 2 * self.HD <= LANES


def kernel(x, positions, norm_mix_pre, w_in, attn_sinks, dn_conv_w, dn_a_log, dn_dt_bias, dn_norm_w, w_branch_attn, w_branch_dn, w_out, norm_mix_post, norm_ffn_pre, w_ffn_up, ffn_conv_w, ffn_conv_b, w_ffn_down, norm_ffn_post, loss_target, m_norm_mix_pre, m_w_in, m_attn_sinks, m_dn_conv_w, m_dn_a_log, m_dn_dt_bias, m_dn_norm_w, m_w_branch_attn, m_w_branch_dn, m_w_out, m_norm_mix_post, m_norm_ffn_pre, m_w_ffn_up, m_ffn_conv_w, m_ffn_conv_b, m_w_ffn_down, m_norm_ffn_post, v_norm_mix_pre, v_w_in, v_attn_sinks, v_dn_conv_w, v_dn_a_log, v_dn_dt_bias, v_dn_norm_w, v_w_branch_attn, v_w_branch_dn, v_w_out, v_norm_mix_post, v_norm_ffn_pre, v_w_ffn_up, v_ffn_conv_w, v_ffn_conv_b, v_w_ffn_down, v_norm_ffn_post):
    cfg = _Cfg(x, w_in, attn_sinks, dn_a_log, w_ffn_up)
    S, D, L, HD = cfg.S, cfg.D, cfg.L, cfg.HD
    o1 = cfg.Z
    ins = w_in.shape[2]
    insp = -(-ins // SHARD_PAD) * SHARD_PAD
    mats = [_Mat("col", D, insp), _Mat("row", *w_branch_attn.shape[1:]), _Mat("row", *w_branch_dn.shape[1:]),
            _Mat("row", *w_out.shape[1:]), _Mat("col", *w_ffn_up.shape[1:]), _Mat("row", *w_ffn_down.shape[1:])]
    me = _chip_index()

    taps = _gather_all(_pack_small([dn_conv_w, ffn_conv_w]))
    tap_shapes = [dn_conv_w.shape, ffn_conv_w.shape]
    per_chip = [_unpack(taps[2 * j], tap_shapes) for j in range(4)]
    dnw_all = jnp.concatenate([per_chip[j][0] for j in range(4)], axis=2)
    fcw_all = jnp.concatenate([per_chip[j][1] for j in range(4)], axis=2)

    def own_blocks(l):
        return [jnp.pad(w_in[l].astype(BF16), ((0, 0), (0, insp - ins))), w_branch_attn[l].astype(BF16),
                w_branch_dn[l].astype(BF16), w_out[l].astype(BF16), w_ffn_up[l].astype(BF16), w_ffn_down[l].astype(BF16)]

    def layer_weights(l, fulls):
        win_sm, wbra, wbrd, wout, wup, wdown = fulls
        win = jnp.concatenate([win_sm[:, j * insp:j * insp + ins] for j in range(4)], axis=1)
        return dict(
            win_sm=win_sm, wbig=jnp.concatenate([win[:, :o1], win[:, o1 + 2 * HD:]], axis=1),
            wba=jnp.pad(win[:, o1:o1 + 2 * HD], ((0, 0), (0, LANES - 2 * HD))),
            wbra=wbra, wbrd=wbrd, wout=wout, wup=wup, wdown=wdown, dnw=dnw_all[l], fcw=fcw_all[l])

    weights = [layer_weights(0, _gather_weights(own_blocks(0), mats))] + [None] * (L - 1)

    inv_freq = ROPE_THETA ** (-jnp.arange(0, 2 * ROT_HALF, 2, dtype=F32) / (2 * ROT_HALF))
    ang = positions.reshape(S, 1).astype(F32) * inv_freq
    cosf = jnp.concatenate([jnp.cos(ang), jnp.cos(ang), jnp.ones((S, HEAD - 2 * ROT_HALF), F32)], axis=1)
    sinf = jnp.concatenate([-jnp.sin(ang), jnp.sin(ang), jnp.zeros((S, HEAD - 2 * ROT_HALF), F32)], axis=1)

    h = x.reshape(S, D)
    saved = []
    for l in range(L):
        wt = weights[l]
        t = dict(h0=h)
        t["xn"] = _rms_fwd(h, norm_mix_pre[l], out_dtype=BF16, name="norm_mix_pre")
        t["proj"] = _mm(t["xn"], wt["wbig"], name="proj")
        t["pba"] = _mm(t["xn"], wt["wba"], name="proj_ba")
        t["oa"] = _attn_fwd(t["proj"], cosf, sinf, attn_sinks[l], cfg)
        t["qkv"] = _dnconv_fwd(t["proj"], wt["dnw"], cfg)
        nxt = own_blocks(l + 1) if l + 1 < L else None
        t["od"], t["st"], *landed = _gdn_fwd(t["qkv"], t["proj"], t["pba"], dn_a_log[l], dn_dt_bias[l], dn_norm_w[l], cfg,
                                             side=_gather_ici_side(nxt, mats) if nxt else None)
        t["pa"] = _mm(t["oa"], wt["wbra"], name="branch_attn")
        t["pd"] = _mm(t["od"], wt["wbrd"], name="branch_dn")
        t["y"] = _merge_fwd(t["proj"], t["pa"], t["pd"], cfg)
        t["mix"] = _mm(t["y"], wt["wout"], name="mix_out")
        t["h1"] = _rms_fwd(t["mix"], norm_mix_post[l], res=h, out_dtype=F32, name="norm_mix_post")
        t["xf"] = _rms_fwd(t["h1"], norm_ffn_pre[l], out_dtype=BF16, name="norm_ffn_pre")
        if nxt:
            t["up"], *landed = _mm(t["xf"], wt["wup"], name="ffn_up_pass", side=_gather_pass_side(landed, mats))
            weights[l + 1] = layer_weights(l + 1, _place_own(landed, nxt, mats))
        else:
            t["up"] = _mm(t["xf"], wt["wup"], name="ffn_up")
        t["act"] = _glu_fwd(t["up"], wt["fcw"], ffn_conv_b[l], cfg)
        t["f"] = _mm(t["act"], wt["wdown"], name="ffn_down")
        h = _rms_fwd(t["f"], norm_ffn_post[l], res=t["h1"], out_dtype=F32, name="norm_ffn_post")
        saved.append(t)

    loss_local, dh = _loss_head(h, loss_target.reshape(S, D))
    loss = lax.psum(loss_local, ("x", "y", "c"))

    cidx = lax.axis_index("c").astype(jnp.int32).reshape(1)
    midx = me.astype(jnp.int32).reshape(1)
    big_grads, small = [None] * L, [None] * L
    reduce_blocks = lambda got, pair: _share_halves([_sum_chips(a, b, midx, cidx) for a, b in zip(got, pair)])
    pair_sums = lambda full, theirs: [_pair_sum(a, b, m, cidx) for a, b, m in zip(full, theirs, mats)]
    unswapped = waiting = None
    for l in reversed(range(L)):
        wt, t = weights[l], saved[l]
        df, g_nfpost = _rms_bwd(t["f"], norm_ffn_post[l], dh, out_dtype=BF16, name="norm_ffn_post_bwd")
        dact = _mm(df, wt["wdown"], tb=True, name="ffn_down_dx")
        g_wdown = _mm(t["act"], df, ta=True, name="ffn_down_dw")
        dup, g_fcw, g_fcb = _glu_bwd(t["up"], dact, wt["fcw"], ffn_conv_b[l], cfg)
        if unswapped:
            dxf, *theirs = _mm(dup, wt["wup"], tb=True, name="ffn_up_dx_swap", side=_swap_side(unswapped[1], mats))
            waiting = (unswapped[0], pair_sums(unswapped[1], theirs))
        else:
            dxf = _mm(dup, wt["wup"], tb=True, name="ffn_up_dx")
        g_wup = _mm(t["xf"], dup, ta=True, name="ffn_up_dw")
        dh1, g_nfp = _rms_bwd(t["h1"], norm_ffn_pre[l], dxf, add=dh, name="norm_ffn_pre_bwd")
        dmix, g_nmpost = _rms_bwd(t["mix"], norm_mix_post[l], dh1, out_dtype=BF16, name="norm_mix_post_bwd")
        dy = _mm(dmix, wt["wout"], tb=True, name="mix_out_dx")
        g_wout = _mm(t["y"], dmix, ta=True, name="mix_out_dw")
        dpa, dpd, dga, dgd = _merge_bwd(t["proj"], t["pa"], t["pd"], dy, cfg)
        doa = _mm(dpa, wt["wbra"], tb=True, out_dtype=BF16, name="branch_attn_dx")
        g_wbra = _mm(t["oa"], dpa, ta=True, name="branch_attn_dw")
        dod = _mm(dpd, wt["wbrd"], tb=True, out_dtype=BF16, name="branch_dn_dx")
        g_wbrd = _mm(t["od"], dpd, ta=True, name="branch_dn_dw")
        dqkv_c, dz, dba, g_alog, g_dt, g_dnnorm, got = _gdn_bwd(
            t["qkv"], t["proj"], t["pba"], dn_a_log[l], dn_dt_bias[l], dn_norm_w[l], t["st"], dod, cfg,
            side=_scatter_side(waiting[1]) if waiting else None)
        if waiting:
            big_grads[waiting[0]] = reduce_blocks(got, waiting[1])
            waiting = None
        dqkv_d, g_dnw = _dnconv_bwd(t["proj"], dqkv_c, wt["dnw"], cfg)
        dqkv_a, g_sinks = _attn_bwd(t["proj"], doa, cosf, sinf, attn_sinks[l], cfg)
        dproj = _chip_padded([dqkv_a, dqkv_d, dba[:, :2 * HD], dz, dga, dgd], ins, insp)
        dxn = _mm(dproj, wt["win_sm"], tb=True, name="proj_dx")
        g_win = _mm(t["xn"], dproj, ta=True, name="proj_dw")
        dh, g_nmp = _rms_bwd(t["h0"], norm_mix_pre[l], dxn, add=dh1, name="norm_mix_pre_bwd")

        unswapped = (l, [g_win, g_wbra, g_wbrd, g_wout, g_wup, g_wdown])
        small[l] = [g_nmp, g_sinks, g_alog, g_dt, g_dnnorm, g_nmpost, g_nfp, g_fcb, g_nfpost, g_dnw, g_fcw]

    last_pair = pair_sums(unswapped[1], _swap_halves(unswapped[1], mats))
    big_grads[unswapped[0]] = reduce_blocks(_scatter_chips(last_pair), last_pair)
    grad_x = dh.reshape(1, S, D)

    small_shapes = [a.shape for a in small[0]]
    summed = _sum_leading(_gather_all(_pack_small([a for l in range(L) for a in small[l]])), name="sum_devices")
    per_layer = _unpack(summed, small_shapes * L)
    red = [jnp.stack([per_layer[l * len(small_shapes) + i] for l in range(L)]) for i in range(len(small_shapes))]
    g_dn_conv = lax.dynamic_slice_in_dim(red[9], me * dn_conv_w.shape[2], dn_conv_w.shape[2], axis=2)
    g_ffn_conv = lax.dynamic_slice_in_dim(red[10], me * ffn_conv_w.shape[2], ffn_conv_w.shape[2], axis=2)
    stacked = lambda i: jnp.stack([big_grads[l][i] for l in range(L)])

    grads = dict(
        norm_mix_pre=red[0], w_in=stacked(0), attn_sinks=red[1], dn_conv_w=g_dn_conv, dn_a_log=red[2],
        dn_dt_bias=red[3], dn_norm_w=red[4], w_branch_attn=stacked(1), w_branch_dn=stacked(2), w_out=stacked(3),
        norm_mix_post=red[5], norm_ffn_pre=red[6], w_ffn_up=stacked(4), ffn_conv_w=g_ffn_conv, ffn_conv_b=red[7],
        w_ffn_down=stacked(5), norm_ffn_post=red[8])

    params = dict(
        norm_mix_pre=(norm_mix_pre, m_norm_mix_pre, v_norm_mix_pre), w_in=(w_in, m_w_in, v_w_in),
        attn_sinks=(attn_sinks, m_attn_sinks, v_attn_sinks), dn_conv_w=(dn_conv_w, m_dn_conv_w, v_dn_conv_w),
        dn_a_log=(dn_a_log, m_dn_a_log, v_dn_a_log), dn_dt_bias=(dn_dt_bias, m_dn_dt_bias, v_dn_dt_bias),
        dn_norm_w=(dn_norm_w, m_dn_norm_w, v_dn_norm_w),
        w_branch_attn=(w_branch_attn, m_w_branch_attn, v_w_branch_attn),
        w_branch_dn=(w_branch_dn, m_w_branch_dn, v_w_branch_dn), w_out=(w_out, m_w_out, v_w_out),
        norm_mix_post=(norm_mix_post, m_norm_mix_post, v_norm_mix_post),
        norm_ffn_pre=(norm_ffn_pre, m_norm_ffn_pre, v_norm_ffn_pre), w_ffn_up=(w_ffn_up, m_w_ffn_up, v_w_ffn_up),
        ffn_conv_w=(ffn_conv_w, m_ffn_conv_w, v_ffn_conv_w), ffn_conv_b=(ffn_conv_b, m_ffn_conv_b, v_ffn_conv_b),
        w_ffn_down=(w_ffn_down, m_w_ffn_down, v_w_ffn_down),
        norm_ffn_post=(norm_ffn_post, m_norm_ffn_post, v_norm_ffn_post))
    names = list(params)

    grad_out, delta, new_m, new_v = {}, {}, {}, {}
    small_names = [n for n in names if params[n][0].ndim == 2]
    for n in names:
        if n in small_names:
            continue
        w, m, v = params[n]
        grad_out[n], delta[n], new_m[n], new_v[n] = _adamw(w, grads[n], m, v, name="adamw_" + n)

    outs = _adamw(*[_pack_small([params[n][i] if i is not None else grads[n] for n in small_names])[None]
                    for i in (0, None, 1, 2)], name="adamw_small")
    shapes = [params[n][0].shape for n in small_names]
    for dst, o in zip((grad_out, delta, new_m, new_v), outs):
        for n, a in zip(small_names, _unpack(o, shapes)):
            dst[n] = a

    return (loss, grad_x, *[grad_out[n] for n in names], *[delta[n] for n in names],
            *[new_m[n] for n in names], *[new_v[n] for n in names])
```

```python
import functools
import math

import jax
import jax.numpy as jnp
from jax import lax
from jax.experimental import pallas as pl
from jax.experimental.pallas import tpu as pltpu

F32, BF16 = jnp.float32, jnp.bfloat16
EPS = 1e-6
HEAD = 128
ATTN_BLOCK = 128
ROT_HALF = 16
ROPE_THETA = 500000.0
CHUNK = 64
DN_CONV, FFN_CONV = 4, 3
GDN_HEADS_PER_STEP = 4
MM_TILES = (512, 1024, 3968)
HALO = 8
LANES = 128
SUM_TR = 512
SHARD_PAD = 1024
SMALL_W = 1024
ADAM_LR, ADAM_B1, ADAM_B2, ADAM_EPS, ADAM_WD, ADAM_STEP = 0.001, 0.9, 0.999, 1e-08, 0.01, 10
VMEM_LIMIT = 56 * 1024 * 1024
MESH = pl.DeviceIdType.MESH
ANY = pl.BlockSpec(memory_space=pl.ANY)


def _cp(*sem):
    return pltpu.CompilerParams(dimension_semantics=sem, vmem_limit_bytes=VMEM_LIMIT)


def _tile(n, pref, unit=LANES):
    if n <= pref:
        return n
    t = (pref // unit) * unit
    while t >= unit:
        if n % t == 0:
            return t
        t -= unit
    raise ValueError((n, pref, unit))


def _sigmoid(x):
    return 1.0 / (1.0 + jnp.exp(-x))


def _softplus(x):
    return jnp.maximum(x, 0.0) + jnp.log(1.0 + jnp.exp(-jnp.abs(x)))


def _dsilu(y):
    s = _sigmoid(y)
    return s * (1.0 + y * (1.0 - s))


class _Side:
    def __init__(self, ins, out_shapes, n_sems, start, finish, in_place=False):
        self.ins, self.out_shapes, self.n_sems = list(ins), list(out_shapes), n_sems
        self.start, self.finish, self.in_place = start, finish, in_place


def _side_split(refs, n_in, n_out, side):
    si, so = (len(side.ins), len(side.out_shapes)) if side else (0, 0)
    ins, sins = refs[:n_in], refs[n_in:n_in + si]
    outs, souts = refs[n_in + si:n_in + si + n_out], refs[n_in + si + n_out:n_in + si + n_out + so]
    rest = refs[n_in + si + n_out + so:]
    scratch, sems = (rest[:-2], rest[-2:]) if side else (rest, ())
    return ins, outs, scratch, (sins, souts, sems)


def _side_steps(side, parts, grid):
    if side is None:
        return (lambda: None), (lambda: None)
    sins, souts, sems = parts
    ids = [pl.program_id(a) for a in range(len(grid))]
    first = functools.reduce(jnp.logical_and, [i == 0 for i in ids])
    last = functools.reduce(jnp.logical_and, [i == n - 1 for i, n in zip(ids, grid)])

    def begin():
        @pl.when(first)
        def _():
            side.start(sins, souts, *sems)

    def end():
        @pl.when(last)
        def _():
            side.finish(sins, souts, *sems)

    return begin, end


def _side_call_args(side, n_in, n_out):
    if side is None:
        return [], [], [], [], [], {}
    sems = [pltpu.SemaphoreType.DMA((side.n_sems,)), pltpu.SemaphoreType.DMA((side.n_sems,))]
    aliases = {n_in + k: n_out + k for k in range(len(side.ins))} if side.in_place else {}
    return [ANY] * len(side.ins), [ANY] * len(side.out_shapes), side.out_shapes, sems, side.ins, aliases


def _mm(a, b, *, ta=False, tb=False, add=None, out_dtype=F32, name, tiles=MM_TILES, side=None):
    if ta:
        kdim, m = a.shape
    else:
        m, kdim = a.shape
    if tb:
        n, k2 = b.shape
    else:
        k2, n = b.shape
    assert kdim == k2, (a.shape, b.shape, ta, tb)
    tm, tn, tk = _tile(m, tiles[0]), _tile(n, tiles[1]), _tile(kdim, tiles[2])
    nk = kdim // tk
    grid = (m // tm, n // tn, nk)
    dims = (((0 if ta else 1,), (1 if tb else 0,)), ((), ()))
    n_in = 2 if add is None else 3

    def body(*refs):
        ins, (o_ref,), (acc_ref,), parts = _side_split(refs, n_in, 1, side)
        begin, end = _side_steps(side, parts, grid)
        begin()
        a_ref, b_ref = ins[:2]
        k = pl.program_id(2)

        @pl.when(k == 0)
        def _():
            acc_ref[...] = jnp.zeros_like(acc_ref)

        acc_ref[...] += lax.dot_general(a_ref[...].astype(BF16), b_ref[...].astype(BF16), dims,
                                        preferred_element_type=F32)

        @pl.when(k == nk - 1)
        def _():
            r = acc_ref[...]
            if add is not None:
                r = r + ins[2][...]
            o_ref[...] = r.astype(o_ref.dtype)

        end()

    a_spec = pl.BlockSpec((tk, tm), lambda i, j, k: (k, i)) if ta else pl.BlockSpec((tm, tk), lambda i, j, k: (i, k))
    b_spec = pl.BlockSpec((tn, tk), lambda i, j, k: (j, k)) if tb else pl.BlockSpec((tk, tn), lambda i, j, k: (k, j))
    o_spec = pl.BlockSpec((tm, tn), lambda i, j, k: (i, j))
    ins, specs = [a, b], [a_spec, b_spec]
    if add is not None:
        ins.append(add)
        specs.append(o_spec)
    s_in, s_out, s_shapes, s_sems, s_ops, aliases = _side_call_args(side, n_in, 1)
    sem = ("arbitrary",) * 3 if side else ("parallel", "parallel", "arbitrary")
    outs = pl.pallas_call(
        body, grid=grid, in_specs=specs + s_in, out_specs=(o_spec, *s_out),
        out_shape=(jax.ShapeDtypeStruct((m, n), out_dtype), *s_shapes),
        scratch_shapes=[pltpu.VMEM((tm, tn), F32)] + s_sems, input_output_aliases=aliases,
        compiler_params=_cp(*sem), name=name,
    )(*ins, *s_ops)
    return outs if side else outs[0]


def _rms_fwd(x, w, res=None, *, out_dtype, name):
    s, d = x.shape
    tr = _tile(s, 256, 8)

    def body(*refs):
        if res is None:
            x_ref, w_ref, o_ref = refs
        else:
            x_ref, w_ref, r_ref, o_ref = refs
        xv = x_ref[...]
        y = xv * lax.rsqrt(jnp.mean(xv * xv, axis=-1, keepdims=True) + EPS) * w_ref[...]
        if res is not None:
            y = y + r_ref[...]
        o_ref[...] = y.astype(o_ref.dtype)

    row = pl.BlockSpec((tr, d), lambda i: (i, 0))
    par = pl.BlockSpec((1, d), lambda i: (0, 0))
    ins, specs = [x, w.reshape(1, d)], [row, par]
    if res is not None:
        ins.append(res)
        specs.append(row)
    return pl.pallas_call(body, grid=(s // tr,), in_specs=specs, out_specs=row,
                          out_shape=jax.ShapeDtypeStruct((s, d), out_dtype),
                          compiler_params=_cp("parallel"), name=name)(*ins)


def _rms_bwd(x, w, dy, add=None, *, out_dtype=F32, name):
    s, d = x.shape
    tr = _tile(s, 256, 8)

    def body(*refs):
        if add is None:
            x_ref, w_ref, dy_ref, dx_ref, dw_ref = refs
        else:
            x_ref, w_ref, dy_ref, add_ref, dx_ref, dw_ref = refs
        xv = x_ref[...]
        dyv = dy_ref[...].astype(F32)
        r = lax.rsqrt(jnp.mean(xv * xv, axis=-1, keepdims=True) + EPS)
        xh = xv * r
        dyw = dyv * w_ref[...]
        dx = r * (dyw - xh * jnp.mean(dyw * xh, axis=-1, keepdims=True))
        if add is not None:
            dx = dx + add_ref[...]
        dx_ref[...] = dx.astype(dx_ref.dtype)

        @pl.when(pl.program_id(0) == 0)
        def _():
            dw_ref[...] = jnp.zeros_like(dw_ref)

        dw_ref[...] += jnp.sum(dyv * xh, axis=0, keepdims=True)

    row = pl.BlockSpec((tr, d), lambda i: (i, 0))
    par = pl.BlockSpec((1, d), lambda i: (0, 0))
    ins, specs = [x, w.reshape(1, d), dy], [row, par, row]
    if add is not None:
        ins.append(add)
        specs.append(row)
    dx, dw = pl.pallas_call(
        body, grid=(s // tr,), in_specs=specs, out_specs=(row, par),
        out_shape=(jax.ShapeDtypeStruct((s, d), out_dtype), jax.ShapeDtypeStruct((1, d), F32)),
        compiler_params=_cp("arbitrary"), name=name)(*ins)
    return dx, dw.reshape(d)


def _loss_head(h, tgt):
    s, d = h.shape
    tr = _tile(s, 256, 8)

    def body(h_ref, t_ref, dh_ref, l_ref):
        e = h_ref[...] - t_ref[...]
        dh_ref[...] = e * (1.0 / d)

        @pl.when(pl.program_id(0) == 0)
        def _():
            l_ref[...] = jnp.zeros_like(l_ref)

        l_ref[...] += 0.5 * jnp.sum(jnp.mean(e * e, axis=-1, keepdims=True), axis=0, keepdims=True)

    row = pl.BlockSpec((tr, d), lambda i: (i, 0))
    dh, l = pl.pallas_call(
        body, grid=(s // tr,), in_specs=[row, row], out_specs=(row, pl.BlockSpec((1, 1), lambda i: (0, 0))),
        out_shape=(jax.ShapeDtypeStruct((s, d), F32), jax.ShapeDtypeStruct((1, 1), F32)),
        compiler_params=_cp("arbitrary"), name="loss_head")(h, tgt)
    return l[0, 0], dh


def _conv_rows(xe, w, width):
    y = None
    for j in range(width):
        s = width - 1 - j
        term = (pltpu.roll(xe, s, 0) if s else xe)[HALO:] * w[j:j + 1, :]
        y = term if y is None else y + term
    return y


def _conv_rows_t(dy_ext, w, width, tr):
    n = dy_ext.shape[0]
    dx = None
    for j in range(width):
        s = width - 1 - j
        term = (pltpu.roll(dy_ext, n - s, 0) if s else dy_ext)[:tr] * w[j:j + 1, :]
        dx = term if dx is None else dx + term
    return dx


def _conv_dw(dy, xe, width):
    rows = []
    for j in range(width):
        s = width - 1 - j
        rows.append(jnp.sum(dy * (pltpu.roll(xe, s, 0) if s else xe)[HALO:], axis=0, keepdims=True))
    return jnp.concatenate(rows, axis=0)


def _halo_specs(tr, tc, coff, nrow_blocks):
    per = tr // HALO
    last = nrow_blocks * per - 1
    cur = pl.BlockSpec((tr, tc), lambda j, r: (r, coff + j))
    prev = pl.BlockSpec((HALO, tc), lambda j, r: (jnp.maximum(r * per - 1, 0), coff + j))
    nxt = pl.BlockSpec((HALO, tc), lambda j, r: (jnp.minimum((r + 1) * per, last), coff + j))
    return cur, prev, nxt


def _dnconv_fwd(proj, w, cfg):
    s, width = cfg.S, 3 * cfg.HD * HEAD
    tc = _tile(math.gcd(cfg.QD, width), 512)
    tr = _tile(s, 512, 8)
    nr = s // tr
    cur, prev, _ = _halo_specs(tr, tc, cfg.QD // tc, nr)

    def body(x_ref, xp_ref, w_ref, o_ref):
        hist = jnp.where(pl.program_id(1) > 0, xp_ref[...], 0.0)
        y = _conv_rows(jnp.concatenate([hist, x_ref[...]], axis=0), w_ref[...], DN_CONV)
        o_ref[...] = y * _sigmoid(y)

    return pl.pallas_call(
        body, grid=(width // tc, nr),
        in_specs=[cur, prev, pl.BlockSpec((DN_CONV, tc), lambda j, r: (0, j))],
        out_specs=pl.BlockSpec((tr, tc), lambda j, r: (r, j)),
        out_shape=jax.ShapeDtypeStruct((s, width), F32),
        compiler_params=_cp("parallel", "parallel"), name="dnconv_fwd")(proj, proj, w)


def _dnconv_bwd(proj, du3, w, cfg):
    s, hw = cfg.S, cfg.HD * HEAD
    width = 3 * hw
    tc = _tile(math.gcd(cfg.QD, hw), 512)
    tr = _tile(s, 512, 8)
    nr = s // tr
    per, last = tr // HALO, s // HALO - 1
    cur, prev, nxt = _halo_specs(tr, tc, cfg.QD // tc, nr)
    pc = hw // tc
    du_cur = pl.BlockSpec((1, tr, tc), lambda j, r: (j // pc, r, j % pc))
    du_nxt = pl.BlockSpec((1, HALO, tc), lambda j, r: (j // pc, jnp.minimum((r + 1) * per, last), j % pc))

    def body(x_ref, xp_ref, xn_ref, du_ref, dun_ref, w_ref, dx_ref, dw_ref):
        r = pl.program_id(1)
        wv = w_ref[...]
        hist = jnp.where(r > 0, xp_ref[...], 0.0)
        xe = jnp.concatenate([hist, x_ref[...]], axis=0)
        y_ext = _conv_rows(jnp.concatenate([xe, xn_ref[...]], axis=0), wv, DN_CONV)
        du_ext = jnp.concatenate([du_ref[0], jnp.where(r < nr - 1, dun_ref[0], 0.0)], axis=0)
        dy_ext = du_ext * _dsilu(y_ext)
        dx_ref[...] = _conv_rows_t(dy_ext, wv, DN_CONV, tr).astype(dx_ref.dtype)

        @pl.when(r == 0)
        def _():
            dw_ref[...] = jnp.zeros_like(dw_ref)

        dw_ref[...] += _conv_dw(dy_ext[:tr], xe, DN_CONV)

    wspec = pl.BlockSpec((DN_CONV, tc), lambda j, r: (0, j))
    return pl.pallas_call(
        body, grid=(width // tc, nr),
        in_specs=[cur, prev, nxt, du_cur, du_nxt, wspec],
        out_specs=(pl.BlockSpec((tr, tc), lambda j, r: (r, j)), wspec),
        out_shape=(jax.ShapeDtypeStruct((s, width), BF16), jax.ShapeDtypeStruct((DN_CONV, width), F32)),
        compiler_params=_cp("parallel", "arbitrary"), name="dnconv_bwd")(proj, proj, proj, du3, du3, w)


def _glu_fwd(up, w, b, cfg):
    s, ff = cfg.S, cfg.FF
    tc = _tile(ff, 512)
    tr = _tile(s, 512, 8)
    nr, nc = s // tr, ff // tc
    g_cur, g_prev, _ = _halo_specs(tr, tc, 0, nr)
    v_cur, v_prev, _ = _halo_specs(tr, tc, nc, nr)

    def wspec(rows, off):
        return pl.BlockSpec((rows, tc), lambda j, r: (0, off + j))

    def body(g_ref, gp_ref, v_ref, vp_ref, wg_ref, wv_ref, bg_ref, bv_ref, o_ref):
        first = pl.program_id(1) > 0
        ug = _conv_rows(jnp.concatenate([jnp.where(first, gp_ref[...], 0.0), g_ref[...]], axis=0), wg_ref[...], FFN_CONV) + bg_ref[...]
        uv = _conv_rows(jnp.concatenate([jnp.where(first, vp_ref[...], 0.0), v_ref[...]], axis=0), wv_ref[...], FFN_CONV) + bv_ref[...]
        o_ref[...] = (ug * _sigmoid(ug) * uv).astype(o_ref.dtype)

    b2 = b.reshape(1, 2 * ff)
    return pl.pallas_call(
        body, grid=(nc, nr),
        in_specs=[g_cur, g_prev, v_cur, v_prev, wspec(FFN_CONV, 0), wspec(FFN_CONV, nc), wspec(1, 0), wspec(1, nc)],
        out_specs=pl.BlockSpec((tr, tc), lambda j, r: (r, j)),
        out_shape=jax.ShapeDtypeStruct((s, ff), BF16),
        compiler_params=_cp("parallel", "parallel"), name="glu_fwd")(up, up, up, up, w, w, b2, b2)


def _glu_bwd(up, dact, w, b, cfg):
    s, ff = cfg.S, cfg.FF
    tc = _tile(ff, 512)
    tr = _tile(s, 512, 8)
    nr, nc = s // tr, ff // tc
    g_cur, g_prev, g_nxt = _halo_specs(tr, tc, 0, nr)
    v_cur, v_prev, v_nxt = _halo_specs(tr, tc, nc, nr)
    d_cur, _, d_nxt = _halo_specs(tr, tc, 0, nr)

    def wspec(rows, off):
        return pl.BlockSpec((rows, tc), lambda j, r: (0, off + j))

    def body(g_ref, gp_ref, gn_ref, v_ref, vp_ref, vn_ref, d_ref, dn_ref, wg_ref, wv_ref, bg_ref, bv_ref,
             dup_ref, dw_ref, db_ref):
        r = pl.program_id(1)
        wg, wv = wg_ref[...], wv_ref[...]
        ge = jnp.concatenate([jnp.where(r > 0, gp_ref[...], 0.0), g_ref[...]], axis=0)
        ve = jnp.concatenate([jnp.where(r > 0, vp_ref[...], 0.0), v_ref[...]], axis=0)
        ug = _conv_rows(jnp.concatenate([ge, gn_ref[...]], axis=0), wg, FFN_CONV) + bg_ref[...]
        uv = _conv_rows(jnp.concatenate([ve, vn_ref[...]], axis=0), wv, FFN_CONV) + bv_ref[...]
        da = jnp.concatenate([d_ref[...].astype(F32), jnp.where(r < nr - 1, dn_ref[...].astype(F32), 0.0)], axis=0)
        dug = da * uv * _dsilu(ug)
        duv = da * ug * _sigmoid(ug)
        dup_ref[0] = _conv_rows_t(dug, wg, FFN_CONV, tr).astype(dup_ref.dtype)
        dup_ref[1] = _conv_rows_t(duv, wv, FFN_CONV, tr).astype(dup_ref.dtype)

        @pl.when(r == 0)
        def _():
            dw_ref[...] = jnp.zeros_like(dw_ref)
            db_ref[...] = jnp.zeros_like(db_ref)

        dw_ref[0] += _conv_dw(dug[:tr], ge, FFN_CONV)
        dw_ref[1] += _conv_dw(duv[:tr], ve, FFN_CONV)
        db_ref[0] += jnp.sum(dug[:tr], axis=0, keepdims=True)
        db_ref[1] += jnp.sum(duv[:tr], axis=0, keepdims=True)

    b2 = b.reshape(1, 2 * ff)
    dup, dw, db = pl.pallas_call(
        body, grid=(nc, nr),
        in_specs=[g_cur, g_prev, g_nxt, v_cur, v_prev, v_nxt, d_cur, d_nxt,
                  wspec(FFN_CONV, 0), wspec(FFN_CONV, nc), wspec(1, 0), wspec(1, nc)],
        out_specs=(pl.BlockSpec((2, tr, tc), lambda j, r: (0, r, j)),
                   pl.BlockSpec((2, FFN_CONV, tc), lambda j, r: (0, 0, j)),
                   pl.BlockSpec((2, 1, tc), lambda j, r: (0, 0, j))),
        out_shape=(jax.ShapeDtypeStruct((2, s, ff), BF16), jax.ShapeDtypeStruct((2, FFN_CONV, ff), F32),
                   jax.ShapeDtypeStruct((2, 1, ff), F32)),
        compiler_params=_cp("parallel", "arbitrary"), name="glu_bwd",
    )(up, up, up, up, up, up, dact, dact, w, w, b2, b2)
    dup = jnp.concatenate([dup[0], dup[1]], axis=1)
    dw = jnp.transpose(dw, (1, 0, 2)).reshape(FFN_CONV, 2 * ff)
    return dup, dw, db.reshape(2 * ff)


def _merge_specs(cfg):
    tc = _tile(math.gcd(math.gcd(cfg.GA, cfg.GD), cfg.D), 512)
    tr = _tile(cfg.S, 512, 8)
    ga = pl.BlockSpec((tr, tc), lambda r, j: (r, cfg.GA // tc + j))
    gd = pl.BlockSpec((tr, tc), lambda r, j: (r, cfg.GD // tc + j))
    pd = pl.BlockSpec((tr, tc), lambda r, j: (r, j))
    return (cfg.S // tr, cfg.D // tc), ga, gd, pd


def _merge_fwd(proj, pa, pdn, cfg):
    grid, ga, gd, pd = _merge_specs(cfg)

    def body(ga_ref, gd_ref, pa_ref, pd_ref, y_ref):
        y_ref[...] = (_sigmoid(ga_ref[...]) * pa_ref[...] + _sigmoid(gd_ref[...]) * pd_ref[...]).astype(y_ref.dtype)

    return pl.pallas_call(body, grid=grid, in_specs=[ga, gd, pd, pd], out_specs=pd,
                          out_shape=jax.ShapeDtypeStruct((cfg.S, cfg.D), BF16),
                          compiler_params=_cp("parallel", "parallel"), name="merge_fwd")(proj, proj, pa, pdn)


def _merge_bwd(proj, pa, pdn, dy, cfg):
    grid, ga, gd, pd = _merge_specs(cfg)

    def body(ga_ref, gd_ref, pa_ref, pd_ref, dy_ref, dpa_ref, dpd_ref, dga_ref, dgd_ref):
        dyv = dy_ref[...]
        sa, sd = _sigmoid(ga_ref[...]), _sigmoid(gd_ref[...])
        dpa_ref[...] = (dyv * sa).astype(BF16)
        dpd_ref[...] = (dyv * sd).astype(BF16)
        dga_ref[...] = (dyv * pa_ref[...] * sa * (1.0 - sa)).astype(BF16)
        dgd_ref[...] = (dyv * pd_ref[...] * sd * (1.0 - sd)).astype(BF16)

    out = jax.ShapeDtypeStruct((cfg.S, cfg.D), BF16)
    return pl.pallas_call(body, grid=grid, in_specs=[ga, gd, pd, pd, pd], out_specs=(pd,) * 4,
                          out_shape=(out,) * 4, compiler_params=_cp("parallel", "parallel"),
                          name="merge_bwd")(proj, proj, pa, pdn, dy)


def _rope(x, cos, sin):
    lane = lax.broadcasted_iota(jnp.int32, x.shape, 1)
    swapped = jnp.where(lane < ROT_HALF, pltpu.roll(x, LANES - ROT_HALF, 1), pltpu.roll(x, ROT_HALF, 1))
    return x * cos + swapped * sin


def _rope_t(dx, cos, sin):
    t = dx * sin
    lane = lax.broadcasted_iota(jnp.int32, dx.shape, 1)
    swapped = jnp.where(lane < ROT_HALF, pltpu.roll(t, LANES - ROT_HALF, 1), pltpu.roll(t, ROT_HALF, 1))
    return dx * cos + swapped


def _band_mask(i, g):
    r = lax.broadcasted_iota(jnp.int32, (g * ATTN_BLOCK, 2 * ATTN_BLOCK), 0) % ATTN_BLOCK
    c = lax.broadcasted_iota(jnp.int32, (g * ATTN_BLOCK, 2 * ATTN_BLOCK), 1)
    dist = r + ATTN_BLOCK - c
    return (dist >= 0) & (dist < ATTN_BLOCK) & ((c >= ATTN_BLOCK) | (i > 0))


def _stack_heads(x, hb):
    return jnp.concatenate([x[:, i * HEAD:(i + 1) * HEAD] for i in range(hb)], axis=0)


def _dot(a, b, ca, cb):
    return lax.dot_general(a.astype(BF16), b.astype(BF16), (((ca,), (cb,)), ((), ())), preferred_element_type=F32)


def _attn_probs(qg, k_cat, sink, valid):
    logits = jnp.where(valid, _dot(qg, k_cat, 1, 1) * (HEAD ** -0.5), -1e30)
    m = jnp.maximum(jnp.max(logits, axis=-1, keepdims=True), sink)
    p = jnp.exp(logits - m)
    es = jnp.exp(sink - m)
    inv = 1.0 / (jnp.sum(p, axis=-1, keepdims=True) + es)
    return p * inv, es * inv


def _attn_specs(cfg, nb, clamp):
    gw = cfg.G * HEAD
    qi = (lambda kv, i: (jnp.minimum(i, nb - 1), kv)) if clamp else (lambda kv, i: (i, kv))
    ci = (lambda i: jnp.minimum(i, nb - 1)) if clamp else (lambda i: i)
    pi = lambda i: jnp.maximum(ci(i) - 1, 0)
    ko, vo = cfg.KA // HEAD, cfg.VA // HEAD
    blk = (ATTN_BLOCK, HEAD)
    specs = [
        pl.BlockSpec((ATTN_BLOCK, gw), qi),
        pl.BlockSpec(blk, lambda kv, i: (ci(i), ko + kv)), pl.BlockSpec(blk, lambda kv, i: (pi(i), ko + kv)),
        pl.BlockSpec(blk, lambda kv, i: (ci(i), vo + kv)), pl.BlockSpec(blk, lambda kv, i: (pi(i), vo + kv)),
        pl.BlockSpec(blk, lambda kv, i: (ci(i), 0)), pl.BlockSpec(blk, lambda kv, i: (pi(i), 0)),
        pl.BlockSpec(blk, lambda kv, i: (ci(i), 0)), pl.BlockSpec(blk, lambda kv, i: (pi(i), 0)),
        pl.BlockSpec((1, 8, LANES), lambda kv, i: (kv, 0, 0)),
    ]
    return specs, pl.BlockSpec((ATTN_BLOCK, gw), qi)


def _sink_rows(sinks, cfg):
    sk = jnp.broadcast_to(sinks.reshape(cfg.KV, cfg.G, 1), (cfg.KV, cfg.G, LANES))
    return jnp.pad(sk, ((0, 0), (0, 8 - cfg.G), (0, 0)))


def _attn_group(q_ref, kc_ref, kp_ref, vc_ref, vp_ref, cc_ref, cp_ref, sc_ref, sp_ref, sk_ref, i, g):
    cc, sc = cc_ref[...], sc_ref[...]
    k_cat = jnp.concatenate([_rope(kp_ref[...], cp_ref[...], sp_ref[...]), _rope(kc_ref[...], cc, sc)], axis=0).astype(BF16)
    v_cat = jnp.concatenate([vp_ref[...], vc_ref[...]], axis=0).astype(BF16)
    cos_g, sin_g = jnp.concatenate([cc] * g, axis=0), jnp.concatenate([sc] * g, axis=0)
    qr = _rope(_stack_heads(q_ref[...], g), cos_g, sin_g)
    sink = jnp.concatenate([jnp.broadcast_to(sk_ref[0, h:h + 1, 0:1], (ATTN_BLOCK, 1)) for h in range(g)], axis=0)
    p, ps = _attn_probs(qr, k_cat, sink, _band_mask(i, g))
    return qr, k_cat, v_cat, cos_g, sin_g, p, ps


def _attn_fwd(proj, cosf, sinf, sinks, cfg):
    nb, g = cfg.S // ATTN_BLOCK, cfg.G
    specs, qspec = _attn_specs(cfg, nb, False)

    def body(*refs):
        o_ref = refs[10]
        _, _, v_cat, _, _, p, _ = _attn_group(*refs[:10], pl.program_id(1), g)
        o = _dot(p, v_cat, 1, 0)
        for h in range(g):
            o_ref[:, h * HEAD:(h + 1) * HEAD] = o[h * ATTN_BLOCK:(h + 1) * ATTN_BLOCK].astype(o_ref.dtype)

    return pl.pallas_call(
        body, grid=(cfg.KV, nb), in_specs=specs, out_specs=qspec,
        out_shape=jax.ShapeDtypeStruct((cfg.S, cfg.HA * HEAD), BF16),
        compiler_params=_cp("parallel", "parallel"), name="attn_fwd",
    )(proj, proj, proj, proj, proj, cosf, cosf, sinf, sinf, _sink_rows(sinks, cfg))


def _attn_bwd(proj, do, cosf, sinf, sinks, cfg):
    nb, g = cfg.S // ATTN_BLOCK, cfg.G
    specs, qspec = _attn_specs(cfg, nb, True)
    kv_out = pl.BlockSpec((ATTN_BLOCK, HEAD), lambda kv, i: (jnp.maximum(i - 1, 0), kv))

    def body(*refs):
        cc_ref, cp_ref, sc_ref, sp_ref = refs[5:9]
        do_ref, dq_ref, dk_ref, dv_ref, dsk_ref, ck_ref, cv_ref = refs[10:]
        i = pl.program_id(1)

        @pl.when(i == 0)
        def _():
            ck_ref[...] = jnp.zeros_like(ck_ref)
            cv_ref[...] = jnp.zeros_like(cv_ref)
            dsk_ref[...] = jnp.zeros_like(dsk_ref)

        @pl.when(i < nb)
        def _():
            qr, k_cat, v_cat, cos_g, sin_g, p, ps = _attn_group(*refs[:10], i, g)
            dos = _stack_heads(do_ref[...], g)
            dp = _dot(dos, v_cat, 1, 1)
            delta = jnp.sum(p * dp, axis=-1, keepdims=True)
            ds = p * (dp - delta) * (HEAD ** -0.5)
            dq = _rope_t(_dot(ds, k_cat, 1, 0), cos_g, sin_g)
            sunk = ps * delta
            for h in range(g):
                rows = slice(h * ATTN_BLOCK, (h + 1) * ATTN_BLOCK)
                dq_ref[:, h * HEAD:(h + 1) * HEAD] = dq[rows].astype(dq_ref.dtype)
                dsk_ref[0, h:h + 1, :] += jnp.broadcast_to(-jnp.sum(sunk[rows], axis=0, keepdims=True), (1, LANES))
            dk_cat = _dot(ds, qr, 0, 0)
            dv_cat = _dot(p, dos, 0, 0)
            dk_ref[...] = (ck_ref[...] + _rope_t(dk_cat[:ATTN_BLOCK], cp_ref[...], sp_ref[...])).astype(dk_ref.dtype)
            dv_ref[...] = (cv_ref[...] + dv_cat[:ATTN_BLOCK]).astype(dv_ref.dtype)
            ck_ref[...] = _rope_t(dk_cat[ATTN_BLOCK:], cc_ref[...], sc_ref[...])
            cv_ref[...] = dv_cat[ATTN_BLOCK:]

        @pl.when(i == nb)
        def _():
            dk_ref[...] = ck_ref[...].astype(dk_ref.dtype)
            dv_ref[...] = cv_ref[...].astype(dv_ref.dtype)

    kvw = cfg.KV * HEAD
    dq, dk, dv, dsk = pl.pallas_call(
        body, grid=(cfg.KV, nb + 1), in_specs=specs + [qspec],
        out_specs=(qspec, kv_out, kv_out, pl.BlockSpec((1, 8, LANES), lambda kv, i: (kv, 0, 0))),
        out_shape=(jax.ShapeDtypeStruct((cfg.S, cfg.HA * HEAD), BF16), jax.ShapeDtypeStruct((cfg.S, kvw), BF16),
                   jax.ShapeDtypeStruct((cfg.S, kvw), BF16), jax.ShapeDtypeStruct((cfg.KV, 8, LANES), F32)),
        scratch_shapes=[pltpu.VMEM((ATTN_BLOCK, HEAD), F32), pltpu.VMEM((ATTN_BLOCK, HEAD), F32)],
        compiler_params=_cp("parallel", "arbitrary"), name="attn_bwd",
    )(proj, proj, proj, proj, proj, cosf, cosf, sinf, sinf, _sink_rows(sinks, cfg), do)
    return jnp.concatenate([dq, dk, dv], axis=1), dsk[:, :cfg.G, 0].reshape(cfg.HA)


@jax.custom_vjp
def _nn(a, b):
    return _dot(a, b, 1, 0)


@jax.custom_vjp
def _nt(a, b):
    return _dot(a, b, 1, 1)


@jax.custom_vjp
def _tn(a, b):
    return _dot(a, b, 0, 0)


_nn.defvjp(lambda a, b: (_nn(a, b), (a, b)), lambda r, g: (_nt(g, r[1]), _tn(r[0], g)))
_nt.defvjp(lambda a, b: (_nt(a, b), (a, b)), lambda r, g: (_nn(g, r[1]), _tn(g, r[0])))
_tn.defvjp(lambda a, b: (_tn(a, b), (a, b)), lambda r, g: (_nt(r[1], g), _nn(r[0], g)))


def _unit_lower_inverse(a):
    ii = lax.broadcasted_iota(jnp.int32, a.shape, 0)
    jj = lax.broadcasted_iota(jnp.int32, a.shape, 1)
    t = jnp.where(ii == jj, 1.0, 0.0) - jnp.where((ii == jj + 1) & (ii % 2 == 1), a, 0.0)
    b = 2
    while b < CHUNK:
        lower_left = (ii // (2 * b) == jj // (2 * b)) & ((ii // b) % 2 == 1) & ((jj // b) % 2 == 0)
        t = t - _dot(_dot(t, jnp.where(lower_left, a, 0.0), 1, 0), t, 1, 0)
        b *= 2
    return t


@jax.custom_vjp
def _tri_inv(a):
    return _unit_lower_inverse(a)


def _tri_inv_fwd(a):
    t = _unit_lower_inverse(a)
    return t, t


def _tri_inv_bwd(t, g):
    return (-_dot(t, _dot(g, t, 1, 1), 0, 0),)


_tri_inv.defvjp(_tri_inv_fwd, _tri_inv_bwd)


def _gdn_chunk(q, k, v, z, b_col, a_col, a_row, al_col, dt_col, al_row, dt_row, nw, s):
    rows = q.shape[0]
    hb = rows // CHUNK
    ii = lax.broadcasted_iota(jnp.int32, (rows, rows), 0)
    jj = lax.broadcasted_iota(jnp.int32, (rows, rows), 1)
    same = (ii // CHUNK) == (jj // CHUNK)
    tri, strict, upper = same & (ii >= jj), same & (ii > jj), same & (ii <= jj)
    beta = _sigmoid(b_col)
    g_col = -jnp.exp(al_col) * _softplus(a_col + dt_col)
    g_row = -jnp.exp(al_row) * _softplus(a_row + dt_row)
    gc_col = jnp.sum(jnp.where(tri, g_row, 0.0), axis=1, keepdims=True)
    gc_row = jnp.sum(jnp.where(upper, g_col, 0.0), axis=0, keepdims=True)
    last = jj == (ii // CHUNK) * CHUNK + (CHUNK - 1)
    gl_col = jnp.sum(jnp.where(last, gc_row, 0.0), axis=1, keepdims=True)
    decay = jnp.where(tri, jnp.exp(jnp.where(tri, gc_col - gc_row, 0.0)), 0.0)
    qn = q * lax.rsqrt(jnp.sum(q * q, axis=-1, keepdims=True) + EPS) * (HEAD ** -0.5)
    kn = k * lax.rsqrt(jnp.sum(k * k, axis=-1, keepdims=True) + EPS)
    t_inv = _tri_inv(jnp.where(strict, beta * _nt(kn, kn) * decay, 0.0))
    eg = jnp.exp(gc_col)
    u = _nn(t_inv, v * beta)
    w = _nn(t_inv, kn * (beta * eg))
    qk = jnp.where(tri, _nt(qn, kn) * decay, 0.0)
    ri = lax.broadcasted_iota(jnp.int32, (rows, hb * HEAD), 0)
    ci = lax.broadcasted_iota(jnp.int32, (rows, hb * HEAD), 1)
    own = (ri // CHUNK) == (ci // HEAD)

    def spread(x):
        return jnp.where(own, jnp.concatenate([x] * hb, axis=1), 0.0)

    v_new = u - _nn(spread(w), s)
    o = _nn(spread(qn * eg), s) + _nn(qk, v_new)
    si = lax.broadcasted_iota(jnp.int32, (hb * HEAD, rows), 0)
    sj = lax.broadcasted_iota(jnp.int32, (hb * HEAD, rows), 1)
    gl_s = jnp.sum(jnp.where(sj == (si // HEAD) * CHUNK + (CHUNK - 1), gc_row, 0.0), axis=1, keepdims=True)
    s1 = s * jnp.exp(gl_s) + _tn(spread(kn * jnp.exp(gl_col - gc_col)), v_new)
    od = o * lax.rsqrt(jnp.mean(o * o, axis=-1, keepdims=True) + EPS) * nw * (z * _sigmoid(z))
    return od, s1


def _gdn_heads_per_step(cfg):
    for hb in (GDN_HEADS_PER_STEP, 2, 1):
        if cfg.HD % hb == 0 and cfg.Z % (hb * HEAD) == 0:
            return hb


def _gdn_in_specs(cfg, nc, rev):
    hb = _gdn_heads_per_step(cfg)
    ng, rows = cfg.HD // hb, hb * CHUNK
    ci = (lambda n: nc - 1 - n) if rev else (lambda n: n)
    zo = cfg.Z // (hb * HEAD)
    blk = (CHUNK, hb * HEAD)
    col = pl.BlockSpec((1, 1, rows, 1), lambda g, n: (g, ci(n), 0, 0))
    row = pl.BlockSpec((1, 1, 1, rows), lambda g, n: (g, ci(n), 0, 0))
    hcol = pl.BlockSpec((1, rows, 1), lambda g, n: (g, 0, 0))
    hrow = pl.BlockSpec((1, 1, rows), lambda g, n: (g, 0, 0))
    return [
        pl.BlockSpec(blk, lambda g, n: (ci(n), g)), pl.BlockSpec(blk, lambda g, n: (ci(n), ng + g)),
        pl.BlockSpec(blk, lambda g, n: (ci(n), 2 * ng + g)), pl.BlockSpec(blk, lambda g, n: (ci(n), zo + g)),
        col, col, row, hcol, hcol, hrow, hrow, pl.BlockSpec((1, HEAD), lambda g, n: (0, 0)),
    ]


def _gdn_args(qkv, proj, pba, a_log, dt_bias, nw, cfg):
    hb, hd, s = _gdn_heads_per_step(cfg), cfg.HD, cfg.S
    ng, nc, rows = hd // hb, s // CHUNK, hb * CHUNK

    def grouped(x):
        return jnp.transpose(x.reshape(nc, CHUNK, ng, hb), (2, 0, 3, 1)).reshape(ng, nc, rows)

    b, a = grouped(pba[:, :hd]), grouped(pba[:, hd:2 * hd])
    per_row = lambda x: jnp.repeat(x.reshape(ng, hb), CHUNK, axis=1)
    al, dt = per_row(a_log), per_row(dt_bias)
    return (qkv, qkv, qkv, proj, b.reshape(ng, nc, rows, 1), a.reshape(ng, nc, rows, 1), a.reshape(ng, nc, 1, rows),
            al.reshape(ng, rows, 1), dt.reshape(ng, rows, 1), al.reshape(ng, 1, rows), dt.reshape(ng, 1, rows),
            nw.reshape(1, HEAD))


def _gdn_load(refs, hb):
    q, k, v, z, b, a, ar, alc, dtc, alr, dtr, nw = refs
    return (_stack_heads(q[...], hb), _stack_heads(k[...], hb), _stack_heads(v[...], hb), _stack_heads(z[...], hb),
            b[0, 0], a[0, 0], ar[0, 0], alc[0], dtc[0], alr[0], dtr[0], nw[...])


def _gdn_fwd(qkv, proj, pba, a_log, dt_bias, nw, cfg, side=None):
    nc, hb = cfg.S // CHUNK, _gdn_heads_per_step(cfg)
    ng = cfg.HD // hb
    grid = (ng, nc)

    def body(*refs):
        ins, (o_ref, st_ref), (s_ref,), parts = _side_split(refs, 12, 2, side)
        begin, end = _side_steps(side, parts, grid)
        begin()

        @pl.when(pl.program_id(1) == 0)
        def _():
            s_ref[...] = jnp.zeros_like(s_ref)

        s0 = s_ref[...]
        st_ref[0, 0] = s0
        od, s1 = _gdn_chunk(*_gdn_load(ins, hb), s0)
        for i in range(hb):
            o_ref[:, i * HEAD:(i + 1) * HEAD] = od[i * CHUNK:(i + 1) * CHUNK].astype(o_ref.dtype)
        s_ref[...] = s1
        end()

    s_in, s_out, s_shapes, s_sems, s_ops, _ = _side_call_args(side, 12, 2)
    return pl.pallas_call(
        body, grid=grid, in_specs=_gdn_in_specs(cfg, nc, False) + s_in,
        out_specs=(pl.BlockSpec((CHUNK, hb * HEAD), lambda g, n: (n, g)),
                   pl.BlockSpec((1, 1, hb * HEAD, HEAD), lambda g, n: (g, n, 0, 0)), *s_out),
        out_shape=(jax.ShapeDtypeStruct((cfg.S, cfg.HD * HEAD), BF16),
                   jax.ShapeDtypeStruct((ng, nc, hb * HEAD, HEAD), F32), *s_shapes),
        scratch_shapes=[pltpu.VMEM((hb * HEAD, HEAD), F32)] + s_sems,
        compiler_params=_cp("arbitrary" if side else "parallel", "arbitrary"),
        name="gdn_fwd_gather" if side else "gdn_fwd",
    )(*_gdn_args(qkv, proj, pba, a_log, dt_bias, nw, cfg), *s_ops)


def _gdn_bwd(qkv, proj, pba, a_log, dt_bias, nw, states, dod, cfg, side=None):
    nc, hd, s, hb = cfg.S // CHUNK, cfg.HD, cfg.S, _gdn_heads_per_step(cfg)
    ng, rows = hd // hb, hb * CHUNK
    rv = lambda n: nc - 1 - n
    col = pl.BlockSpec((1, 1, rows, 1), lambda g, n: (g, rv(n), 0, 0))
    row = pl.BlockSpec((1, 1, 1, rows), lambda g, n: (g, rv(n), 0, 0))
    hcol = pl.BlockSpec((1, rows, 1), lambda g, n: (g, 0, 0))
    hrow = pl.BlockSpec((1, 1, rows), lambda g, n: (g, 0, 0))

    grid = (ng, nc)

    def body(*refs):
        ins, outs, (ds_ref,), parts = _side_split(refs, 14, 10, side)
        begin, end = _side_steps(side, parts, grid)
        begin()
        st_ref, dod_ref = ins[12:14]
        dqkv_ref, dz_ref, db_ref, da_ref, dar_ref, dalc_ref, ddtc_ref, dalr_ref, ddtr_ref, dnw_ref = outs
        g, n = pl.program_id(0), pl.program_id(1)

        @pl.when(n == 0)
        def _():
            ds_ref[...] = jnp.zeros_like(ds_ref)
            for r in (dalc_ref, ddtc_ref, dalr_ref, ddtr_ref):
                r[...] = jnp.zeros_like(r)

        @pl.when((n == 0) & (g == 0))
        def _():
            dnw_ref[...] = jnp.zeros_like(dnw_ref)

        _, vjp = jax.vjp(_gdn_chunk, *_gdn_load(ins[:12], hb), st_ref[0, 0])
        dq, dk, dv, dz, db, da, dar, dalc, ddtc, dalr, ddtr, dnw, ds0 = vjp(
            (_stack_heads(dod_ref[...].astype(F32), hb), ds_ref[...]))
        for i in range(hb):
            sl, rs = slice(i * HEAD, (i + 1) * HEAD), slice(i * CHUNK, (i + 1) * CHUNK)
            dqkv_ref[0, :, sl], dqkv_ref[1, :, sl], dqkv_ref[2, :, sl] = dq[rs], dk[rs], dv[rs]
            dz_ref[:, sl] = dz[rs].astype(dz_ref.dtype)
        db_ref[0, 0], da_ref[0, 0], dar_ref[0, 0] = db, da, dar
        dalc_ref[0] += dalc
        ddtc_ref[0] += ddtc
        dalr_ref[0] += dalr
        ddtr_ref[0] += ddtr
        dnw_ref[...] += dnw
        ds_ref[...] = ds0
        end()

    blk = pl.BlockSpec((CHUNK, hb * HEAD), lambda g, n: (rv(n), g))
    f32 = lambda *sh: jax.ShapeDtypeStruct(sh, F32)
    s_in, s_out, s_shapes, s_sems, s_ops, _ = _side_call_args(side, 14, 10)
    outs = pl.pallas_call(
        body, grid=grid,
        in_specs=_gdn_in_specs(cfg, nc, True) + [pl.BlockSpec((1, 1, hb * HEAD, HEAD), lambda g, n: (g, rv(n), 0, 0)), blk] + s_in,
        out_specs=(pl.BlockSpec((3, CHUNK, hb * HEAD), lambda g, n: (0, rv(n), g)), blk, col, col, row,
                   hcol, hcol, hrow, hrow, pl.BlockSpec((1, HEAD), lambda g, n: (0, 0)), *s_out),
        out_shape=(f32(3, s, hd * HEAD), jax.ShapeDtypeStruct((s, hd * HEAD), BF16),
                   f32(ng, nc, rows, 1), f32(ng, nc, rows, 1), f32(ng, nc, 1, rows),
                   f32(ng, rows, 1), f32(ng, rows, 1), f32(ng, 1, rows), f32(ng, 1, rows), f32(1, HEAD), *s_shapes),
        scratch_shapes=[pltpu.VMEM((hb * HEAD, HEAD), F32)] + s_sems,
        compiler_params=_cp("arbitrary", "arbitrary"), name="gdn_bwd_scatter" if side else "gdn_bwd",
    )(*_gdn_args(qkv, proj, pba, a_log, dt_bias, nw, cfg), states, dod, *s_ops)
    dqkv, dz, db, da, dar, dalc, ddtc, dalr, ddtr, dnw = outs[:10]
    side_outs = list(outs[10:])

    def ungrouped(x):
        return jnp.transpose(x.reshape(ng, nc, hb, CHUNK), (1, 3, 0, 2)).reshape(s, hd)

    per_head = lambda c, r: (c.reshape(ng, hb, CHUNK) + r.reshape(ng, hb, CHUNK)).sum(axis=-1).reshape(hd)
    d_b = ungrouped(db.reshape(ng, nc, rows))
    d_a = ungrouped(da.reshape(ng, nc, rows) + dar.reshape(ng, nc, rows))
    dba = jnp.concatenate([d_b, d_a, jnp.zeros((s, LANES - 2 * hd), F32)], axis=1).astype(BF16)
    return dqkv, dz, dba, per_head(dalc, dalr), per_head(ddtc, ddtr), dnw.reshape(HEAD), side_outs


def _adamw(w, g, m, v, *, name):
    nl, r, c = w.shape
    gc = g.shape[2]
    tr = _tile(r, max(8, (1 << 18) // c // 8 * 8), 8) if r % 8 == 0 else r
    c1, c2 = 1.0 - ADAM_B1 ** ADAM_STEP, 1.0 - ADAM_B2 ** ADAM_STEP

    def body(w_ref, g_ref, m_ref, v_ref, go_ref, d_ref, nm_ref, nv_ref):
        gv = g_ref[:, :c]
        nm = ADAM_B1 * m_ref[...] + (1.0 - ADAM_B1) * gv
        nv = ADAM_B2 * v_ref[...] + (1.0 - ADAM_B2) * (gv * gv)
        go_ref[...] = gv
        d_ref[...] = -ADAM_LR * ((nm / c1) / (jnp.sqrt(nv / c2) + ADAM_EPS) + ADAM_WD * w_ref[...])
        nm_ref[...] = nm
        nv_ref[...] = nv

    spec = pl.BlockSpec((None, tr, c), lambda l, i: (l, i, 0))
    gspec = pl.BlockSpec((None, tr, gc), lambda l, i: (l, i, 0))
    out = jax.ShapeDtypeStruct((nl, r, c), F32)
    return pl.pallas_call(body, grid=(nl, r // tr), in_specs=[spec, gspec, spec, spec], out_specs=(spec,) * 4,
                          out_shape=(out,) * 4, compiler_params=_cp("parallel", "parallel"), name=name)(w, g, m, v)


def _sum_leading(a, *, name):
    n, r, c = a.shape
    tr = _tile(r, SUM_TR, 8)

    def body(a_ref, o_ref):
        acc = a_ref[0].astype(F32)
        for i in range(1, n):
            acc = acc + a_ref[i].astype(F32)
        o_ref[...] = acc

    return pl.pallas_call(body, grid=(r // tr,), in_specs=[pl.BlockSpec((n, tr, c), lambda i: (0, i, 0))],
                          out_specs=pl.BlockSpec((tr, c), lambda i: (i, 0)),
                          out_shape=jax.ShapeDtypeStruct((r, c), F32), compiler_params=_cp("parallel"), name=name)(a)


class _Mat:
    def __init__(self, kind, rows, cols):
        self.kind, self.rows, self.cols = kind, rows, cols
        self.full = (rows, 4 * cols) if kind == "col" else (4 * rows, cols)
        self.half = rows // 2

    def block(self, ref, j, c=None):
        r0 = (0 if self.kind == "col" else j * self.rows) + (0 if c is None else c * self.half)
        rows = pl.ds(_aligned(r0, 16), self.rows if c is None else self.half)
        return ref.at[rows, pl.ds(_aligned(j * self.cols, LANES), self.cols)] if self.kind == "col" else ref.at[rows, :]

    def own_half(self, ref, c):
        return ref.at[pl.ds(_aligned(c * self.half, 16), self.half), :]


def _aligned(v, unit):
    return v if isinstance(v, int) else pl.multiple_of(v, unit)


def _place():
    x, y, c = lax.axis_index("x"), lax.axis_index("y"), lax.axis_index("c")
    chips = [(1 - x, y), (x, 1 - y), (1 - x, 1 - y)]
    return x, y, c, chips


def _chip_index():
    return 2 * lax.axis_index("x") + lax.axis_index("y")


def _remote(src, dst, send_sems, recv_sems, k, to):
    return pltpu.make_async_remote_copy(src_ref=src, dst_ref=dst, send_sem=send_sems.at[k], recv_sem=recv_sems.at[k],
                                        device_id=to, device_id_type=MESH)


def _comm_call(body, ins, out_shapes, n_sems, name):
    return pl.pallas_call(
        body, out_shape=tuple(out_shapes), in_specs=[ANY] * len(ins), out_specs=tuple([ANY] * len(out_shapes)),
        scratch_shapes=[pltpu.SemaphoreType.DMA((n_sems,)), pltpu.SemaphoreType.DMA((n_sems,))],
        compiler_params=pltpu.CompilerParams(has_side_effects=True), name=name)(*ins)


def _gather_copies(own_refs, full_refs, send_sems, recv_sems, mats, base=0):
    x, y, c, chips = _place()
    me = 2 * x + y
    return [(_remote(m.own_half(own_refs[k], c), m.block(full_refs[k], me, c), send_sems, recv_sems, base + 3 * k + j, (cx, cy, c)),
             _remote(m.block(full_refs[k], 2 * cx + cy, c), m.block(full_refs[k], 2 * cx + cy, c), send_sems, recv_sems,
                     base + 3 * k + j, (cx, cy, c)))
            for k, m in enumerate(mats) for j, (cx, cy) in enumerate(chips)]


def _pass_copies(full_refs, send_sems, recv_sems, mats, base=0):
    x, y, c, chips = _place()
    sib = (x, y, 1 - c)
    return [(_remote(m.block(full_refs[k], 2 * cx + cy, c), m.block(full_refs[k], 2 * cx + cy, c), send_sems, recv_sems,
                     base + 3 * k + j, sib),
             _remote(m.block(full_refs[k], 2 * cx + cy, 1 - c), m.block(full_refs[k], 2 * cx + cy, 1 - c), send_sems,
                     recv_sems, base + 3 * k + j, sib))
            for k, m in enumerate(mats) for j, (cx, cy) in enumerate(chips)]


def _start_all(pairs):
    for cp, _ in pairs:
        cp.start()


def _finish_all(pairs):
    for _, landing in pairs:
        landing.wait_recv()
    for cp, _ in pairs:
        cp.wait_send()


def _gather_ici_side(own, mats):
    return _Side(own, [jax.ShapeDtypeStruct(m.full, a.dtype) for m, a in zip(mats, own)], 3 * len(mats),
                 lambda i, o, ss, rs: _start_all(_gather_copies(i, o, ss, rs, mats)),
                 lambda i, o, ss, rs: _finish_all(_gather_copies(i, o, ss, rs, mats)))


def _gather_pass_side(fulls, mats):
    return _Side(fulls, [jax.ShapeDtypeStruct(f.shape, f.dtype) for f in fulls], 3 * len(mats),
                 lambda i, o, ss, rs: _start_all(_pass_copies(o, ss, rs, mats)),
                 lambda i, o, ss, rs: _finish_all(_pass_copies(o, ss, rs, mats)), in_place=True)


def _place_own(fulls, own, mats):
    me = _chip_index().astype(jnp.int32).reshape(1)

    def place(full, blk, m):
        tr = _tile(m.rows, 512, 16)
        nr = m.rows // tr
        window = (lambda i, me: (i, me[0])) if m.kind == "col" else (lambda i, me: (me[0] * nr + i, 0))

        def body(me_ref, b_ref, f_ref, o_ref):
            o_ref[...] = b_ref[...]

        return pl.pallas_call(
            body,
            grid_spec=pltpu.PrefetchScalarGridSpec(
                num_scalar_prefetch=1, grid=(nr,),
                in_specs=[pl.BlockSpec((tr, m.cols), lambda i, me: (i, 0)), ANY],
                out_specs=pl.BlockSpec((tr, m.cols), window)),
            out_shape=jax.ShapeDtypeStruct(full.shape, full.dtype), input_output_aliases={2: 0},
            compiler_params=_cp("parallel"), name="place_own")(me, blk, full)

    return [place(o, a, m) for o, a, m in zip(fulls, own, mats)]


def _gather_weights(own, mats):
    n = len(mats)

    def body(*refs):
        own_refs, full_refs, (send_sems, recv_sems) = refs[:n], refs[n:2 * n], refs[2 * n:]
        first = _gather_copies(own_refs, full_refs, send_sems, recv_sems, mats)
        second = _pass_copies(full_refs, send_sems, recv_sems, mats, base=3 * n)
        _start_all(first)
        for (_, landed), (fwd, _) in zip(first, second):
            landed.wait_recv()
            fwd.start()
        for _, landing in second:
            landing.wait_recv()
        for cp, _ in first + second:
            cp.wait_send()

    outs = _comm_call(body, own, [jax.ShapeDtypeStruct(m.full, a.dtype) for m, a in zip(mats, own)], 6 * n,
                      "gather_weights")
    return _place_own(outs, own, mats)


def _swap_copies(g_refs, b_refs, send_sems, recv_sems, mats):
    x, y, c, _ = _place()
    cps = []
    for g, b, m in zip(g_refs, b_refs, mats):
        if m.kind == "col":
            cps.append(_remote(m.own_half(g, 1 - c), b, send_sems, recv_sems, len(cps), (x, y, 1 - c)))
        else:
            for j in range(4):
                cps.append(_remote(m.block(g, j, 1 - c), b.at[j], send_sems, recv_sems, len(cps), (x, y, 1 - c)))
    return [(cp, cp) for cp in cps]


def _swap_shapes(mats):
    return [jax.ShapeDtypeStruct((m.half, 4 * m.cols) if m.kind == "col" else (4, m.half, m.cols), F32) for m in mats]


def _swap_side(grads, mats):
    return _Side(grads, _swap_shapes(mats), sum(1 if m.kind == "col" else 4 for m in mats),
                 lambda i, o, ss, rs: _start_all(_swap_copies(i, o, ss, rs, mats)),
                 lambda i, o, ss, rs: _finish_all(_swap_copies(i, o, ss, rs, mats)))


def _swap_halves(grads, mats):
    def body(*refs):
        n = len(mats)
        pairs = _swap_copies(refs[:n], refs[n:2 * n], *refs[2 * n:], mats)
        _start_all(pairs)
        _finish_all(pairs)

    return _comm_call(body, grads, _swap_shapes(mats), sum(1 if m.kind == "col" else 4 for m in mats), "rs_swap_halves")


def _pair_sum(own, recv, m, cidx):
    tr = _tile(m.half, 256, 16)
    nh = m.half // tr
    if m.kind == "col":
        own_spec = pl.BlockSpec((tr, m.cols), lambda j, i, c: (c[0] * nh + i, j))
        recv_spec = pl.BlockSpec((tr, m.cols), lambda j, i, c: (i, j))
    else:
        own_spec = pl.BlockSpec((tr, m.cols), lambda j, i, c: ((2 * j + c[0]) * nh + i, 0))
        recv_spec = pl.BlockSpec((None, tr, m.cols), lambda j, i, c: (j, i, 0))

    def body(c_ref, a_ref, b_ref, o_ref):
        o_ref[...] = (a_ref[...] + b_ref[...]).astype(o_ref.dtype)

    return pl.pallas_call(
        body,
        grid_spec=pltpu.PrefetchScalarGridSpec(
            num_scalar_prefetch=1, grid=(4, nh), in_specs=[own_spec, recv_spec],
            out_specs=pl.BlockSpec((None, tr, m.cols), lambda j, i, c: (j, i, 0))),
        out_shape=jax.ShapeDtypeStruct((4, m.half, m.cols), BF16),
        compiler_params=_cp("parallel", "parallel"), name="rs_pair_sum")(cidx, own, recv)


def _scatter_copies(s_refs, b_refs, send_sems, recv_sems):
    x, y, c, chips = _place()
    me = 2 * x + y
    return [(_remote(s.at[2 * cx + cy], b.at[me], send_sems, recv_sems, 3 * k + j, (cx, cy, c)),
             _remote(b.at[2 * cx + cy], b.at[2 * cx + cy], send_sems, recv_sems, 3 * k + j, (cx, cy, c)))
            for k, (s, b) in enumerate(zip(s_refs, b_refs)) for j, (cx, cy) in enumerate(chips)]


def _scatter_side(sps):
    return _Side(sps, [jax.ShapeDtypeStruct(s.shape, s.dtype) for s in sps], 3 * len(sps),
                 lambda i, o, ss, rs: _start_all(_scatter_copies(i, o, ss, rs)),
                 lambda i, o, ss, rs: _finish_all(_scatter_copies(i, o, ss, rs)))


def _scatter_chips(sps):
    def body(*refs):
        n = len(sps)
        pairs = _scatter_copies(refs[:n], refs[n:2 * n], *refs[2 * n:])
        _start_all(pairs)
        _finish_all(pairs)

    return _comm_call(body, sps, [jax.ShapeDtypeStruct(s.shape, s.dtype) for s in sps], 3 * len(sps), "rs_scatter_chips")


def _sum_chips(got, sp, me, cidx):
    _, h, w = got.shape
    tr = _tile(h, 256, 16)
    nh = h // tr

    def body(me_ref, c_ref, g_ref, s_ref, o_ref):
        acc = None
        for s in range(4):
            term = jnp.where(me_ref[0] == s, s_ref[...], g_ref[s]).astype(F32)
            acc = term if acc is None else acc + term
        o_ref[...] = acc

    return pl.pallas_call(
        body,
        grid_spec=pltpu.PrefetchScalarGridSpec(
            num_scalar_prefetch=2, grid=(nh,),
            in_specs=[pl.BlockSpec((4, tr, w), lambda i, me, c: (0, i, 0)),
                      pl.BlockSpec((None, tr, w), lambda i, me, c: (me[0], i, 0))],
            out_specs=pl.BlockSpec((tr, w), lambda i, me, c: (c[0] * nh + i, 0))),
        out_shape=jax.ShapeDtypeStruct((2 * h, w), F32), compiler_params=_cp("parallel"), name="rs_sum_chips",
    )(me, cidx, got, sp)


def _share_halves(fs):
    n = len(fs)

    def body(*refs):
        f_refs, (send_sems, recv_sems) = refs[n:2 * n], refs[2 * n:]
        x, y, c, _ = _place()
        cps = []
        for k, f in enumerate(f_refs):
            h = f.shape[0] // 2
            mine, theirs = f.at[pl.ds(_aligned(c * h, 16), h), :], f.at[pl.ds(_aligned((1 - c) * h, 16), h), :]
            cps.append((_remote(mine, mine, send_sems, recv_sems, k, (x, y, 1 - c)),
                        _remote(theirs, theirs, send_sems, recv_sems, k, (x, y, 1 - c))))
        _start_all(cps)
        _finish_all(cps)

    return pl.pallas_call(
        body, out_shape=tuple(jax.ShapeDtypeStruct(f.shape, f.dtype) for f in fs), in_specs=[ANY] * n,
        out_specs=tuple([ANY] * n), input_output_aliases={k: k for k in range(n)},
        scratch_shapes=[pltpu.SemaphoreType.DMA((n,)), pltpu.SemaphoreType.DMA((n,))],
        compiler_params=pltpu.CompilerParams(has_side_effects=True), name="rs_share_halves")(*fs)


def _gather_all(p):
    r, w = p.shape
    rel = [(dx, dy, dc) for dx in (0, 1) for dy in (0, 1) for dc in (0, 1)][1:]

    def body(p_ref, g_ref, send_sems, recv_sems):
        x, y, c = lax.axis_index("x"), lax.axis_index("y"), lax.axis_index("c")
        me = 4 * x + 2 * y + c
        g_ref[me] = p_ref[...]
        peers = [(jnp.bitwise_xor(x, dx), jnp.bitwise_xor(y, dy), jnp.bitwise_xor(c, dc)) for dx, dy, dc in rel]
        cps = [_remote(p_ref, g_ref.at[me], send_sems, recv_sems, k, peer) for k, peer in enumerate(peers)]
        for cp in cps:
            cp.start()
        for k, (px, py, pc) in enumerate(peers):
            slot = g_ref.at[4 * px + 2 * py + pc]
            _remote(slot, slot, send_sems, recv_sems, k, (px, py, pc)).wait_recv()
        for cp in cps:
            cp.wait_send()

    return pl.pallas_call(
        body, out_shape=jax.ShapeDtypeStruct((8, r, w), p.dtype),
        in_specs=[pl.BlockSpec(memory_space=pltpu.VMEM)], out_specs=pl.BlockSpec(memory_space=pltpu.VMEM),
        scratch_shapes=[pltpu.SemaphoreType.DMA((7,)), pltpu.SemaphoreType.DMA((7,))],
        compiler_params=pltpu.CompilerParams(has_side_effects=True, vmem_limit_bytes=VMEM_LIMIT),
        name=("gather_all_%dx%d" % (r, w)))(p)


def _pack_small(arrs):
    flat = jnp.concatenate([a.reshape(-1) for a in arrs])
    rows = -(-flat.shape[0] // (SMALL_W * 8)) * 8
    return jnp.pad(flat, (0, rows * SMALL_W - flat.shape[0])).reshape(rows, SMALL_W)


def _unpack(buf, shapes):
    flat = buf.reshape(-1)
    out, off = [], 0
    for sh in shapes:
        n = math.prod(sh)
        out.append(flat[off:off + n].reshape(sh))
        off += n
    return out


def _chip_padded(pieces, width, padded):
    out, used = [], 0
    for p in pieces:
        at = 0
        while at < p.shape[1]:
            take = min(p.shape[1] - at, width - used)
            out.append(p[:, at:at + take])
            at, used = at + take, used + take
            if used == width:
                out.append(jnp.zeros((p.shape[0], padded - width), p.dtype))
                used = 0
    assert used == 0
    return jnp.concatenate(out, axis=1)


class _Cfg:
    def __init__(self, x, w_in, attn_sinks, dn_a_log, w_ffn_up):
        self.S, self.D = x.shape[1], x.shape[2]
        self.L = w_in.shape[0]
        self.HA, self.HD = attn_sinks.shape[1], dn_a_log.shape[1]
        self.IN = 4 * w_in.shape[2]
        kvw = self.IN - self.HA * HEAD - 4 * self.HD * HEAD - 2 * self.HD - 2 * self.D
        self.KV = kvw // (2 * HEAD)
        self.G = self.HA // self.KV
        self.FF = 2 * w_ffn_up.shape[2]
        self.KA = self.HA * HEAD
        self.VA = self.KA + self.KV * HEAD
        self.QD = self.VA + self.KV * HEAD
        self.Z = self.QD + 3 * self.HD * HEAD
        self.GA = self.Z + self.HD * HEAD
        self.GD = self.GA + self.D
        self.NB = self.GD + self.D
        assert self.NB + 2 * self.HD == self.IN and self.G <= 8 and 2 * self.HD <= LANES


def kernel(x, positions, norm_mix_pre, w_in, attn_sinks, dn_conv_w, dn_a_log, dn_dt_bias, dn_norm_w, w_branch_attn, w_branch_dn, w_out, norm_mix_post, norm_ffn_pre, w_ffn_up, ffn_conv_w, ffn_conv_b, w_ffn_down, norm_ffn_post, loss_target, m_norm_mix_pre, m_w_in, m_attn_sinks, m_dn_conv_w, m_dn_a_log, m_dn_dt_bias, m_dn_norm_w, m_w_branch_attn, m_w_branch_dn, m_w_out, m_norm_mix_post, m_norm_ffn_pre, m_w_ffn_up, m_ffn_conv_w, m_ffn_conv_b, m_w_ffn_down, m_norm_ffn_post, v_norm_mix_pre, v_w_in, v_attn_sinks, v_dn_conv_w, v_dn_a_log, v_dn_dt_bias, v_dn_norm_w, v_w_branch_attn, v_w_branch_dn, v_w_out, v_norm_mix_post, v_norm_ffn_pre, v_w_ffn_up, v_ffn_conv_w, v_ffn_conv_b, v_w_ffn_down, v_norm_ffn_post):
    cfg = _Cfg(x, w_in, attn_sinks, dn_a_log, w_ffn_up)
    S, D, L, HD = cfg.S, cfg.D, cfg.L, cfg.HD
    o1 = cfg.Z
    ins = w_in.shape[2]
    insp = -(-ins // SHARD_PAD) * SHARD_PAD
    mats = [_Mat("col", D, insp), _Mat("row", *w_branch_attn.shape[1:]), _Mat("row", *w_branch_dn.shape[1:]),
            _Mat("row", *w_out.shape[1:]), _Mat("col", *w_ffn_up.shape[1:]), _Mat("row", *w_ffn_down.shape[1:])]
    me = _chip_index()

    taps = _gather_all(_pack_small([dn_conv_w, ffn_conv_w]))
    tap_shapes = [dn_conv_w.shape, ffn_conv_w.shape]
    per_chip = [_unpack(taps[2 * j], tap_shapes) for j in range(4)]
    dnw_all = jnp.concatenate([per_chip[j][0] for j in range(4)], axis=2)
    fcw_all = jnp.concatenate([per_chip[j][1] for j in range(4)], axis=2)

    def own_blocks(l):
        return [jnp.pad(w_in[l].astype(BF16), ((0, 0), (0, insp - ins))), w_branch_attn[l].astype(BF16),
                w_branch_dn[l].astype(BF16), w_out[l].astype(BF16), w_ffn_up[l].astype(BF16), w_ffn_down[l].astype(BF16)]

    def layer_weights(l, fulls):
        win_sm, wbra, wbrd, wout, wup, wdown = fulls
        win = jnp.concatenate([win_sm[:, j * insp:j * insp + ins] for j in range(4)], axis=1)
        return dict(
            win_sm=win_sm, wbig=jnp.concatenate([win[:, :o1], win[:, o1 + 2 * HD:]], axis=1),
            wba=jnp.pad(win[:, o1:o1 + 2 * HD], ((0, 0), (0, LANES - 2 * HD))),
            wbra=wbra, wbrd=wbrd, wout=wout, wup=wup, wdown=wdown, dnw=dnw_all[l], fcw=fcw_all[l])

    weights = [layer_weights(0, _gather_weights(own_blocks(0), mats))] + [None] * (L - 1)

    inv_freq = ROPE_THETA ** (-jnp.arange(0, 2 * ROT_HALF, 2, dtype=F32) / (2 * ROT_HALF))
    ang = positions.reshape(S, 1).astype(F32) * inv_freq
    cosf = jnp.concatenate([jnp.cos(ang), jnp.cos(ang), jnp.ones((S, HEAD - 2 * ROT_HALF), F32)], axis=1)
    sinf = jnp.concatenate([-jnp.sin(ang), jnp.sin(ang), jnp.zeros((S, HEAD - 2 * ROT_HALF), F32)], axis=1)

    h = x.reshape(S, D)
    saved = []
    for l in range(L):
        wt = weights[l]
        t = dict(h0=h)
        t["xn"] = _rms_fwd(h, norm_mix_pre[l], out_dtype=BF16, name="norm_mix_pre")
        nxt = own_blocks(l + 1) if l + 1 < L else None
        if nxt:
            t["proj"], win_landed = _mm(t["xn"], wt["wbig"], name="proj_gather", side=_gather_ici_side(nxt[:1], mats[:1]))
        else:
            t["proj"] = _mm(t["xn"], wt["wbig"], name="proj")
        t["pba"] = _mm(t["xn"], wt["wba"], name="proj_ba")
        t["oa"] = _attn_fwd(t["proj"], cosf, sinf, attn_sinks[l], cfg)
        t["qkv"] = _dnconv_fwd(t["proj"], wt["dnw"], cfg)
        t["od"], t["st"], *landed = _gdn_fwd(t["qkv"], t["proj"], t["pba"], dn_a_log[l], dn_dt_bias[l], dn_norm_w[l], cfg,
                                             side=_gather_ici_side(nxt[1:], mats[1:]) if nxt else None)
        t["pa"] = _mm(t["oa"], wt["wbra"], name="branch_attn")
        t["pd"] = _mm(t["od"], wt["wbrd"], name="branch_dn")
        t["y"] = _merge_fwd(t["proj"], t["pa"], t["pd"], cfg)
        t["mix"] = _mm(t["y"], wt["wout"], name="mix_out")
        t["h1"] = _rms_fwd(t["mix"], norm_mix_post[l], res=h, out_dtype=F32, name="norm_mix_post")
        t["xf"] = _rms_fwd(t["h1"], norm_ffn_pre[l], out_dtype=BF16, name="norm_ffn_pre")
        if nxt:
            t["up"], *landed = _mm(t["xf"], wt["wup"], name="ffn_up_pass",
                                   side=_gather_pass_side([win_landed] + landed, mats))
            weights[l + 1] = layer_weights(l + 1, _place_own(landed, nxt, mats))
        else:
            t["up"] = _mm(t["xf"], wt["wup"], name="ffn_up")
        t["act"] = _glu_fwd(t["up"], wt["fcw"], ffn_conv_b[l], cfg)
        t["f"] = _mm(t["act"], wt["wdown"], name="ffn_down")
        h = _rms_fwd(t["f"], norm_ffn_post[l], res=t["h1"], out_dtype=F32, name="norm_ffn_post")
        saved.append(t)

    loss_local, dh = _loss_head(h, loss_target.reshape(S, D))
    loss = lax.psum(loss_local, ("x", "y", "c"))

    cidx = lax.axis_index("c").astype(jnp.int32).reshape(1)
    midx = me.astype(jnp.int32).reshape(1)
    big_grads, small = [None] * L, [None] * L
    reduce_blocks = lambda got, pair: _share_halves([_sum_chips(a, b, midx, cidx) for a, b in zip(got, pair)])
    pair_sums = lambda full, theirs: [_pair_sum(a, b, m, cidx) for a, b, m in zip(full, theirs, mats)]
    unswapped = waiting = None
    for l in reversed(range(L)):
        wt, t = weights[l], saved[l]
        df, g_nfpost = _rms_bwd(t["f"], norm_ffn_post[l], dh, out_dtype=BF16, name="norm_ffn_post_bwd")
        dact = _mm(df, wt["wdown"], tb=True, name="ffn_down_dx")
        g_wdown = _mm(t["act"], df, ta=True, name="ffn_down_dw")
        dup, g_fcw, g_fcb = _glu_bwd(t["up"], dact, wt["fcw"], ffn_conv_b[l], cfg)
        if unswapped:
            dxf, *theirs = _mm(dup, wt["wup"], tb=True, name="ffn_up_dx_swap", side=_swap_side(unswapped[1], mats))
            waiting = (unswapped[0], pair_sums(unswapped[1], theirs))
        else:
            dxf = _mm(dup, wt["wup"], tb=True, name="ffn_up_dx")
        g_wup = _mm(t["xf"], dup, ta=True, name="ffn_up_dw")
        dh1, g_nfp = _rms_bwd(t["h1"], norm_ffn_pre[l], dxf, add=dh, name="norm_ffn_pre_bwd")
        dmix, g_nmpost = _rms_bwd(t["mix"], norm_mix_post[l], dh1, out_dtype=BF16, name="norm_mix_post_bwd")
        dy = _mm(dmix, wt["wout"], tb=True, name="mix_out_dx")
        g_wout = _mm(t["y"], dmix, ta=True, name="mix_out_dw")
        dpa, dpd, dga, dgd = _merge_bwd(t["proj"], t["pa"], t["pd"], dy, cfg)
        doa = _mm(dpa, wt["wbra"], tb=True, out_dtype=BF16, name="branch_attn_dx")
        g_wbra = _mm(t["oa"], dpa, ta=True, name="branch_attn_dw")
        dod = _mm(dpd, wt["wbrd"], tb=True, out_dtype=BF16, name="branch_dn_dx")
        g_wbrd = _mm(t["od"], dpd, ta=True, name="branch_dn_dw")
        dqkv_c, dz, dba, g_alog, g_dt, g_dnnorm, got = _gdn_bwd(
            t["qkv"], t["proj"], t["pba"], dn_a_log[l], dn_dt_bias[l], dn_norm_w[l], t["st"], dod, cfg,
            side=_scatter_side(waiting[1]) if waiting else None)
        if waiting:
            big_grads[waiting[0]] = reduce_blocks(got, waiting[1])
            waiting = None
        dqkv_d, g_dnw = _dnconv_bwd(t["proj"], dqkv_c, wt["dnw"], cfg)
        dqkv_a, g_sinks = _attn_bwd(t["proj"], doa, cosf, sinf, attn_sinks[l], cfg)
        dproj = _chip_padded([dqkv_a, dqkv_d, dba[:, :2 * HD], dz, dga, dgd], ins, insp)
        dxn = _mm(dproj, wt["win_sm"], tb=True, name="proj_dx")
        g_win = _mm(t["xn"], dproj, ta=True, name="proj_dw")
        dh, g_nmp = _rms_bwd(t["h0"], norm_mix_pre[l], dxn, add=dh1, name="norm_mix_pre_bwd")

        unswapped = (l, [g_win, g_wbra, g_wbrd, g_wout, g_wup, g_wdown])
        small[l] = [g_nmp, g_sinks, g_alog, g_dt, g_dnnorm, g_nmpost, g_nfp, g_fcb, g_nfpost, g_dnw, g_fcw]

    last_pair = pair_sums(unswapped[1], _swap_halves(unswapped[1], mats))
    big_grads[unswapped[0]] = reduce_blocks(_scatter_chips(last_pair), last_pair)
    grad_x = dh.reshape(1, S, D)

    small_shapes = [a.shape for a in small[0]]
    summed = _sum_leading(_gather_all(_pack_small([a for l in range(L) for a in small[l]])), name="sum_devices")
    per_layer = _unpack(summed, small_shapes * L)
    red = [jnp.stack([per_layer[l * len(small_shapes) + i] for l in range(L)]) for i in range(len(small_shapes))]
    g_dn_conv = lax.dynamic_slice_in_dim(red[9], me * dn_conv_w.shape[2], dn_conv_w.shape[2], axis=2)
    g_ffn_conv = lax.dynamic_slice_in_dim(red[10], me * ffn_conv_w.shape[2], ffn_conv_w.shape[2], axis=2)
    stacked = lambda i: jnp.stack([big_grads[l][i] for l in range(L)])

    grads = dict(
        norm_mix_pre=red[0], w_in=stacked(0), attn_sinks=red[1], dn_conv_w=g_dn_conv, dn_a_log=red[2],
        dn_dt_bias=red[3], dn_norm_w=red[4], w_branch_attn=stacked(1), w_branch_dn=stacked(2), w_out=stacked(3),
        norm_mix_post=red[5], norm_ffn_pre=red[6], w_ffn_up=stacked(4), ffn_conv_w=g_ffn_conv, ffn_conv_b=red[7],
        w_ffn_down=stacked(5), norm_ffn_post=red[8])

    params = dict(
        norm_mix_pre=(norm_mix_pre, m_norm_mix_pre, v_norm_mix_pre), w_in=(w_in, m_w_in, v_w_in),
        attn_sinks=(attn_sinks, m_attn_sinks, v_attn_sinks), dn_conv_w=(dn_conv_w, m_dn_conv_w, v_dn_conv_w),
        dn_a_log=(dn_a_log, m_dn_a_log, v_dn_a_log), dn_dt_bias=(dn_dt_bias, m_dn_dt_bias, v_dn_dt_bias),
        dn_norm_w=(dn_norm_w, m_dn_norm_w, v_dn_norm_w),
        w_branch_attn=(w_branch_attn, m_w_branch_attn, v_w_branch_attn),
        w_branch_dn=(w_branch_dn, m_w_branch_dn, v_w_branch_dn), w_out=(w_out, m_w_out, v_w_out),
        norm_mix_post=(norm_mix_post, m_norm_mix_post, v_norm_mix_post),
        norm_ffn_pre=(norm_ffn_pre, m_norm_ffn_pre, v_norm_ffn_pre), w_ffn_up=(w_ffn_up, m_w_ffn_up, v_w_ffn_up),
        ffn_conv_w=(ffn_conv_w, m_ffn_conv_w, v_ffn_conv_w), ffn_conv_b=(ffn_conv_b, m_ffn_conv_b, v_ffn_conv_b),
        w_ffn_down=(w_ffn_down, m_w_ffn_down, v_w_ffn_down),
        norm_ffn_post=(norm_ffn_post, m_norm_ffn_post, v_norm_ffn_post))
    names = list(params)

    grad_out, delta, new_m, new_v = {}, {}, {}, {}
    small_names = [n for n in names if params[n][0].ndim == 2]
    for n in names:
        if n in small_names:
            continue
        w, m, v = params[n]
        grad_out[n], delta[n], new_m[n], new_v[n] = _adamw(w, grads[n], m, v, name="adamw_" + n)

    outs = _adamw(*[_pack_small([params[n][i] if i is not None else grads[n] for n in small_names])[None]
                    for i in (0, None, 1, 2)], name="adamw_small")
    shapes = [params[n][0].shape for n in small_names]
    for dst, o in zip((grad_out, delta, new_m, new_v), outs):
        for n, a in zip(small_names, _unpack(o, shapes)):
            dst[n] = a

    return (loss, grad_x, *[grad_out[n] for n in names], *[delta[n] for n in names],
            *[new_m[n] for n in names], *[new_v[n] for n in names])
```

```python
import functools
import math

import jax
import jax.numpy as jnp
from jax import lax
from jax.experimental import pallas as pl
from jax.experimental.pallas import tpu as pltpu

F32, BF16 = jnp.float32, jnp.bfloat16
EPS = 1e-6
HEAD = 128
ATTN_BLOCK = 128
ROT_HALF = 16
ROPE_THETA = 500000.0
CHUNK = 64
DN_CONV, FFN_CONV = 4, 3
GDN_HEADS_PER_STEP = 4
MM_TILES = (512, 1024, 3968)
HALO = 8
LANES = 128
SUM_TR = 512
SHARD_PAD = 1024
SMALL_W = 1024
ADAM_LR, ADAM_B1, ADAM_B2, ADAM_EPS, ADAM_WD, ADAM_STEP = 0.001, 0.9, 0.999, 1e-08, 0.01, 10
VMEM_LIMIT = 56 * 1024 * 1024
MESH = pl.DeviceIdType.MESH
ANY = pl.BlockSpec(memory_space=pl.ANY)


def _cp(*sem):
    return pltpu.CompilerParams(dimension_semantics=sem, vmem_limit_bytes=VMEM_LIMIT)


def _tile(n, pref, unit=LANES):
    if n <= pref:
        return n
    t = (pref // unit) * unit
    while t >= unit:
        if n % t == 0:
            return t
        t -= unit
    raise ValueError((n, pref, unit))


def _sigmoid(x):
    return 1.0 / (1.0 + jnp.exp(-x))


def _softplus(x):
    return jnp.maximum(x, 0.0) + jnp.log(1.0 + jnp.exp(-jnp.abs(x)))


def _dsilu(y):
    s = _sigmoid(y)
    return s * (1.0 + y * (1.0 - s))


class _Side:
    def __init__(self, ins, out_shapes, n_sems, start, finish, in_place=False):
        self.ins, self.out_shapes, self.n_sems = list(ins), list(out_shapes), n_sems
        self.start, self.finish, self.in_place = start, finish, in_place


def _side_split(refs, n_in, n_out, side):
    si, so = (len(side.ins), len(side.out_shapes)) if side else (0, 0)
    ins, sins = refs[:n_in], refs[n_in:n_in + si]
    outs, souts = refs[n_in + si:n_in + si + n_out], refs[n_in + si + n_out:n_in + si + n_out + so]
    rest = refs[n_in + si + n_out + so:]
    scratch, sems = (rest[:-2], rest[-2:]) if side else (rest, ())
    return ins, outs, scratch, (sins, souts, sems)


def _side_steps(side, parts, grid):
    if side is None:
        return (lambda: None), (lambda: None)
    sins, souts, sems = parts
    ids = [pl.program_id(a) for a in range(len(grid))]
    first = functools.reduce(jnp.logical_and, [i == 0 for i in ids])
    last = functools.reduce(jnp.logical_and, [i == n - 1 for i, n in zip(ids, grid)])

    def begin():
        @pl.when(first)
        def _():
            side.start(sins, souts, *sems)

    def end():
        @pl.when(last)
        def _():
            side.finish(sins, souts, *sems)

    return begin, end


def _side_call_args(side, n_in, n_out):
    if side is None:
        return [], [], [], [], [], {}
    sems = [pltpu.SemaphoreType.DMA((side.n_sems,)), pltpu.SemaphoreType.DMA((side.n_sems,))]
    aliases = {n_in + k: n_out + k for k in range(len(side.ins))} if side.in_place else {}
    return [ANY] * len(side.ins), [ANY] * len(side.out_shapes), side.out_shapes, sems, side.ins, aliases


def _mm(a, b, *, ta=False, tb=False, add=None, out_dtype=F32, name, tiles=MM_TILES, side=None):
    if ta:
        kdim, m = a.shape
    else:
        m, kdim = a.shape
    if tb:
        n, k2 = b.shape
    else:
        k2, n = b.shape
    assert kdim == k2, (a.shape, b.shape, ta, tb)
    tm, tn, tk = _tile(m, tiles[0]), _tile(n, tiles[1]), _tile(kdim, tiles[2])
    nk = kdim // tk
    grid = (m // tm, n // tn, nk)
    dims = (((0 if ta else 1,), (1 if tb else 0,)), ((), ()))
    n_in = 2 if add is None else 3

    def body(*refs):
        ins, (o_ref,), (acc_ref,), parts = _side_split(refs, n_in, 1, side)
        begin, end = _side_steps(side, parts, grid)
        begin()
        a_ref, b_ref = ins[:2]
        k = pl.program_id(2)

        @pl.when(k == 0)
        def _():
            acc_ref[...] = jnp.zeros_like(acc_ref)

        acc_ref[...] += lax.dot_general(a_ref[...].astype(BF16), b_ref[...].astype(BF16), dims,
                                        preferred_element_type=F32)

        @pl.when(k == nk - 1)
        def _():
            r = acc_ref[...]
            if add is not None:
                r = r + ins[2][...]
            o_ref[...] = r.astype(o_ref.dtype)

        end()

    a_spec = pl.BlockSpec((tk, tm), lambda i, j, k: (k, i)) if ta else pl.BlockSpec((tm, tk), lambda i, j, k: (i, k))
    b_spec = pl.BlockSpec((tn, tk), lambda i, j, k: (j, k)) if tb else pl.BlockSpec((tk, tn), lambda i, j, k: (k, j))
    o_spec = pl.BlockSpec((tm, tn), lambda i, j, k: (i, j))
    ins, specs = [a, b], [a_spec, b_spec]
    if add is not None:
        ins.append(add)
        specs.append(o_spec)
    s_in, s_out, s_shapes, s_sems, s_ops, aliases = _side_call_args(side, n_in, 1)
    sem = ("arbitrary",) * 3 if side else ("parallel", "parallel", "arbitrary")
    outs = pl.pallas_call(
        body, grid=grid, in_specs=specs + s_in, out_specs=(o_spec, *s_out),
        out_shape=(jax.ShapeDtypeStruct((m, n), out_dtype), *s_shapes),
        scratch_shapes=[pltpu.VMEM((tm, tn), F32)] + s_sems, input_output_aliases=aliases,
        compiler_params=_cp(*sem), name=name,
    )(*ins, *s_ops)
    return outs if side else outs[0]


def _rms_fwd(x, w, res=None, *, out_dtype, name):
    s, d = x.shape
    tr = _tile(s, 256, 8)

    def body(*refs):
        if res is None:
            x_ref, w_ref, o_ref = refs
        else:
            x_ref, w_ref, r_ref, o_ref = refs
        xv = x_ref[...]
        y = xv * lax.rsqrt(jnp.mean(xv * xv, axis=-1, keepdims=True) + EPS) * w_ref[...]
        if res is not None:
            y = y + r_ref[...]
        o_ref[...] = y.astype(o_ref.dtype)

    row = pl.BlockSpec((tr, d), lambda i: (i, 0))
    par = pl.BlockSpec((1, d), lambda i: (0, 0))
    ins, specs = [x, w.reshape(1, d)], [row, par]
    if res is not None:
        ins.append(res)
        specs.append(row)
    return pl.pallas_call(body, grid=(s // tr,), in_specs=specs, out_specs=row,
                          out_shape=jax.ShapeDtypeStruct((s, d), out_dtype),
                          compiler_params=_cp("parallel"), name=name)(*ins)


def _rms_bwd(x, w, dy, add=None, *, out_dtype=F32, name):
    s, d = x.shape
    tr = _tile(s, 256, 8)

    def body(*refs):
        if add is None:
            x_ref, w_ref, dy_ref, dx_ref, dw_ref = refs
        else:
            x_ref, w_ref, dy_ref, add_ref, dx_ref, dw_ref = refs
        xv = x_ref[...]
        dyv = dy_ref[...].astype(F32)
        r = lax.rsqrt(jnp.mean(xv * xv, axis=-1, keepdims=True) + EPS)
        xh = xv * r
        dyw = dyv * w_ref[...]
        dx = r * (dyw - xh * jnp.mean(dyw * xh, axis=-1, keepdims=True))
        if add is not None:
            dx = dx + add_ref[...]
        dx_ref[...] = dx.astype(dx_ref.dtype)

        @pl.when(pl.program_id(0) == 0)
        def _():
            dw_ref[...] = jnp.zeros_like(dw_ref)

        dw_ref[...] += jnp.sum(dyv * xh, axis=0, keepdims=True)

    row = pl.BlockSpec((tr, d), lambda i: (i, 0))
    par = pl.BlockSpec((1, d), lambda i: (0, 0))
    ins, specs = [x, w.reshape(1, d), dy], [row, par, row]
    if add is not None:
        ins.append(add)
        specs.append(row)
    dx, dw = pl.pallas_call(
        body, grid=(s // tr,), in_specs=specs, out_specs=(row, par),
        out_shape=(jax.ShapeDtypeStruct((s, d), out_dtype), jax.ShapeDtypeStruct((1, d), F32)),
        compiler_params=_cp("arbitrary"), name=name)(*ins)
    return dx, dw.reshape(d)


def _loss_head(h, tgt):
    s, d = h.shape
    tr = _tile(s, 256, 8)

    def body(h_ref, t_ref, dh_ref, l_ref):
        e = h_ref[...] - t_ref[...]
        dh_ref[...] = e * (1.0 / d)

        @pl.when(pl.program_id(0) == 0)
        def _():
            l_ref[...] = jnp.zeros_like(l_ref)

        l_ref[...] += 0.5 * jnp.sum(jnp.mean(e * e, axis=-1, keepdims=True), axis=0, keepdims=True)

    row = pl.BlockSpec((tr, d), lambda i: (i, 0))
    dh, l = pl.pallas_call(
        body, grid=(s // tr,), in_specs=[row, row], out_specs=(row, pl.BlockSpec((1, 1), lambda i: (0, 0))),
        out_shape=(jax.ShapeDtypeStruct((s, d), F32), jax.ShapeDtypeStruct((1, 1), F32)),
        compiler_params=_cp("arbitrary"), name="loss_head")(h, tgt)
    return l[0, 0], dh


def _conv_rows(xe, w, width):
    y = None
    for j in range(width):
        s = width - 1 - j
        term = (pltpu.roll(xe, s, 0) if s else xe)[HALO:] * w[j:j + 1, :]
        y = term if y is None else y + term
    return y


def _conv_rows_t(dy_ext, w, width, tr):
    n = dy_ext.shape[0]
    dx = None
    for j in range(width):
        s = width - 1 - j
        term = (pltpu.roll(dy_ext, n - s, 0) if s else dy_ext)[:tr] * w[j:j + 1, :]
        dx = term if dx is None else dx + term
    return dx


def _conv_dw(dy, xe, width):
    rows = []
    for j in range(width):
        s = width - 1 - j
        rows.append(jnp.sum(dy * (pltpu.roll(xe, s, 0) if s else xe)[HALO:], axis=0, keepdims=True))
    return jnp.concatenate(rows, axis=0)


def _halo_specs(tr, tc, coff, nrow_blocks):
    per = tr // HALO
    last = nrow_blocks * per - 1
    cur = pl.BlockSpec((tr, tc), lambda j, r: (r, coff + j))
    prev = pl.BlockSpec((HALO, tc), lambda j, r: (jnp.maximum(r * per - 1, 0), coff + j))
    nxt = pl.BlockSpec((HALO, tc), lambda j, r: (jnp.minimum((r + 1) * per, last), coff + j))
    return cur, prev, nxt


def _dnconv_fwd(proj, w, cfg):
    s, width = cfg.S, 3 * cfg.HD * HEAD
    tc = _tile(math.gcd(cfg.QD, width), 512)
    tr = _tile(s, 512, 8)
    nr = s // tr
    cur, prev, _ = _halo_specs(tr, tc, cfg.QD // tc, nr)

    def body(x_ref, xp_ref, w_ref, o_ref):
        hist = jnp.where(pl.program_id(1) > 0, xp_ref[...], 0.0)
        y = _conv_rows(jnp.concatenate([hist, x_ref[...]], axis=0), w_ref[...], DN_CONV)
        o_ref[...] = y * _sigmoid(y)

    return pl.pallas_call(
        body, grid=(width // tc, nr),
        in_specs=[cur, prev, pl.BlockSpec((DN_CONV, tc), lambda j, r: (0, j))],
        out_specs=pl.BlockSpec((tr, tc), lambda j, r: (r, j)),
        out_shape=jax.ShapeDtypeStruct((s, width), F32),
        compiler_params=_cp("parallel", "parallel"), name="dnconv_fwd")(proj, proj, w)


def _dnconv_bwd(proj, du3, w, cfg):
    s, hw = cfg.S, cfg.HD * HEAD
    width = 3 * hw
    tc = _tile(math.gcd(cfg.QD, hw), 512)
    tr = _tile(s, 512, 8)
    nr = s // tr
    per, last = tr // HALO, s // HALO - 1
    cur, prev, nxt = _halo_specs(tr, tc, cfg.QD // tc, nr)
    pc = hw // tc
    du_cur = pl.BlockSpec((1, tr, tc), lambda j, r: (j // pc, r, j % pc))
    du_nxt = pl.BlockSpec((1, HALO, tc), lambda j, r: (j // pc, jnp.minimum((r + 1) * per, last), j % pc))

    def body(x_ref, xp_ref, xn_ref, du_ref, dun_ref, w_ref, dx_ref, dw_ref):
        r = pl.program_id(1)
        wv = w_ref[...]
        hist = jnp.where(r > 0, xp_ref[...], 0.0)
        xe = jnp.concatenate([hist, x_ref[...]], axis=0)
        y_ext = _conv_rows(jnp.concatenate([xe, xn_ref[...]], axis=0), wv, DN_CONV)
        du_ext = jnp.concatenate([du_ref[0], jnp.where(r < nr - 1, dun_ref[0], 0.0)], axis=0)
        dy_ext = du_ext * _dsilu(y_ext)
        dx_ref[...] = _conv_rows_t(dy_ext, wv, DN_CONV, tr).astype(dx_ref.dtype)

        @pl.when(r == 0)
        def _():
            dw_ref[...] = jnp.zeros_like(dw_ref)

        dw_ref[...] += _conv_dw(dy_ext[:tr], xe, DN_CONV)

    wspec = pl.BlockSpec((DN_CONV, tc), lambda j, r: (0, j))
    return pl.pallas_call(
        body, grid=(width // tc, nr),
        in_specs=[cur, prev, nxt, du_cur, du_nxt, wspec],
        out_specs=(pl.BlockSpec((tr, tc), lambda j, r: (r, j)), wspec),
        out_shape=(jax.ShapeDtypeStruct((s, width), BF16), jax.ShapeDtypeStruct((DN_CONV, width), F32)),
        compiler_params=_cp("parallel", "arbitrary"), name="dnconv_bwd")(proj, proj, proj, du3, du3, w)


def _glu_fwd(up, w, b, cfg):
    s, ff = cfg.S, cfg.FF
    tc = _tile(ff, 512)
    tr = _tile(s, 512, 8)
    nr, nc = s // tr, ff // tc
    g_cur, g_prev, _ = _halo_specs(tr, tc, 0, nr)
    v_cur, v_prev, _ = _halo_specs(tr, tc, nc, nr)

    def wspec(rows, off):
        return pl.BlockSpec((rows, tc), lambda j, r: (0, off + j))

    def body(g_ref, gp_ref, v_ref, vp_ref, wg_ref, wv_ref, bg_ref, bv_ref, o_ref):
        first = pl.program_id(1) > 0
        ug = _conv_rows(jnp.concatenate([jnp.where(first, gp_ref[...], 0.0), g_ref[...]], axis=0), wg_ref[...], FFN_CONV) + bg_ref[...]
        uv = _conv_rows(jnp.concatenate([jnp.where(first, vp_ref[...], 0.0), v_ref[...]], axis=0), wv_ref[...], FFN_CONV) + bv_ref[...]
        o_ref[...] = (ug * _sigmoid(ug) * uv).astype(o_ref.dtype)

    b2 = b.reshape(1, 2 * ff)
    return pl.pallas_call(
        body, grid=(nc, nr),
        in_specs=[g_cur, g_prev, v_cur, v_prev, wspec(FFN_CONV, 0), wspec(FFN_CONV, nc), wspec(1, 0), wspec(1, nc)],
        out_specs=pl.BlockSpec((tr, tc), lambda j, r: (r, j)),
        out_shape=jax.ShapeDtypeStruct((s, ff), BF16),
        compiler_params=_cp("parallel", "parallel"), name="glu_fwd")(up, up, up, up, w, w, b2, b2)


def _glu_bwd(up, dact, w, b, cfg):
    s, ff = cfg.S, cfg.FF
    tc = _tile(ff, 512)
    tr = _tile(s, 512, 8)
    nr, nc = s // tr, ff // tc
    g_cur, g_prev, g_nxt = _halo_specs(tr, tc, 0, nr)
    v_cur, v_prev, v_nxt = _halo_specs(tr, tc, nc, nr)
    d_cur, _, d_nxt = _halo_specs(tr, tc, 0, nr)

    def wspec(rows, off):
        return pl.BlockSpec((rows, tc), lambda j, r: (0, off + j))

    def body(g_ref, gp_ref, gn_ref, v_ref, vp_ref, vn_ref, d_ref, dn_ref, wg_ref, wv_ref, bg_ref, bv_ref,
             dup_ref, dw_ref, db_ref):
        r = pl.program_id(1)
        wg, wv = wg_ref[...], wv_ref[...]
        ge = jnp.concatenate([jnp.where(r > 0, gp_ref[...], 0.0), g_ref[...]], axis=0)
        ve = jnp.concatenate([jnp.where(r > 0, vp_ref[...], 0.0), v_ref[...]], axis=0)
        ug = _conv_rows(jnp.concatenate([ge, gn_ref[...]], axis=0), wg, FFN_CONV) + bg_ref[...]
        uv = _conv_rows(jnp.concatenate([ve, vn_ref[...]], axis=0), wv, FFN_CONV) + bv_ref[...]
        da = jnp.concatenate([d_ref[...].astype(F32), jnp.where(r < nr - 1, dn_ref[...].astype(F32), 0.0)], axis=0)
        sg = _sigmoid(ug)
        dug = da * uv * (sg * (1.0 + ug * (1.0 - sg)))
        duv = da * ug * sg
        dup_ref[0] = _conv_rows_t(dug, wg, FFN_CONV, tr).astype(dup_ref.dtype)
        dup_ref[1] = _conv_rows_t(duv, wv, FFN_CONV, tr).astype(dup_ref.dtype)

        @pl.when(r == 0)
        def _():
            dw_ref[...] = jnp.zeros_like(dw_ref)
            db_ref[...] = jnp.zeros_like(db_ref)

        dw_ref[0] += _conv_dw(dug[:tr], ge, FFN_CONV)
        dw_ref[1] += _conv_dw(duv[:tr], ve, FFN_CONV)
        db_ref[0] += jnp.sum(dug[:tr], axis=0, keepdims=True)
        db_ref[1] += jnp.sum(duv[:tr], axis=0, keepdims=True)

    b2 = b.reshape(1, 2 * ff)
    dup, dw, db = pl.pallas_call(
        body, grid=(nc, nr),
        in_specs=[g_cur, g_prev, g_nxt, v_cur, v_prev, v_nxt, d_cur, d_nxt,
                  wspec(FFN_CONV, 0), wspec(FFN_CONV, nc), wspec(1, 0), wspec(1, nc)],
        out_specs=(pl.BlockSpec((2, tr, tc), lambda j, r: (0, r, j)),
                   pl.BlockSpec((2, FFN_CONV, tc), lambda j, r: (0, 0, j)),
                   pl.BlockSpec((2, 1, tc), lambda j, r: (0, 0, j))),
        out_shape=(jax.ShapeDtypeStruct((2, s, ff), BF16), jax.ShapeDtypeStruct((2, FFN_CONV, ff), F32),
                   jax.ShapeDtypeStruct((2, 1, ff), F32)),
        compiler_params=_cp("parallel", "arbitrary"), name="glu_bwd",
    )(up, up, up, up, up, up, dact, dact, w, w, b2, b2)
    dup = jnp.concatenate([dup[0], dup[1]], axis=1)
    dw = jnp.transpose(dw, (1, 0, 2)).reshape(FFN_CONV, 2 * ff)
    return dup, dw, db.reshape(2 * ff)


def _merge_specs(cfg):
    tc = _tile(math.gcd(math.gcd(cfg.GA, cfg.GD), cfg.D), 512)
    tr = _tile(cfg.S, 512, 8)
    ga = pl.BlockSpec((tr, tc), lambda r, j: (r, cfg.GA // tc + j))
    gd = pl.BlockSpec((tr, tc), lambda r, j: (r, cfg.GD // tc + j))
    pd = pl.BlockSpec((tr, tc), lambda r, j: (r, j))
    return (cfg.S // tr, cfg.D // tc), ga, gd, pd


def _merge_fwd(proj, pa, pdn, cfg):
    grid, ga, gd, pd = _merge_specs(cfg)

    def body(ga_ref, gd_ref, pa_ref, pd_ref, y_ref):
        y_ref[...] = (_sigmoid(ga_ref[...]) * pa_ref[...] + _sigmoid(gd_ref[...]) * pd_ref[...]).astype(y_ref.dtype)

    return pl.pallas_call(body, grid=grid, in_specs=[ga, gd, pd, pd], out_specs=pd,
                          out_shape=jax.ShapeDtypeStruct((cfg.S, cfg.D), BF16),
                          compiler_params=_cp("parallel", "parallel"), name="merge_fwd")(proj, proj, pa, pdn)


def _merge_bwd(proj, pa, pdn, dy, cfg):
    grid, ga, gd, pd = _merge_specs(cfg)

    def body(ga_ref, gd_ref, pa_ref, pd_ref, dy_ref, dpa_ref, dpd_ref, dga_ref, dgd_ref):
        dyv = dy_ref[...]
        sa, sd = _sigmoid(ga_ref[...]), _sigmoid(gd_ref[...])
        dpa_ref[...] = (dyv * sa).astype(BF16)
        dpd_ref[...] = (dyv * sd).astype(BF16)
        dga_ref[...] = (dyv * pa_ref[...] * sa * (1.0 - sa)).astype(BF16)
        dgd_ref[...] = (dyv * pd_ref[...] * sd * (1.0 - sd)).astype(BF16)

    out = jax.ShapeDtypeStruct((cfg.S, cfg.D), BF16)
    return pl.pallas_call(body, grid=grid, in_specs=[ga, gd, pd, pd, pd], out_specs=(pd,) * 4,
                          out_shape=(out,) * 4, compiler_params=_cp("parallel", "parallel"),
                          name="merge_bwd")(proj, proj, pa, pdn, dy)


def _rope(x, cos, sin):
    lane = lax.broadcasted_iota(jnp.int32, x.shape, 1)
    swapped = jnp.where(lane < ROT_HALF, pltpu.roll(x, LANES - ROT_HALF, 1), pltpu.roll(x, ROT_HALF, 1))
    return x * cos + swapped * sin


def _rope_t(dx, cos, sin):
    t = dx * sin
    lane = lax.broadcasted_iota(jnp.int32, dx.shape, 1)
    swapped = jnp.where(lane < ROT_HALF, pltpu.roll(t, LANES - ROT_HALF, 1), pltpu.roll(t, ROT_HALF, 1))
    return dx * cos + swapped


def _band_mask(i, g):
    r = lax.broadcasted_iota(jnp.int32, (g * ATTN_BLOCK, 2 * ATTN_BLOCK), 0) % ATTN_BLOCK
    c = lax.broadcasted_iota(jnp.int32, (g * ATTN_BLOCK, 2 * ATTN_BLOCK), 1)
    dist = r + ATTN_BLOCK - c
    return (dist >= 0) & (dist < ATTN_BLOCK) & ((c >= ATTN_BLOCK) | (i > 0))


def _stack_heads(x, hb):
    return jnp.concatenate([x[:, i * HEAD:(i + 1) * HEAD] for i in range(hb)], axis=0)


def _dot(a, b, ca, cb):
    return lax.dot_general(a.astype(BF16), b.astype(BF16), (((ca,), (cb,)), ((), ())), preferred_element_type=F32)


def _attn_probs(qg, k_cat, sink, valid):
    logits = jnp.where(valid, _dot(qg, k_cat, 1, 1) * (HEAD ** -0.5), -1e30)
    m = jnp.maximum(jnp.max(logits, axis=-1, keepdims=True), sink)
    p = jnp.exp(logits - m)
    es = jnp.exp(sink - m)
    inv = 1.0 / (jnp.sum(p, axis=-1, keepdims=True) + es)
    return p * inv, es * inv


def _attn_specs(cfg, nb, clamp):
    gw = cfg.G * HEAD
    qi = (lambda kv, i: (jnp.minimum(i, nb - 1), kv)) if clamp else (lambda kv, i: (i, kv))
    ci = (lambda i: jnp.minimum(i, nb - 1)) if clamp else (lambda i: i)
    pi = lambda i: jnp.maximum(ci(i) - 1, 0)
    ko, vo = cfg.KA // HEAD, cfg.VA // HEAD
    blk = (ATTN_BLOCK, HEAD)
    specs = [
        pl.BlockSpec((ATTN_BLOCK, gw), qi),
        pl.BlockSpec(blk, lambda kv, i: (ci(i), ko + kv)), pl.BlockSpec(blk, lambda kv, i: (pi(i), ko + kv)),
        pl.BlockSpec(blk, lambda kv, i: (ci(i), vo + kv)), pl.BlockSpec(blk, lambda kv, i: (pi(i), vo + kv)),
        pl.BlockSpec(blk, lambda kv, i: (ci(i), 0)), pl.BlockSpec(blk, lambda kv, i: (pi(i), 0)),
        pl.BlockSpec(blk, lambda kv, i: (ci(i), 0)), pl.BlockSpec(blk, lambda kv, i: (pi(i), 0)),
        pl.BlockSpec((1, 8, LANES), lambda kv, i: (kv, 0, 0)),
    ]
    return specs, pl.BlockSpec((ATTN_BLOCK, gw), qi)


def _sink_rows(sinks, cfg):
    sk = jnp.broadcast_to(sinks.reshape(cfg.KV, cfg.G, 1), (cfg.KV, cfg.G, LANES))
    return jnp.pad(sk, ((0, 0), (0, 8 - cfg.G), (0, 0)))


def _attn_group(q_ref, kc_ref, kp_ref, vc_ref, vp_ref, cc_ref, cp_ref, sc_ref, sp_ref, sk_ref, i, g):
    cc, sc = cc_ref[...], sc_ref[...]
    k_cat = jnp.concatenate([_rope(kp_ref[...], cp_ref[...], sp_ref[...]), _rope(kc_ref[...], cc, sc)], axis=0).astype(BF16)
    v_cat = jnp.concatenate([vp_ref[...], vc_ref[...]], axis=0).astype(BF16)
    cos_g, sin_g = jnp.concatenate([cc] * g, axis=0), jnp.concatenate([sc] * g, axis=0)
    qr = _rope(_stack_heads(q_ref[...], g), cos_g, sin_g)
    sink = jnp.concatenate([jnp.broadcast_to(sk_ref[0, h:h + 1, 0:1], (ATTN_BLOCK, 1)) for h in range(g)], axis=0)
    p, ps = _attn_probs(qr, k_cat, sink, _band_mask(i, g))
    return qr, k_cat, v_cat, cos_g, sin_g, p, ps


def _attn_fwd(proj, cosf, sinf, sinks, cfg):
    nb, g = cfg.S // ATTN_BLOCK, cfg.G
    specs, qspec = _attn_specs(cfg, nb, False)

    def body(*refs):
        o_ref = refs[10]
        _, _, v_cat, _, _, p, _ = _attn_group(*refs[:10], pl.program_id(1), g)
        o = _dot(p, v_cat, 1, 0)
        for h in range(g):
            o_ref[:, h * HEAD:(h + 1) * HEAD] = o[h * ATTN_BLOCK:(h + 1) * ATTN_BLOCK].astype(o_ref.dtype)

    return pl.pallas_call(
        body, grid=(cfg.KV, nb), in_specs=specs, out_specs=qspec,
        out_shape=jax.ShapeDtypeStruct((cfg.S, cfg.HA * HEAD), BF16),
        compiler_params=_cp("parallel", "parallel"), name="attn_fwd",
    )(proj, proj, proj, proj, proj, cosf, cosf, sinf, sinf, _sink_rows(sinks, cfg))


def _attn_bwd(proj, do, cosf, sinf, sinks, cfg):
    nb, g = cfg.S // ATTN_BLOCK, cfg.G
    specs, qspec = _attn_specs(cfg, nb, True)
    kv_out = pl.BlockSpec((ATTN_BLOCK, HEAD), lambda kv, i: (jnp.maximum(i - 1, 0), kv))

    def body(*refs):
        cc_ref, cp_ref, sc_ref, sp_ref = refs[5:9]
        do_ref, dq_ref, dk_ref, dv_ref, dsk_ref, ck_ref, cv_ref = refs[10:]
        i = pl.program_id(1)

        @pl.when(i == 0)
        def _():
            ck_ref[...] = jnp.zeros_like(ck_ref)
            cv_ref[...] = jnp.zeros_like(cv_ref)
            dsk_ref[...] = jnp.zeros_like(dsk_ref)

        @pl.when(i < nb)
        def _():
            qr, k_cat, v_cat, cos_g, sin_g, p, ps = _attn_group(*refs[:10], i, g)
            dos = _stack_heads(do_ref[...], g)
            dp = _dot(dos, v_cat, 1, 1)
            delta = jnp.sum(p * dp, axis=-1, keepdims=True)
            ds = p * (dp - delta) * (HEAD ** -0.5)
            dq = _rope_t(_dot(ds, k_cat, 1, 0), cos_g, sin_g)
            sunk = ps * delta
            for h in range(g):
                rows = slice(h * ATTN_BLOCK, (h + 1) * ATTN_BLOCK)
                dq_ref[:, h * HEAD:(h + 1) * HEAD] = dq[rows].astype(dq_ref.dtype)
                dsk_ref[0, h:h + 1, :] += jnp.broadcast_to(-jnp.sum(sunk[rows], axis=0, keepdims=True), (1, LANES))
            dk_cat = _dot(ds, qr, 0, 0)
            dv_cat = _dot(p, dos, 0, 0)
            dk_ref[...] = (ck_ref[...] + _rope_t(dk_cat[:ATTN_BLOCK], cp_ref[...], sp_ref[...])).astype(dk_ref.dtype)
            dv_ref[...] = (cv_ref[...] + dv_cat[:ATTN_BLOCK]).astype(dv_ref.dtype)
            ck_ref[...] = _rope_t(dk_cat[ATTN_BLOCK:], cc_ref[...], sc_ref[...])
            cv_ref[...] = dv_cat[ATTN_BLOCK:]

        @pl.when(i == nb)
        def _():
            dk_ref[...] = ck_ref[...].astype(dk_ref.dtype)
            dv_ref[...] = cv_ref[...].astype(dv_ref.dtype)

    kvw = cfg.KV * HEAD
    dq, dk, dv, dsk = pl.pallas_call(
        body, grid=(cfg.KV, nb + 1), in_specs=specs + [qspec],
        out_specs=(qspec, kv_out, kv_out, pl.BlockSpec((1, 8, LANES), lambda kv, i: (kv, 0, 0))),
        out_shape=(jax.ShapeDtypeStruct((cfg.S, cfg.HA * HEAD), BF16), jax.ShapeDtypeStruct((cfg.S, kvw), BF16),
                   jax.ShapeDtypeStruct((cfg.S, kvw), BF16), jax.ShapeDtypeStruct((cfg.KV, 8, LANES), F32)),
        scratch_shapes=[pltpu.VMEM((ATTN_BLOCK, HEAD), F32), pltpu.VMEM((ATTN_BLOCK, HEAD), F32)],
        compiler_params=_cp("parallel", "arbitrary"), name="attn_bwd",
    )(proj, proj, proj, proj, proj, cosf, cosf, sinf, sinf, _sink_rows(sinks, cfg), do)
    return jnp.concatenate([dq, dk, dv], axis=1), dsk[:, :cfg.G, 0].reshape(cfg.HA)


@jax.custom_vjp
def _nn(a, b):
    return _dot(a, b, 1, 0)


@jax.custom_vjp
def _nt(a, b):
    return _dot(a, b, 1, 1)


@jax.custom_vjp
def _tn(a, b):
    return _dot(a, b, 0, 0)


_nn.defvjp(lambda a, b: (_nn(a, b), (a, b)), lambda r, g: (_nt(g, r[1]), _tn(r[0], g)))
_nt.defvjp(lambda a, b: (_nt(a, b), (a, b)), lambda r, g: (_nn(g, r[1]), _tn(g, r[0])))
_tn.defvjp(lambda a, b: (_tn(a, b), (a, b)), lambda r, g: (_nt(r[1], g), _nn(r[0], g)))


def _unit_lower_inverse(a):
    ii = lax.broadcasted_iota(jnp.int32, a.shape, 0)
    jj = lax.broadcasted_iota(jnp.int32, a.shape, 1)
    t = jnp.where(ii == jj, 1.0, 0.0) - jnp.where((ii == jj + 1) & (ii % 2 == 1), a, 0.0)
    b = 2
    while b < CHUNK:
        lower_left = (ii // (2 * b) == jj // (2 * b)) & ((ii // b) % 2 == 1) & ((jj // b) % 2 == 0)
        t = t - _dot(_dot(t, jnp.where(lower_left, a, 0.0), 1, 0), t, 1, 0)
        b *= 2
    return t


@jax.custom_vjp
def _tri_inv(a):
    return _unit_lower_inverse(a)


def _tri_inv_fwd(a):
    t = _unit_lower_inverse(a)
    return t, t


def _tri_inv_bwd(t, g):
    return (-_dot(t, _dot(g, t, 1, 1), 0, 0),)


_tri_inv.defvjp(_tri_inv_fwd, _tri_inv_bwd)


@jax.custom_vjp
def _tri_inv_saved(a, t):
    return t


_tri_inv_saved.defvjp(lambda a, t: (t, t), lambda t, g: (_tri_inv_bwd(t, g)[0], jnp.zeros_like(t)))


def _gdn_chunk(q, k, v, z, b_col, a_col, a_row, al_col, dt_col, al_row, dt_row, nw, s, t_saved=None, emit_t=False):
    rows = q.shape[0]
    hb = rows // CHUNK
    ii = lax.broadcasted_iota(jnp.int32, (rows, rows), 0)
    jj = lax.broadcasted_iota(jnp.int32, (rows, rows), 1)
    same = (ii // CHUNK) == (jj // CHUNK)
    tri, strict, upper = same & (ii >= jj), same & (ii > jj), same & (ii <= jj)
    beta = _sigmoid(b_col)
    g_col = -jnp.exp(al_col) * _softplus(a_col + dt_col)
    g_row = -jnp.exp(al_row) * _softplus(a_row + dt_row)
    gc_col = jnp.sum(jnp.where(tri, g_row, 0.0), axis=1, keepdims=True)
    gc_row = jnp.sum(jnp.where(upper, g_col, 0.0), axis=0, keepdims=True)
    last = jj == (ii // CHUNK) * CHUNK + (CHUNK - 1)
    gl_col = jnp.sum(jnp.where(last, gc_row, 0.0), axis=1, keepdims=True)
    decay = jnp.where(tri, jnp.exp(jnp.where(tri, gc_col - gc_row, 0.0)), 0.0)
    qn = q * lax.rsqrt(jnp.sum(q * q, axis=-1, keepdims=True) + EPS) * (HEAD ** -0.5)
    kn = k * lax.rsqrt(jnp.sum(k * k, axis=-1, keepdims=True) + EPS)
    a_mat = jnp.where(strict, beta * _nt(kn, kn) * decay, 0.0)
    t_inv = _tri_inv(a_mat) if t_saved is None else _tri_inv_saved(a_mat, t_saved)
    eg = jnp.exp(gc_col)
    u = _nn(t_inv, v * beta)
    w = _nn(t_inv, kn * (beta * eg))
    qk = jnp.where(tri, _nt(qn, kn) * decay, 0.0)
    ri = lax.broadcasted_iota(jnp.int32, (rows, hb * HEAD), 0)
    ci = lax.broadcasted_iota(jnp.int32, (rows, hb * HEAD), 1)
    own = (ri // CHUNK) == (ci // HEAD)

    def spread(x):
        return jnp.where(own, jnp.concatenate([x] * hb, axis=1), 0.0)

    v_new = u - _nn(spread(w), s)
    o = _nn(spread(qn * eg), s) + _nn(qk, v_new)
    si = lax.broadcasted_iota(jnp.int32, (hb * HEAD, rows), 0)
    sj = lax.broadcasted_iota(jnp.int32, (hb * HEAD, rows), 1)
    gl_s = jnp.sum(jnp.where(sj == (si // HEAD) * CHUNK + (CHUNK - 1), gc_row, 0.0), axis=1, keepdims=True)
    s1 = s * jnp.exp(gl_s) + _tn(spread(kn * jnp.exp(gl_col - gc_col)), v_new)
    od = o * lax.rsqrt(jnp.mean(o * o, axis=-1, keepdims=True) + EPS) * nw * (z * _sigmoid(z))
    return (od, s1, t_inv) if emit_t else (od, s1)


def _gdn_heads_per_step(cfg):
    for hb in (GDN_HEADS_PER_STEP, 2, 1):
        if cfg.HD % hb == 0 and cfg.Z % (hb * HEAD) == 0:
            return hb


def _gdn_in_specs(cfg, nc, rev):
    hb = _gdn_heads_per_step(cfg)
    ng, rows = cfg.HD // hb, hb * CHUNK
    ci = (lambda n: nc - 1 - n) if rev else (lambda n: n)
    zo = cfg.Z // (hb * HEAD)
    blk = (CHUNK, hb * HEAD)
    col = pl.BlockSpec((1, 1, rows, 1), lambda g, n: (g, ci(n), 0, 0))
    row = pl.BlockSpec((1, 1, 1, rows), lambda g, n: (g, ci(n), 0, 0))
    hcol = pl.BlockSpec((1, rows, 1), lambda g, n: (g, 0, 0))
    hrow = pl.BlockSpec((1, 1, rows), lambda g, n: (g, 0, 0))
    return [
        pl.BlockSpec(blk, lambda g, n: (ci(n), g)), pl.BlockSpec(blk, lambda g, n: (ci(n), ng + g)),
        pl.BlockSpec(blk, lambda g, n: (ci(n), 2 * ng + g)), pl.BlockSpec(blk, lambda g, n: (ci(n), zo + g)),
        col, col, row, hcol, hcol, hrow, hrow, pl.BlockSpec((1, HEAD), lambda g, n: (0, 0)),
    ]


def _gdn_args(qkv, proj, pba, a_log, dt_bias, nw, cfg):
    hb, hd, s = _gdn_heads_per_step(cfg), cfg.HD, cfg.S
    ng, nc, rows = hd // hb, s // CHUNK, hb * CHUNK

    def grouped(x):
        return jnp.transpose(x.reshape(nc, CHUNK, ng, hb), (2, 0, 3, 1)).reshape(ng, nc, rows)

    b, a = grouped(pba[:, :hd]), grouped(pba[:, hd:2 * hd])
    per_row = lambda x: jnp.repeat(x.reshape(ng, hb), CHUNK, axis=1)
    al, dt = per_row(a_log), per_row(dt_bias)
    return (qkv, qkv, qkv, proj, b.reshape(ng, nc, rows, 1), a.reshape(ng, nc, rows, 1), a.reshape(ng, nc, 1, rows),
            al.reshape(ng, rows, 1), dt.reshape(ng, rows, 1), al.reshape(ng, 1, rows), dt.reshape(ng, 1, rows),
            nw.reshape(1, HEAD))


def _gdn_load(refs, hb):
    q, k, v, z, b, a, ar, alc, dtc, alr, dtr, nw = refs
    return (_stack_heads(q[...], hb), _stack_heads(k[...], hb), _stack_heads(v[...], hb), _stack_heads(z[...], hb),
            b[0, 0], a[0, 0], ar[0, 0], alc[0], dtc[0], alr[0], dtr[0], nw[...])


def _gdn_fwd(qkv, proj, pba, a_log, dt_bias, nw, cfg, side=None):
    nc, hb = cfg.S // CHUNK, _gdn_heads_per_step(cfg)
    ng = cfg.HD // hb
    grid = (ng, nc)

    def body(*refs):
        ins, (o_ref, st_ref, t_ref), (s_ref,), parts = _side_split(refs, 12, 3, side)
        begin, end = _side_steps(side, parts, grid)
        begin()

        @pl.when(pl.program_id(1) == 0)
        def _():
            s_ref[...] = jnp.zeros_like(s_ref)

        s0 = s_ref[...]
        st_ref[0, 0] = s0
        od, s1, t_inv = _gdn_chunk(*_gdn_load(ins, hb), s0, emit_t=True)
        t_ref[0, 0] = t_inv.astype(t_ref.dtype)
        for i in range(hb):
            o_ref[:, i * HEAD:(i + 1) * HEAD] = od[i * CHUNK:(i + 1) * CHUNK].astype(o_ref.dtype)
        s_ref[...] = s1
        end()

    s_in, s_out, s_shapes, s_sems, s_ops, _ = _side_call_args(side, 12, 3)
    rows = hb * CHUNK
    return pl.pallas_call(
        body, grid=grid, in_specs=_gdn_in_specs(cfg, nc, False) + s_in,
        out_specs=(pl.BlockSpec((CHUNK, hb * HEAD), lambda g, n: (n, g)),
                   pl.BlockSpec((1, 1, hb * HEAD, HEAD), lambda g, n: (g, n, 0, 0)),
                   pl.BlockSpec((1, 1, rows, rows), lambda g, n: (g, n, 0, 0)), *s_out),
        out_shape=(jax.ShapeDtypeStruct((cfg.S, cfg.HD * HEAD), BF16),
                   jax.ShapeDtypeStruct((ng, nc, hb * HEAD, HEAD), F32),
                   jax.ShapeDtypeStruct((ng, nc, rows, rows), BF16), *s_shapes),
        scratch_shapes=[pltpu.VMEM((hb * HEAD, HEAD), F32)] + s_sems,
        compiler_params=_cp("arbitrary" if side else "parallel", "arbitrary"),
        name="gdn_fwd_gather" if side else "gdn_fwd",
    )(*_gdn_args(qkv, proj, pba, a_log, dt_bias, nw, cfg), *s_ops)


def _gdn_bwd(qkv, proj, pba, a_log, dt_bias, nw, states, t_inv, dod, cfg, side=None):
    nc, hd, s, hb = cfg.S // CHUNK, cfg.HD, cfg.S, _gdn_heads_per_step(cfg)
    ng, rows = hd // hb, hb * CHUNK
    rv = lambda n: nc - 1 - n
    col = pl.BlockSpec((1, 1, rows, 1), lambda g, n: (g, rv(n), 0, 0))
    row = pl.BlockSpec((1, 1, 1, rows), lambda g, n: (g, rv(n), 0, 0))
    hcol = pl.BlockSpec((1, rows, 1), lambda g, n: (g, 0, 0))
    hrow = pl.BlockSpec((1, 1, rows), lambda g, n: (g, 0, 0))

    grid = (ng, nc)

    def body(*refs):
        ins, outs, (ds_ref,), parts = _side_split(refs, 15, 10, side)
        begin, end = _side_steps(side, parts, grid)
        begin()
        st_ref, t_ref, dod_ref = ins[12:15]
        dqkv_ref, dz_ref, db_ref, da_ref, dar_ref, dalc_ref, ddtc_ref, dalr_ref, ddtr_ref, dnw_ref = outs
        g, n = pl.program_id(0), pl.program_id(1)

        @pl.when(n == 0)
        def _():
            ds_ref[...] = jnp.zeros_like(ds_ref)
            for r in (dalc_ref, ddtc_ref, dalr_ref, ddtr_ref):
                r[...] = jnp.zeros_like(r)

        @pl.when((n == 0) & (g == 0))
        def _():
            dnw_ref[...] = jnp.zeros_like(dnw_ref)

        chunk = functools.partial(_gdn_chunk, t_saved=t_ref[0, 0].astype(F32))
        _, vjp = jax.vjp(chunk, *_gdn_load(ins[:12], hb), st_ref[0, 0])
        dq, dk, dv, dz, db, da, dar, dalc, ddtc, dalr, ddtr, dnw, ds0 = vjp(
            (_stack_heads(dod_ref[...].astype(F32), hb), ds_ref[...]))
        for i in range(hb):
            sl, rs = slice(i * HEAD, (i + 1) * HEAD), slice(i * CHUNK, (i + 1) * CHUNK)
            dqkv_ref[0, :, sl], dqkv_ref[1, :, sl], dqkv_ref[2, :, sl] = dq[rs], dk[rs], dv[rs]
            dz_ref[:, sl] = dz[rs].astype(dz_ref.dtype)
        db_ref[0, 0], da_ref[0, 0], dar_ref[0, 0] = db, da, dar
        dalc_ref[0] += dalc
        ddtc_ref[0] += ddtc
        dalr_ref[0] += dalr
        ddtr_ref[0] += ddtr
        dnw_ref[...] += dnw
        ds_ref[...] = ds0
        end()

    blk = pl.BlockSpec((CHUNK, hb * HEAD), lambda g, n: (rv(n), g))
    f32 = lambda *sh: jax.ShapeDtypeStruct(sh, F32)
    s_in, s_out, s_shapes, s_sems, s_ops, _ = _side_call_args(side, 15, 10)
    outs = pl.pallas_call(
        body, grid=grid,
        in_specs=_gdn_in_specs(cfg, nc, True) + [pl.BlockSpec((1, 1, hb * HEAD, HEAD), lambda g, n: (g, rv(n), 0, 0)),
                                                 pl.BlockSpec((1, 1, rows, rows), lambda g, n: (g, rv(n), 0, 0)), blk] + s_in,
        out_specs=(pl.BlockSpec((3, CHUNK, hb * HEAD), lambda g, n: (0, rv(n), g)), blk, col, col, row,
                   hcol, hcol, hrow, hrow, pl.BlockSpec((1, HEAD), lambda g, n: (0, 0)), *s_out),
        out_shape=(f32(3, s, hd * HEAD), jax.ShapeDtypeStruct((s, hd * HEAD), BF16),
                   f32(ng, nc, rows, 1), f32(ng, nc, rows, 1), f32(ng, nc, 1, rows),
                   f32(ng, rows, 1), f32(ng, rows, 1), f32(ng, 1, rows), f32(ng, 1, rows), f32(1, HEAD), *s_shapes),
        scratch_shapes=[pltpu.VMEM((hb * HEAD, HEAD), F32)] + s_sems,
        compiler_params=_cp("arbitrary", "arbitrary"), name="gdn_bwd_scatter" if side else "gdn_bwd",
    )(*_gdn_args(qkv, proj, pba, a_log, dt_bias, nw, cfg), states, t_inv, dod, *s_ops)
    dqkv, dz, db, da, dar, dalc, ddtc, dalr, ddtr, dnw = outs[:10]
    side_outs = list(outs[10:])

    def ungrouped(x):
        return jnp.transpose(x.reshape(ng, nc, hb, CHUNK), (1, 3, 0, 2)).reshape(s, hd)

    per_head = lambda c, r: (c.reshape(ng, hb, CHUNK) + r.reshape(ng, hb, CHUNK)).sum(axis=-1).reshape(hd)
    d_b = ungrouped(db.reshape(ng, nc, rows))
    d_a = ungrouped(da.reshape(ng, nc, rows) + dar.reshape(ng, nc, rows))
    dba = jnp.concatenate([d_b, d_a, jnp.zeros((s, LANES - 2 * hd), F32)], axis=1).astype(BF16)
    return dqkv, dz, dba, per_head(dalc, dalr), per_head(ddtc, ddtr), dnw.reshape(HEAD), side_outs


def _adamw(w, g, m, v, *, name):
    nl, r, c = w.shape
    gc = g.shape[2]
    tr = _tile(r, max(8, (1 << 18) // c // 8 * 8), 8) if r % 8 == 0 else r
    c1, c2 = 1.0 - ADAM_B1 ** ADAM_STEP, 1.0 - ADAM_B2 ** ADAM_STEP

    def body(w_ref, g_ref, m_ref, v_ref, go_ref, d_ref, nm_ref, nv_ref):
        gv = g_ref[:, :c]
        nm = ADAM_B1 * m_ref[...] + (1.0 - ADAM_B1) * gv
        nv = ADAM_B2 * v_ref[...] + (1.0 - ADAM_B2) * (gv * gv)
        go_ref[...] = gv
        d_ref[...] = -ADAM_LR * ((nm / c1) / (jnp.sqrt(nv / c2) + ADAM_EPS) + ADAM_WD * w_ref[...])
        nm_ref[...] = nm
        nv_ref[...] = nv

    spec = pl.BlockSpec((None, tr, c), lambda l, i: (l, i, 0))
    gspec = pl.BlockSpec((None, tr, gc), lambda l, i: (l, i, 0))
    out = jax.ShapeDtypeStruct((nl, r, c), F32)
    return pl.pallas_call(body, grid=(nl, r // tr), in_specs=[spec, gspec, spec, spec], out_specs=(spec,) * 4,
                          out_shape=(out,) * 4, compiler_params=_cp("parallel", "parallel"), name=name)(w, g, m, v)


def _sum_leading(a, *, name):
    n, r, c = a.shape
    tr = _tile(r, SUM_TR, 8)

    def body(a_ref, o_ref):
        acc = a_ref[0].astype(F32)
        for i in range(1, n):
            acc = acc + a_ref[i].astype(F32)
        o_ref[...] = acc

    return pl.pallas_call(body, grid=(r // tr,), in_specs=[pl.BlockSpec((n, tr, c), lambda i: (0, i, 0))],
                          out_specs=pl.BlockSpec((tr, c), lambda i: (i, 0)),
                          out_shape=jax.ShapeDtypeStruct((r, c), F32), compiler_params=_cp("parallel"), name=name)(a)


class _Mat:
    def __init__(self, kind, rows, cols):
        self.kind, self.rows, self.cols = kind, rows, cols
        self.full = (rows, 4 * cols) if kind == "col" else (4 * rows, cols)
        self.half = rows // 2

    def block(self, ref, j, c=None):
        r0 = (0 if self.kind == "col" else j * self.rows) + (0 if c is None else c * self.half)
        rows = pl.ds(_aligned(r0, 16), self.rows if c is None else self.half)
        return ref.at[rows, pl.ds(_aligned(j * self.cols, LANES), self.cols)] if self.kind == "col" else ref.at[rows, :]

    def own_half(self, ref, c):
        return ref.at[pl.ds(_aligned(c * self.half, 16), self.half), :]


def _aligned(v, unit):
    return v if isinstance(v, int) else pl.multiple_of(v, unit)


def _place():
    x, y, c = lax.axis_index("x"), lax.axis_index("y"), lax.axis_index("c")
    chips = [(1 - x, y), (x, 1 - y), (1 - x, 1 - y)]
    return x, y, c, chips


def _chip_index():
    return 2 * lax.axis_index("x") + lax.axis_index("y")


def _remote(src, dst, send_sems, recv_sems, k, to):
    return pltpu.make_async_remote_copy(src_ref=src, dst_ref=dst, send_sem=send_sems.at[k], recv_sem=recv_sems.at[k],
                                        device_id=to, device_id_type=MESH)


def _comm_call(body, ins, out_shapes, n_sems, name):
    return pl.pallas_call(
        body, out_shape=tuple(out_shapes), in_specs=[ANY] * len(ins), out_specs=tuple([ANY] * len(out_shapes)),
        scratch_shapes=[pltpu.SemaphoreType.DMA((n_sems,)), pltpu.SemaphoreType.DMA((n_sems,))],
        compiler_params=pltpu.CompilerParams(has_side_effects=True), name=name)(*ins)


def _gather_copies(own_refs, full_refs, send_sems, recv_sems, mats, base=0):
    x, y, c, chips = _place()
    me = 2 * x + y
    return [(_remote(m.own_half(own_refs[k], c), m.block(full_refs[k], me, c), send_sems, recv_sems, base + 3 * k + j, (cx, cy, c)),
             _remote(m.block(full_refs[k], 2 * cx + cy, c), m.block(full_refs[k], 2 * cx + cy, c), send_sems, recv_sems,
                     base + 3 * k + j, (cx, cy, c)))
            for k, m in enumerate(mats) for j, (cx, cy) in enumerate(chips)]


def _pass_copies(full_refs, send_sems, recv_sems, mats, base=0):
    x, y, c, chips = _place()
    sib = (x, y, 1 - c)
    return [(_remote(m.block(full_refs[k], 2 * cx + cy, c), m.block(full_refs[k], 2 * cx + cy, c), send_sems, recv_sems,
                     base + 3 * k + j, sib),
             _remote(m.block(full_refs[k], 2 * cx + cy, 1 - c), m.block(full_refs[k], 2 * cx + cy, 1 - c), send_sems,
                     recv_sems, base + 3 * k + j, sib))
            for k, m in enumerate(mats) for j, (cx, cy) in enumerate(chips)]


def _start_all(pairs):
    for cp, _ in pairs:
        cp.start()


def _finish_all(pairs):
    for _, landing in pairs:
        landing.wait_recv()
    for cp, _ in pairs:
        cp.wait_send()


def _gather_ici_side(own, mats):
    return _Side(own, [jax.ShapeDtypeStruct(m.full, a.dtype) for m, a in zip(mats, own)], 3 * len(mats),
                 lambda i, o, ss, rs: _start_all(_gather_copies(i, o, ss, rs, mats)),
                 lambda i, o, ss, rs: _finish_all(_gather_copies(i, o, ss, rs, mats)))


def _gather_pass_side(fulls, mats):
    return _Side(fulls, [jax.ShapeDtypeStruct(f.shape, f.dtype) for f in fulls], 3 * len(mats),
                 lambda i, o, ss, rs: _start_all(_pass_copies(o, ss, rs, mats)),
                 lambda i, o, ss, rs: _finish_all(_pass_copies(o, ss, rs, mats)), in_place=True)


def _place_own(fulls, own, mats):
    me = _chip_index().astype(jnp.int32).reshape(1)

    def place(full, blk, m):
        tr = _tile(m.rows, 512, 16)
        nr = m.rows // tr
        window = (lambda i, me: (i, me[0])) if m.kind == "col" else (lambda i, me: (me[0] * nr + i, 0))

        def body(me_ref, b_ref, f_ref, o_ref):
            o_ref[...] = b_ref[...]

        return pl.pallas_call(
            body,
            grid_spec=pltpu.PrefetchScalarGridSpec(
                num_scalar_prefetch=1, grid=(nr,),
                in_specs=[pl.BlockSpec((tr, m.cols), lambda i, me: (i, 0)), ANY],
                out_specs=pl.BlockSpec((tr, m.cols), window)),
            out_shape=jax.ShapeDtypeStruct(full.shape, full.dtype), input_output_aliases={2: 0},
            compiler_params=_cp("parallel"), name="place_own")(me, blk, full)

    return [place(o, a, m) for o, a, m in zip(fulls, own, mats)]


def _gather_weights(own, mats):
    n = len(mats)

    def body(*refs):
        own_refs, full_refs, (send_sems, recv_sems) = refs[:n], refs[n:2 * n], refs[2 * n:]
        first = _gather_copies(own_refs, full_refs, send_sems, recv_sems, mats)
        second = _pass_copies(full_refs, send_sems, recv_sems, mats, base=3 * n)
        _start_all(first)
        for (_, landed), (fwd, _) in zip(first, second):
            landed.wait_recv()
            fwd.start()
        for _, landing in second:
            landing.wait_recv()
        for cp, _ in first + second:
            cp.wait_send()

    outs = _comm_call(body, own, [jax.ShapeDtypeStruct(m.full, a.dtype) for m, a in zip(mats, own)], 6 * n,
                      "gather_weights")
    return _place_own(outs, own, mats)


def _swap_copies(g_refs, b_refs, send_sems, recv_sems, mats):
    x, y, c, _ = _place()
    cps = []
    for g, b, m in zip(g_refs, b_refs, mats):
        if m.kind == "col":
            cps.append(_remote(m.own_half(g, 1 - c), b, send_sems, recv_sems, len(cps), (x, y, 1 - c)))
        else:
            for j in range(4):
                cps.append(_remote(m.block(g, j, 1 - c), b.at[j], send_sems, recv_sems, len(cps), (x, y, 1 - c)))
    return [(cp, cp) for cp in cps]


def _swap_shapes(mats):
    return [jax.ShapeDtypeStruct((m.half, 4 * m.cols) if m.kind == "col" else (4, m.half, m.cols), F32) for m in mats]


def _swap_side(grads, mats):
    return _Side(grads, _swap_shapes(mats), sum(1 if m.kind == "col" else 4 for m in mats),
                 lambda i, o, ss, rs: _start_all(_swap_copies(i, o, ss, rs, mats)),
                 lambda i, o, ss, rs: _finish_all(_swap_copies(i, o, ss, rs, mats)))


def _swap_halves(grads, mats):
    def body(*refs):
        n = len(mats)
        pairs = _swap_copies(refs[:n], refs[n:2 * n], *refs[2 * n:], mats)
        _start_all(pairs)
        _finish_all(pairs)

    return _comm_call(body, grads, _swap_shapes(mats), sum(1 if m.kind == "col" else 4 for m in mats), "rs_swap_halves")


def _pair_sum(own, recv, m, cidx):
    tr = _tile(m.half, 256, 16)
    nh = m.half // tr
    if m.kind == "col":
        own_spec = pl.BlockSpec((tr, m.cols), lambda j, i, c: (c[0] * nh + i, j))
        recv_spec = pl.BlockSpec((tr, m.cols), lambda j, i, c: (i, j))
    else:
        own_spec = pl.BlockSpec((tr, m.cols), lambda j, i, c: ((2 * j + c[0]) * nh + i, 0))
        recv_spec = pl.BlockSpec((None, tr, m.cols), lambda j, i, c: (j, i, 0))

    def body(c_ref, a_ref, b_ref, o_ref):
        o_ref[...] = (a_ref[...] + b_ref[...]).astype(o_ref.dtype)

    return pl.pallas_call(
        body,
        grid_spec=pltpu.PrefetchScalarGridSpec(
            num_scalar_prefetch=1, grid=(4, nh), in_specs=[own_spec, recv_spec],
            out_specs=pl.BlockSpec((None, tr, m.cols), lambda j, i, c: (j, i, 0))),
        out_shape=jax.ShapeDtypeStruct((4, m.half, m.cols), BF16),
        compiler_params=_cp("parallel", "parallel"), name="rs_pair_sum")(cidx, own, recv)


def _scatter_copies(s_refs, b_refs, send_sems, recv_sems):
    x, y, c, chips = _place()
    me = 2 * x + y
    return [(_remote(s.at[2 * cx + cy], b.at[me], send_sems, recv_sems, 3 * k + j, (cx, cy, c)),
             _remote(b.at[2 * cx + cy], b.at[2 * cx + cy], send_sems, recv_sems, 3 * k + j, (cx, cy, c)))
            for k, (s, b) in enumerate(zip(s_refs, b_refs)) for j, (cx, cy) in enumerate(chips)]


def _scatter_side(sps):
    return _Side(sps, [jax.ShapeDtypeStruct(s.shape, s.dtype) for s in sps], 3 * len(sps),
                 lambda i, o, ss, rs: _start_all(_scatter_copies(i, o, ss, rs)),
                 lambda i, o, ss, rs: _finish_all(_scatter_copies(i, o, ss, rs)))


def _scatter_chips(sps):
    def body(*refs):
        n = len(sps)
        pairs = _scatter_copies(refs[:n], refs[n:2 * n], *refs[2 * n:])
        _start_all(pairs)
        _finish_all(pairs)

    return _comm_call(body, sps, [jax.ShapeDtypeStruct(s.shape, s.dtype) for s in sps], 3 * len(sps), "rs_scatter_chips")


def _sum_chips(got, sp, me, cidx):
    _, h, w = got.shape
    tr = _tile(h, 256, 16)
    nh = h // tr

    def body(me_ref, c_ref, g_ref, s_ref, o_ref):
        acc = None
        for s in range(4):
            term = jnp.where(me_ref[0] == s, s_ref[...], g_ref[s]).astype(F32)
            acc = term if acc is None else acc + term
        o_ref[...] = acc

    return pl.pallas_call(
        body,
        grid_spec=pltpu.PrefetchScalarGridSpec(
            num_scalar_prefetch=2, grid=(nh,),
            in_specs=[pl.BlockSpec((4, tr, w), lambda i, me, c: (0, i, 0)),
                      pl.BlockSpec((None, tr, w), lambda i, me, c: (me[0], i, 0))],
            out_specs=pl.BlockSpec((tr, w), lambda i, me, c: (c[0] * nh + i, 0))),
        out_shape=jax.ShapeDtypeStruct((2 * h, w), F32), compiler_params=_cp("parallel"), name="rs_sum_chips",
    )(me, cidx, got, sp)


def _share_halves(fs):
    n = len(fs)

    def body(*refs):
        f_refs, (send_sems, recv_sems) = refs[n:2 * n], refs[2 * n:]
        x, y, c, _ = _place()
        cps = []
        for k, f in enumerate(f_refs):
            h = f.shape[0] // 2
            mine, theirs = f.at[pl.ds(_aligned(c * h, 16), h), :], f.at[pl.ds(_aligned((1 - c) * h, 16), h), :]
            cps.append((_remote(mine, mine, send_sems, recv_sems, k, (x, y, 1 - c)),
                        _remote(theirs, theirs, send_sems, recv_sems, k, (x, y, 1 - c))))
        _start_all(cps)
        _finish_all(cps)

    return pl.pallas_call(
        body, out_shape=tuple(jax.ShapeDtypeStruct(f.shape, f.dtype) for f in fs), in_specs=[ANY] * n,
        out_specs=tuple([ANY] * n), input_output_aliases={k: k for k in range(n)},
        scratch_shapes=[pltpu.SemaphoreType.DMA((n,)), pltpu.SemaphoreType.DMA((n,))],
        compiler_params=pltpu.CompilerParams(has_side_effects=True), name="rs_share_halves")(*fs)


def _gather_all(p):
    r, w = p.shape
    rel = [(dx, dy, dc) for dx in (0, 1) for dy in (0, 1) for dc in (0, 1)][1:]

    def body(p_ref, g_ref, send_sems, recv_sems):
        x, y, c = lax.axis_index("x"), lax.axis_index("y"), lax.axis_index("c")
        me = 4 * x + 2 * y + c
        g_ref[me] = p_ref[...]
        peers = [(jnp.bitwise_xor(x, dx), jnp.bitwise_xor(y, dy), jnp.bitwise_xor(c, dc)) for dx, dy, dc in rel]
        cps = [_remote(p_ref, g_ref.at[me], send_sems, recv_sems, k, peer) for k, peer in enumerate(peers)]
        for cp in cps:
            cp.start()
        for k, (px, py, pc) in enumerate(peers):
            slot = g_ref.at[4 * px + 2 * py + pc]
            _remote(slot, slot, send_sems, recv_sems, k, (px, py, pc)).wait_recv()
        for cp in cps:
            cp.wait_send()

    return pl.pallas_call(
        body, out_shape=jax.ShapeDtypeStruct((8, r, w), p.dtype),
        in_specs=[pl.BlockSpec(memory_space=pltpu.VMEM)], out_specs=pl.BlockSpec(memory_space=pltpu.VMEM),
        scratch_shapes=[pltpu.SemaphoreType.DMA((7,)), pltpu.SemaphoreType.DMA((7,))],
        compiler_params=pltpu.CompilerParams(has_side_effects=True, vmem_limit_bytes=VMEM_LIMIT),
        name=("gather_all_%dx%d" % (r, w)))(p)


def _pack_small(arrs):
    flat = jnp.concatenate([a.reshape(-1) for a in arrs])
    rows = -(-flat.shape[0] // (SMALL_W * 8)) * 8
    return jnp.pad(flat, (0, rows * SMALL_W - flat.shape[0])).reshape(rows, SMALL_W)


def _unpack(buf, shapes):
    flat = buf.reshape(-1)
    out, off = [], 0
    for sh in shapes:
        n = math.prod(sh)
        out.append(flat[off:off + n].reshape(sh))
        off += n
    return out


def _chip_padded(pieces, width, padded):
    out, used = [], 0
    for p in pieces:
        at = 0
        while at < p.shape[1]:
            take = min(p.shape[1] - at, width - used)
            out.append(p[:, at:at + take])
            at, used = at + take, used + take
            if used == width:
                out.append(jnp.zeros((p.shape[0], padded - width), p.dtype))
                used = 0
    assert used == 0
    return jnp.concatenate(out, axis=1)


class _Cfg:
    def __init__(self, x, w_in, attn_sinks, dn_a_log, w_ffn_up):
        self.S, self.D = x.shape[1], x.shape[2]
        self.L = w_in.shape[0]
        self.HA, self.HD = attn_sinks.shape[1], dn_a_log.shape[1]
        self.IN = 4 * w_in.shape[2]
        kvw = self.IN - self.HA * HEAD - 4 * self.HD * HEAD - 2 * self.HD - 2 * self.D
        self.KV = kvw // (2 * HEAD)
        self.G = self.HA // self.KV
        self.FF = 2 * w_ffn_up.shape[2]
        self.KA = self.HA * HEAD
        self.VA = self.KA + self.KV * HEAD
        self.QD = self.VA + self.KV * HEAD
        self.Z = self.QD + 3 * self.HD * HEAD
        self.GA = self.Z + self.HD * HEAD
        self.GD = self.GA + self.D
        self.NB = self.GD + self.D
        assert self.NB + 2 * self.HD == self.IN and self.G <= 8 and 2 * self.HD <= LANES


def kernel(x, positions, norm_mix_pre, w_in, attn_sinks, dn_conv_w, dn_a_log, dn_dt_bias, dn_norm_w, w_branch_attn, w_branch_dn, w_out, norm_mix_post, norm_ffn_pre, w_ffn_up, ffn_conv_w, ffn_conv_b, w_ffn_down, norm_ffn_post, loss_target, m_norm_mix_pre, m_w_in, m_attn_sinks, m_dn_conv_w, m_dn_a_log, m_dn_dt_bias, m_dn_norm_w, m_w_branch_attn, m_w_branch_dn, m_w_out, m_norm_mix_post, m_norm_ffn_pre, m_w_ffn_up, m_ffn_conv_w, m_ffn_conv_b, m_w_ffn_down, m_norm_ffn_post, v_norm_mix_pre, v_w_in, v_attn_sinks, v_dn_conv_w, v_dn_a_log, v_dn_dt_bias, v_dn_norm_w, v_w_branch_attn, v_w_branch_dn, v_w_out, v_norm_mix_post, v_norm_ffn_pre, v_w_ffn_up, v_ffn_conv_w, v_ffn_conv_b, v_w_ffn_down, v_norm_ffn_post):
    cfg = _Cfg(x, w_in, attn_sinks, dn_a_log, w_ffn_up)
    S, D, L, HD = cfg.S, cfg.D, cfg.L, cfg.HD
    o1 = cfg.Z
    ins = w_in.shape[2]
    insp = -(-ins // SHARD_PAD) * SHARD_PAD
    mats = [_Mat("col", D, insp), _Mat("row", *w_branch_attn.shape[1:]), _Mat("row", *w_branch_dn.shape[1:]),
            _Mat("row", *w_out.shape[1:]), _Mat("col", *w_ffn_up.shape[1:]), _Mat("row", *w_ffn_down.shape[1:])]
    me = _chip_index()

    taps = _gather_all(_pack_small([dn_conv_w, ffn_conv_w]))
    tap_shapes = [dn_conv_w.shape, ffn_conv_w.shape]
    per_chip = [_unpack(taps[2 * j], tap_shapes) for j in range(4)]
    dnw_all = jnp.concatenate([per_chip[j][0] for j in range(4)], axis=2)
    fcw_all = jnp.concatenate([per_chip[j][1] for j in range(4)], axis=2)

    def own_blocks(l):
        return [jnp.pad(w_in[l].astype(BF16), ((0, 0), (0, insp - ins))), w_branch_attn[l].astype(BF16),
                w_branch_dn[l].astype(BF16), w_out[l].astype(BF16), w_ffn_up[l].astype(BF16), w_ffn_down[l].astype(BF16)]

    def layer_weights(l, fulls):
        win_sm, wbra, wbrd, wout, wup, wdown = fulls
        win = jnp.concatenate([win_sm[:, j * insp:j * insp + ins] for j in range(4)], axis=1)
        return dict(
            win_sm=win_sm, wbig=jnp.concatenate([win[:, :o1], win[:, o1 + 2 * HD:]], axis=1),
            wba=jnp.pad(win[:, o1:o1 + 2 * HD], ((0, 0), (0, LANES - 2 * HD))),
            wbra=wbra, wbrd=wbrd, wout=wout, wup=wup, wdown=wdown, dnw=dnw_all[l], fcw=fcw_all[l])

    weights = [layer_weights(0, _gather_weights(own_blocks(0), mats))] + [None] * (L - 1)

    inv_freq = ROPE_THETA ** (-jnp.arange(0, 2 * ROT_HALF, 2, dtype=F32) / (2 * ROT_HALF))
    ang = positions.reshape(S, 1).astype(F32) * inv_freq
    cosf = jnp.concatenate([jnp.cos(ang), jnp.cos(ang), jnp.ones((S, HEAD - 2 * ROT_HALF), F32)], axis=1)
    sinf = jnp.concatenate([-jnp.sin(ang), jnp.sin(ang), jnp.zeros((S, HEAD - 2 * ROT_HALF), F32)], axis=1)

    h = x.reshape(S, D)
    saved = []
    for l in range(L):
        wt = weights[l]
        t = dict(h0=h)
        t["xn"] = _rms_fwd(h, norm_mix_pre[l], out_dtype=BF16, name="norm_mix_pre")
        nxt = own_blocks(l + 1) if l + 1 < L else None
        if nxt:
            t["proj"], win_landed = _mm(t["xn"], wt["wbig"], name="proj_gather", side=_gather_ici_side(nxt[:1], mats[:1]))
        else:
            t["proj"] = _mm(t["xn"], wt["wbig"], name="proj")
        t["pba"] = _mm(t["xn"], wt["wba"], name="proj_ba")
        t["oa"] = _attn_fwd(t["proj"], cosf, sinf, attn_sinks[l], cfg)
        t["qkv"] = _dnconv_fwd(t["proj"], wt["dnw"], cfg)
        t["od"], t["st"], t["tinv"], *landed = _gdn_fwd(t["qkv"], t["proj"], t["pba"], dn_a_log[l], dn_dt_bias[l], dn_norm_w[l], cfg,
                                             side=_gather_ici_side(nxt[1:], mats[1:]) if nxt else None)
        t["pa"] = _mm(t["oa"], wt["wbra"], name="branch_attn")
        t["pd"] = _mm(t["od"], wt["wbrd"], name="branch_dn")
        t["y"] = _merge_fwd(t["proj"], t["pa"], t["pd"], cfg)
        t["mix"] = _mm(t["y"], wt["wout"], name="mix_out")
        t["h1"] = _rms_fwd(t["mix"], norm_mix_post[l], res=h, out_dtype=F32, name="norm_mix_post")
        t["xf"] = _rms_fwd(t["h1"], norm_ffn_pre[l], out_dtype=BF16, name="norm_ffn_pre")
        if nxt:
            t["up"], *landed = _mm(t["xf"], wt["wup"], name="ffn_up_pass",
                                   side=_gather_pass_side([win_landed] + landed, mats))
            weights[l + 1] = layer_weights(l + 1, _place_own(landed, nxt, mats))
        else:
            t["up"] = _mm(t["xf"], wt["wup"], name="ffn_up")
        t["act"] = _glu_fwd(t["up"], wt["fcw"], ffn_conv_b[l], cfg)
        t["f"] = _mm(t["act"], wt["wdown"], name="ffn_down")
        h = _rms_fwd(t["f"], norm_ffn_post[l], res=t["h1"], out_dtype=F32, name="norm_ffn_post")
        saved.append(t)

    loss_local, dh = _loss_head(h, loss_target.reshape(S, D))
    loss = lax.psum(loss_local, ("x", "y", "c"))

    cidx = lax.axis_index("c").astype(jnp.int32).reshape(1)
    midx = me.astype(jnp.int32).reshape(1)
    big_grads, small = [None] * L, [None] * L
    reduce_blocks = lambda got, pair: _share_halves([_sum_chips(a, b, midx, cidx) for a, b in zip(got, pair)])
    pair_sums = lambda full, theirs: [_pair_sum(a, b, m, cidx) for a, b, m in zip(full, theirs, mats)]
    unswapped = waiting = None
    for l in reversed(range(L)):
        wt, t = weights[l], saved[l]
        df, g_nfpost = _rms_bwd(t["f"], norm_ffn_post[l], dh, out_dtype=BF16, name="norm_ffn_post_bwd")
        dact = _mm(df, wt["wdown"], tb=True, name="ffn_down_dx")
        g_wdown = _mm(t["act"], df, ta=True, name="ffn_down_dw")
        dup, g_fcw, g_fcb = _glu_bwd(t["up"], dact, wt["fcw"], ffn_conv_b[l], cfg)
        if unswapped:
            dxf, *theirs = _mm(dup, wt["wup"], tb=True, name="ffn_up_dx_swap", side=_swap_side(unswapped[1], mats))
            waiting = (unswapped[0], pair_sums(unswapped[1], theirs))
        else:
            dxf = _mm(dup, wt["wup"], tb=True, name="ffn_up_dx")
        g_wup = _mm(t["xf"], dup, ta=True, name="ffn_up_dw")
        dh1, g_nfp = _rms_bwd(t["h1"], norm_ffn_pre[l], dxf, add=dh, name="norm_ffn_pre_bwd")
        dmix, g_nmpost = _rms_bwd(t["mix"], norm_mix_post[l], dh1, out_dtype=BF16, name="norm_mix_post_bwd")
        dy = _mm(dmix, wt["wout"], tb=True, name="mix_out_dx")
        g_wout = _mm(t["y"], dmix, ta=True, name="mix_out_dw")
        dpa, dpd, dga, dgd = _merge_bwd(t["proj"], t["pa"], t["pd"], dy, cfg)
        doa = _mm(dpa, wt["wbra"], tb=True, out_dtype=BF16, name="branch_attn_dx")
        g_wbra = _mm(t["oa"], dpa, ta=True, name="branch_attn_dw")
        dod = _mm(dpd, wt["wbrd"], tb=True, out_dtype=BF16, name="branch_dn_dx")
        g_wbrd = _mm(t["od"], dpd, ta=True, name="branch_dn_dw")
        dqkv_c, dz, dba, g_alog, g_dt, g_dnnorm, got = _gdn_bwd(
            t["qkv"], t["proj"], t["pba"], dn_a_log[l], dn_dt_bias[l], dn_norm_w[l], t["st"], t["tinv"], dod, cfg,
            side=_scatter_side(waiting[1]) if waiting else None)
        if waiting:
            big_grads[waiting[0]] = reduce_blocks(got, waiting[1])
            waiting = None
        dqkv_d, g_dnw = _dnconv_bwd(t["proj"], dqkv_c, wt["dnw"], cfg)
        dqkv_a, g_sinks = _attn_bwd(t["proj"], doa, cosf, sinf, attn_sinks[l], cfg)
        dproj = _chip_padded([dqkv_a, dqkv_d, dba[:, :2 * HD], dz, dga, dgd], ins, insp)
        dxn = _mm(dproj, wt["win_sm"], tb=True, name="proj_dx")
        g_win = _mm(t["xn"], dproj, ta=True, name="proj_dw")
        dh, g_nmp = _rms_bwd(t["h0"], norm_mix_pre[l], dxn, add=dh1, name="norm_mix_pre_bwd")

        unswapped = (l, [g_win, g_wbra, g_wbrd, g_wout, g_wup, g_wdown])
        small[l] = [g_nmp, g_sinks, g_alog, g_dt, g_dnnorm, g_nmpost, g_nfp, g_fcb, g_nfpost, g_dnw, g_fcw]

    last_pair = pair_sums(unswapped[1], _swap_halves(unswapped[1], mats))
    big_grads[unswapped[0]] = reduce_blocks(_scatter_chips(last_pair), last_pair)
    grad_x = dh.reshape(1, S, D)

    small_shapes = [a.shape for a in small[0]]
    summed = _sum_leading(_gather_all(_pack_small([a for l in range(L) for a in small[l]])), name="sum_devices")
    per_layer = _unpack(summed, small_shapes * L)
    red = [jnp.stack([per_layer[l * len(small_shapes) + i] for l in range(L)]) for i in range(len(small_shapes))]
    g_dn_conv = lax.dynamic_slice_in_dim(red[9], me * dn_conv_w.shape[2], dn_conv_w.shape[2], axis=2)
    g_ffn_conv = lax.dynamic_slice_in_dim(red[10], me * ffn_conv_w.shape[2], ffn_conv_w.shape[2], axis=2)
    stacked = lambda i: jnp.stack([big_grads[l][i] for l in range(L)])

    grads = dict(
        norm_mix_pre=red[0], w_in=stacked(0), attn_sinks=red[1], dn_conv_w=g_dn_conv, dn_a_log=red[2],
        dn_dt_bias=red[3], dn_norm_w=red[4], w_branch_attn=stacked(1), w_branch_dn=stacked(2), w_out=stacked(3),
        norm_mix_post=red[5], norm_ffn_pre=red[6], w_ffn_up=stacked(4), ffn_conv_w=g_ffn_conv, ffn_conv_b=red[7],
        w_ffn_down=stacked(5), norm_ffn_post=red[8])

    params = dict(
        norm_mix_pre=(norm_mix_pre, m_norm_mix_pre, v_norm_mix_pre), w_in=(w_in, m_w_in, v_w_in),
        attn_sinks=(attn_sinks, m_attn_sinks, v_attn_sinks), dn_conv_w=(dn_conv_w, m_dn_conv_w, v_dn_conv_w),
        dn_a_log=(dn_a_log, m_dn_a_log, v_dn_a_log), dn_dt_bias=(dn_dt_bias, m_dn_dt_bias, v_dn_dt_bias),
        dn_norm_w=(dn_norm_w, m_dn_norm_w, v_dn_norm_w),
        w_branch_attn=(w_branch_attn, m_w_branch_attn, v_w_branch_attn),
        w_branch_dn=(w_branch_dn, m_w_branch_dn, v_w_branch_dn), w_out=(w_out, m_w_out, v_w_out),
        norm_mix_post=(norm_mix_post, m_norm_mix_post, v_norm_mix_post),
        norm_ffn_pre=(norm_ffn_pre, m_norm_ffn_pre, v_norm_ffn_pre), w_ffn_up=(w_ffn_up, m_w_ffn_up, v_w_ffn_up),
        ffn_conv_w=(ffn_conv_w, m_ffn_conv_w, v_ffn_conv_w), ffn_conv_b=(ffn_conv_b, m_ffn_conv_b, v_ffn_conv_b),
        w_ffn_down=(w_ffn_down, m_w_ffn_down, v_w_ffn_down),
        norm_ffn_post=(norm_ffn_post, m_norm_ffn_post, v_norm_ffn_post))
    names = list(params)

    grad_out, delta, new_m, new_v = {}, {}, {}, {}
    small_names = [n for n in names if params[n][0].ndim == 2]
    for n in names:
        if n in small_names:
            continue
        w, m, v = params[n]
        grad_out[n], delta[n], new_m[n], new_v[n] = _adamw(w, grads[n], m, v, name="adamw_" + n)

    outs = _adamw(*[_pack_small([params[n][i] if i is not None else grads[n] for n in small_names])[None]
                    for i in (0, None, 1, 2)], name="adamw_small")
    shapes = [params[n][0].shape for n in small_names]
    for dst, o in zip((grad_out, delta, new_m, new_v), outs):
        for n, a in zip(small_names, _unpack(o, shapes)):
            dst[n] = a

    return (loss, grad_x, *[grad_out[n] for n in names], *[delta[n] for n in names],
            *[new_m[n] for n in names], *[new_v[n] for n in names])
```

```python
import functools
import math

import jax
import jax.numpy as jnp
from jax import lax
from jax.experimental import pallas as pl
from jax.experimental.pallas import tpu as pltpu

F32, BF16 = jnp.float32, jnp.bfloat16
EPS = 1e-6
HEAD = 128
ATTN_BLOCK = 128
ROT_HALF = 16
ROPE_THETA = 500000.0
CHUNK = 64
DN_CONV, FFN_CONV = 4, 3
GDN_HEADS_PER_STEP = 4
MM_TILES = (512, 1024, 3968)
HALO = 8
LANES = 128
SUM_TR = 512
SHARD_PAD = 1024
SMALL_W = 1024
ADAM_LR, ADAM_B1, ADAM_B2, ADAM_EPS, ADAM_WD, ADAM_STEP = 0.001, 0.9, 0.999, 1e-08, 0.01, 10
VMEM_LIMIT = 56 * 1024 * 1024
MESH = pl.DeviceIdType.MESH
ANY = pl.BlockSpec(memory_space=pl.ANY)


def _cp(*sem):
    return pltpu.CompilerParams(dimension_semantics=sem, vmem_limit_bytes=VMEM_LIMIT)


def _tile(n, pref, unit=LANES):
    if n <= pref:
        return n
    t = (pref // unit) * unit
    while t >= unit:
        if n % t == 0:
            return t
        t -= unit
    raise ValueError((n, pref, unit))


def _sigmoid(x):
    return 1.0 / (1.0 + jnp.exp(-x))


def _softplus(x):
    return jnp.maximum(x, 0.0) + jnp.log(1.0 + jnp.exp(-jnp.abs(x)))


def _dsilu(y):
    s = _sigmoid(y)
    return s * (1.0 + y * (1.0 - s))


class _Side:
    def __init__(self, ins, out_shapes, n_sems, start, finish, in_place=False):
        self.ins, self.out_shapes, self.n_sems = list(ins), list(out_shapes), n_sems
        self.start, self.finish, self.in_place = start, finish, in_place


def _side_split(refs, n_in, n_out, side):
    si, so = (len(side.ins), len(side.out_shapes)) if side else (0, 0)
    ins, sins = refs[:n_in], refs[n_in:n_in + si]
    outs, souts = refs[n_in + si:n_in + si + n_out], refs[n_in + si + n_out:n_in + si + n_out + so]
    rest = refs[n_in + si + n_out + so:]
    scratch, sems = (rest[:-2], rest[-2:]) if side else (rest, ())
    return ins, outs, scratch, (sins, souts, sems)


def _side_steps(side, parts, grid):
    if side is None:
        return (lambda: None), (lambda: None)
    sins, souts, sems = parts
    ids = [pl.program_id(a) for a in range(len(grid))]
    first = functools.reduce(jnp.logical_and, [i == 0 for i in ids])
    last = functools.reduce(jnp.logical_and, [i == n - 1 for i, n in zip(ids, grid)])

    def begin():
        @pl.when(first)
        def _():
            side.start(sins, souts, *sems)

    def end():
        @pl.when(last)
        def _():
            side.finish(sins, souts, *sems)

    return begin, end


def _side_call_args(side, n_in, n_out):
    if side is None:
        return [], [], [], [], [], {}
    sems = [pltpu.SemaphoreType.DMA((side.n_sems,)), pltpu.SemaphoreType.DMA((side.n_sems,))]
    aliases = {n_in + k: n_out + k for k in range(len(side.ins))} if side.in_place else {}
    return [ANY] * len(side.ins), [ANY] * len(side.out_shapes), side.out_shapes, sems, side.ins, aliases


def _mm(a, b, *, ta=False, tb=False, add=None, out_dtype=F32, name, tiles=MM_TILES, side=None):
    if ta:
        kdim, m = a.shape
    else:
        m, kdim = a.shape
    if tb:
        n, k2 = b.shape
    else:
        k2, n = b.shape
    assert kdim == k2, (a.shape, b.shape, ta, tb)
    tm, tn, tk = _tile(m, tiles[0]), _tile(n, tiles[1]), _tile(kdim, tiles[2])
    nk = kdim // tk
    grid = (m // tm, n // tn, nk)
    dims = (((0 if ta else 1,), (1 if tb else 0,)), ((), ()))
    n_in = 2 if add is None else 3

    def body(*refs):
        ins, (o_ref,), (acc_ref,), parts = _side_split(refs, n_in, 1, side)
        begin, end = _side_steps(side, parts, grid)
        begin()
        a_ref, b_ref = ins[:2]
        k = pl.program_id(2)

        @pl.when(k == 0)
        def _():
            acc_ref[...] = jnp.zeros_like(acc_ref)

        acc_ref[...] += lax.dot_general(a_ref[...].astype(BF16), b_ref[...].astype(BF16), dims,
                                        preferred_element_type=F32)

        @pl.when(k == nk - 1)
        def _():
            r = acc_ref[...]
            if add is not None:
                r = r + ins[2][...]
            o_ref[...] = r.astype(o_ref.dtype)

        end()

    a_spec = pl.BlockSpec((tk, tm), lambda i, j, k: (k, i)) if ta else pl.BlockSpec((tm, tk), lambda i, j, k: (i, k))
    b_spec = pl.BlockSpec((tn, tk), lambda i, j, k: (j, k)) if tb else pl.BlockSpec((tk, tn), lambda i, j, k: (k, j))
    o_spec = pl.BlockSpec((tm, tn), lambda i, j, k: (i, j))
    ins, specs = [a, b], [a_spec, b_spec]
    if add is not None:
        ins.append(add)
        specs.append(o_spec)
    s_in, s_out, s_shapes, s_sems, s_ops, aliases = _side_call_args(side, n_in, 1)
    sem = ("arbitrary",) * 3 if side else ("parallel", "parallel", "arbitrary")
    outs = pl.pallas_call(
        body, grid=grid, in_specs=specs + s_in, out_specs=(o_spec, *s_out),
        out_shape=(jax.ShapeDtypeStruct((m, n), out_dtype), *s_shapes),
        scratch_shapes=[pltpu.VMEM((tm, tn), F32)] + s_sems, input_output_aliases=aliases,
        compiler_params=_cp(*sem), name=name,
    )(*ins, *s_ops)
    return outs if side else outs[0]


def _rms_fwd(x, w, res=None, *, out_dtype, name):
    s, d = x.shape
    tr = _tile(s, 256, 8)

    def body(*refs):
        if res is None:
            x_ref, w_ref, o_ref = refs
        else:
            x_ref, w_ref, r_ref, o_ref = refs
        xv = x_ref[...]
        y = xv * lax.rsqrt(jnp.mean(xv * xv, axis=-1, keepdims=True) + EPS) * w_ref[...]
        if res is not None:
            y = y + r_ref[...]
        o_ref[...] = y.astype(o_ref.dtype)

    row = pl.BlockSpec((tr, d), lambda i: (i, 0))
    par = pl.BlockSpec((1, d), lambda i: (0, 0))
    ins, specs = [x, w.reshape(1, d)], [row, par]
    if res is not None:
        ins.append(res)
        specs.append(row)
    return pl.pallas_call(body, grid=(s // tr,), in_specs=specs, out_specs=row,
                          out_shape=jax.ShapeDtypeStruct((s, d), out_dtype),
                          compiler_params=_cp("parallel"), name=name)(*ins)


def _rms_bwd(x, w, dy, add=None, *, out_dtype=F32, name):
    s, d = x.shape
    tr = _tile(s, 256, 8)

    def body(*refs):
        if add is None:
            x_ref, w_ref, dy_ref, dx_ref, dw_ref = refs
        else:
            x_ref, w_ref, dy_ref, add_ref, dx_ref, dw_ref = refs
        xv = x_ref[...]
        dyv = dy_ref[...].astype(F32)
        r = lax.rsqrt(jnp.mean(xv * xv, axis=-1, keepdims=True) + EPS)
        xh = xv * r
        dyw = dyv * w_ref[...]
        dx = r * (dyw - xh * jnp.mean(dyw * xh, axis=-1, keepdims=True))
        if add is not None:
            dx = dx + add_ref[...]
        dx_ref[...] = dx.astype(dx_ref.dtype)

        @pl.when(pl.program_id(0) == 0)
        def _():
            dw_ref[...] = jnp.zeros_like(dw_ref)

        dw_ref[...] += jnp.sum(dyv * xh, axis=0, keepdims=True)

    row = pl.BlockSpec((tr, d), lambda i: (i, 0))
    par = pl.BlockSpec((1, d), lambda i: (0, 0))
    ins, specs = [x, w.reshape(1, d), dy], [row, par, row]
    if add is not None:
        ins.append(add)
        specs.append(row)
    dx, dw = pl.pallas_call(
        body, grid=(s // tr,), in_specs=specs, out_specs=(row, par),
        out_shape=(jax.ShapeDtypeStruct((s, d), out_dtype), jax.ShapeDtypeStruct((1, d), F32)),
        compiler_params=_cp("arbitrary"), name=name)(*ins)
    return dx, dw.reshape(d)


def _loss_head(h, tgt):
    s, d = h.shape
    tr = _tile(s, 256, 8)

    def body(h_ref, t_ref, dh_ref, l_ref):
        e = h_ref[...] - t_ref[...]
        dh_ref[...] = e * (1.0 / d)

        @pl.when(pl.program_id(0) == 0)
        def _():
            l_ref[...] = jnp.zeros_like(l_ref)

        l_ref[...] += 0.5 * jnp.sum(jnp.mean(e * e, axis=-1, keepdims=True), axis=0, keepdims=True)

    row = pl.BlockSpec((tr, d), lambda i: (i, 0))
    dh, l = pl.pallas_call(
        body, grid=(s // tr,), in_specs=[row, row], out_specs=(row, pl.BlockSpec((1, 1), lambda i: (0, 0))),
        out_shape=(jax.ShapeDtypeStruct((s, d), F32), jax.ShapeDtypeStruct((1, 1), F32)),
        compiler_params=_cp("arbitrary"), name="loss_head")(h, tgt)
    return l[0, 0], dh


def _conv_rows(xe, w, width):
    y = None
    for j in range(width):
        s = width - 1 - j
        term = (pltpu.roll(xe, s, 0) if s else xe)[HALO:] * w[j:j + 1, :]
        y = term if y is None else y + term
    return y


def _conv_rows_t(dy_ext, w, width, tr):
    n = dy_ext.shape[0]
    dx = None
    for j in range(width):
        s = width - 1 - j
        term = (pltpu.roll(dy_ext, n - s, 0) if s else dy_ext)[:tr] * w[j:j + 1, :]
        dx = term if dx is None else dx + term
    return dx


def _conv_dw(dy, xe, width):
    rows = []
    for j in range(width):
        s = width - 1 - j
        rows.append(jnp.sum(dy * (pltpu.roll(xe, s, 0) if s else xe)[HALO:], axis=0, keepdims=True))
    return jnp.concatenate(rows, axis=0)


def _halo_specs(tr, tc, coff, nrow_blocks):
    per = tr // HALO
    last = nrow_blocks * per - 1
    cur = pl.BlockSpec((tr, tc), lambda j, r: (r, coff + j))
    prev = pl.BlockSpec((HALO, tc), lambda j, r: (jnp.maximum(r * per - 1, 0), coff + j))
    nxt = pl.BlockSpec((HALO, tc), lambda j, r: (jnp.minimum((r + 1) * per, last), coff + j))
    return cur, prev, nxt


def _dnconv_fwd(proj, w, cfg):
    s, width = cfg.S, 3 * cfg.HD * HEAD
    tc = _tile(math.gcd(cfg.QD, width), 512)
    tr = _tile(s, 512, 8)
    nr = s // tr
    cur, prev, _ = _halo_specs(tr, tc, cfg.QD // tc, nr)

    def body(x_ref, xp_ref, w_ref, o_ref):
        hist = jnp.where(pl.program_id(1) > 0, xp_ref[...], 0.0)
        y = _conv_rows(jnp.concatenate([hist, x_ref[...]], axis=0), w_ref[...], DN_CONV)
        o_ref[...] = y * _sigmoid(y)

    return pl.pallas_call(
        body, grid=(width // tc, nr),
        in_specs=[cur, prev, pl.BlockSpec((DN_CONV, tc), lambda j, r: (0, j))],
        out_specs=pl.BlockSpec((tr, tc), lambda j, r: (r, j)),
        out_shape=jax.ShapeDtypeStruct((s, width), F32),
        compiler_params=_cp("parallel", "parallel"), name="dnconv_fwd")(proj, proj, w)


def _dnconv_bwd(proj, du3, w, cfg):
    s, hw = cfg.S, cfg.HD * HEAD
    width = 3 * hw
    tc = _tile(math.gcd(cfg.QD, hw), 512)
    tr = _tile(s, 512, 8)
    nr = s // tr
    per, last = tr // HALO, s // HALO - 1
    cur, prev, nxt = _halo_specs(tr, tc, cfg.QD // tc, nr)
    pc = hw // tc
    du_cur = pl.BlockSpec((1, tr, tc), lambda j, r: (j // pc, r, j % pc))
    du_nxt = pl.BlockSpec((1, HALO, tc), lambda j, r: (j // pc, jnp.minimum((r + 1) * per, last), j % pc))

    def body(x_ref, xp_ref, xn_ref, du_ref, dun_ref, w_ref, dx_ref, dw_ref):
        r = pl.program_id(1)
        wv = w_ref[...]
        hist = jnp.where(r > 0, xp_ref[...], 0.0)
        xe = jnp.concatenate([hist, x_ref[...]], axis=0)
        y_ext = _conv_rows(jnp.concatenate([xe, xn_ref[...]], axis=0), wv, DN_CONV)
        du_ext = jnp.concatenate([du_ref[0], jnp.where(r < nr - 1, dun_ref[0], 0.0)], axis=0)
        dy_ext = du_ext * _dsilu(y_ext)
        dx_ref[...] = _conv_rows_t(dy_ext, wv, DN_CONV, tr).astype(dx_ref.dtype)

        @pl.when(r == 0)
        def _():
            dw_ref[...] = jnp.zeros_like(dw_ref)

        dw_ref[...] += _conv_dw(dy_ext[:tr], xe, DN_CONV)

    wspec = pl.BlockSpec((DN_CONV, tc), lambda j, r: (0, j))
    return pl.pallas_call(
        body, grid=(width // tc, nr),
        in_specs=[cur, prev, nxt, du_cur, du_nxt, wspec],
        out_specs=(pl.BlockSpec((tr, tc), lambda j, r: (r, j)), wspec),
        out_shape=(jax.ShapeDtypeStruct((s, width), BF16), jax.ShapeDtypeStruct((DN_CONV, width), F32)),
        compiler_params=_cp("parallel", "arbitrary"), name="dnconv_bwd")(proj, proj, proj, du3, du3, w)


def _glu_fwd(up, w, b, cfg):
    s, ff = cfg.S, cfg.FF
    tc = _tile(ff, 512)
    tr = _tile(s, 512, 8)
    nr, nc = s // tr, ff // tc
    g_cur, g_prev, _ = _halo_specs(tr, tc, 0, nr)
    v_cur, v_prev, _ = _halo_specs(tr, tc, nc, nr)

    def wspec(rows, off):
        return pl.BlockSpec((rows, tc), lambda j, r: (0, off + j))

    def body(g_ref, gp_ref, v_ref, vp_ref, wg_ref, wv_ref, bg_ref, bv_ref, o_ref):
        first = pl.program_id(1) > 0
        ug = _conv_rows(jnp.concatenate([jnp.where(first, gp_ref[...], 0.0), g_ref[...]], axis=0), wg_ref[...], FFN_CONV) + bg_ref[...]
        uv = _conv_rows(jnp.concatenate([jnp.where(first, vp_ref[...], 0.0), v_ref[...]], axis=0), wv_ref[...], FFN_CONV) + bv_ref[...]
        o_ref[...] = (ug * _sigmoid(ug) * uv).astype(o_ref.dtype)

    b2 = b.reshape(1, 2 * ff)
    return pl.pallas_call(
        body, grid=(nc, nr),
        in_specs=[g_cur, g_prev, v_cur, v_prev, wspec(FFN_CONV, 0), wspec(FFN_CONV, nc), wspec(1, 0), wspec(1, nc)],
        out_specs=pl.BlockSpec((tr, tc), lambda j, r: (r, j)),
        out_shape=jax.ShapeDtypeStruct((s, ff), BF16),
        compiler_params=_cp("parallel", "parallel"), name="glu_fwd")(up, up, up, up, w, w, b2, b2)


def _glu_bwd(up, dact, w, b, cfg):
    s, ff = cfg.S, cfg.FF
    tc = _tile(ff, 512)
    tr = _tile(s, 512, 8)
    nr, nc = s // tr, ff // tc
    g_cur, g_prev, g_nxt = _halo_specs(tr, tc, 0, nr)
    v_cur, v_prev, v_nxt = _halo_specs(tr, tc, nc, nr)
    d_cur, _, d_nxt = _halo_specs(tr, tc, 0, nr)

    def wspec(rows, off):
        return pl.BlockSpec((rows, tc), lambda j, r: (0, off + j))

    def body(g_ref, gp_ref, gn_ref, v_ref, vp_ref, vn_ref, d_ref, dn_ref, wg_ref, wv_ref, bg_ref, bv_ref,
             dup_ref, dw_ref, db_ref):
        r = pl.program_id(1)
        wg, wv = wg_ref[...], wv_ref[...]
        ge = jnp.concatenate([jnp.where(r > 0, gp_ref[...], 0.0), g_ref[...]], axis=0)
        ve = jnp.concatenate([jnp.where(r > 0, vp_ref[...], 0.0), v_ref[...]], axis=0)
        ug = _conv_rows(jnp.concatenate([ge, gn_ref[...]], axis=0), wg, FFN_CONV) + bg_ref[...]
        uv = _conv_rows(jnp.concatenate([ve, vn_ref[...]], axis=0), wv, FFN_CONV) + bv_ref[...]
        da = jnp.concatenate([d_ref[...].astype(F32), jnp.where(r < nr - 1, dn_ref[...].astype(F32), 0.0)], axis=0)
        sg = _sigmoid(ug)
        dug = da * uv * (sg * (1.0 + ug * (1.0 - sg)))
        duv = da * ug * sg
        dup_ref[0] = _conv_rows_t(dug, wg, FFN_CONV, tr).astype(dup_ref.dtype)
        dup_ref[1] = _conv_rows_t(duv, wv, FFN_CONV, tr).astype(dup_ref.dtype)

        @pl.when(r == 0)
        def _():
            dw_ref[...] = jnp.zeros_like(dw_ref)
            db_ref[...] = jnp.zeros_like(db_ref)

        dw_ref[0] += _conv_dw(dug[:tr], ge, FFN_CONV)
        dw_ref[1] += _conv_dw(duv[:tr], ve, FFN_CONV)
        db_ref[0] += jnp.sum(dug[:tr], axis=0, keepdims=True)
        db_ref[1] += jnp.sum(duv[:tr], axis=0, keepdims=True)

    b2 = b.reshape(1, 2 * ff)
    dup, dw, db = pl.pallas_call(
        body, grid=(nc, nr),
        in_specs=[g_cur, g_prev, g_nxt, v_cur, v_prev, v_nxt, d_cur, d_nxt,
                  wspec(FFN_CONV, 0), wspec(FFN_CONV, nc), wspec(1, 0), wspec(1, nc)],
        out_specs=(pl.BlockSpec((2, tr, tc), lambda j, r: (0, r, j)),
                   pl.BlockSpec((2, FFN_CONV, tc), lambda j, r: (0, 0, j)),
                   pl.BlockSpec((2, 1, tc), lambda j, r: (0, 0, j))),
        out_shape=(jax.ShapeDtypeStruct((2, s, ff), BF16), jax.ShapeDtypeStruct((2, FFN_CONV, ff), F32),
                   jax.ShapeDtypeStruct((2, 1, ff), F32)),
        compiler_params=_cp("parallel", "arbitrary"), name="glu_bwd",
    )(up, up, up, up, up, up, dact, dact, w, w, b2, b2)
    dup = jnp.concatenate([dup[0], dup[1]], axis=1)
    dw = jnp.transpose(dw, (1, 0, 2)).reshape(FFN_CONV, 2 * ff)
    return dup, dw, db.reshape(2 * ff)


def _merge_specs(cfg):
    tc = _tile(math.gcd(math.gcd(cfg.GA, cfg.GD), cfg.D), 512)
    tr = _tile(cfg.S, 512, 8)
    ga = pl.BlockSpec((tr, tc), lambda r, j: (r, cfg.GA // tc + j))
    gd = pl.BlockSpec((tr, tc), lambda r, j: (r, cfg.GD // tc + j))
    pd = pl.BlockSpec((tr, tc), lambda r, j: (r, j))
    return (cfg.S // tr, cfg.D // tc), ga, gd, pd


def _merge_fwd(proj, pa, pdn, cfg):
    grid, ga, gd, pd = _merge_specs(cfg)

    def body(ga_ref, gd_ref, pa_ref, pd_ref, y_ref):
        y_ref[...] = (_sigmoid(ga_ref[...]) * pa_ref[...] + _sigmoid(gd_ref[...]) * pd_ref[...]).astype(y_ref.dtype)

    return pl.pallas_call(body, grid=grid, in_specs=[ga, gd, pd, pd], out_specs=pd,
                          out_shape=jax.ShapeDtypeStruct((cfg.S, cfg.D), BF16),
                          compiler_params=_cp("parallel", "parallel"), name="merge_fwd")(proj, proj, pa, pdn)


def _merge_bwd(proj, pa, pdn, dy, cfg):
    grid, ga, gd, pd = _merge_specs(cfg)

    def body(ga_ref, gd_ref, pa_ref, pd_ref, dy_ref, dpa_ref, dpd_ref, dga_ref, dgd_ref):
        dyv = dy_ref[...]
        sa, sd = _sigmoid(ga_ref[...]), _sigmoid(gd_ref[...])
        dpa_ref[...] = (dyv * sa).astype(BF16)
        dpd_ref[...] = (dyv * sd).astype(BF16)
        dga_ref[...] = (dyv * pa_ref[...] * sa * (1.0 - sa)).astype(BF16)
        dgd_ref[...] = (dyv * pd_ref[...] * sd * (1.0 - sd)).astype(BF16)

    out = jax.ShapeDtypeStruct((cfg.S, cfg.D), BF16)
    return pl.pallas_call(body, grid=grid, in_specs=[ga, gd, pd, pd, pd], out_specs=(pd,) * 4,
                          out_shape=(out,) * 4, compiler_params=_cp("parallel", "parallel"),
                          name="merge_bwd")(proj, proj, pa, pdn, dy)


def _rope(x, cos, sin):
    lane = lax.broadcasted_iota(jnp.int32, x.shape, 1)
    swapped = jnp.where(lane < ROT_HALF, pltpu.roll(x, LANES - ROT_HALF, 1), pltpu.roll(x, ROT_HALF, 1))
    return x * cos + swapped * sin


def _rope_t(dx, cos, sin):
    t = dx * sin
    lane = lax.broadcasted_iota(jnp.int32, dx.shape, 1)
    swapped = jnp.where(lane < ROT_HALF, pltpu.roll(t, LANES - ROT_HALF, 1), pltpu.roll(t, ROT_HALF, 1))
    return dx * cos + swapped


def _band_mask(i, g):
    r = lax.broadcasted_iota(jnp.int32, (g * ATTN_BLOCK, 2 * ATTN_BLOCK), 0) % ATTN_BLOCK
    c = lax.broadcasted_iota(jnp.int32, (g * ATTN_BLOCK, 2 * ATTN_BLOCK), 1)
    dist = r + ATTN_BLOCK - c
    return (dist >= 0) & (dist < ATTN_BLOCK) & ((c >= ATTN_BLOCK) | (i > 0))


def _stack_heads(x, hb):
    return jnp.concatenate([x[:, i * HEAD:(i + 1) * HEAD] for i in range(hb)], axis=0)


def _dot(a, b, ca, cb):
    return lax.dot_general(a.astype(BF16), b.astype(BF16), (((ca,), (cb,)), ((), ())), preferred_element_type=F32)


def _attn_probs(qg, k_cat, sink, valid):
    logits = jnp.where(valid, _dot(qg, k_cat, 1, 1) * (HEAD ** -0.5), -1e30)
    m = jnp.maximum(jnp.max(logits, axis=-1, keepdims=True), sink)
    p = jnp.exp(logits - m)
    es = jnp.exp(sink - m)
    inv = 1.0 / (jnp.sum(p, axis=-1, keepdims=True) + es)
    return p * inv, es * inv


def _attn_specs(cfg, nb, clamp):
    gw = cfg.G * HEAD
    qi = (lambda kv, i: (jnp.minimum(i, nb - 1), kv)) if clamp else (lambda kv, i: (i, kv))
    ci = (lambda i: jnp.minimum(i, nb - 1)) if clamp else (lambda i: i)
    pi = lambda i: jnp.maximum(ci(i) - 1, 0)
    ko, vo = cfg.KA // HEAD, cfg.VA // HEAD
    blk = (ATTN_BLOCK, HEAD)
    specs = [
        pl.BlockSpec((ATTN_BLOCK, gw), qi),
        pl.BlockSpec(blk, lambda kv, i: (ci(i), ko + kv)), pl.BlockSpec(blk, lambda kv, i: (pi(i), ko + kv)),
        pl.BlockSpec(blk, lambda kv, i: (ci(i), vo + kv)), pl.BlockSpec(blk, lambda kv, i: (pi(i), vo + kv)),
        pl.BlockSpec(blk, lambda kv, i: (ci(i), 0)), pl.BlockSpec(blk, lambda kv, i: (pi(i), 0)),
        pl.BlockSpec(blk, lambda kv, i: (ci(i), 0)), pl.BlockSpec(blk, lambda kv, i: (pi(i), 0)),
        pl.BlockSpec((1, 8, LANES), lambda kv, i: (kv, 0, 0)),
    ]
    return specs, pl.BlockSpec((ATTN_BLOCK, gw), qi)


def _sink_rows(sinks, cfg):
    sk = jnp.broadcast_to(sinks.reshape(cfg.KV, cfg.G, 1), (cfg.KV, cfg.G, LANES))
    return jnp.pad(sk, ((0, 0), (0, 8 - cfg.G), (0, 0)))


def _attn_group(q_ref, kc_ref, kp_ref, vc_ref, vp_ref, cc_ref, cp_ref, sc_ref, sp_ref, sk_ref, i, g):
    cc, sc = cc_ref[...], sc_ref[...]
    k_cat = jnp.concatenate([_rope(kp_ref[...], cp_ref[...], sp_ref[...]), _rope(kc_ref[...], cc, sc)], axis=0).astype(BF16)
    v_cat = jnp.concatenate([vp_ref[...], vc_ref[...]], axis=0).astype(BF16)
    cos_g, sin_g = jnp.concatenate([cc] * g, axis=0), jnp.concatenate([sc] * g, axis=0)
    qr = _rope(_stack_heads(q_ref[...], g), cos_g, sin_g)
    sink = jnp.concatenate([jnp.broadcast_to(sk_ref[0, h:h + 1, 0:1], (ATTN_BLOCK, 1)) for h in range(g)], axis=0)
    p, ps = _attn_probs(qr, k_cat, sink, _band_mask(i, g))
    return qr, k_cat, v_cat, cos_g, sin_g, p, ps


def _attn_fwd(proj, cosf, sinf, sinks, cfg):
    nb, g = cfg.S // ATTN_BLOCK, cfg.G
    specs, qspec = _attn_specs(cfg, nb, False)

    def body(*refs):
        o_ref = refs[10]
        _, _, v_cat, _, _, p, _ = _attn_group(*refs[:10], pl.program_id(1), g)
        o = _dot(p, v_cat, 1, 0)
        for h in range(g):
            o_ref[:, h * HEAD:(h + 1) * HEAD] = o[h * ATTN_BLOCK:(h + 1) * ATTN_BLOCK].astype(o_ref.dtype)

    return pl.pallas_call(
        body, grid=(cfg.KV, nb), in_specs=specs, out_specs=qspec,
        out_shape=jax.ShapeDtypeStruct((cfg.S, cfg.HA * HEAD), BF16),
        compiler_params=_cp("parallel", "parallel"), name="attn_fwd",
    )(proj, proj, proj, proj, proj, cosf, cosf, sinf, sinf, _sink_rows(sinks, cfg))


def _attn_bwd(proj, do, cosf, sinf, sinks, cfg):
    nb, g = cfg.S // ATTN_BLOCK, cfg.G
    specs, qspec = _attn_specs(cfg, nb, True)
    kv_out = pl.BlockSpec((ATTN_BLOCK, HEAD), lambda kv, i: (jnp.maximum(i - 1, 0), kv))

    def body(*refs):
        cc_ref, cp_ref, sc_ref, sp_ref = refs[5:9]
        do_ref, dq_ref, dk_ref, dv_ref, dsk_ref, ck_ref, cv_ref = refs[10:]
        i = pl.program_id(1)

        @pl.when(i == 0)
        def _():
            ck_ref[...] = jnp.zeros_like(ck_ref)
            cv_ref[...] = jnp.zeros_like(cv_ref)
            dsk_ref[...] = jnp.zeros_like(dsk_ref)

        @pl.when(i < nb)
        def _():
            qr, k_cat, v_cat, cos_g, sin_g, p, ps = _attn_group(*refs[:10], i, g)
            dos = _stack_heads(do_ref[...], g)
            dp = _dot(dos, v_cat, 1, 1)
            delta = jnp.sum(p * dp, axis=-1, keepdims=True)
            ds = p * (dp - delta) * (HEAD ** -0.5)
            dq = _rope_t(_dot(ds, k_cat, 1, 0), cos_g, sin_g)
            sunk = ps * delta
            for h in range(g):
                rows = slice(h * ATTN_BLOCK, (h + 1) * ATTN_BLOCK)
                dq_ref[:, h * HEAD:(h + 1) * HEAD] = dq[rows].astype(dq_ref.dtype)
                dsk_ref[0, h:h + 1, :] += jnp.broadcast_to(-jnp.sum(sunk[rows], axis=0, keepdims=True), (1, LANES))
            dk_cat = _dot(ds, qr, 0, 0)
            dv_cat = _dot(p, dos, 0, 0)
            dk_ref[...] = (ck_ref[...] + _rope_t(dk_cat[:ATTN_BLOCK], cp_ref[...], sp_ref[...])).astype(dk_ref.dtype)
            dv_ref[...] = (cv_ref[...] + dv_cat[:ATTN_BLOCK]).astype(dv_ref.dtype)
            ck_ref[...] = _rope_t(dk_cat[ATTN_BLOCK:], cc_ref[...], sc_ref[...])
            cv_ref[...] = dv_cat[ATTN_BLOCK:]

        @pl.when(i == nb)
        def _():
            dk_ref[...] = ck_ref[...].astype(dk_ref.dtype)
            dv_ref[...] = cv_ref[...].astype(dv_ref.dtype)

    kvw = cfg.KV * HEAD
    dq, dk, dv, dsk = pl.pallas_call(
        body, grid=(cfg.KV, nb + 1), in_specs=specs + [qspec],
        out_specs=(qspec, kv_out, kv_out, pl.BlockSpec((1, 8, LANES), lambda kv, i: (kv, 0, 0))),
        out_shape=(jax.ShapeDtypeStruct((cfg.S, cfg.HA * HEAD), BF16), jax.ShapeDtypeStruct((cfg.S, kvw), BF16),
                   jax.ShapeDtypeStruct((cfg.S, kvw), BF16), jax.ShapeDtypeStruct((cfg.KV, 8, LANES), F32)),
        scratch_shapes=[pltpu.VMEM((ATTN_BLOCK, HEAD), F32), pltpu.VMEM((ATTN_BLOCK, HEAD), F32)],
        compiler_params=_cp("parallel", "arbitrary"), name="attn_bwd",
    )(proj, proj, proj, proj, proj, cosf, cosf, sinf, sinf, _sink_rows(sinks, cfg), do)
    return jnp.concatenate([dq, dk, dv], axis=1), dsk[:, :cfg.G, 0].reshape(cfg.HA)


@jax.custom_vjp
def _nn(a, b):
    return _dot(a, b, 1, 0)


@jax.custom_vjp
def _nt(a, b):
    return _dot(a, b, 1, 1)


@jax.custom_vjp
def _tn(a, b):
    return _dot(a, b, 0, 0)


_nn.defvjp(lambda a, b: (_nn(a, b), (a, b)), lambda r, g: (_nt(g, r[1]), _tn(r[0], g)))
_nt.defvjp(lambda a, b: (_nt(a, b), (a, b)), lambda r, g: (_nn(g, r[1]), _tn(g, r[0])))
_tn.defvjp(lambda a, b: (_tn(a, b), (a, b)), lambda r, g: (_nt(r[1], g), _nn(r[0], g)))


def _unit_lower_inverse(a):
    ii = lax.broadcasted_iota(jnp.int32, a.shape, 0)
    jj = lax.broadcasted_iota(jnp.int32, a.shape, 1)
    t = jnp.where(ii == jj, 1.0, 0.0) - jnp.where((ii == jj + 1) & (ii % 2 == 1), a, 0.0)
    b = 2
    while b < CHUNK:
        lower_left = (ii // (2 * b) == jj // (2 * b)) & ((ii // b) % 2 == 1) & ((jj // b) % 2 == 0)
        t = t - _dot(_dot(t, jnp.where(lower_left, a, 0.0), 1, 0), t, 1, 0)
        b *= 2
    return t


@jax.custom_vjp
def _tri_inv(a):
    return _unit_lower_inverse(a)


def _tri_inv_fwd(a):
    t = _unit_lower_inverse(a)
    return t, t


def _tri_inv_bwd(t, g):
    return (-_dot(t, _dot(g, t, 1, 1), 0, 0),)


_tri_inv.defvjp(_tri_inv_fwd, _tri_inv_bwd)


@jax.custom_vjp
def _tri_inv_saved(a, t):
    return t


_tri_inv_saved.defvjp(lambda a, t: (t, t), lambda t, g: (_tri_inv_bwd(t, g)[0], jnp.zeros_like(t)))


def _gdn_chunk(q, k, v, z, b_col, a_col, a_row, al_col, dt_col, al_row, dt_row, nw, s, t_saved=None, emit_t=False):
    rows = q.shape[0]
    hb = rows // CHUNK
    ii = lax.broadcasted_iota(jnp.int32, (rows, rows), 0)
    jj = lax.broadcasted_iota(jnp.int32, (rows, rows), 1)
    same = (ii // CHUNK) == (jj // CHUNK)
    tri, strict, upper = same & (ii >= jj), same & (ii > jj), same & (ii <= jj)
    beta = _sigmoid(b_col)
    g_col = -jnp.exp(al_col) * _softplus(a_col + dt_col)
    g_row = -jnp.exp(al_row) * _softplus(a_row + dt_row)
    gc_col = jnp.sum(jnp.where(tri, g_row, 0.0), axis=1, keepdims=True)
    gc_row = jnp.sum(jnp.where(upper, g_col, 0.0), axis=0, keepdims=True)
    last = jj == (ii // CHUNK) * CHUNK + (CHUNK - 1)
    gl_col = jnp.sum(jnp.where(last, gc_row, 0.0), axis=1, keepdims=True)
    decay = jnp.where(tri, jnp.exp(jnp.where(tri, gc_col - gc_row, 0.0)), 0.0)
    qn = q * lax.rsqrt(jnp.sum(q * q, axis=-1, keepdims=True) + EPS) * (HEAD ** -0.5)
    kn = k * lax.rsqrt(jnp.sum(k * k, axis=-1, keepdims=True) + EPS)
    a_mat = jnp.where(strict, beta * _nt(kn, kn) * decay, 0.0)
    t_inv = _tri_inv(a_mat) if t_saved is None else _tri_inv_saved(a_mat, t_saved)
    eg = jnp.exp(gc_col)
    u = _nn(t_inv, v * beta)
    w = _nn(t_inv, kn * (beta * eg))
    qk = jnp.where(tri, _nt(qn, kn) * decay, 0.0)
    ri = lax.broadcasted_iota(jnp.int32, (rows, hb * HEAD), 0)
    ci = lax.broadcasted_iota(jnp.int32, (rows, hb * HEAD), 1)
    own = (ri // CHUNK) == (ci // HEAD)

    def spread(x):
        return jnp.where(own, jnp.concatenate([x] * hb, axis=1), 0.0)

    v_new = u - _nn(spread(w), s)
    o = _nn(spread(qn * eg), s) + _nn(qk, v_new)
    si = lax.broadcasted_iota(jnp.int32, (hb * HEAD, rows), 0)
    sj = lax.broadcasted_iota(jnp.int32, (hb * HEAD, rows), 1)
    gl_s = jnp.sum(jnp.where(sj == (si // HEAD) * CHUNK + (CHUNK - 1), gc_row, 0.0), axis=1, keepdims=True)
    s1 = s * jnp.exp(gl_s) + _tn(spread(kn * jnp.exp(gl_col - gc_col)), v_new)
    od = o * lax.rsqrt(jnp.mean(o * o, axis=-1, keepdims=True) + EPS) * nw * (z * _sigmoid(z))
    return (od, s1, t_inv) if emit_t else (od, s1)


def _gdn_heads_per_step(cfg):
    for hb in (GDN_HEADS_PER_STEP, 2, 1):
        if cfg.HD % hb == 0 and cfg.Z % (hb * HEAD) == 0:
            return hb


def _gdn_in_specs(cfg, nc, rev):
    hb = _gdn_heads_per_step(cfg)
    ng, rows = cfg.HD // hb, hb * CHUNK
    ci = (lambda n: nc - 1 - n) if rev else (lambda n: n)
    zo = cfg.Z // (hb * HEAD)
    blk = (CHUNK, hb * HEAD)
    col = pl.BlockSpec((1, 1, rows, 1), lambda g, n: (g, ci(n), 0, 0))
    row = pl.BlockSpec((1, 1, 1, rows), lambda g, n: (g, ci(n), 0, 0))
    hcol = pl.BlockSpec((1, rows, 1), lambda g, n: (g, 0, 0))
    hrow = pl.BlockSpec((1, 1, rows), lambda g, n: (g, 0, 0))
    return [
        pl.BlockSpec(blk, lambda g, n: (ci(n), g)), pl.BlockSpec(blk, lambda g, n: (ci(n), ng + g)),
        pl.BlockSpec(blk, lambda g, n: (ci(n), 2 * ng + g)), pl.BlockSpec(blk, lambda g, n: (ci(n), zo + g)),
        col, col, row, hcol, hcol, hrow, hrow, pl.BlockSpec((1, HEAD), lambda g, n: (0, 0)),
    ]


def _gdn_args(qkv, proj, pba, a_log, dt_bias, nw, cfg):
    hb, hd, s = _gdn_heads_per_step(cfg), cfg.HD, cfg.S
    ng, nc, rows = hd // hb, s // CHUNK, hb * CHUNK

    def grouped(x):
        return jnp.transpose(x.reshape(nc, CHUNK, ng, hb), (2, 0, 3, 1)).reshape(ng, nc, rows)

    b, a = grouped(pba[:, :hd]), grouped(pba[:, hd:2 * hd])
    per_row = lambda x: jnp.repeat(x.reshape(ng, hb), CHUNK, axis=1)
    al, dt = per_row(a_log), per_row(dt_bias)
    return (qkv, qkv, qkv, proj, b.reshape(ng, nc, rows, 1), a.reshape(ng, nc, rows, 1), a.reshape(ng, nc, 1, rows),
            al.reshape(ng, rows, 1), dt.reshape(ng, rows, 1), al.reshape(ng, 1, rows), dt.reshape(ng, 1, rows),
            nw.reshape(1, HEAD))


def _gdn_load(refs, hb):
    q, k, v, z, b, a, ar, alc, dtc, alr, dtr, nw = refs
    return (_stack_heads(q[...], hb), _stack_heads(k[...], hb), _stack_heads(v[...], hb), _stack_heads(z[...], hb),
            b[0, 0], a[0, 0], ar[0, 0], alc[0], dtc[0], alr[0], dtr[0], nw[...])


def _gdn_fwd(qkv, proj, pba, a_log, dt_bias, nw, cfg, side=None):
    nc, hb = cfg.S // CHUNK, _gdn_heads_per_step(cfg)
    ng = cfg.HD // hb
    grid = (ng, nc)

    def body(*refs):
        ins, (o_ref, st_ref, t_ref), (s_ref,), parts = _side_split(refs, 12, 3, side)
        begin, end = _side_steps(side, parts, grid)
        begin()

        @pl.when(pl.program_id(1) == 0)
        def _():
            s_ref[...] = jnp.zeros_like(s_ref)

        s0 = s_ref[...]
        st_ref[0, 0] = s0
        od, s1, t_inv = _gdn_chunk(*_gdn_load(ins, hb), s0, emit_t=True)
        t_ref[0, 0] = t_inv.astype(t_ref.dtype)
        for i in range(hb):
            o_ref[:, i * HEAD:(i + 1) * HEAD] = od[i * CHUNK:(i + 1) * CHUNK].astype(o_ref.dtype)
        s_ref[...] = s1
        end()

    s_in, s_out, s_shapes, s_sems, s_ops, _ = _side_call_args(side, 12, 3)
    rows = hb * CHUNK
    return pl.pallas_call(
        body, grid=grid, in_specs=_gdn_in_specs(cfg, nc, False) + s_in,
        out_specs=(pl.BlockSpec((CHUNK, hb * HEAD), lambda g, n: (n, g)),
                   pl.BlockSpec((1, 1, hb * HEAD, HEAD), lambda g, n: (g, n, 0, 0)),
                   pl.BlockSpec((1, 1, rows, rows), lambda g, n: (g, n, 0, 0)), *s_out),
        out_shape=(jax.ShapeDtypeStruct((cfg.S, cfg.HD * HEAD), BF16),
                   jax.ShapeDtypeStruct((ng, nc, hb * HEAD, HEAD), F32),
                   jax.ShapeDtypeStruct((ng, nc, rows, rows), BF16), *s_shapes),
        scratch_shapes=[pltpu.VMEM((hb * HEAD, HEAD), F32)] + s_sems,
        compiler_params=_cp("arbitrary" if side else "parallel", "arbitrary"),
        name="gdn_fwd_gather" if side else "gdn_fwd",
    )(*_gdn_args(qkv, proj, pba, a_log, dt_bias, nw, cfg), *s_ops)


def _gdn_bwd(qkv, proj, pba, a_log, dt_bias, nw, states, t_inv, dod, cfg, side=None):
    nc, hd, s, hb = cfg.S // CHUNK, cfg.HD, cfg.S, _gdn_heads_per_step(cfg)
    ng, rows = hd // hb, hb * CHUNK
    rv = lambda n: nc - 1 - n
    col = pl.BlockSpec((1, 1, rows, 1), lambda g, n: (g, rv(n), 0, 0))
    row = pl.BlockSpec((1, 1, 1, rows), lambda g, n: (g, rv(n), 0, 0))
    hcol = pl.BlockSpec((1, rows, 1), lambda g, n: (g, 0, 0))
    hrow = pl.BlockSpec((1, 1, rows), lambda g, n: (g, 0, 0))

    grid = (ng, nc)

    def body(*refs):
        ins, outs, (ds_ref,), parts = _side_split(refs, 15, 10, side)
        begin, end = _side_steps(side, parts, grid)
        begin()
        st_ref, t_ref, dod_ref = ins[12:15]
        dqkv_ref, dz_ref, db_ref, da_ref, dar_ref, dalc_ref, ddtc_ref, dalr_ref, ddtr_ref, dnw_ref = outs
        g, n = pl.program_id(0), pl.program_id(1)

        @pl.when(n == 0)
        def _():
            ds_ref[...] = jnp.zeros_like(ds_ref)
            for r in (dalc_ref, ddtc_ref, dalr_ref, ddtr_ref):
                r[...] = jnp.zeros_like(r)

        @pl.when((n == 0) & (g == 0))
        def _():
            dnw_ref[...] = jnp.zeros_like(dnw_ref)

        chunk = functools.partial(_gdn_chunk, t_saved=t_ref[0, 0].astype(F32))
        _, vjp = jax.vjp(chunk, *_gdn_load(ins[:12], hb), st_ref[0, 0])
        dq, dk, dv, dz, db, da, dar, dalc, ddtc, dalr, ddtr, dnw, ds0 = vjp(
            (_stack_heads(dod_ref[...].astype(F32), hb), ds_ref[...]))
        for i in range(hb):
            sl, rs = slice(i * HEAD, (i + 1) * HEAD), slice(i * CHUNK, (i + 1) * CHUNK)
            dqkv_ref[0, :, sl], dqkv_ref[1, :, sl], dqkv_ref[2, :, sl] = dq[rs], dk[rs], dv[rs]
            dz_ref[:, sl] = dz[rs].astype(dz_ref.dtype)
        db_ref[0, 0], da_ref[0, 0], dar_ref[0, 0] = db, da, dar
        dalc_ref[0] += dalc
        ddtc_ref[0] += ddtc
        dalr_ref[0] += dalr
        ddtr_ref[0] += ddtr
        dnw_ref[...] += dnw
        ds_ref[...] = ds0
        end()

    blk = pl.BlockSpec((CHUNK, hb * HEAD), lambda g, n: (rv(n), g))
    f32 = lambda *sh: jax.ShapeDtypeStruct(sh, F32)
    s_in, s_out, s_shapes, s_sems, s_ops, _ = _side_call_args(side, 15, 10)
    outs = pl.pallas_call(
        body, grid=grid,
        in_specs=_gdn_in_specs(cfg, nc, True) + [pl.BlockSpec((1, 1, hb * HEAD, HEAD), lambda g, n: (g, rv(n), 0, 0)),
                                                 pl.BlockSpec((1, 1, rows, rows), lambda g, n: (g, rv(n), 0, 0)), blk] + s_in,
        out_specs=(pl.BlockSpec((3, CHUNK, hb * HEAD), lambda g, n: (0, rv(n), g)), blk, col, col, row,
                   hcol, hcol, hrow, hrow, pl.BlockSpec((1, HEAD), lambda g, n: (0, 0)), *s_out),
        out_shape=(f32(3, s, hd * HEAD), jax.ShapeDtypeStruct((s, hd * HEAD), BF16),
                   f32(ng, nc, rows, 1), f32(ng, nc, rows, 1), f32(ng, nc, 1, rows),
                   f32(ng, rows, 1), f32(ng, rows, 1), f32(ng, 1, rows), f32(ng, 1, rows), f32(1, HEAD), *s_shapes),
        scratch_shapes=[pltpu.VMEM((hb * HEAD, HEAD), F32)] + s_sems,
        compiler_params=_cp("arbitrary", "arbitrary"), name="gdn_bwd_scatter" if side else "gdn_bwd",
    )(*_gdn_args(qkv, proj, pba, a_log, dt_bias, nw, cfg), states, t_inv, dod, *s_ops)
    dqkv, dz, db, da, dar, dalc, ddtc, dalr, ddtr, dnw = outs[:10]
    side_outs = list(outs[10:])

    def ungrouped(x):
        return jnp.transpose(x.reshape(ng, nc, hb, CHUNK), (1, 3, 0, 2)).reshape(s, hd)

    per_head = lambda c, r: (c.reshape(ng, hb, CHUNK) + r.reshape(ng, hb, CHUNK)).sum(axis=-1).reshape(hd)
    d_b = ungrouped(db.reshape(ng, nc, rows))
    d_a = ungrouped(da.reshape(ng, nc, rows) + dar.reshape(ng, nc, rows))
    dba = jnp.concatenate([d_b, d_a, jnp.zeros((s, LANES - 2 * hd), F32)], axis=1).astype(BF16)
    return dqkv, dz, dba, per_head(dalc, dalr), per_head(ddtc, ddtr), dnw.reshape(HEAD), side_outs


def _adamw(w, g, m, v, *, name):
    nl, r, c = w.shape
    gc = g.shape[2]
    tr = _tile(r, max(8, (1 << 18) // c // 8 * 8), 8) if r % 8 == 0 else r
    c1, c2 = 1.0 - ADAM_B1 ** ADAM_STEP, 1.0 - ADAM_B2 ** ADAM_STEP

    def body(w_ref, g_ref, m_ref, v_ref, go_ref, d_ref, nm_ref, nv_ref):
        gv = g_ref[:, :c]
        nm = ADAM_B1 * m_ref[...] + (1.0 - ADAM_B1) * gv
        nv = ADAM_B2 * v_ref[...] + (1.0 - ADAM_B2) * (gv * gv)
        go_ref[...] = gv
        d_ref[...] = -ADAM_LR * ((nm / c1) / (jnp.sqrt(nv / c2) + ADAM_EPS) + ADAM_WD * w_ref[...])
        nm_ref[...] = nm
        nv_ref[...] = nv

    spec = pl.BlockSpec((None, tr, c), lambda l, i: (l, i, 0))
    gspec = pl.BlockSpec((None, tr, gc), lambda l, i: (l, i, 0))
    out = jax.ShapeDtypeStruct((nl, r, c), F32)
    return pl.pallas_call(body, grid=(nl, r // tr), in_specs=[spec, gspec, spec, spec], out_specs=(spec,) * 4,
                          out_shape=(out,) * 4, compiler_params=_cp("parallel", "parallel"), name=name)(w, g, m, v)


def _sum_leading(a, *, name):
    n, r, c = a.shape
    tr = _tile(r, SUM_TR, 8)

    def body(a_ref, o_ref):
        acc = a_ref[0].astype(F32)
        for i in range(1, n):
            acc = acc + a_ref[i].astype(F32)
        o_ref[...] = acc

    return pl.pallas_call(body, grid=(r // tr,), in_specs=[pl.BlockSpec((n, tr, c), lambda i: (0, i, 0))],
                          out_specs=pl.BlockSpec((tr, c), lambda i: (i, 0)),
                          out_shape=jax.ShapeDtypeStruct((r, c), F32), compiler_params=_cp("parallel"), name=name)(a)


class _Mat:
    def __init__(self, kind, rows, cols):
        self.kind, self.rows, self.cols = kind, rows, cols
        self.full = (rows, 4 * cols) if kind == "col" else (4 * rows, cols)
        self.half = rows // 2

    def block(self, ref, j, c=None):
        r0 = (0 if self.kind == "col" else j * self.rows) + (0 if c is None else c * self.half)
        rows = pl.ds(_aligned(r0, 16), self.rows if c is None else self.half)
        return ref.at[rows, pl.ds(_aligned(j * self.cols, LANES), self.cols)] if self.kind == "col" else ref.at[rows, :]

    def own_half(self, ref, c):
        return ref.at[pl.ds(_aligned(c * self.half, 16), self.half), :]


def _aligned(v, unit):
    return v if isinstance(v, int) else pl.multiple_of(v, unit)


def _place():
    x, y, c = lax.axis_index("x"), lax.axis_index("y"), lax.axis_index("c")
    chips = [(1 - x, y), (x, 1 - y), (1 - x, 1 - y)]
    return x, y, c, chips


def _chip_index():
    return 2 * lax.axis_index("x") + lax.axis_index("y")


def _remote(src, dst, send_sems, recv_sems, k, to):
    return pltpu.make_async_remote_copy(src_ref=src, dst_ref=dst, send_sem=send_sems.at[k], recv_sem=recv_sems.at[k],
                                        device_id=to, device_id_type=MESH)


def _comm_call(body, ins, out_shapes, n_sems, name):
    return pl.pallas_call(
        body, out_shape=tuple(out_shapes), in_specs=[ANY] * len(ins), out_specs=tuple([ANY] * len(out_shapes)),
        scratch_shapes=[pltpu.SemaphoreType.DMA((n_sems,)), pltpu.SemaphoreType.DMA((n_sems,))],
        compiler_params=pltpu.CompilerParams(has_side_effects=True), name=name)(*ins)


def _gather_copies(own_refs, full_refs, send_sems, recv_sems, mats, base=0):
    x, y, c, chips = _place()
    me = 2 * x + y
    return [(_remote(m.own_half(own_refs[k], c), m.block(full_refs[k], me, c), send_sems, recv_sems, base + 3 * k + j, (cx, cy, c)),
             _remote(m.block(full_refs[k], 2 * cx + cy, c), m.block(full_refs[k], 2 * cx + cy, c), send_sems, recv_sems,
                     base + 3 * k + j, (cx, cy, c)))
            for k, m in enumerate(mats) for j, (cx, cy) in enumerate(chips)]


def _pass_copies(full_refs, send_sems, recv_sems, mats, base=0):
    x, y, c, chips = _place()
    sib = (x, y, 1 - c)
    return [(_remote(m.block(full_refs[k], 2 * cx + cy, c), m.block(full_refs[k], 2 * cx + cy, c), send_sems, recv_sems,
                     base + 3 * k + j, sib),
             _remote(m.block(full_refs[k], 2 * cx + cy, 1 - c), m.block(full_refs[k], 2 * cx + cy, 1 - c), send_sems,
                     recv_sems, base + 3 * k + j, sib))
            for k, m in enumerate(mats) for j, (cx, cy) in enumerate(chips)]


def _start_all(pairs):
    for cp, _ in pairs:
        cp.start()


def _finish_all(pairs):
    for _, landing in pairs:
        landing.wait_recv()
    for cp, _ in pairs:
        cp.wait_send()


def _gather_ici_side(own, mats):
    return _Side(own, [jax.ShapeDtypeStruct(m.full, a.dtype) for m, a in zip(mats, own)], 3 * len(mats),
                 lambda i, o, ss, rs: _start_all(_gather_copies(i, o, ss, rs, mats)),
                 lambda i, o, ss, rs: _finish_all(_gather_copies(i, o, ss, rs, mats)))


def _gather_pass_side(fulls, mats):
    return _Side(fulls, [jax.ShapeDtypeStruct(f.shape, f.dtype) for f in fulls], 3 * len(mats),
                 lambda i, o, ss, rs: _start_all(_pass_copies(o, ss, rs, mats)),
                 lambda i, o, ss, rs: _finish_all(_pass_copies(o, ss, rs, mats)), in_place=True)


def _place_own(fulls, own, mats):
    me = _chip_index().astype(jnp.int32).reshape(1)

    def place(full, blk, m):
        tr = _tile(m.rows, 512, 16)
        nr = m.rows // tr
        window = (lambda i, me: (i, me[0])) if m.kind == "col" else (lambda i, me: (me[0] * nr + i, 0))

        def body(me_ref, b_ref, f_ref, o_ref):
            o_ref[...] = b_ref[...]

        return pl.pallas_call(
            body,
            grid_spec=pltpu.PrefetchScalarGridSpec(
                num_scalar_prefetch=1, grid=(nr,),
                in_specs=[pl.BlockSpec((tr, m.cols), lambda i, me: (i, 0)), ANY],
                out_specs=pl.BlockSpec((tr, m.cols), window)),
            out_shape=jax.ShapeDtypeStruct(full.shape, full.dtype), input_output_aliases={2: 0},
            compiler_params=_cp("parallel"), name="place_own")(me, blk, full)

    return [place(o, a, m) for o, a, m in zip(fulls, own, mats)]


def _gather_weights(own, mats):
    n = len(mats)

    def body(*refs):
        own_refs, full_refs, (send_sems, recv_sems) = refs[:n], refs[n:2 * n], refs[2 * n:]
        first = _gather_copies(own_refs, full_refs, send_sems, recv_sems, mats)
        second = _pass_copies(full_refs, send_sems, recv_sems, mats, base=3 * n)
        _start_all(first)
        for (_, landed), (fwd, _) in zip(first, second):
            landed.wait_recv()
            fwd.start()
        for _, landing in second:
            landing.wait_recv()
        for cp, _ in first + second:
            cp.wait_send()

    outs = _comm_call(body, own, [jax.ShapeDtypeStruct(m.full, a.dtype) for m, a in zip(mats, own)], 6 * n,
                      "gather_weights")
    return _place_own(outs, own, mats)


def _swap_copies(g_refs, b_refs, send_sems, recv_sems, mats):
    x, y, c, _ = _place()
    cps = []
    for g, b, m in zip(g_refs, b_refs, mats):
        if m.kind == "col":
            cps.append(_remote(m.own_half(g, 1 - c), b, send_sems, recv_sems, len(cps), (x, y, 1 - c)))
        else:
            for j in range(4):
                cps.append(_remote(m.block(g, j, 1 - c), b.at[j], send_sems, recv_sems, len(cps), (x, y, 1 - c)))
    return [(cp, cp) for cp in cps]


def _swap_shapes(mats):
    return [jax.ShapeDtypeStruct((m.half, 4 * m.cols) if m.kind == "col" else (4, m.half, m.cols), F32) for m in mats]


def _swap_side(grads, mats):
    return _Side(grads, _swap_shapes(mats), sum(1 if m.kind == "col" else 4 for m in mats),
                 lambda i, o, ss, rs: _start_all(_swap_copies(i, o, ss, rs, mats)),
                 lambda i, o, ss, rs: _finish_all(_swap_copies(i, o, ss, rs, mats)))


def _swap_halves(grads, mats):
    def body(*refs):
        n = len(mats)
        pairs = _swap_copies(refs[:n], refs[n:2 * n], *refs[2 * n:], mats)
        _start_all(pairs)
        _finish_all(pairs)

    return _comm_call(body, grads, _swap_shapes(mats), sum(1 if m.kind == "col" else 4 for m in mats), "rs_swap_halves")


def _pair_sum(own, recv, m, cidx):
    tr = _tile(m.half, 256, 16)
    nh = m.half // tr
    if m.kind == "col":
        own_spec = pl.BlockSpec((tr, m.cols), lambda j, i, c: (c[0] * nh + i, j))
        recv_spec = pl.BlockSpec((tr, m.cols), lambda j, i, c: (i, j))
    else:
        own_spec = pl.BlockSpec((tr, m.cols), lambda j, i, c: ((2 * j + c[0]) * nh + i, 0))
        recv_spec = pl.BlockSpec((None, tr, m.cols), lambda j, i, c: (j, i, 0))

    def body(c_ref, a_ref, b_ref, o_ref):
        o_ref[...] = (a_ref[...] + b_ref[...]).astype(o_ref.dtype)

    return pl.pallas_call(
        body,
        grid_spec=pltpu.PrefetchScalarGridSpec(
            num_scalar_prefetch=1, grid=(4, nh), in_specs=[own_spec, recv_spec],
            out_specs=pl.BlockSpec((None, tr, m.cols), lambda j, i, c: (j, i, 0))),
        out_shape=jax.ShapeDtypeStruct((4, m.half, m.cols), BF16),
        compiler_params=_cp("parallel", "parallel"), name="rs_pair_sum")(cidx, own, recv)


def _scatter_copies(s_refs, b_refs, send_sems, recv_sems):
    x, y, c, chips = _place()
    me = 2 * x + y
    return [(_remote(s.at[2 * cx + cy], b.at[me], send_sems, recv_sems, 3 * k + j, (cx, cy, c)),
             _remote(b.at[2 * cx + cy], b.at[2 * cx + cy], send_sems, recv_sems, 3 * k + j, (cx, cy, c)))
            for k, (s, b) in enumerate(zip(s_refs, b_refs)) for j, (cx, cy) in enumerate(chips)]


def _scatter_side(sps):
    return _Side(sps, [jax.ShapeDtypeStruct(s.shape, s.dtype) for s in sps], 3 * len(sps),
                 lambda i, o, ss, rs: _start_all(_scatter_copies(i, o, ss, rs)),
                 lambda i, o, ss, rs: _finish_all(_scatter_copies(i, o, ss, rs)))


def _scatter_chips(sps):
    def body(*refs):
        n = len(sps)
        pairs = _scatter_copies(refs[:n], refs[n:2 * n], *refs[2 * n:])
        _start_all(pairs)
        _finish_all(pairs)

    return _comm_call(body, sps, [jax.ShapeDtypeStruct(s.shape, s.dtype) for s in sps], 3 * len(sps), "rs_scatter_chips")


def _sum_chips(got, sp, me, cidx):
    _, h, w = got.shape
    tr = _tile(h, 256, 16)
    nh = h // tr

    def body(me_ref, c_ref, g_ref, s_ref, o_ref):
        acc = None
        for s in range(4):
            term = jnp.where(me_ref[0] == s, s_ref[...], g_ref[s]).astype(F32)
            acc = term if acc is None else acc + term
        o_ref[...] = acc

    return pl.pallas_call(
        body,
        grid_spec=pltpu.PrefetchScalarGridSpec(
            num_scalar_prefetch=2, grid=(nh,),
            in_specs=[pl.BlockSpec((4, tr, w), lambda i, me, c: (0, i, 0)),
                      pl.BlockSpec((None, tr, w), lambda i, me, c: (me[0], i, 0))],
            out_specs=pl.BlockSpec((tr, w), lambda i, me, c: (c[0] * nh + i, 0))),
        out_shape=jax.ShapeDtypeStruct((2 * h, w), F32), compiler_params=_cp("parallel"), name="rs_sum_chips",
    )(me, cidx, got, sp)


def _share_halves(fs):
    n = len(fs)

    def body(*refs):
        f_refs, (send_sems, recv_sems) = refs[n:2 * n], refs[2 * n:]
        x, y, c, _ = _place()
        cps = []
        for k, f in enumerate(f_refs):
            h = f.shape[0] // 2
            mine, theirs = f.at[pl.ds(_aligned(c * h, 16), h), :], f.at[pl.ds(_aligned((1 - c) * h, 16), h), :]
            cps.append((_remote(mine, mine, send_sems, recv_sems, k, (x, y, 1 - c)),
                        _remote(theirs, theirs, send_sems, recv_sems, k, (x, y, 1 - c))))
        _start_all(cps)
        _finish_all(cps)

    return pl.pallas_call(
        body, out_shape=tuple(jax.ShapeDtypeStruct(f.shape, f.dtype) for f in fs), in_specs=[ANY] * n,
        out_specs=tuple([ANY] * n), input_output_aliases={k: k for k in range(n)},
        scratch_shapes=[pltpu.SemaphoreType.DMA((n,)), pltpu.SemaphoreType.DMA((n,))],
        compiler_params=pltpu.CompilerParams(has_side_effects=True), name="rs_share_halves")(*fs)


def _gather_all(p):
    r, w = p.shape
    rel = [(dx, dy, dc) for dx in (0, 1) for dy in (0, 1) for dc in (0, 1)][1:]

    def body(p_ref, g_ref, send_sems, recv_sems):
        x, y, c = lax.axis_index("x"), lax.axis_index("y"), lax.axis_index("c")
        me = 4 * x + 2 * y + c
        g_ref[me] = p_ref[...]
        peers = [(jnp.bitwise_xor(x, dx), jnp.bitwise_xor(y, dy), jnp.bitwise_xor(c, dc)) for dx, dy, dc in rel]
        cps = [_remote(p_ref, g_ref.at[me], send_sems, recv_sems, k, peer) for k, peer in enumerate(peers)]
        for cp in cps:
            cp.start()
        for k, (px, py, pc) in enumerate(peers):
            slot = g_ref.at[4 * px + 2 * py + pc]
            _remote(slot, slot, send_sems, recv_sems, k, (px, py, pc)).wait_recv()
        for cp in cps:
            cp.wait_send()

    return pl.pallas_call(
        body, out_shape=jax.ShapeDtypeStruct((8, r, w), p.dtype),
        in_specs=[pl.BlockSpec(memory_space=pltpu.VMEM)], out_specs=pl.BlockSpec(memory_space=pltpu.VMEM),
        scratch_shapes=[pltpu.SemaphoreType.DMA((7,)), pltpu.SemaphoreType.DMA((7,))],
        compiler_params=pltpu.CompilerParams(has_side_effects=True, vmem_limit_bytes=VMEM_LIMIT),
        name=("gather_all_%dx%d" % (r, w)))(p)


def _pack_small(arrs):
    flat = jnp.concatenate([a.reshape(-1) for a in arrs])
    rows = -(-flat.shape[0] // (SMALL_W * 8)) * 8
    return jnp.pad(flat, (0, rows * SMALL_W - flat.shape[0])).reshape(rows, SMALL_W)


def _unpack(buf, shapes):
    flat = buf.reshape(-1)
    out, off = [], 0
    for sh in shapes:
        n = math.prod(sh)
        out.append(flat[off:off + n].reshape(sh))
        off += n
    return out


def _chip_padded(pieces, width, padded):
    out, used = [], 0
    for p in pieces:
        at = 0
        while at < p.shape[1]:
            take = min(p.shape[1] - at, width - used)
            out.append(p[:, at:at + take])
            at, used = at + take, used + take
            if used == width:
                out.append(jnp.zeros((p.shape[0], padded - width), p.dtype))
                used = 0
    assert used == 0
    return jnp.concatenate(out, axis=1)


class _Cfg:
    def __init__(self, x, w_in, attn_sinks, dn_a_log, w_ffn_up):
        self.S, self.D = x.shape[1], x.shape[2]
        self.L = w_in.shape[0]
        self.HA, self.HD = attn_sinks.shape[1], dn_a_log.shape[1]
        self.IN = 4 * w_in.shape[2]
        kvw = self.IN - self.HA * HEAD - 4 * self.HD * HEAD - 2 * self.HD - 2 * self.D
        self.KV = kvw // (2 * HEAD)
        self.G = self.HA // self.KV
        self.FF = 2 * w_ffn_up.shape[2]
        self.KA = self.HA * HEAD
        self.VA = self.KA + self.KV * HEAD
        self.QD = self.VA + self.KV * HEAD
        self.Z = self.QD + 3 * self.HD * HEAD
        self.GA = self.Z + self.HD * HEAD
        self.GD = self.GA + self.D
        self.NB = self.GD + self.D
        assert self.NB + 2 * self.HD == self.IN and self.G <= 8 and 2 * self.HD <= LANES


def kernel(x, positions, norm_mix_pre, w_in, attn_sinks, dn_conv_w, dn_a_log, dn_dt_bias, dn_norm_w, w_branch_attn, w_branch_dn, w_out, norm_mix_post, norm_ffn_pre, w_ffn_up, ffn_conv_w, ffn_conv_b, w_ffn_down, norm_ffn_post, loss_target, m_norm_mix_pre, m_w_in, m_attn_sinks, m_dn_conv_w, m_dn_a_log, m_dn_dt_bias, m_dn_norm_w, m_w_branch_attn, m_w_branch_dn, m_w_out, m_norm_mix_post, m_norm_ffn_pre, m_w_ffn_up, m_ffn_conv_w, m_ffn_conv_b, m_w_ffn_down, m_norm_ffn_post, v_norm_mix_pre, v_w_in, v_attn_sinks, v_dn_conv_w, v_dn_a_log, v_dn_dt_bias, v_dn_norm_w, v_w_branch_attn, v_w_branch_dn, v_w_out, v_norm_mix_post, v_norm_ffn_pre, v_w_ffn_up, v_ffn_conv_w, v_ffn_conv_b, v_w_ffn_down, v_norm_ffn_post):
    cfg = _Cfg(x, w_in, attn_sinks, dn_a_log, w_ffn_up)
    S, D, L, HD = cfg.S, cfg.D, cfg.L, cfg.HD
    o1 = cfg.Z
    ins = w_in.shape[2]
    insp = -(-ins // SHARD_PAD) * SHARD_PAD
    mats = [_Mat("col", D, insp), _Mat("row", *w_branch_attn.shape[1:]), _Mat("row", *w_branch_dn.shape[1:]),
            _Mat("row", *w_out.shape[1:]), _Mat("col", *w_ffn_up.shape[1:]), _Mat("row", *w_ffn_down.shape[1:])]
    me = _chip_index()

    taps = _gather_all(_pack_small([dn_conv_w, ffn_conv_w]))
    tap_shapes = [dn_conv_w.shape, ffn_conv_w.shape]
    per_chip = [_unpack(taps[2 * j], tap_shapes) for j in range(4)]
    dnw_all = jnp.concatenate([per_chip[j][0] for j in range(4)], axis=2)
    fcw_all = jnp.concatenate([per_chip[j][1] for j in range(4)], axis=2)

    def own_blocks(l):
        return [jnp.pad(w_in[l].astype(BF16), ((0, 0), (0, insp - ins))), w_branch_attn[l].astype(BF16),
                w_branch_dn[l].astype(BF16), w_out[l].astype(BF16), w_ffn_up[l].astype(BF16), w_ffn_down[l].astype(BF16)]

    def layer_weights(l, fulls):
        win_sm, wbra, wbrd, wout, wup, wdown = fulls
        win = jnp.concatenate([win_sm[:, j * insp:j * insp + ins] for j in range(4)], axis=1)
        return dict(
            win_sm=win_sm, wbig=jnp.concatenate([win[:, :o1], win[:, o1 + 2 * HD:]], axis=1),
            wba=jnp.pad(win[:, o1:o1 + 2 * HD], ((0, 0), (0, LANES - 2 * HD))),
            wbra=wbra, wbrd=wbrd, wout=wout, wup=wup, wdown=wdown, dnw=dnw_all[l], fcw=fcw_all[l])

    weights = [layer_weights(0, _gather_weights(own_blocks(0), mats))] + [None] * (L - 1)

    inv_freq = ROPE_THETA ** (-jnp.arange(0, 2 * ROT_HALF, 2, dtype=F32) / (2 * ROT_HALF))
    ang = positions.reshape(S, 1).astype(F32) * inv_freq
    cosf = jnp.concatenate([jnp.cos(ang), jnp.cos(ang), jnp.ones((S, HEAD - 2 * ROT_HALF), F32)], axis=1)
    sinf = jnp.concatenate([-jnp.sin(ang), jnp.sin(ang), jnp.zeros((S, HEAD - 2 * ROT_HALF), F32)], axis=1)

    h = x.reshape(S, D)
    saved = []
    for l in range(L):
        wt = weights[l]
        t = dict(h0=h)
        t["xn"] = _rms_fwd(h, norm_mix_pre[l], out_dtype=BF16, name="norm_mix_pre")
        nxt = own_blocks(l + 1) if l + 1 < L else None
        if nxt:
            t["proj"], win_landed = _mm(t["xn"], wt["wbig"], name="proj_gather", side=_gather_ici_side(nxt[:1], mats[:1]))
        else:
            t["proj"] = _mm(t["xn"], wt["wbig"], name="proj")
        t["pba"] = _mm(t["xn"], wt["wba"], name="proj_ba")
        t["oa"] = _attn_fwd(t["proj"], cosf, sinf, attn_sinks[l], cfg)
        t["qkv"] = _dnconv_fwd(t["proj"], wt["dnw"], cfg)
        t["od"], t["st"], t["tinv"], *landed = _gdn_fwd(t["qkv"], t["proj"], t["pba"], dn_a_log[l], dn_dt_bias[l], dn_norm_w[l], cfg,
                                             side=_gather_ici_side(nxt[1:], mats[1:]) if nxt else None)
        t["pa"] = _mm(t["oa"], wt["wbra"], name="branch_attn")
        t["pd"] = _mm(t["od"], wt["wbrd"], name="branch_dn")
        t["y"] = _merge_fwd(t["proj"], t["pa"], t["pd"], cfg)
        t["mix"] = _mm(t["y"], wt["wout"], name="mix_out")
        t["h1"] = _rms_fwd(t["mix"], norm_mix_post[l], res=h, out_dtype=F32, name="norm_mix_post")
        t["xf"] = _rms_fwd(t["h1"], norm_ffn_pre[l], out_dtype=BF16, name="norm_ffn_pre")
        if nxt:
            t["up"], *landed = _mm(t["xf"], wt["wup"], name="ffn_up_pass",
                                   side=_gather_pass_side([win_landed] + landed, mats))
            weights[l + 1] = layer_weights(l + 1, _place_own(landed, nxt, mats))
        else:
            t["up"] = _mm(t["xf"], wt["wup"], name="ffn_up")
        t["act"] = _glu_fwd(t["up"], wt["fcw"], ffn_conv_b[l], cfg)
        t["f"] = _mm(t["act"], wt["wdown"], name="ffn_down")
        h = _rms_fwd(t["f"], norm_ffn_post[l], res=t["h1"], out_dtype=F32, name="norm_ffn_post")
        saved.append(t)

    loss_local, dh = _loss_head(h, loss_target.reshape(S, D))
    loss = lax.psum(loss_local, ("x", "y", "c"))

    cidx = lax.axis_index("c").astype(jnp.int32).reshape(1)
    midx = me.astype(jnp.int32).reshape(1)
    big_grads, small = [None] * L, [None] * L
    reduce_blocks = lambda got, pair: _share_halves([_sum_chips(a, b, midx, cidx) for a, b in zip(got, pair)])
    pair_sums = lambda full, theirs: [_pair_sum(a, b, m, cidx) for a, b, m in zip(full, theirs, mats)]
    unswapped = waiting = None
    early = [4]
    late = [k for k in range(len(mats)) if k not in early]
    pick = lambda xs, ks: [xs[k] for k in ks]
    for l in reversed(range(L)):
        wt, t = weights[l], saved[l]
        df, g_nfpost = _rms_bwd(t["f"], norm_ffn_post[l], dh, out_dtype=BF16, name="norm_ffn_post_bwd")
        dact = _mm(df, wt["wdown"], tb=True, name="ffn_down_dx")
        if unswapped:
            g_wdown, their_win = _mm(t["act"], df, ta=True, name="ffn_down_dw_swap",
                                     side=_swap_side(unswapped[1][:1], mats[:1]))
        else:
            g_wdown = _mm(t["act"], df, ta=True, name="ffn_down_dw")
        dup, g_fcw, g_fcb = _glu_bwd(t["up"], dact, wt["fcw"], ffn_conv_b[l], cfg)
        if unswapped:
            dxf, *theirs = _mm(dup, wt["wup"], tb=True, name="ffn_up_dx_swap",
                               side=_swap_side(unswapped[1][1:], mats[1:]))
            waiting = (unswapped[0], pair_sums(unswapped[1], [their_win] + theirs))
            g_wup, *got_early = _mm(t["xf"], dup, ta=True, name="ffn_up_dw_scatter",
                                    side=_scatter_side(pick(waiting[1], early)))
        else:
            dxf = _mm(dup, wt["wup"], tb=True, name="ffn_up_dx")
            g_wup = _mm(t["xf"], dup, ta=True, name="ffn_up_dw")
        dh1, g_nfp = _rms_bwd(t["h1"], norm_ffn_pre[l], dxf, add=dh, name="norm_ffn_pre_bwd")
        dmix, g_nmpost = _rms_bwd(t["mix"], norm_mix_post[l], dh1, out_dtype=BF16, name="norm_mix_post_bwd")
        dy = _mm(dmix, wt["wout"], tb=True, name="mix_out_dx")
        g_wout = _mm(t["y"], dmix, ta=True, name="mix_out_dw")
        dpa, dpd, dga, dgd = _merge_bwd(t["proj"], t["pa"], t["pd"], dy, cfg)
        doa = _mm(dpa, wt["wbra"], tb=True, out_dtype=BF16, name="branch_attn_dx")
        g_wbra = _mm(t["oa"], dpa, ta=True, name="branch_attn_dw")
        dod = _mm(dpd, wt["wbrd"], tb=True, out_dtype=BF16, name="branch_dn_dx")
        g_wbrd = _mm(t["od"], dpd, ta=True, name="branch_dn_dw")
        dqkv_c, dz, dba, g_alog, g_dt, g_dnnorm, got = _gdn_bwd(
            t["qkv"], t["proj"], t["pba"], dn_a_log[l], dn_dt_bias[l], dn_norm_w[l], t["st"], t["tinv"], dod, cfg,
            side=_scatter_side(pick(waiting[1], late)) if waiting else None)
        if waiting:
            landed = dict(zip(early + late, got_early + got))
            big_grads[waiting[0]] = reduce_blocks([landed[k] for k in range(len(mats))], waiting[1])
            waiting = None
        dqkv_d, g_dnw = _dnconv_bwd(t["proj"], dqkv_c, wt["dnw"], cfg)
        dqkv_a, g_sinks = _attn_bwd(t["proj"], doa, cosf, sinf, attn_sinks[l], cfg)
        dproj = _chip_padded([dqkv_a, dqkv_d, dba[:, :2 * HD], dz, dga, dgd], ins, insp)
        dxn = _mm(dproj, wt["win_sm"], tb=True, name="proj_dx")
        g_win = _mm(t["xn"], dproj, ta=True, name="proj_dw")
        dh, g_nmp = _rms_bwd(t["h0"], norm_mix_pre[l], dxn, add=dh1, name="norm_mix_pre_bwd")

        unswapped = (l, [g_win, g_wbra, g_wbrd, g_wout, g_wup, g_wdown])
        small[l] = [g_nmp, g_sinks, g_alog, g_dt, g_dnnorm, g_nmpost, g_nfp, g_fcb, g_nfpost, g_dnw, g_fcw]

    last_pair = pair_sums(unswapped[1], _swap_halves(unswapped[1], mats))
    big_grads[unswapped[0]] = reduce_blocks(_scatter_chips(last_pair), last_pair)
    grad_x = dh.reshape(1, S, D)

    small_shapes = [a.shape for a in small[0]]
    summed = _sum_leading(_gather_all(_pack_small([a for l in range(L) for a in small[l]])), name="sum_devices")
    per_layer = _unpack(summed, small_shapes * L)
    red = [jnp.stack([per_layer[l * len(small_shapes) + i] for l in range(L)]) for i in range(len(small_shapes))]
    g_dn_conv = lax.dynamic_slice_in_dim(red[9], me * dn_conv_w.shape[2], dn_conv_w.shape[2], axis=2)
    g_ffn_conv = lax.dynamic_slice_in_dim(red[10], me * ffn_conv_w.shape[2], ffn_conv_w.shape[2], axis=2)
    stacked = lambda i: jnp.stack([big_grads[l][i] for l in range(L)])

    grads = dict(
        norm_mix_pre=red[0], w_in=stacked(0), attn_sinks=red[1], dn_conv_w=g_dn_conv, dn_a_log=red[2],
        dn_dt_bias=red[3], dn_norm_w=red[4], w_branch_attn=stacked(1), w_branch_dn=stacked(2), w_out=stacked(3),
        norm_mix_post=red[5], norm_ffn_pre=red[6], w_ffn_up=stacked(4), ffn_conv_w=g_ffn_conv, ffn_conv_b=red[7],
        w_ffn_down=stacked(5), norm_ffn_post=red[8])

    params = dict(
        norm_mix_pre=(norm_mix_pre, m_norm_mix_pre, v_norm_mix_pre), w_in=(w_in, m_w_in, v_w_in),
        attn_sinks=(attn_sinks, m_attn_sinks, v_attn_sinks), dn_conv_w=(dn_conv_w, m_dn_conv_w, v_dn_conv_w),
        dn_a_log=(dn_a_log, m_dn_a_log, v_dn_a_log), dn_dt_bias=(dn_dt_bias, m_dn_dt_bias, v_dn_dt_bias),
        dn_norm_w=(dn_norm_w, m_dn_norm_w, v_dn_norm_w),
        w_branch_attn=(w_branch_attn, m_w_branch_attn, v_w_branch_attn),
        w_branch_dn=(w_branch_dn, m_w_branch_dn, v_w_branch_dn), w_out=(w_out, m_w_out, v_w_out),
        norm_mix_post=(norm_mix_post, m_norm_mix_post, v_norm_mix_post),
        norm_ffn_pre=(norm_ffn_pre, m_norm_ffn_pre, v_norm_ffn_pre), w_ffn_up=(w_ffn_up, m_w_ffn_up, v_w_ffn_up),
        ffn_conv_w=(ffn_conv_w, m_ffn_conv_w, v_ffn_conv_w), ffn_conv_b=(ffn_conv_b, m_ffn_conv_b, v_ffn_conv_b),
        w_ffn_down=(w_ffn_down, m_w_ffn_down, v_w_ffn_down),
        norm_ffn_post=(norm_ffn_post, m_norm_ffn_post, v_norm_ffn_post))
    names = list(params)

    grad_out, delta, new_m, new_v = {}, {}, {}, {}
    small_names = [n for n in names if params[n][0].ndim == 2]
    for n in names:
        if n in small_names:
            continue
        w, m, v = params[n]
        grad_out[n], delta[n], new_m[n], new_v[n] = _adamw(w, grads[n], m, v, name="adamw_" + n)

    outs = _adamw(*[_pack_small([params[n][i] if i is not None else grads[n] for n in small_names])[None]
                    for i in (0, None, 1, 2)], name="adamw_small")
    shapes = [params[n][0].shape for n in small_names]
    for dst, o in zip((grad_out, delta, new_m, new_v), outs):
        for n, a in zip(small_names, _unpack(o, shapes)):
            dst[n] = a

    return (loss, grad_x, *[grad_out[n] for n in names], *[delta[n] for n in names],
            *[new_m[n] for n in names], *[new_v[n] for n in names])
```

```python
import functools
import math

import jax
import jax.numpy as jnp
from jax import lax
from jax.experimental import pallas as pl
from jax.experimental.pallas import tpu as pltpu

F32, BF16 = jnp.float32, jnp.bfloat16
EPS = 1e-6
HEAD = 128
ATTN_BLOCK = 128
ROT_HALF = 16
ROPE_THETA = 500000.0
CHUNK = 64
DN_CONV, FFN_CONV = 4, 3
GDN_HEADS_PER_STEP = 4
MM_TILES = (512, 1024, 3968)
HALO = 8
LANES = 128
SUM_TR = 512
SHARD_PAD = 1024
SMALL_W = 1024
ADAM_LR, ADAM_B1, ADAM_B2, ADAM_EPS, ADAM_WD, ADAM_STEP = 0.001, 0.9, 0.999, 1e-08, 0.01, 10
VMEM_LIMIT = 56 * 1024 * 1024
MESH = pl.DeviceIdType.MESH
ANY = pl.BlockSpec(memory_space=pl.ANY)


def _cp(*sem):
    return pltpu.CompilerParams(dimension_semantics=sem, vmem_limit_bytes=VMEM_LIMIT)


def _tile(n, pref, unit=LANES):
    if n <= pref:
        return n
    t = (pref // unit) * unit
    while t >= unit:
        if n % t == 0:
            return t
        t -= unit
    raise ValueError((n, pref, unit))


def _sigmoid(x):
    return 1.0 / (1.0 + jnp.exp(-x))


def _softplus(x):
    return jnp.maximum(x, 0.0) + jnp.log(1.0 + jnp.exp(-jnp.abs(x)))


def _dsilu(y):
    s = _sigmoid(y)
    return s * (1.0 + y * (1.0 - s))


class _Side:
    def __init__(self, ins, out_shapes, n_sems, start, finish, in_place=False):
        self.ins, self.out_shapes, self.n_sems = list(ins), list(out_shapes), n_sems
        self.start, self.finish, self.in_place = start, finish, in_place


def _side_split(refs, n_in, n_out, side):
    si, so = (len(side.ins), len(side.out_shapes)) if side else (0, 0)
    ins, sins = refs[:n_in], refs[n_in:n_in + si]
    outs, souts = refs[n_in + si:n_in + si + n_out], refs[n_in + si + n_out:n_in + si + n_out + so]
    rest = refs[n_in + si + n_out + so:]
    scratch, sems = (rest[:-2], rest[-2:]) if side else (rest, ())
    return ins, outs, scratch, (sins, souts, sems)


def _side_steps(side, parts, grid):
    if side is None:
        return (lambda: None), (lambda: None)
    sins, souts, sems = parts
    ids = [pl.program_id(a) for a in range(len(grid))]
    first = functools.reduce(jnp.logical_and, [i == 0 for i in ids])
    last = functools.reduce(jnp.logical_and, [i == n - 1 for i, n in zip(ids, grid)])

    def begin():
        @pl.when(first)
        def _():
            side.start(sins, souts, *sems)

    def end():
        @pl.when(last)
        def _():
            side.finish(sins, souts, *sems)

    return begin, end


def _side_call_args(side, n_in, n_out):
    if side is None:
        return [], [], [], [], [], {}
    sems = [pltpu.SemaphoreType.DMA((side.n_sems,)), pltpu.SemaphoreType.DMA((side.n_sems,))]
    aliases = {n_in + k: n_out + k for k in range(len(side.ins))} if side.in_place else {}
    return [ANY] * len(side.ins), [ANY] * len(side.out_shapes), side.out_shapes, sems, side.ins, aliases


def _mm(a, b, *, ta=False, tb=False, add=None, out_dtype=F32, name, tiles=MM_TILES, side=None):
    if ta:
        kdim, m = a.shape
    else:
        m, kdim = a.shape
    if tb:
        n, k2 = b.shape
    else:
        k2, n = b.shape
    assert kdim == k2, (a.shape, b.shape, ta, tb)
    tm, tn, tk = _tile(m, tiles[0]), _tile(n, tiles[1]), _tile(kdim, tiles[2])
    nk = kdim // tk
    grid = (m // tm, n // tn, nk)
    dims = (((0 if ta else 1,), (1 if tb else 0,)), ((), ()))
    n_in = 2 if add is None else 3

    def body(*refs):
        ins, (o_ref,), (acc_ref,), parts = _side_split(refs, n_in, 1, side)
        begin, end = _side_steps(side, parts, grid)
        begin()
        a_ref, b_ref = ins[:2]
        k = pl.program_id(2)

        @pl.when(k == 0)
        def _():
            acc_ref[...] = jnp.zeros_like(acc_ref)

        acc_ref[...] += lax.dot_general(a_ref[...].astype(BF16), b_ref[...].astype(BF16), dims,
                                        preferred_element_type=F32)

        @pl.when(k == nk - 1)
        def _():
            r = acc_ref[...]
            if add is not None:
                r = r + ins[2][...]
            o_ref[...] = r.astype(o_ref.dtype)

        end()

    a_spec = pl.BlockSpec((tk, tm), lambda i, j, k: (k, i)) if ta else pl.BlockSpec((tm, tk), lambda i, j, k: (i, k))
    b_spec = pl.BlockSpec((tn, tk), lambda i, j, k: (j, k)) if tb else pl.BlockSpec((tk, tn), lambda i, j, k: (k, j))
    o_spec = pl.BlockSpec((tm, tn), lambda i, j, k: (i, j))
    ins, specs = [a, b], [a_spec, b_spec]
    if add is not None:
        ins.append(add)
        specs.append(o_spec)
    s_in, s_out, s_shapes, s_sems, s_ops, aliases = _side_call_args(side, n_in, 1)
    sem = ("arbitrary",) * 3 if side else ("parallel", "parallel", "arbitrary")
    outs = pl.pallas_call(
        body, grid=grid, in_specs=specs + s_in, out_specs=(o_spec, *s_out),
        out_shape=(jax.ShapeDtypeStruct((m, n), out_dtype), *s_shapes),
        scratch_shapes=[pltpu.VMEM((tm, tn), F32)] + s_sems, input_output_aliases=aliases,
        compiler_params=_cp(*sem), name=name,
    )(*ins, *s_ops)
    return outs if side else outs[0]


def _rms_fwd(x, w, res=None, *, out_dtype, name):
    s, d = x.shape
    tr = _tile(s, 256, 8)

    def body(*refs):
        if res is None:
            x_ref, w_ref, o_ref = refs
        else:
            x_ref, w_ref, r_ref, o_ref = refs
        xv = x_ref[...]
        y = xv * lax.rsqrt(jnp.mean(xv * xv, axis=-1, keepdims=True) + EPS) * w_ref[...]
        if res is not None:
            y = y + r_ref[...]
        o_ref[...] = y.astype(o_ref.dtype)

    row = pl.BlockSpec((tr, d), lambda i: (i, 0))
    par = pl.BlockSpec((1, d), lambda i: (0, 0))
    ins, specs = [x, w.reshape(1, d)], [row, par]
    if res is not None:
        ins.append(res)
        specs.append(row)
    return pl.pallas_call(body, grid=(s // tr,), in_specs=specs, out_specs=row,
                          out_shape=jax.ShapeDtypeStruct((s, d), out_dtype),
                          compiler_params=_cp("parallel"), name=name)(*ins)


def _rms_bwd(x, w, dy, add=None, *, out_dtype=F32, name):
    s, d = x.shape
    tr = _tile(s, 256, 8)

    def body(*refs):
        if add is None:
            x_ref, w_ref, dy_ref, dx_ref, dw_ref = refs
        else:
            x_ref, w_ref, dy_ref, add_ref, dx_ref, dw_ref = refs
        xv = x_ref[...]
        dyv = dy_ref[...].astype(F32)
        r = lax.rsqrt(jnp.mean(xv * xv, axis=-1, keepdims=True) + EPS)
        xh = xv * r
        dyw = dyv * w_ref[...]
        dx = r * (dyw - xh * jnp.mean(dyw * xh, axis=-1, keepdims=True))
        if add is not None:
            dx = dx + add_ref[...]
        dx_ref[...] = dx.astype(dx_ref.dtype)

        @pl.when(pl.program_id(0) == 0)
        def _():
            dw_ref[...] = jnp.zeros_like(dw_ref)

        dw_ref[...] += jnp.sum(dyv * xh, axis=0, keepdims=True)

    row = pl.BlockSpec((tr, d), lambda i: (i, 0))
    par = pl.BlockSpec((1, d), lambda i: (0, 0))
    ins, specs = [x, w.reshape(1, d), dy], [row, par, row]
    if add is not None:
        ins.append(add)
        specs.append(row)
    dx, dw = pl.pallas_call(
        body, grid=(s // tr,), in_specs=specs, out_specs=(row, par),
        out_shape=(jax.ShapeDtypeStruct((s, d), out_dtype), jax.ShapeDtypeStruct((1, d), F32)),
        compiler_params=_cp("arbitrary"), name=name)(*ins)
    return dx, dw.reshape(d)


def _loss_head(h, tgt):
    s, d = h.shape
    tr = _tile(s, 256, 8)

    def body(h_ref, t_ref, dh_ref, l_ref):
        e = h_ref[...] - t_ref[...]
        dh_ref[...] = e * (1.0 / d)

        @pl.when(pl.program_id(0) == 0)
        def _():
            l_ref[...] = jnp.zeros_like(l_ref)

        l_ref[...] += 0.5 * jnp.sum(jnp.mean(e * e, axis=-1, keepdims=True), axis=0, keepdims=True)

    row = pl.BlockSpec((tr, d), lambda i: (i, 0))
    dh, l = pl.pallas_call(
        body, grid=(s // tr,), in_specs=[row, row], out_specs=(row, pl.BlockSpec((1, 1), lambda i: (0, 0))),
        out_shape=(jax.ShapeDtypeStruct((s, d), F32), jax.ShapeDtypeStruct((1, 1), F32)),
        compiler_params=_cp("arbitrary"), name="loss_head")(h, tgt)
    return l[0, 0], dh


def _conv_rows(xe, w, width):
    y = None
    for j in range(width):
        s = width - 1 - j
        term = (pltpu.roll(xe, s, 0) if s else xe)[HALO:] * w[j:j + 1, :]
        y = term if y is None else y + term
    return y


def _conv_rows_t(dy_ext, w, width, tr):
    n = dy_ext.shape[0]
    dx = None
    for j in range(width):
        s = width - 1 - j
        term = (pltpu.roll(dy_ext, n - s, 0) if s else dy_ext)[:tr] * w[j:j + 1, :]
        dx = term if dx is None else dx + term
    return dx


def _conv_dw(dy, xe, width):
    rows = []
    for j in range(width):
        s = width - 1 - j
        rows.append(jnp.sum(dy * (pltpu.roll(xe, s, 0) if s else xe)[HALO:], axis=0, keepdims=True))
    return jnp.concatenate(rows, axis=0)


def _halo_specs(tr, tc, coff, nrow_blocks):
    per = tr // HALO
    last = nrow_blocks * per - 1
    cur = pl.BlockSpec((tr, tc), lambda j, r: (r, coff + j))
    prev = pl.BlockSpec((HALO, tc), lambda j, r: (jnp.maximum(r * per - 1, 0), coff + j))
    nxt = pl.BlockSpec((HALO, tc), lambda j, r: (jnp.minimum((r + 1) * per, last), coff + j))
    return cur, prev, nxt


def _dnconv_fwd(proj, w, cfg):
    s, width = cfg.S, 3 * cfg.HD * HEAD
    tc = _tile(math.gcd(cfg.QD, width), 512)
    tr = _tile(s, 512, 8)
    nr = s // tr
    cur, prev, _ = _halo_specs(tr, tc, cfg.QD // tc, nr)

    def body(x_ref, xp_ref, w_ref, o_ref):
        hist = jnp.where(pl.program_id(1) > 0, xp_ref[...], 0.0)
        y = _conv_rows(jnp.concatenate([hist, x_ref[...]], axis=0), w_ref[...], DN_CONV)
        o_ref[...] = y * _sigmoid(y)

    return pl.pallas_call(
        body, grid=(width // tc, nr),
        in_specs=[cur, prev, pl.BlockSpec((DN_CONV, tc), lambda j, r: (0, j))],
        out_specs=pl.BlockSpec((tr, tc), lambda j, r: (r, j)),
        out_shape=jax.ShapeDtypeStruct((s, width), F32),
        compiler_params=_cp("parallel", "parallel"), name="dnconv_fwd")(proj, proj, w)


def _dnconv_bwd(proj, du3, w, cfg):
    s, hw = cfg.S, cfg.HD * HEAD
    width = 3 * hw
    tc = _tile(math.gcd(cfg.QD, hw), 512)
    tr = _tile(s, 512, 8)
    nr = s // tr
    per, last = tr // HALO, s // HALO - 1
    cur, prev, nxt = _halo_specs(tr, tc, cfg.QD // tc, nr)
    pc = hw // tc
    du_cur = pl.BlockSpec((1, tr, tc), lambda j, r: (j // pc, r, j % pc))
    du_nxt = pl.BlockSpec((1, HALO, tc), lambda j, r: (j // pc, jnp.minimum((r + 1) * per, last), j % pc))

    def body(x_ref, xp_ref, xn_ref, du_ref, dun_ref, w_ref, dx_ref, dw_ref):
        r = pl.program_id(1)
        wv = w_ref[...]
        hist = jnp.where(r > 0, xp_ref[...], 0.0)
        xe = jnp.concatenate([hist, x_ref[...]], axis=0)
        y_ext = _conv_rows(jnp.concatenate([xe, xn_ref[...]], axis=0), wv, DN_CONV)
        du_ext = jnp.concatenate([du_ref[0], jnp.where(r < nr - 1, dun_ref[0], 0.0)], axis=0)
        dy_ext = du_ext * _dsilu(y_ext)
        dx_ref[...] = _conv_rows_t(dy_ext, wv, DN_CONV, tr).astype(dx_ref.dtype)

        @pl.when(r == 0)
        def _():
            dw_ref[...] = jnp.zeros_like(dw_ref)

        dw_ref[...] += _conv_dw(dy_ext[:tr], xe, DN_CONV)

    wspec = pl.BlockSpec((DN_CONV, tc), lambda j, r: (0, j))
    return pl.pallas_call(
        body, grid=(width // tc, nr),
        in_specs=[cur, prev, nxt, du_cur, du_nxt, wspec],
        out_specs=(pl.BlockSpec((tr, tc), lambda j, r: (r, j)), wspec),
        out_shape=(jax.ShapeDtypeStruct((s, width), BF16), jax.ShapeDtypeStruct((DN_CONV, width), F32)),
        compiler_params=_cp("parallel", "arbitrary"), name="dnconv_bwd")(proj, proj, proj, du3, du3, w)


def _glu_fwd(up, w, b, cfg):
    s, ff = cfg.S, cfg.FF
    tc = _tile(ff, 512)
    tr = _tile(s, 512, 8)
    nr, nc = s // tr, ff // tc
    g_cur, g_prev, _ = _halo_specs(tr, tc, 0, nr)
    v_cur, v_prev, _ = _halo_specs(tr, tc, nc, nr)

    def wspec(rows, off):
        return pl.BlockSpec((rows, tc), lambda j, r: (0, off + j))

    def body(g_ref, gp_ref, v_ref, vp_ref, wg_ref, wv_ref, bg_ref, bv_ref, o_ref):
        first = pl.program_id(1) > 0
        ug = _conv_rows(jnp.concatenate([jnp.where(first, gp_ref[...], 0.0), g_ref[...]], axis=0), wg_ref[...], FFN_CONV) + bg_ref[...]
        uv = _conv_rows(jnp.concatenate([jnp.where(first, vp_ref[...], 0.0), v_ref[...]], axis=0), wv_ref[...], FFN_CONV) + bv_ref[...]
        o_ref[...] = (ug * _sigmoid(ug) * uv).astype(o_ref.dtype)

    b2 = b.reshape(1, 2 * ff)
    return pl.pallas_call(
        body, grid=(nc, nr),
        in_specs=[g_cur, g_prev, v_cur, v_prev, wspec(FFN_CONV, 0), wspec(FFN_CONV, nc), wspec(1, 0), wspec(1, nc)],
        out_specs=pl.BlockSpec((tr, tc), lambda j, r: (r, j)),
        out_shape=jax.ShapeDtypeStruct((s, ff), BF16),
        compiler_params=_cp("parallel", "parallel"), name="glu_fwd")(up, up, up, up, w, w, b2, b2)


def _glu_bwd(up, dact, w, b, cfg):
    s, ff = cfg.S, cfg.FF
    tc = _tile(ff, 512)
    tr = _tile(s, 512, 8)
    nr, nc = s // tr, ff // tc
    g_cur, g_prev, g_nxt = _halo_specs(tr, tc, 0, nr)
    v_cur, v_prev, v_nxt = _halo_specs(tr, tc, nc, nr)
    d_cur, _, d_nxt = _halo_specs(tr, tc, 0, nr)

    def wspec(rows, off):
        return pl.BlockSpec((rows, tc), lambda j, r: (0, off + j))

    def body(g_ref, gp_ref, gn_ref, v_ref, vp_ref, vn_ref, d_ref, dn_ref, wg_ref, wv_ref, bg_ref, bv_ref,
             dup_ref, dw_ref, db_ref):
        r = pl.program_id(1)
        wg, wv = wg_ref[...], wv_ref[...]
        ge = jnp.concatenate([jnp.where(r > 0, gp_ref[...], 0.0), g_ref[...]], axis=0)
        ve = jnp.concatenate([jnp.where(r > 0, vp_ref[...], 0.0), v_ref[...]], axis=0)
        ug = _conv_rows(jnp.concatenate([ge, gn_ref[...]], axis=0), wg, FFN_CONV) + bg_ref[...]
        uv = _conv_rows(jnp.concatenate([ve, vn_ref[...]], axis=0), wv, FFN_CONV) + bv_ref[...]
        da = jnp.concatenate([d_ref[...].astype(F32), jnp.where(r < nr - 1, dn_ref[...].astype(F32), 0.0)], axis=0)
        sg = _sigmoid(ug)
        dug = da * uv * (sg * (1.0 + ug * (1.0 - sg)))
        duv = da * ug * sg
        dup_ref[0] = _conv_rows_t(dug, wg, FFN_CONV, tr).astype(dup_ref.dtype)
        dup_ref[1] = _conv_rows_t(duv, wv, FFN_CONV, tr).astype(dup_ref.dtype)

        @pl.when(r == 0)
        def _():
            dw_ref[...] = jnp.zeros_like(dw_ref)
            db_ref[...] = jnp.zeros_like(db_ref)

        dw_ref[0] += _conv_dw(dug[:tr], ge, FFN_CONV)
        dw_ref[1] += _conv_dw(duv[:tr], ve, FFN_CONV)
        db_ref[0] += jnp.sum(dug[:tr], axis=0, keepdims=True)
        db_ref[1] += jnp.sum(duv[:tr], axis=0, keepdims=True)

    b2 = b.reshape(1, 2 * ff)
    dup, dw, db = pl.pallas_call(
        body, grid=(nc, nr),
        in_specs=[g_cur, g_prev, g_nxt, v_cur, v_prev, v_nxt, d_cur, d_nxt,
                  wspec(FFN_CONV, 0), wspec(FFN_CONV, nc), wspec(1, 0), wspec(1, nc)],
        out_specs=(pl.BlockSpec((2, tr, tc), lambda j, r: (0, r, j)),
                   pl.BlockSpec((2, FFN_CONV, tc), lambda j, r: (0, 0, j)),
                   pl.BlockSpec((2, 1, tc), lambda j, r: (0, 0, j))),
        out_shape=(jax.ShapeDtypeStruct((2, s, ff), BF16), jax.ShapeDtypeStruct((2, FFN_CONV, ff), F32),
                   jax.ShapeDtypeStruct((2, 1, ff), F32)),
        compiler_params=_cp("parallel", "arbitrary"), name="glu_bwd",
    )(up, up, up, up, up, up, dact, dact, w, w, b2, b2)
    dup = jnp.concatenate([dup[0], dup[1]], axis=1)
    dw = jnp.transpose(dw, (1, 0, 2)).reshape(FFN_CONV, 2 * ff)
    return dup, dw, db.reshape(2 * ff)


def _merge_specs(cfg):
    tc = _tile(math.gcd(math.gcd(cfg.GA, cfg.GD), cfg.D), 512)
    tr = _tile(cfg.S, 512, 8)
    ga = pl.BlockSpec((tr, tc), lambda r, j: (r, cfg.GA // tc + j))
    gd = pl.BlockSpec((tr, tc), lambda r, j: (r, cfg.GD // tc + j))
    pd = pl.BlockSpec((tr, tc), lambda r, j: (r, j))
    return (cfg.S // tr, cfg.D // tc), ga, gd, pd


def _merge_fwd(proj, pa, pdn, cfg):
    grid, ga, gd, pd = _merge_specs(cfg)

    def body(ga_ref, gd_ref, pa_ref, pd_ref, y_ref):
        y_ref[...] = (_sigmoid(ga_ref[...]) * pa_ref[...] + _sigmoid(gd_ref[...]) * pd_ref[...]).astype(y_ref.dtype)

    return pl.pallas_call(body, grid=grid, in_specs=[ga, gd, pd, pd], out_specs=pd,
                          out_shape=jax.ShapeDtypeStruct((cfg.S, cfg.D), BF16),
                          compiler_params=_cp("parallel", "parallel"), name="merge_fwd")(proj, proj, pa, pdn)


def _merge_bwd(proj, pa, pdn, dy, cfg):
    grid, ga, gd, pd = _merge_specs(cfg)

    def body(ga_ref, gd_ref, pa_ref, pd_ref, dy_ref, dpa_ref, dpd_ref, dga_ref, dgd_ref):
        dyv = dy_ref[...]
        sa, sd = _sigmoid(ga_ref[...]), _sigmoid(gd_ref[...])
        dpa_ref[...] = (dyv * sa).astype(BF16)
        dpd_ref[...] = (dyv * sd).astype(BF16)
        dga_ref[...] = (dyv * pa_ref[...] * sa * (1.0 - sa)).astype(BF16)
        dgd_ref[...] = (dyv * pd_ref[...] * sd * (1.0 - sd)).astype(BF16)

    out = jax.ShapeDtypeStruct((cfg.S, cfg.D), BF16)
    return pl.pallas_call(body, grid=grid, in_specs=[ga, gd, pd, pd, pd], out_specs=(pd,) * 4,
                          out_shape=(out,) * 4, compiler_params=_cp("parallel", "parallel"),
                          name="merge_bwd")(proj, proj, pa, pdn, dy)


def _rope(x, cos, sin):
    lane = lax.broadcasted_iota(jnp.int32, x.shape, 1)
    swapped = jnp.where(lane < ROT_HALF, pltpu.roll(x, LANES - ROT_HALF, 1), pltpu.roll(x, ROT_HALF, 1))
    return x * cos + swapped * sin


def _rope_t(dx, cos, sin):
    t = dx * sin
    lane = lax.broadcasted_iota(jnp.int32, dx.shape, 1)
    swapped = jnp.where(lane < ROT_HALF, pltpu.roll(t, LANES - ROT_HALF, 1), pltpu.roll(t, ROT_HALF, 1))
    return dx * cos + swapped


def _band_mask(i, g):
    r = lax.broadcasted_iota(jnp.int32, (g * ATTN_BLOCK, 2 * ATTN_BLOCK), 0) % ATTN_BLOCK
    c = lax.broadcasted_iota(jnp.int32, (g * ATTN_BLOCK, 2 * ATTN_BLOCK), 1)
    dist = r + ATTN_BLOCK - c
    return (dist >= 0) & (dist < ATTN_BLOCK) & ((c >= ATTN_BLOCK) | (i > 0))


def _stack_heads(x, hb):
    return jnp.concatenate([x[:, i * HEAD:(i + 1) * HEAD] for i in range(hb)], axis=0)


def _dot(a, b, ca, cb):
    return lax.dot_general(a.astype(BF16), b.astype(BF16), (((ca,), (cb,)), ((), ())), preferred_element_type=F32)


def _attn_probs(qg, k_cat, sink, valid):
    logits = jnp.where(valid, _dot(qg, k_cat, 1, 1) * (HEAD ** -0.5), -1e30)
    m = jnp.maximum(jnp.max(logits, axis=-1, keepdims=True), sink)
    p = jnp.exp(logits - m)
    es = jnp.exp(sink - m)
    inv = 1.0 / (jnp.sum(p, axis=-1, keepdims=True) + es)
    return p * inv, es * inv


def _attn_specs(cfg, nb, clamp):
    gw = cfg.G * HEAD
    qi = (lambda kv, i: (jnp.minimum(i, nb - 1), kv)) if clamp else (lambda kv, i: (i, kv))
    ci = (lambda i: jnp.minimum(i, nb - 1)) if clamp else (lambda i: i)
    pi = lambda i: jnp.maximum(ci(i) - 1, 0)
    ko, vo = cfg.KA // HEAD, cfg.VA // HEAD
    blk = (ATTN_BLOCK, HEAD)
    specs = [
        pl.BlockSpec((ATTN_BLOCK, gw), qi),
        pl.BlockSpec(blk, lambda kv, i: (ci(i), ko + kv)), pl.BlockSpec(blk, lambda kv, i: (pi(i), ko + kv)),
        pl.BlockSpec(blk, lambda kv, i: (ci(i), vo + kv)), pl.BlockSpec(blk, lambda kv, i: (pi(i), vo + kv)),
        pl.BlockSpec(blk, lambda kv, i: (ci(i), 0)), pl.BlockSpec(blk, lambda kv, i: (pi(i), 0)),
        pl.BlockSpec(blk, lambda kv, i: (ci(i), 0)), pl.BlockSpec(blk, lambda kv, i: (pi(i), 0)),
        pl.BlockSpec((1, 8, LANES), lambda kv, i: (kv, 0, 0)),
    ]
    return specs, pl.BlockSpec((ATTN_BLOCK, gw), qi)


def _sink_rows(sinks, cfg):
    sk = jnp.broadcast_to(sinks.reshape(cfg.KV, cfg.G, 1), (cfg.KV, cfg.G, LANES))
    return jnp.pad(sk, ((0, 0), (0, 8 - cfg.G), (0, 0)))


def _attn_group(q_ref, kc_ref, kp_ref, vc_ref, vp_ref, cc_ref, cp_ref, sc_ref, sp_ref, sk_ref, i, g):
    cc, sc = cc_ref[...], sc_ref[...]
    k_cat = jnp.concatenate([_rope(kp_ref[...], cp_ref[...], sp_ref[...]), _rope(kc_ref[...], cc, sc)], axis=0).astype(BF16)
    v_cat = jnp.concatenate([vp_ref[...], vc_ref[...]], axis=0).astype(BF16)
    cos_g, sin_g = jnp.concatenate([cc] * g, axis=0), jnp.concatenate([sc] * g, axis=0)
    qr = _rope(_stack_heads(q_ref[...], g), cos_g, sin_g)
    sink = jnp.concatenate([jnp.broadcast_to(sk_ref[0, h:h + 1, 0:1], (ATTN_BLOCK, 1)) for h in range(g)], axis=0)
    p, ps = _attn_probs(qr, k_cat, sink, _band_mask(i, g))
    return qr, k_cat, v_cat, cos_g, sin_g, p, ps


def _attn_fwd(proj, cosf, sinf, sinks, cfg):
    nb, g = cfg.S // ATTN_BLOCK, cfg.G
    specs, qspec = _attn_specs(cfg, nb, False)

    def body(*refs):
        o_ref = refs[10]
        _, _, v_cat, _, _, p, _ = _attn_group(*refs[:10], pl.program_id(1), g)
        o = _dot(p, v_cat, 1, 0)
        for h in range(g):
            o_ref[:, h * HEAD:(h + 1) * HEAD] = o[h * ATTN_BLOCK:(h + 1) * ATTN_BLOCK].astype(o_ref.dtype)

    return pl.pallas_call(
        body, grid=(cfg.KV, nb), in_specs=specs, out_specs=qspec,
        out_shape=jax.ShapeDtypeStruct((cfg.S, cfg.HA * HEAD), BF16),
        compiler_params=_cp("parallel", "parallel"), name="attn_fwd",
    )(proj, proj, proj, proj, proj, cosf, cosf, sinf, sinf, _sink_rows(sinks, cfg))


def _attn_bwd(proj, do, cosf, sinf, sinks, cfg):
    nb, g = cfg.S // ATTN_BLOCK, cfg.G
    specs, qspec = _attn_specs(cfg, nb, True)
    kv_out = pl.BlockSpec((ATTN_BLOCK, HEAD), lambda kv, i: (jnp.maximum(i - 1, 0), kv))

    def body(*refs):
        cc_ref, cp_ref, sc_ref, sp_ref = refs[5:9]
        do_ref, dq_ref, dk_ref, dv_ref, dsk_ref, ck_ref, cv_ref = refs[10:]
        i = pl.program_id(1)

        @pl.when(i == 0)
        def _():
            ck_ref[...] = jnp.zeros_like(ck_ref)
            cv_ref[...] = jnp.zeros_like(cv_ref)
            dsk_ref[...] = jnp.zeros_like(dsk_ref)

        @pl.when(i < nb)
        def _():
            qr, k_cat, v_cat, cos_g, sin_g, p, ps = _attn_group(*refs[:10], i, g)
            dos = _stack_heads(do_ref[...], g)
            dp = _dot(dos, v_cat, 1, 1)
            delta = jnp.sum(p * dp, axis=-1, keepdims=True)
            ds = p * (dp - delta) * (HEAD ** -0.5)
            dq = _rope_t(_dot(ds, k_cat, 1, 0), cos_g, sin_g)
            sunk = ps * delta
            for h in range(g):
                rows = slice(h * ATTN_BLOCK, (h + 1) * ATTN_BLOCK)
                dq_ref[:, h * HEAD:(h + 1) * HEAD] = dq[rows].astype(dq_ref.dtype)
                dsk_ref[0, h:h + 1, :] += jnp.broadcast_to(-jnp.sum(sunk[rows], axis=0, keepdims=True), (1, LANES))
            dk_cat = _dot(ds, qr, 0, 0)
            dv_cat = _dot(p, dos, 0, 0)
            dk_ref[...] = (ck_ref[...] + _rope_t(dk_cat[:ATTN_BLOCK], cp_ref[...], sp_ref[...])).astype(dk_ref.dtype)
            dv_ref[...] = (cv_ref[...] + dv_cat[:ATTN_BLOCK]).astype(dv_ref.dtype)
            ck_ref[...] = _rope_t(dk_cat[ATTN_BLOCK:], cc_ref[...], sc_ref[...])
            cv_ref[...] = dv_cat[ATTN_BLOCK:]

        @pl.when(i == nb)
        def _():
            dk_ref[...] = ck_ref[...].astype(dk_ref.dtype)
            dv_ref[...] = cv_ref[...].astype(dv_ref.dtype)

    kvw = cfg.KV * HEAD
    dq, dk, dv, dsk = pl.pallas_call(
        body, grid=(cfg.KV, nb + 1), in_specs=specs + [qspec],
        out_specs=(qspec, kv_out, kv_out, pl.BlockSpec((1, 8, LANES), lambda kv, i: (kv, 0, 0))),
        out_shape=(jax.ShapeDtypeStruct((cfg.S, cfg.HA * HEAD), BF16), jax.ShapeDtypeStruct((cfg.S, kvw), BF16),
                   jax.ShapeDtypeStruct((cfg.S, kvw), BF16), jax.ShapeDtypeStruct((cfg.KV, 8, LANES), F32)),
        scratch_shapes=[pltpu.VMEM((ATTN_BLOCK, HEAD), F32), pltpu.VMEM((ATTN_BLOCK, HEAD), F32)],
        compiler_params=_cp("parallel", "arbitrary"), name="attn_bwd",
    )(proj, proj, proj, proj, proj, cosf, cosf, sinf, sinf, _sink_rows(sinks, cfg), do)
    return jnp.concatenate([dq, dk, dv], axis=1), dsk[:, :cfg.G, 0].reshape(cfg.HA)


@jax.custom_vjp
def _nn(a, b):
    return _dot(a, b, 1, 0)


@jax.custom_vjp
def _nt(a, b):
    return _dot(a, b, 1, 1)


@jax.custom_vjp
def _tn(a, b):
    return _dot(a, b, 0, 0)


_nn.defvjp(lambda a, b: (_nn(a, b), (a, b)), lambda r, g: (_nt(g, r[1]), _tn(r[0], g)))
_nt.defvjp(lambda a, b: (_nt(a, b), (a, b)), lambda r, g: (_nn(g, r[1]), _tn(g, r[0])))
_tn.defvjp(lambda a, b: (_tn(a, b), (a, b)), lambda r, g: (_nt(r[1], g), _nn(r[0], g)))


def _unit_lower_inverse(a):
    ii = lax.broadcasted_iota(jnp.int32, a.shape, 0)
    jj = lax.broadcasted_iota(jnp.int32, a.shape, 1)
    t = jnp.where(ii == jj, 1.0, 0.0) - jnp.where((ii == jj + 1) & (ii % 2 == 1), a, 0.0)
    b = 2
    while b < CHUNK:
        lower_left = (ii // (2 * b) == jj // (2 * b)) & ((ii // b) % 2 == 1) & ((jj // b) % 2 == 0)
        t = t - _dot(_dot(t, jnp.where(lower_left, a, 0.0), 1, 0), t, 1, 0)
        b *= 2
    return t


@jax.custom_vjp
def _tri_inv(a):
    return _unit_lower_inverse(a)


def _tri_inv_fwd(a):
    t = _unit_lower_inverse(a)
    return t, t


def _tri_inv_bwd(t, g):
    return (-_dot(t, _dot(g, t, 1, 1), 0, 0),)


_tri_inv.defvjp(_tri_inv_fwd, _tri_inv_bwd)


@jax.custom_vjp
def _tri_inv_saved(a, t):
    return t


_tri_inv_saved.defvjp(lambda a, t: (t, t), lambda t, g: (_tri_inv_bwd(t, g)[0], jnp.zeros_like(t)))


def _gdn_chunk(q, k, v, z, b_col, a_col, a_row, al_col, dt_col, al_row, dt_row, nw, s, t_saved=None, emit_t=False):
    rows = q.shape[0]
    hb = rows // CHUNK
    ii = lax.broadcasted_iota(jnp.int32, (rows, rows), 0)
    jj = lax.broadcasted_iota(jnp.int32, (rows, rows), 1)
    same = (ii // CHUNK) == (jj // CHUNK)
    tri, strict, upper = same & (ii >= jj), same & (ii > jj), same & (ii <= jj)
    beta = _sigmoid(b_col)
    g_col = -jnp.exp(al_col) * _softplus(a_col + dt_col)
    g_row = -jnp.exp(al_row) * _softplus(a_row + dt_row)
    gc_col = jnp.sum(jnp.where(tri, g_row, 0.0), axis=1, keepdims=True)
    gc_row = jnp.sum(jnp.where(upper, g_col, 0.0), axis=0, keepdims=True)
    last = jj == (ii // CHUNK) * CHUNK + (CHUNK - 1)
    gl_col = jnp.sum(jnp.where(last, gc_row, 0.0), axis=1, keepdims=True)
    decay = jnp.where(tri, jnp.exp(jnp.where(tri, gc_col - gc_row, 0.0)), 0.0)
    qn = q * lax.rsqrt(jnp.sum(q * q, axis=-1, keepdims=True) + EPS) * (HEAD ** -0.5)
    kn = k * lax.rsqrt(jnp.sum(k * k, axis=-1, keepdims=True) + EPS)
    a_mat = jnp.where(strict, beta * _nt(kn, kn) * decay, 0.0)
    t_inv = _tri_inv(a_mat) if t_saved is None else _tri_inv_saved(a_mat, t_saved)
    eg = jnp.exp(gc_col)
    u = _nn(t_inv, v * beta)
    w = _nn(t_inv, kn * (beta * eg))
    qk = jnp.where(tri, _nt(qn, kn) * decay, 0.0)
    ri = lax.broadcasted_iota(jnp.int32, (rows, hb * HEAD), 0)
    ci = lax.broadcasted_iota(jnp.int32, (rows, hb * HEAD), 1)
    own = (ri // CHUNK) == (ci // HEAD)

    def spread(x):
        return jnp.where(own, jnp.concatenate([x] * hb, axis=1), 0.0)

    v_new = u - _nn(spread(w), s)
    o = _nn(spread(qn * eg), s) + _nn(qk, v_new)
    si = lax.broadcasted_iota(jnp.int32, (hb * HEAD, rows), 0)
    sj = lax.broadcasted_iota(jnp.int32, (hb * HEAD, rows), 1)
    gl_s = jnp.sum(jnp.where(sj == (si // HEAD) * CHUNK + (CHUNK - 1), gc_row, 0.0), axis=1, keepdims=True)
    s1 = s * jnp.exp(gl_s) + _tn(spread(kn * jnp.exp(gl_col - gc_col)), v_new)
    od = o * lax.rsqrt(jnp.mean(o * o, axis=-1, keepdims=True) + EPS) * nw * (z * _sigmoid(z))
    return (od, s1, t_inv) if emit_t else (od, s1)


def _gdn_heads_per_step(cfg):
    for hb in (GDN_HEADS_PER_STEP, 2, 1):
        if cfg.HD % hb == 0 and cfg.Z % (hb * HEAD) == 0:
            return hb


def _gdn_in_specs(cfg, nc, rev):
    hb = _gdn_heads_per_step(cfg)
    ng, rows = cfg.HD // hb, hb * CHUNK
    ci = (lambda n: nc - 1 - n) if rev else (lambda n: n)
    zo = cfg.Z // (hb * HEAD)
    blk = (CHUNK, hb * HEAD)
    col = pl.BlockSpec((1, 1, rows, 1), lambda g, n: (g, ci(n), 0, 0))
    row = pl.BlockSpec((1, 1, 1, rows), lambda g, n: (g, ci(n), 0, 0))
    hcol = pl.BlockSpec((1, rows, 1), lambda g, n: (g, 0, 0))
    hrow = pl.BlockSpec((1, 1, rows), lambda g, n: (g, 0, 0))
    return [
        pl.BlockSpec(blk, lambda g, n: (ci(n), g)), pl.BlockSpec(blk, lambda g, n: (ci(n), ng + g)),
        pl.BlockSpec(blk, lambda g, n: (ci(n), 2 * ng + g)), pl.BlockSpec(blk, lambda g, n: (ci(n), zo + g)),
        col, col, row, hcol, hcol, hrow, hrow, pl.BlockSpec((1, HEAD), lambda g, n: (0, 0)),
    ]


def _gdn_args(qkv, proj, pba, a_log, dt_bias, nw, cfg):
    hb, hd, s = _gdn_heads_per_step(cfg), cfg.HD, cfg.S
    ng, nc, rows = hd // hb, s // CHUNK, hb * CHUNK

    def grouped(x):
        return jnp.transpose(x.reshape(nc, CHUNK, ng, hb), (2, 0, 3, 1)).reshape(ng, nc, rows)

    b, a = grouped(pba[:, :hd]), grouped(pba[:, hd:2 * hd])
    per_row = lambda x: jnp.repeat(x.reshape(ng, hb), CHUNK, axis=1)
    al, dt = per_row(a_log), per_row(dt_bias)
    return (qkv, qkv, qkv, proj, b.reshape(ng, nc, rows, 1), a.reshape(ng, nc, rows, 1), a.reshape(ng, nc, 1, rows),
            al.reshape(ng, rows, 1), dt.reshape(ng, rows, 1), al.reshape(ng, 1, rows), dt.reshape(ng, 1, rows),
            nw.reshape(1, HEAD))


def _gdn_load(refs, hb):
    q, k, v, z, b, a, ar, alc, dtc, alr, dtr, nw = refs
    return (_stack_heads(q[...], hb), _stack_heads(k[...], hb), _stack_heads(v[...], hb), _stack_heads(z[...], hb),
            b[0, 0], a[0, 0], ar[0, 0], alc[0], dtc[0], alr[0], dtr[0], nw[...])


def _gdn_fwd(qkv, proj, pba, a_log, dt_bias, nw, cfg, side=None):
    nc, hb = cfg.S // CHUNK, _gdn_heads_per_step(cfg)
    ng = cfg.HD // hb
    grid = (ng, nc)

    def body(*refs):
        ins, (o_ref, st_ref, t_ref), (s_ref,), parts = _side_split(refs, 12, 3, side)
        begin, end = _side_steps(side, parts, grid)
        begin()

        @pl.when(pl.program_id(1) == 0)
        def _():
            s_ref[...] = jnp.zeros_like(s_ref)

        s0 = s_ref[...]
        st_ref[0, 0] = s0
        od, s1, t_inv = _gdn_chunk(*_gdn_load(ins, hb), s0, emit_t=True)
        t_ref[0, 0] = t_inv.astype(t_ref.dtype)
        for i in range(hb):
            o_ref[:, i * HEAD:(i + 1) * HEAD] = od[i * CHUNK:(i + 1) * CHUNK].astype(o_ref.dtype)
        s_ref[...] = s1
        end()

    s_in, s_out, s_shapes, s_sems, s_ops, _ = _side_call_args(side, 12, 3)
    rows = hb * CHUNK
    return pl.pallas_call(
        body, grid=grid, in_specs=_gdn_in_specs(cfg, nc, False) + s_in,
        out_specs=(pl.BlockSpec((CHUNK, hb * HEAD), lambda g, n: (n, g)),
                   pl.BlockSpec((1, 1, hb * HEAD, HEAD), lambda g, n: (g, n, 0, 0)),
                   pl.BlockSpec((1, 1, rows, rows), lambda g, n: (g, n, 0, 0)), *s_out),
        out_shape=(jax.ShapeDtypeStruct((cfg.S, cfg.HD * HEAD), BF16),
                   jax.ShapeDtypeStruct((ng, nc, hb * HEAD, HEAD), F32),
                   jax.ShapeDtypeStruct((ng, nc, rows, rows), BF16), *s_shapes),
        scratch_shapes=[pltpu.VMEM((hb * HEAD, HEAD), F32)] + s_sems,
        compiler_params=_cp("arbitrary" if side else "parallel", "arbitrary"),
        name="gdn_fwd_gather" if side else "gdn_fwd",
    )(*_gdn_args(qkv, proj, pba, a_log, dt_bias, nw, cfg), *s_ops)


def _gdn_bwd(qkv, proj, pba, a_log, dt_bias, nw, states, t_inv, dod, cfg, side=None):
    nc, hd, s, hb = cfg.S // CHUNK, cfg.HD, cfg.S, _gdn_heads_per_step(cfg)
    ng, rows = hd // hb, hb * CHUNK
    rv = lambda n: nc - 1 - n
    col = pl.BlockSpec((1, 1, rows, 1), lambda g, n: (g, rv(n), 0, 0))
    row = pl.BlockSpec((1, 1, 1, rows), lambda g, n: (g, rv(n), 0, 0))
    hcol = pl.BlockSpec((1, rows, 1), lambda g, n: (g, 0, 0))
    hrow = pl.BlockSpec((1, 1, rows), lambda g, n: (g, 0, 0))

    grid = (ng, nc)

    def body(*refs):
        ins, outs, (ds_ref,), parts = _side_split(refs, 15, 10, side)
        begin, end = _side_steps(side, parts, grid)
        begin()
        st_ref, t_ref, dod_ref = ins[12:15]
        dqkv_ref, dz_ref, db_ref, da_ref, dar_ref, dalc_ref, ddtc_ref, dalr_ref, ddtr_ref, dnw_ref = outs
        g, n = pl.program_id(0), pl.program_id(1)

        @pl.when(n == 0)
        def _():
            ds_ref[...] = jnp.zeros_like(ds_ref)
            for r in (dalc_ref, ddtc_ref, dalr_ref, ddtr_ref):
                r[...] = jnp.zeros_like(r)

        @pl.when((n == 0) & (g == 0))
        def _():
            dnw_ref[...] = jnp.zeros_like(dnw_ref)

        chunk = functools.partial(_gdn_chunk, t_saved=t_ref[0, 0].astype(F32))
        _, vjp = jax.vjp(chunk, *_gdn_load(ins[:12], hb), st_ref[0, 0])
        dq, dk, dv, dz, db, da, dar, dalc, ddtc, dalr, ddtr, dnw, ds0 = vjp(
            (_stack_heads(dod_ref[...].astype(F32), hb), ds_ref[...]))
        for i in range(hb):
            sl, rs = slice(i * HEAD, (i + 1) * HEAD), slice(i * CHUNK, (i + 1) * CHUNK)
            dqkv_ref[0, :, sl], dqkv_ref[1, :, sl], dqkv_ref[2, :, sl] = dq[rs], dk[rs], dv[rs]
            dz_ref[:, sl] = dz[rs].astype(dz_ref.dtype)
        db_ref[0, 0], da_ref[0, 0], dar_ref[0, 0] = db, da, dar
        dalc_ref[0] += dalc
        ddtc_ref[0] += ddtc
        dalr_ref[0] += dalr
        ddtr_ref[0] += ddtr
        dnw_ref[...] += dnw
        ds_ref[...] = ds0
        end()

    blk = pl.BlockSpec((CHUNK, hb * HEAD), lambda g, n: (rv(n), g))
    f32 = lambda *sh: jax.ShapeDtypeStruct(sh, F32)
    s_in, s_out, s_shapes, s_sems, s_ops, _ = _side_call_args(side, 15, 10)
    outs = pl.pallas_call(
        body, grid=grid,
        in_specs=_gdn_in_specs(cfg, nc, True) + [pl.BlockSpec((1, 1, hb * HEAD, HEAD), lambda g, n: (g, rv(n), 0, 0)),
                                                 pl.BlockSpec((1, 1, rows, rows), lambda g, n: (g, rv(n), 0, 0)), blk] + s_in,
        out_specs=(pl.BlockSpec((3, CHUNK, hb * HEAD), lambda g, n: (0, rv(n), g)), blk, col, col, row,
                   hcol, hcol, hrow, hrow, pl.BlockSpec((1, HEAD), lambda g, n: (0, 0)), *s_out),
        out_shape=(f32(3, s, hd * HEAD), jax.ShapeDtypeStruct((s, hd * HEAD), BF16),
                   f32(ng, nc, rows, 1), f32(ng, nc, rows, 1), f32(ng, nc, 1, rows),
                   f32(ng, rows, 1), f32(ng, rows, 1), f32(ng, 1, rows), f32(ng, 1, rows), f32(1, HEAD), *s_shapes),
        scratch_shapes=[pltpu.VMEM((hb * HEAD, HEAD), F32)] + s_sems,
        compiler_params=_cp("arbitrary", "arbitrary"), name="gdn_bwd_scatter" if side else "gdn_bwd",
    )(*_gdn_args(qkv, proj, pba, a_log, dt_bias, nw, cfg), states, t_inv, dod, *s_ops)
    dqkv, dz, db, da, dar, dalc, ddtc, dalr, ddtr, dnw = outs[:10]
    side_outs = list(outs[10:])

    def ungrouped(x):
        return jnp.transpose(x.reshape(ng, nc, hb, CHUNK), (1, 3, 0, 2)).reshape(s, hd)

    per_head = lambda c, r: (c.reshape(ng, hb, CHUNK) + r.reshape(ng, hb, CHUNK)).sum(axis=-1).reshape(hd)
    d_b = ungrouped(db.reshape(ng, nc, rows))
    d_a = ungrouped(da.reshape(ng, nc, rows) + dar.reshape(ng, nc, rows))
    dba = jnp.concatenate([d_b, d_a, jnp.zeros((s, LANES - 2 * hd), F32)], axis=1).astype(BF16)
    return dqkv, dz, dba, per_head(dalc, dalr), per_head(ddtc, ddtr), dnw.reshape(HEAD), side_outs


def _adamw(w, g, m, v, *, name):
    nl, r, c = w.shape
    gc = g.shape[2]
    tr = _tile(r, max(8, (1 << 18) // c // 8 * 8), 8) if r % 8 == 0 else r
    c1, c2 = 1.0 - ADAM_B1 ** ADAM_STEP, 1.0 - ADAM_B2 ** ADAM_STEP

    def body(w_ref, g_ref, m_ref, v_ref, go_ref, d_ref, nm_ref, nv_ref):
        gv = g_ref[:, :c]
        nm = ADAM_B1 * m_ref[...] + (1.0 - ADAM_B1) * gv
        nv = ADAM_B2 * v_ref[...] + (1.0 - ADAM_B2) * (gv * gv)
        go_ref[...] = gv
        d_ref[...] = -ADAM_LR * ((nm / c1) / (jnp.sqrt(nv / c2) + ADAM_EPS) + ADAM_WD * w_ref[...])
        nm_ref[...] = nm
        nv_ref[...] = nv

    spec = pl.BlockSpec((None, tr, c), lambda l, i: (l, i, 0))
    gspec = pl.BlockSpec((None, tr, gc), lambda l, i: (l, i, 0))
    out = jax.ShapeDtypeStruct((nl, r, c), F32)
    return pl.pallas_call(body, grid=(nl, r // tr), in_specs=[spec, gspec, spec, spec], out_specs=(spec,) * 4,
                          out_shape=(out,) * 4, compiler_params=_cp("parallel", "parallel"), name=name)(w, g, m, v)


def _sum_leading(a, *, name):
    n, r, c = a.shape
    tr = _tile(r, SUM_TR, 8)

    def body(a_ref, o_ref):
        acc = a_ref[0].astype(F32)
        for i in range(1, n):
            acc = acc + a_ref[i].astype(F32)
        o_ref[...] = acc

    return pl.pallas_call(body, grid=(r // tr,), in_specs=[pl.BlockSpec((n, tr, c), lambda i: (0, i, 0))],
                          out_specs=pl.BlockSpec((tr, c), lambda i: (i, 0)),
                          out_shape=jax.ShapeDtypeStruct((r, c), F32), compiler_params=_cp("parallel"), name=name)(a)


class _Mat:
    def __init__(self, kind, rows, cols):
        self.kind, self.rows, self.cols = kind, rows, cols
        self.full = (rows, 4 * cols) if kind == "col" else (4 * rows, cols)
        self.half = rows // 2

    def block(self, ref, j, c=None):
        r0 = (0 if self.kind == "col" else j * self.rows) + (0 if c is None else c * self.half)
        rows = pl.ds(_aligned(r0, 16), self.rows if c is None else self.half)
        return ref.at[rows, pl.ds(_aligned(j * self.cols, LANES), self.cols)] if self.kind == "col" else ref.at[rows, :]

    def own_half(self, ref, c):
        return ref.at[pl.ds(_aligned(c * self.half, 16), self.half), :]


def _aligned(v, unit):
    return v if isinstance(v, int) else pl.multiple_of(v, unit)


def _place():
    x, y, c = lax.axis_index("x"), lax.axis_index("y"), lax.axis_index("c")
    chips = [(1 - x, y), (x, 1 - y), (1 - x, 1 - y)]
    return x, y, c, chips


def _chip_index():
    return 2 * lax.axis_index("x") + lax.axis_index("y")


def _remote(src, dst, send_sems, recv_sems, k, to):
    return pltpu.make_async_remote_copy(src_ref=src, dst_ref=dst, send_sem=send_sems.at[k], recv_sem=recv_sems.at[k],
                                        device_id=to, device_id_type=MESH)


def _comm_call(body, ins, out_shapes, n_sems, name):
    return pl.pallas_call(
        body, out_shape=tuple(out_shapes), in_specs=[ANY] * len(ins), out_specs=tuple([ANY] * len(out_shapes)),
        scratch_shapes=[pltpu.SemaphoreType.DMA((n_sems,)), pltpu.SemaphoreType.DMA((n_sems,))],
        compiler_params=pltpu.CompilerParams(has_side_effects=True), name=name)(*ins)


def _gather_copies(own_refs, full_refs, send_sems, recv_sems, mats, base=0):
    x, y, c, chips = _place()
    me = 2 * x + y
    return [(_remote(m.own_half(own_refs[k], c), m.block(full_refs[k], me, c), send_sems, recv_sems, base + 3 * k + j, (cx, cy, c)),
             _remote(m.block(full_refs[k], 2 * cx + cy, c), m.block(full_refs[k], 2 * cx + cy, c), send_sems, recv_sems,
                     base + 3 * k + j, (cx, cy, c)))
            for k, m in enumerate(mats) for j, (cx, cy) in enumerate(chips)]


def _pass_copies(full_refs, send_sems, recv_sems, mats, base=0):
    x, y, c, chips = _place()
    sib = (x, y, 1 - c)
    return [(_remote(m.block(full_refs[k], 2 * cx + cy, c), m.block(full_refs[k], 2 * cx + cy, c), send_sems, recv_sems,
                     base + 3 * k + j, sib),
             _remote(m.block(full_refs[k], 2 * cx + cy, 1 - c), m.block(full_refs[k], 2 * cx + cy, 1 - c), send_sems,
                     recv_sems, base + 3 * k + j, sib))
            for k, m in enumerate(mats) for j, (cx, cy) in enumerate(chips)]


def _start_all(pairs):
    for cp, _ in pairs:
        cp.start()


def _finish_all(pairs):
    for _, landing in pairs:
        landing.wait_recv()
    for cp, _ in pairs:
        cp.wait_send()


def _gather_ici_side(own, mats):
    return _Side(own, [jax.ShapeDtypeStruct(m.full, a.dtype) for m, a in zip(mats, own)], 3 * len(mats),
                 lambda i, o, ss, rs: _start_all(_gather_copies(i, o, ss, rs, mats)),
                 lambda i, o, ss, rs: _finish_all(_gather_copies(i, o, ss, rs, mats)))


def _gather_pass_side(fulls, mats):
    return _Side(fulls, [jax.ShapeDtypeStruct(f.shape, f.dtype) for f in fulls], 3 * len(mats),
                 lambda i, o, ss, rs: _start_all(_pass_copies(o, ss, rs, mats)),
                 lambda i, o, ss, rs: _finish_all(_pass_copies(o, ss, rs, mats)), in_place=True)


def _place_own(fulls, own, mats):
    me = _chip_index().astype(jnp.int32).reshape(1)

    def place(full, blk, m):
        tr = _tile(m.rows, 512, 16)
        nr = m.rows // tr
        window = (lambda i, me: (i, me[0])) if m.kind == "col" else (lambda i, me: (me[0] * nr + i, 0))

        def body(me_ref, b_ref, f_ref, o_ref):
            o_ref[...] = b_ref[...]

        return pl.pallas_call(
            body,
            grid_spec=pltpu.PrefetchScalarGridSpec(
                num_scalar_prefetch=1, grid=(nr,),
                in_specs=[pl.BlockSpec((tr, m.cols), lambda i, me: (i, 0)), ANY],
                out_specs=pl.BlockSpec((tr, m.cols), window)),
            out_shape=jax.ShapeDtypeStruct(full.shape, full.dtype), input_output_aliases={2: 0},
            compiler_params=_cp("parallel"), name="place_own")(me, blk, full)

    return [place(o, a, m) for o, a, m in zip(fulls, own, mats)]


def _gather_weights(own, mats):
    n = len(mats)

    def body(*refs):
        own_refs, full_refs, (send_sems, recv_sems) = refs[:n], refs[n:2 * n], refs[2 * n:]
        first = _gather_copies(own_refs, full_refs, send_sems, recv_sems, mats)
        second = _pass_copies(full_refs, send_sems, recv_sems, mats, base=3 * n)
        _start_all(first)
        for (_, landed), (fwd, _) in zip(first, second):
            landed.wait_recv()
            fwd.start()
        for _, landing in second:
            landing.wait_recv()
        for cp, _ in first + second:
            cp.wait_send()

    outs = _comm_call(body, own, [jax.ShapeDtypeStruct(m.full, a.dtype) for m, a in zip(mats, own)], 6 * n,
                      "gather_weights")
    return _place_own(outs, own, mats)


def _swap_copies(g_refs, b_refs, send_sems, recv_sems, mats):
    x, y, c, _ = _place()
    cps = []
    for g, b, m in zip(g_refs, b_refs, mats):
        if m.kind == "col":
            cps.append(_remote(m.own_half(g, 1 - c), b, send_sems, recv_sems, len(cps), (x, y, 1 - c)))
        else:
            for j in range(4):
                cps.append(_remote(m.block(g, j, 1 - c), b.at[j], send_sems, recv_sems, len(cps), (x, y, 1 - c)))
    return [(cp, cp) for cp in cps]


def _swap_shapes(mats):
    return [jax.ShapeDtypeStruct((m.half, 4 * m.cols) if m.kind == "col" else (4, m.half, m.cols), F32) for m in mats]


def _swap_side(grads, mats):
    return _Side(grads, _swap_shapes(mats), sum(1 if m.kind == "col" else 4 for m in mats),
                 lambda i, o, ss, rs: _start_all(_swap_copies(i, o, ss, rs, mats)),
                 lambda i, o, ss, rs: _finish_all(_swap_copies(i, o, ss, rs, mats)))


def _swap_halves(grads, mats):
    def body(*refs):
        n = len(mats)
        pairs = _swap_copies(refs[:n], refs[n:2 * n], *refs[2 * n:], mats)
        _start_all(pairs)
        _finish_all(pairs)

    return _comm_call(body, grads, _swap_shapes(mats), sum(1 if m.kind == "col" else 4 for m in mats), "rs_swap_halves")


def _pair_sum(own, recv, m, cidx):
    tr = _tile(m.half, 256, 16)
    nh = m.half // tr
    if m.kind == "col":
        own_spec = pl.BlockSpec((tr, m.cols), lambda j, i, c: (c[0] * nh + i, j))
        recv_spec = pl.BlockSpec((tr, m.cols), lambda j, i, c: (i, j))
    else:
        own_spec = pl.BlockSpec((tr, m.cols), lambda j, i, c: ((2 * j + c[0]) * nh + i, 0))
        recv_spec = pl.BlockSpec((None, tr, m.cols), lambda j, i, c: (j, i, 0))

    def body(c_ref, a_ref, b_ref, o_ref):
        o_ref[...] = (a_ref[...] + b_ref[...]).astype(o_ref.dtype)

    return pl.pallas_call(
        body,
        grid_spec=pltpu.PrefetchScalarGridSpec(
            num_scalar_prefetch=1, grid=(4, nh), in_specs=[own_spec, recv_spec],
            out_specs=pl.BlockSpec((None, tr, m.cols), lambda j, i, c: (j, i, 0))),
        out_shape=jax.ShapeDtypeStruct((4, m.half, m.cols), BF16),
        compiler_params=_cp("parallel", "parallel"), name="rs_pair_sum")(cidx, own, recv)


def _scatter_copies(s_refs, b_refs, send_sems, recv_sems):
    x, y, c, chips = _place()
    me = 2 * x + y
    return [(_remote(s.at[2 * cx + cy], b.at[me], send_sems, recv_sems, 3 * k + j, (cx, cy, c)),
             _remote(b.at[2 * cx + cy], b.at[2 * cx + cy], send_sems, recv_sems, 3 * k + j, (cx, cy, c)))
            for k, (s, b) in enumerate(zip(s_refs, b_refs)) for j, (cx, cy) in enumerate(chips)]


def _scatter_side(sps):
    return _Side(sps, [jax.ShapeDtypeStruct(s.shape, s.dtype) for s in sps], 3 * len(sps),
                 lambda i, o, ss, rs: _start_all(_scatter_copies(i, o, ss, rs)),
                 lambda i, o, ss, rs: _finish_all(_scatter_copies(i, o, ss, rs)))


def _scatter_chips(sps):
    def body(*refs):
        n = len(sps)
        pairs = _scatter_copies(refs[:n], refs[n:2 * n], *refs[2 * n:])
        _start_all(pairs)
        _finish_all(pairs)

    return _comm_call(body, sps, [jax.ShapeDtypeStruct(s.shape, s.dtype) for s in sps], 3 * len(sps), "rs_scatter_chips")


def _sum_chips(got, sp, me, cidx):
    _, h, w = got.shape
    tr = _tile(h, 256, 16)
    nh = h // tr

    def body(me_ref, c_ref, g_ref, s_ref, o_ref):
        acc = None
        for s in range(4):
            term = jnp.where(me_ref[0] == s, s_ref[...], g_ref[s]).astype(F32)
            acc = term if acc is None else acc + term
        o_ref[...] = acc

    return pl.pallas_call(
        body,
        grid_spec=pltpu.PrefetchScalarGridSpec(
            num_scalar_prefetch=2, grid=(nh,),
            in_specs=[pl.BlockSpec((4, tr, w), lambda i, me, c: (0, i, 0)),
                      pl.BlockSpec((None, tr, w), lambda i, me, c: (me[0], i, 0))],
            out_specs=pl.BlockSpec((tr, w), lambda i, me, c: (c[0] * nh + i, 0))),
        out_shape=jax.ShapeDtypeStruct((2 * h, w), F32), compiler_params=_cp("parallel"), name="rs_sum_chips",
    )(me, cidx, got, sp)


def _share_halves(fs):
    n = len(fs)

    def body(*refs):
        f_refs, (send_sems, recv_sems) = refs[n:2 * n], refs[2 * n:]
        x, y, c, _ = _place()
        cps = []
        for k, f in enumerate(f_refs):
            h = f.shape[0] // 2
            mine, theirs = f.at[pl.ds(_aligned(c * h, 16), h), :], f.at[pl.ds(_aligned((1 - c) * h, 16), h), :]
            cps.append((_remote(mine, mine, send_sems, recv_sems, k, (x, y, 1 - c)),
                        _remote(theirs, theirs, send_sems, recv_sems, k, (x, y, 1 - c))))
        _start_all(cps)
        _finish_all(cps)

    return pl.pallas_call(
        body, out_shape=tuple(jax.ShapeDtypeStruct(f.shape, f.dtype) for f in fs), in_specs=[ANY] * n,
        out_specs=tuple([ANY] * n), input_output_aliases={k: k for k in range(n)},
        scratch_shapes=[pltpu.SemaphoreType.DMA((n,)), pltpu.SemaphoreType.DMA((n,))],
        compiler_params=pltpu.CompilerParams(has_side_effects=True), name="rs_share_halves")(*fs)


def _gather_all(p):
    r, w = p.shape
    rel = [(dx, dy, dc) for dx in (0, 1) for dy in (0, 1) for dc in (0, 1)][1:]

    def body(p_ref, g_ref, send_sems, recv_sems):
        x, y, c = lax.axis_index("x"), lax.axis_index("y"), lax.axis_index("c")
        me = 4 * x + 2 * y + c
        g_ref[me] = p_ref[...]
        peers = [(jnp.bitwise_xor(x, dx), jnp.bitwise_xor(y, dy), jnp.bitwise_xor(c, dc)) for dx, dy, dc in rel]
        cps = [_remote(p_ref, g_ref.at[me], send_sems, recv_sems, k, peer) for k, peer in enumerate(peers)]
        for cp in cps:
            cp.start()
        for k, (px, py, pc) in enumerate(peers):
            slot = g_ref.at[4 * px + 2 * py + pc]
            _remote(slot, slot, send_sems, recv_sems, k, (px, py, pc)).wait_recv()
        for cp in cps:
            cp.wait_send()

    return pl.pallas_call(
        body, out_shape=jax.ShapeDtypeStruct((8, r, w), p.dtype),
        in_specs=[pl.BlockSpec(memory_space=pltpu.VMEM)], out_specs=pl.BlockSpec(memory_space=pltpu.VMEM),
        scratch_shapes=[pltpu.SemaphoreType.DMA((7,)), pltpu.SemaphoreType.DMA((7,))],
        compiler_params=pltpu.CompilerParams(has_side_effects=True, vmem_limit_bytes=VMEM_LIMIT),
        name=("gather_all_%dx%d" % (r, w)))(p)


def _pack_small(arrs):
    flat = jnp.concatenate([a.reshape(-1) for a in arrs])
    rows = -(-flat.shape[0] // (SMALL_W * 8)) * 8
    return jnp.pad(flat, (0, rows * SMALL_W - flat.shape[0])).reshape(rows, SMALL_W)


def _unpack(buf, shapes):
    flat = buf.reshape(-1)
    out, off = [], 0
    for sh in shapes:
        n = math.prod(sh)
        out.append(flat[off:off + n].reshape(sh))
        off += n
    return out


def _chip_padded(pieces, width, padded):
    out, used = [], 0
    for p in pieces:
        at = 0
        while at < p.shape[1]:
            take = min(p.shape[1] - at, width - used)
            out.append(p[:, at:at + take])
            at, used = at + take, used + take
            if used == width:
                out.append(jnp.zeros((p.shape[0], padded - width), p.dtype))
                used = 0
    assert used == 0
    return jnp.concatenate(out, axis=1)


class _Cfg:
    def __init__(self, x, w_in, attn_sinks, dn_a_log, w_ffn_up):
        self.S, self.D = x.shape[1], x.shape[2]
        self.L = w_in.shape[0]
        self.HA, self.HD = attn_sinks.shape[1], dn_a_log.shape[1]
        self.IN = 4 * w_in.shape[2]
        kvw = self.IN - self.HA * HEAD - 4 * self.HD * HEAD - 2 * self.HD - 2 * self.D
        self.KV = kvw // (2 * HEAD)
        self.G = self.HA // self.KV
        self.FF = 2 * w_ffn_up.shape[2]
        self.KA = self.HA * HEAD
        self.VA = self.KA + self.KV * HEAD
        self.QD = self.VA + self.KV * HEAD
        self.Z = self.QD + 3 * self.HD * HEAD
        self.GA = self.Z + self.HD * HEAD
        self.GD = self.GA + self.D
        self.NB = self.GD + self.D
        assert self.NB + 2 * self.HD == self.IN and self.G <= 8 and 2 * self.HD <= LANES


def kernel(x, positions, norm_mix_pre, w_in, attn_sinks, dn_conv_w, dn_a_log, dn_dt_bias, dn_norm_w, w_branch_attn, w_branch_dn, w_out, norm_mix_post, norm_ffn_pre, w_ffn_up, ffn_conv_w, ffn_conv_b, w_ffn_down, norm_ffn_post, loss_target, m_norm_mix_pre, m_w_in, m_attn_sinks, m_dn_conv_w, m_dn_a_log, m_dn_dt_bias, m_dn_norm_w, m_w_branch_attn, m_w_branch_dn, m_w_out, m_norm_mix_post, m_norm_ffn_pre, m_w_ffn_up, m_ffn_conv_w, m_ffn_conv_b, m_w_ffn_down, m_norm_ffn_post, v_norm_mix_pre, v_w_in, v_attn_sinks, v_dn_conv_w, v_dn_a_log, v_dn_dt_bias, v_dn_norm_w, v_w_branch_attn, v_w_branch_dn, v_w_out, v_norm_mix_post, v_norm_ffn_pre, v_w_ffn_up, v_ffn_conv_w, v_ffn_conv_b, v_w_ffn_down, v_norm_ffn_post):
    cfg = _Cfg(x, w_in, attn_sinks, dn_a_log, w_ffn_up)
    S, D, L, HD = cfg.S, cfg.D, cfg.L, cfg.HD
    o1 = cfg.Z
    ins = w_in.shape[2]
    insp = -(-ins // SHARD_PAD) * SHARD_PAD
    mats = [_Mat("col", D, insp), _Mat("row", *w_branch_attn.shape[1:]), _Mat("row", *w_branch_dn.shape[1:]),
            _Mat("row", *w_out.shape[1:]), _Mat("col", *w_ffn_up.shape[1:]), _Mat("row", *w_ffn_down.shape[1:])]
    me = _chip_index()

    taps = _gather_all(_pack_small([dn_conv_w, ffn_conv_w]))
    tap_shapes = [dn_conv_w.shape, ffn_conv_w.shape]
    per_chip = [_unpack(taps[2 * j], tap_shapes) for j in range(4)]
    dnw_all = jnp.concatenate([per_chip[j][0] for j in range(4)], axis=2)
    fcw_all = jnp.concatenate([per_chip[j][1] for j in range(4)], axis=2)

    def own_blocks(l):
        return [jnp.pad(w_in[l].astype(BF16), ((0, 0), (0, insp - ins))), w_branch_attn[l].astype(BF16),
                w_branch_dn[l].astype(BF16), w_out[l].astype(BF16), w_ffn_up[l].astype(BF16), w_ffn_down[l].astype(BF16)]

    def layer_weights(l, fulls):
        win_sm, wbra, wbrd, wout, wup, wdown = fulls
        win = jnp.concatenate([win_sm[:, j * insp:j * insp + ins] for j in range(4)], axis=1)
        return dict(
            win_sm=win_sm, wbig=jnp.concatenate([win[:, :o1], win[:, o1 + 2 * HD:]], axis=1),
            wba=jnp.pad(win[:, o1:o1 + 2 * HD], ((0, 0), (0, LANES - 2 * HD))),
            wbra=wbra, wbrd=wbrd, wout=wout, wup=wup, wdown=wdown, dnw=dnw_all[l], fcw=fcw_all[l])

    weights = [layer_weights(0, _gather_weights(own_blocks(0), mats))] + [None] * (L - 1)

    inv_freq = ROPE_THETA ** (-jnp.arange(0, 2 * ROT_HALF, 2, dtype=F32) / (2 * ROT_HALF))
    ang = positions.reshape(S, 1).astype(F32) * inv_freq
    cosf = jnp.concatenate([jnp.cos(ang), jnp.cos(ang), jnp.ones((S, HEAD - 2 * ROT_HALF), F32)], axis=1)
    sinf = jnp.concatenate([-jnp.sin(ang), jnp.sin(ang), jnp.zeros((S, HEAD - 2 * ROT_HALF), F32)], axis=1)

    h = x.reshape(S, D)
    saved = []
    for l in range(L):
        wt = weights[l]
        t = dict(h0=h)
        t["xn"] = _rms_fwd(h, norm_mix_pre[l], out_dtype=BF16, name="norm_mix_pre")
        nxt = own_blocks(l + 1) if l + 1 < L else None
        if nxt:
            t["proj"], win_landed = _mm(t["xn"], wt["wbig"], name="proj_gather", side=_gather_ici_side(nxt[:1], mats[:1]))
        else:
            t["proj"] = _mm(t["xn"], wt["wbig"], name="proj")
        t["pba"] = _mm(t["xn"], wt["wba"], name="proj_ba")
        t["oa"] = _attn_fwd(t["proj"], cosf, sinf, attn_sinks[l], cfg)
        t["qkv"] = _dnconv_fwd(t["proj"], wt["dnw"], cfg)
        t["od"], t["st"], t["tinv"], *landed = _gdn_fwd(t["qkv"], t["proj"], t["pba"], dn_a_log[l], dn_dt_bias[l], dn_norm_w[l], cfg,
                                             side=_gather_ici_side(nxt[1:], mats[1:]) if nxt else None)
        t["pa"] = _mm(t["oa"], wt["wbra"], name="branch_attn")
        t["pd"] = _mm(t["od"], wt["wbrd"], name="branch_dn")
        t["y"] = _merge_fwd(t["proj"], t["pa"], t["pd"], cfg)
        t["mix"] = _mm(t["y"], wt["wout"], name="mix_out")
        t["h1"] = _rms_fwd(t["mix"], norm_mix_post[l], res=h, out_dtype=F32, name="norm_mix_post")
        t["xf"] = _rms_fwd(t["h1"], norm_ffn_pre[l], out_dtype=BF16, name="norm_ffn_pre")
        if nxt:
            t["up"], *landed = _mm(t["xf"], wt["wup"], name="ffn_up_pass",
                                   side=_gather_pass_side([win_landed] + landed, mats))
            weights[l + 1] = layer_weights(l + 1, _place_own(landed, nxt, mats))
        else:
            t["up"] = _mm(t["xf"], wt["wup"], name="ffn_up")
        t["act"] = _glu_fwd(t["up"], wt["fcw"], ffn_conv_b[l], cfg)
        t["f"] = _mm(t["act"], wt["wdown"], name="ffn_down")
        h = _rms_fwd(t["f"], norm_ffn_post[l], res=t["h1"], out_dtype=F32, name="norm_ffn_post")
        saved.append(t)

    loss_local, dh = _loss_head(h, loss_target.reshape(S, D))
    loss = lax.psum(loss_local, ("x", "y", "c"))

    cidx = lax.axis_index("c").astype(jnp.int32).reshape(1)
    midx = me.astype(jnp.int32).reshape(1)
    big_grads, small = [None] * L, [None] * L
    reduce_blocks = lambda got, pair: _share_halves([_sum_chips(a, b, midx, cidx) for a, b in zip(got, pair)])
    pair_sums = lambda full, theirs: [_pair_sum(a, b, m, cidx) for a, b, m in zip(full, theirs, mats)]
    unswapped = waiting = None
    early = [4]
    late = [k for k in range(len(mats)) if k not in early]
    pick = lambda xs, ks: [xs[k] for k in ks]
    for l in reversed(range(L)):
        wt, t = weights[l], saved[l]
        df, g_nfpost = _rms_bwd(t["f"], norm_ffn_post[l], dh, out_dtype=BF16, name="norm_ffn_post_bwd")
        dact = _mm(df, wt["wdown"], tb=True, name="ffn_down_dx")
        if unswapped:
            g_wdown, their_win = _mm(t["act"], df, ta=True, name="ffn_down_dw_swap",
                                     side=_swap_side(unswapped[1][:1], mats[:1]))
        else:
            g_wdown = _mm(t["act"], df, ta=True, name="ffn_down_dw")
        dup, g_fcw, g_fcb = _glu_bwd(t["up"], dact, wt["fcw"], ffn_conv_b[l], cfg)
        if unswapped:
            dxf, *theirs = _mm(dup, wt["wup"], tb=True, name="ffn_up_dx_swap",
                               side=_swap_side(unswapped[1][1:], mats[1:]))
            waiting = (unswapped[0], pair_sums(unswapped[1], [their_win] + theirs))
            g_wup, *got_early = _mm(t["xf"], dup, ta=True, name="ffn_up_dw_scatter",
                                    side=_scatter_side(pick(waiting[1], early)))
        else:
            dxf = _mm(dup, wt["wup"], tb=True, name="ffn_up_dx")
            g_wup = _mm(t["xf"], dup, ta=True, name="ffn_up_dw")
        dh1, g_nfp = _rms_bwd(t["h1"], norm_ffn_pre[l], dxf, add=dh, name="norm_ffn_pre_bwd")
        dmix, g_nmpost = _rms_bwd(t["mix"], norm_mix_post[l], dh1, out_dtype=BF16, name="norm_mix_post_bwd")
        dy = _mm(dmix, wt["wout"], tb=True, name="mix_out_dx")
        g_wout = _mm(t["y"], dmix, ta=True, name="mix_out_dw")
        dpa, dpd, dga, dgd = _merge_bwd(t["proj"], t["pa"], t["pd"], dy, cfg)
        doa = _mm(dpa, wt["wbra"], tb=True, out_dtype=BF16, name="branch_attn_dx")
        g_wbra = _mm(t["oa"], dpa, ta=True, name="branch_attn_dw")
        dod = _mm(dpd, wt["wbrd"], tb=True, out_dtype=BF16, name="branch_dn_dx")
        g_wbrd = _mm(t["od"], dpd, ta=True, name="branch_dn_dw")
        dqkv_c, dz, dba, g_alog, g_dt, g_dnnorm, got = _gdn_bwd(
            t["qkv"], t["proj"], t["pba"], dn_a_log[l], dn_dt_bias[l], dn_norm_w[l], t["st"], t["tinv"], dod, cfg,
            side=_scatter_side(pick(waiting[1], late)) if waiting else None)
        if waiting:
            landed = dict(zip(early + late, got_early + got))
            big_grads[waiting[0]] = reduce_blocks([landed[k] for k in range(len(mats))], waiting[1])
            waiting = None
        dqkv_d, g_dnw = _dnconv_bwd(t["proj"], dqkv_c, wt["dnw"], cfg)
        dqkv_a, g_sinks = _attn_bwd(t["proj"], doa, cosf, sinf, attn_sinks[l], cfg)
        dproj = _chip_padded([dqkv_a, dqkv_d, dba[:, :2 * HD], dz, dga, dgd], ins, insp)
        if l > 0:
            dxn = _mm(dproj, wt["win_sm"], tb=True, name="proj_dx")
            g_win = _mm(t["xn"], dproj, ta=True, name="proj_dw")
        else:
            rest = [g_wbra, g_wbrd, g_wout, g_wup, g_wdown]
            dxn, *their_rest = _mm(dproj, wt["win_sm"], tb=True, name="proj_dx_swap", side=_swap_side(rest, mats[1:]))
            rest_pair = [_pair_sum(a, b, m, cidx) for a, b, m in zip(rest, their_rest, mats[1:])]
            g_win, *got_rest = _mm(t["xn"], dproj, ta=True, name="proj_dw_scatter", side=_scatter_side(rest_pair))
        dh, g_nmp = _rms_bwd(t["h0"], norm_mix_pre[l], dxn, add=dh1, name="norm_mix_pre_bwd")

        unswapped = (l, [g_win, g_wbra, g_wbrd, g_wout, g_wup, g_wdown])
        small[l] = [g_nmp, g_sinks, g_alog, g_dt, g_dnnorm, g_nmpost, g_nfp, g_fcb, g_nfpost, g_dnw, g_fcw]

    win_pair = [_pair_sum(g_win, _swap_halves([g_win], mats[:1])[0], mats[0], cidx)]
    big_grads[0] = reduce_blocks(list(_scatter_chips(win_pair)) + got_rest, win_pair + rest_pair)
    grad_x = dh.reshape(1, S, D)

    small_shapes = [a.shape for a in small[0]]
    summed = _sum_leading(_gather_all(_pack_small([a for l in range(L) for a in small[l]])), name="sum_devices")
    per_layer = _unpack(summed, small_shapes * L)
    red = [jnp.stack([per_layer[l * len(small_shapes) + i] for l in range(L)]) for i in range(len(small_shapes))]
    g_dn_conv = lax.dynamic_slice_in_dim(red[9], me * dn_conv_w.shape[2], dn_conv_w.shape[2], axis=2)
    g_ffn_conv = lax.dynamic_slice_in_dim(red[10], me * ffn_conv_w.shape[2], ffn_conv_w.shape[2], axis=2)
    stacked = lambda i: jnp.stack([big_grads[l][i] for l in range(L)])

    grads = dict(
        norm_mix_pre=red[0], w_in=stacked(0), attn_sinks=red[1], dn_conv_w=g_dn_conv, dn_a_log=red[2],
        dn_dt_bias=red[3], dn_norm_w=red[4], w_branch_attn=stacked(1), w_branch_dn=stacked(2), w_out=stacked(3),
        norm_mix_post=red[5], norm_ffn_pre=red[6], w_ffn_up=stacked(4), ffn_conv_w=g_ffn_conv, ffn_conv_b=red[7],
        w_ffn_down=stacked(5), norm_ffn_post=red[8])

    params = dict(
        norm_mix_pre=(norm_mix_pre, m_norm_mix_pre, v_norm_mix_pre), w_in=(w_in, m_w_in, v_w_in),
        attn_sinks=(attn_sinks, m_attn_sinks, v_attn_sinks), dn_conv_w=(dn_conv_w, m_dn_conv_w, v_dn_conv_w),
        dn_a_log=(dn_a_log, m_dn_a_log, v_dn_a_log), dn_dt_bias=(dn_dt_bias, m_dn_dt_bias, v_dn_dt_bias),
        dn_norm_w=(dn_norm_w, m_dn_norm_w, v_dn_norm_w),
        w_branch_attn=(w_branch_attn, m_w_branch_attn, v_w_branch_attn),
        w_branch_dn=(w_branch_dn, m_w_branch_dn, v_w_branch_dn), w_out=(w_out, m_w_out, v_w_out),
        norm_mix_post=(norm_mix_post, m_norm_mix_post, v_norm_mix_post),
        norm_ffn_pre=(norm_ffn_pre, m_norm_ffn_pre, v_norm_ffn_pre), w_ffn_up=(w_ffn_up, m_w_ffn_up, v_w_ffn_up),
        ffn_conv_w=(ffn_conv_w, m_ffn_conv_w, v_ffn_conv_w), ffn_conv_b=(ffn_conv_b, m_ffn_conv_b, v_ffn_conv_b),
        w_ffn_down=(w_ffn_down, m_w_ffn_down, v_w_ffn_down),
        norm_ffn_post=(norm_ffn_post, m_norm_ffn_post, v_norm_ffn_post))
    names = list(params)

    grad_out, delta, new_m, new_v = {}, {}, {}, {}
    small_names = [n for n in names if params[n][0].ndim == 2]
    for n in names:
        if n in small_names:
            continue
        w, m, v = params[n]
        grad_out[n], delta[n], new_m[n], new_v[n] = _adamw(w, grads[n], m, v, name="adamw_" + n)

    outs = _adamw(*[_pack_small([params[n][i] if i is not None else grads[n] for n in small_names])[None]
                    for i in (0, None, 1, 2)], name="adamw_small")
    shapes = [params[n][0].shape for n in small_names]
    for dst, o in zip((grad_out, delta, new_m, new_v), outs):
        for n, a in zip(small_names, _unpack(o, shapes)):
            dst[n] = a

    return (loss, grad_x, *[grad_out[n] for n in names], *[delta[n] for n in names],
            *[new_m[n] for n in names], *[new_v[n] for n in names])
```
